```python
import math
import jax, jax.numpy as jnp
from jax import lax
import numpy as np

D_MODEL = 4096
BATCH = 4
SEQ = 2048
DEPTH = 1

N_MEM = 256
Q_BLOCK = 128
ROPE_THETA = 10000.0
EPS = 1e-6
FOX_HEADS = 16
FOX_HEAD_DIM = 128
DSA_HEADS = 16
DSA_NOPE_DIM = 128
DSA_ROPE_DIM = 64
DSA_V_DIM = 128
Q_LORA = 1024
KV_LORA = 256
IDX_HEADS = 32
IDX_DIM = 128
DSA_TOPK_MAX = 256
MEM_HEADS = 4
MEM_HEAD_DIM = 128
N_GROUPS = 8
EXPERTS_PER_GROUP = 8
N_EXPERTS = N_GROUPS * EXPERTS_PER_GROUP
TOPK_IN_GROUP = 2
D_EXPERT = 512
EXPERT_BLOCK = 128

FOX_WIDTH = FOX_HEADS * FOX_HEAD_DIM
DSA_WIDTH = DSA_HEADS * DSA_V_DIM
IN_SPLITS = (FOX_WIDTH, FOX_WIDTH, FOX_WIDTH, FOX_HEADS, Q_LORA, KV_LORA, DSA_ROPE_DIM, IDX_DIM, IDX_HEADS, D_MODEL, D_MODEL)
IN_WIDTH = sum(IN_SPLITS)

kernel_name = "fox_dsa_gated_hier_moe_block"


def rmsnorm(x, g):
    x32 = x.astype(jnp.float32)
    y = x32 * lax.rsqrt(jnp.mean(x32 * x32, axis=-1, keepdims=True) + EPS)
    return (y * g.astype(jnp.float32)).astype(x.dtype)


def layernorm(x, g, b):
    x32 = x.astype(jnp.float32)
    mu = jnp.mean(x32, axis=-1, keepdims=True)
    xc = x32 - mu
    y = xc * lax.rsqrt(jnp.mean(xc * xc, axis=-1, keepdims=True) + EPS)
    return (y * g.astype(jnp.float32) + b.astype(jnp.float32)).astype(x.dtype)


def rope(x, pos):
    d = x.shape[-1]
    inv = jnp.power(ROPE_THETA, -jnp.arange(0, d, 2, dtype=jnp.float32) / d)
    ang = pos.astype(jnp.float32)[:, None] * inv[None, :]
    shape = (1, x.shape[1]) + (1,) * (x.ndim - 3) + (d // 2,)
    cos = jnp.cos(ang).reshape(shape)
    sin = jnp.sin(ang).reshape(shape)
    x32 = x.astype(jnp.float32)
    x1, x2 = x32[..., : d // 2], x32[..., d // 2:]
    return jnp.concatenate([x1 * cos - x2 * sin, x2 * cos + x1 * sin], axis=-1).astype(x.dtype)


def fox_attention(q, k, v, logf):
    B, S, H, dh = q.shape
    cum = jnp.cumsum(logf, axis=1).transpose(0, 2, 1)
    scale = dh ** -0.5
    outs = []
    for i in range(S // Q_BLOCK):
        q0, q1 = i * Q_BLOCK, (i + 1) * Q_BLOCK
        s = jnp.einsum('bqhd,bkhd->bhqk', q[:, q0:q1], k[:, :q1], preferred_element_type=jnp.float32) * scale
        decay = cum[:, :, q0:q1, None] - cum[:, :, None, :q1]
        causal = jnp.arange(q0, q1)[:, None] >= jnp.arange(q1)[None, :]
        s = jnp.where(causal, s + decay, -jnp.inf)
        p = jax.nn.softmax(s, axis=-1).astype(v.dtype)
        outs.append(jnp.einsum('bhqk,bkhd->bqhd', p, v[:, :q1]))
    return jnp.concatenate(outs, axis=1)


def dsa_attention(q_cat, kv_cat, q_idx, k_idx, w_idx, topk):
    B, S = q_cat.shape[:2]
    nb = S // Q_BLOCK

    def blocks(a):
        return a.reshape((B, nb, Q_BLOCK) + a.shape[2:]).swapaxes(0, 1)

    qpos = jnp.arange(S, dtype=jnp.int32).reshape(nb, Q_BLOCK)
    kpos = jnp.arange(S, dtype=jnp.int32)
    scale = (DSA_NOPE_DIM + DSA_ROPE_DIM) ** -0.5

    def one_block(args):
        qc, qi, wi, qp = args
        dots = jnp.einsum('bqhd,bkd->bqhk', qi, k_idx, preferred_element_type=jnp.float32)
        score = jnp.einsum('bqh,bqhk->bqk', wi, jax.nn.relu(dots))
        score = jnp.where(kpos[None, None, :] <= qp[None, :, None], score, -jnp.inf)
        _, sel = lax.top_k(score, topk)
        kv_sel = jax.vmap(lambda kv, idx: kv[idx])(kv_cat, sel)
        s = jnp.einsum('bqhc,bqkc->bqhk', qc, kv_sel, preferred_element_type=jnp.float32) * scale
        valid = sel <= qp[None, :, None]
        s = jnp.where(valid[:, :, None, :], s, -jnp.inf)
        p = jax.nn.softmax(s, axis=-1).astype(kv_cat.dtype)
        return jnp.einsum('bqhk,bqkr->bqhr', p, kv_sel[..., :KV_LORA])

    o = lax.map(one_block, (blocks(q_cat), blocks(q_idx), blocks(w_idx), qpos))
    return o.swapaxes(0, 1).reshape(B, S, o.shape[3], o.shape[4])


def mixer_block(h, pos, topk, w_in, b_f, g_q_lat, g_kv_lat, g_idx_k, b_idx_k, w_uq, w_idx_q, w_uk, w_uv, w_up_a, w_up_b, w_out):
    B, S, _ = h.shape
    cuts = np.cumsum(IN_SPLITS)[:-1].tolist()
    qa, ka, va, fa, cq, ckv, kr, ik, iw, ga, gb = jnp.split(h @ w_in, cuts, axis=-1)
    hs = (B, S, FOX_HEADS, FOX_HEAD_DIM)
    logf = jax.nn.log_sigmoid(fa.astype(jnp.float32) + b_f.astype(jnp.float32))
    y_a = fox_attention(qa.reshape(hs), ka.reshape(hs), va.reshape(hs), logf).reshape(B, S, FOX_WIDTH) @ w_up_a
    cq = rmsnorm(cq, g_q_lat)
    q = (cq @ w_uq).reshape(B, S, DSA_HEADS, DSA_NOPE_DIM + DSA_ROPE_DIM)
    q_nope, q_rope = q[..., :DSA_NOPE_DIM], rope(q[..., DSA_NOPE_DIM:], pos)
    q_lat = jnp.einsum('bshn,hnr->bshr', q_nope, w_uk)
    q_cat = jnp.concatenate([q_lat, q_rope], axis=-1)
    kv_cat = jnp.concatenate([rmsnorm(ckv, g_kv_lat), rope(kr, pos)], axis=-1)
    q_idx = rope((cq @ w_idx_q).reshape(B, S, IDX_HEADS, IDX_DIM), pos)
    k_idx = rope(layernorm(ik, g_idx_k, b_idx_k), pos)
    w_idx = iw.astype(jnp.float32) * (IDX_HEADS * IDX_DIM) ** -0.5
    o_lat = dsa_attention(q_cat, kv_cat, q_idx, k_idx, w_idx, topk)
    y_b = jnp.einsum('bshr,hrv->bshv', o_lat, w_uv).reshape(B, S, DSA_WIDTH) @ w_up_b
    merged = jax.nn.sigmoid(ga) * y_a + jax.nn.sigmoid(gb) * y_b
    return merged @ w_out


def memory_cross_attention(h, mem_n, w_qm, w_km, w_vm, w_om):
    B, S, _ = h.shape
    M = mem_n.shape[1]
    q = (h @ w_qm).reshape(B, S, MEM_HEADS, MEM_HEAD_DIM)
    k = (mem_n @ w_km).reshape(B, M, MEM_HEADS, MEM_HEAD_DIM)
    v = (mem_n @ w_vm).reshape(B, M, MEM_HEADS, MEM_HEAD_DIM)
    s = jnp.einsum('bqhd,bmhd->bhqm', q, k, preferred_element_type=jnp.float32) * MEM_HEAD_DIM ** -0.5
    p = jax.nn.softmax(s, axis=-1).astype(v.dtype)
    return jnp.einsum('bhqm,bmhd->bqhd', p, v).reshape(B, S, MEM_HEADS * MEM_HEAD_DIM) @ w_om


def hierarchical_moe(h, w_rg, b_rg, w_re, b_re, w_gate, w_up, w_down):
    B, S, D = h.shape
    N = B * S
    M = N * TOPK_IN_GROUP
    x = h.reshape(N, D)
    x32 = x.astype(jnp.float32)
    p_grp = jax.nn.softmax(x32 @ w_rg.astype(jnp.float32) + b_rg.astype(jnp.float32), axis=-1)
    gate_g, grp = lax.top_k(p_grp, 1)
    logit_e = (x32 @ w_re.astype(jnp.float32) + b_re.astype(jnp.float32)).reshape(N, N_GROUPS, EXPERTS_PER_GROUP)
    logit_e = logit_e[jnp.arange(N), grp[:, 0]]
    val_e, loc_e = lax.top_k(logit_e, TOPK_IN_GROUP)
    weight = gate_g * jax.nn.softmax(val_e, axis=-1)
    expert = grp * EXPERTS_PER_GROUP + loc_e
    e_flat = expert.reshape(M)
    w_flat = weight.reshape(M)
    tok_flat = jnp.arange(M, dtype=jnp.int32) // TOPK_IN_GROUP
    order = jnp.argsort(e_flat)
    e_sorted, tok_sorted, w_sorted = e_flat[order], tok_flat[order], w_flat[order]
    counts = jnp.bincount(e_flat, length=N_EXPERTS)
    starts = jnp.cumsum(counts) - counts
    padded = (counts + EXPERT_BLOCK - 1) // EXPERT_BLOCK * EXPERT_BLOCK
    pend = jnp.cumsum(padded)
    pstart = pend - padded
    dest = pstart[e_sorted] + jnp.arange(M, dtype=jnp.int32) - starts[e_sorted]
    cap = -(-(M + N_EXPERTS * (EXPERT_BLOCK - 1)) // EXPERT_BLOCK) * EXPERT_BLOCK
    nblk = cap // EXPERT_BLOCK
    x_buf = jnp.zeros((cap, D), x.dtype).at[dest].set(x[tok_sorted])
    blk_expert = jnp.minimum(jnp.searchsorted(pend, jnp.arange(nblk, dtype=jnp.int32) * EXPERT_BLOCK, side='right'), N_EXPERTS - 1)

    def expert_block(args):
        xb, e = args
        return (jax.nn.silu(xb @ w_gate[e]) * (xb @ w_up[e])) @ w_down[e]

    y_buf = lax.map(expert_block, (x_buf.reshape(nblk, EXPERT_BLOCK, D), blk_expert)).reshape(cap, D)
    y = y_buf[dest] * w_sorted[:, None].astype(x.dtype)
    return jax.ops.segment_sum(y, tok_sorted, num_segments=N).reshape(B, S, D)


def setup_inputs(seed: int = 0) -> dict:
    key = jax.random.key(seed)
    ks = iter(jax.random.split(key, 48))
    L = DEPTH

    def nrm(shape, scale):
        return jax.random.normal(next(ks), shape, jnp.float32) * scale

    def gain(shape):
        return 1.0 + 0.02 * jax.random.normal(next(ks), shape, jnp.float32)

    return {
        'x': nrm((BATCH, SEQ, D_MODEL), 1.0),
        'mem': nrm((BATCH, N_MEM, D_MODEL), 1.0),
        'g_norm_mix': gain((L, D_MODEL)),
        'w_in': nrm((L, D_MODEL, IN_WIDTH), D_MODEL ** -0.5),
        'b_f': jax.random.uniform(next(ks), (L, FOX_HEADS), jnp.float32, 1.0, 4.0),
        'g_q_lat': gain((L, Q_LORA)),
        'g_kv_lat': gain((L, KV_LORA)),
        'g_idx_k': gain((L, IDX_DIM)),
        'b_idx_k': nrm((L, IDX_DIM), 0.02),
        'w_uq': nrm((L, Q_LORA, DSA_HEADS * (DSA_NOPE_DIM + DSA_ROPE_DIM)), Q_LORA ** -0.5),
        'w_idx_q': nrm((L, Q_LORA, IDX_HEADS * IDX_DIM), Q_LORA ** -0.5),
        'w_uk': nrm((L, DSA_HEADS, DSA_NOPE_DIM, KV_LORA), DSA_NOPE_DIM ** -0.5),
        'w_uv': nrm((L, DSA_HEADS, KV_LORA, DSA_V_DIM), KV_LORA ** -0.5),
        'w_up_a': nrm((L, FOX_WIDTH, D_MODEL), FOX_WIDTH ** -0.5),
        'w_up_b': nrm((L, DSA_WIDTH, D_MODEL), DSA_WIDTH ** -0.5),
        'w_out': nrm((L, D_MODEL, D_MODEL), D_MODEL ** -0.5),
        'g_norm_mem_x': gain((L, D_MODEL)),
        'g_mem': gain((L, D_MODEL)),
        'w_qm': nrm((L, D_MODEL, MEM_HEADS * MEM_HEAD_DIM), D_MODEL ** -0.5),
        'w_km': nrm((L, D_MODEL, MEM_HEADS * MEM_HEAD_DIM), D_MODEL ** -0.5),
        'w_vm': nrm((L, D_MODEL, MEM_HEADS * MEM_HEAD_DIM), D_MODEL ** -0.5),
        'w_om': nrm((L, MEM_HEADS * MEM_HEAD_DIM, D_MODEL), (MEM_HEADS * MEM_HEAD_DIM) ** -0.5),
        'g_norm_ffn': gain((L, D_MODEL)),
        'w_rg': nrm((L, D_MODEL, N_GROUPS), D_MODEL ** -0.5),
        'b_rg': nrm((L, N_GROUPS), 0.01),
        'w_re': nrm((L, D_MODEL, N_EXPERTS), D_MODEL ** -0.5),
        'b_re': nrm((L, N_EXPERTS), 0.01),
        'w_gate': nrm((L, N_EXPERTS, D_MODEL, D_EXPERT), D_MODEL ** -0.5),
        'w_up': nrm((L, N_EXPERTS, D_MODEL, D_EXPERT), D_MODEL ** -0.5),
        'w_down': nrm((L, N_EXPERTS, D_EXPERT, D_MODEL), D_EXPERT ** -0.5),
        'g_final': gain((D_MODEL,)),
    }


def reference(x, mem, g_norm_mix, w_in, b_f, g_q_lat, g_kv_lat, g_idx_k, b_idx_k, w_uq, w_idx_q, w_uk, w_uv,
              w_up_a, w_up_b, w_out, g_norm_mem_x, g_mem, w_qm, w_km, w_vm, w_om, g_norm_ffn,
              w_rg, b_rg, w_re, b_re, w_gate, w_up, w_down, g_final):
    S = x.shape[1]
    pos = jnp.arange(S, dtype=jnp.int32)
    topk = min(DSA_TOPK_MAX, S // 4)
    for l in range(DEPTH):
        h = rmsnorm(x, g_norm_mix[l])
        x = x + mixer_block(h, pos, topk, w_in[l], b_f[l], g_q_lat[l], g_kv_lat[l], g_idx_k[l], b_idx_k[l],
                            w_uq[l], w_idx_q[l], w_uk[l], w_uv[l], w_up_a[l], w_up_b[l], w_out[l])
        x = x + memory_cross_attention(rmsnorm(x, g_norm_mem_x[l]), rmsnorm(mem, g_mem[l]),
                                       w_qm[l], w_km[l], w_vm[l], w_om[l])
        x = x + hierarchical_moe(rmsnorm(x, g_norm_ffn[l]), w_rg[l], b_rg[l], w_re[l], b_re[l],
                                 w_gate[l], w_up[l], w_down[l])
    return rmsnorm(x, g_final)
```

```python
import functools

import jax
import jax.numpy as jnp
import numpy as np
from jax import lax
from jax.experimental import pallas as pl
from jax.experimental.pallas import tpu as pltpu

F32 = jnp.float32
BF16 = jnp.bfloat16

EPS = 1e-6
ROPE_THETA = 10000.0
FOX_HEADS = 16
FOX_HEAD_DIM = 128
DSA_HEADS = 16
DSA_NOPE_DIM = 128
DSA_ROPE_DIM = 64
DSA_V_DIM = 128
Q_LORA = 1024
KV_LORA = 256
IDX_HEADS = 32
IDX_DIM = 128
DSA_TOPK_MAX = 256
MEM_HEADS = 4
MEM_HEAD_DIM = 128
N_GROUPS = 8
EXPERTS_PER_GROUP = 8
N_EXPERTS = N_GROUPS * EXPERTS_PER_GROUP
TOPK_IN_GROUP = 2
D_EXPERT = 512

FOX_WIDTH = FOX_HEADS * FOX_HEAD_DIM
LANES = 128
KV_PAD = KV_LORA + LANES
NEG_BIG = -1e30
VMEM_LIMIT_BYTES = 56 * 1024 * 1024
ITEM_ROWS = 512
ITEM_SUB = 128
F_CHUNK = 128


def _params(*sem):
    return pltpu.CompilerParams(dimension_semantics=sem, vmem_limit_bytes=VMEM_LIMIT_BYTES)


def _rmsnorm_kernel(x_ref, g_ref, o_ref):
    x = x_ref[...].astype(F32)
    ms = jnp.mean(x * x, axis=-1, keepdims=True)
    o_ref[...] = (x * lax.rsqrt(ms + EPS) * g_ref[...]).astype(o_ref.dtype)


def _rmsnorm(x2d, g, out_dtype, tm=256):
    n, d = x2d.shape
    return pl.pallas_call(
        _rmsnorm_kernel,
        grid=(n // tm,),
        in_specs=[pl.BlockSpec((tm, d), lambda i: (i, 0)), pl.BlockSpec((1, d), lambda i: (0, 0))],
        out_specs=pl.BlockSpec((tm, d), lambda i: (i, 0)),
        out_shape=jax.ShapeDtypeStruct((n, d), out_dtype),
        compiler_params=_params("arbitrary"),
        name="rmsnorm",
    )(x2d, g.reshape(1, d).astype(F32))


def _mm_kernel(a_ref, b_ref, *rest, has_res, head_out, rope):
    pos = 0
    res_ref = cos_ref = sin_ref = None
    if has_res:
        res_ref = rest[pos]
        pos += 1
    if rope:
        cos_ref, sin_ref = rest[pos], rest[pos + 1]
        pos += 2
    o_ref = rest[pos]
    wb_ref = rest[pos + 1]

    @pl.when(pl.program_id(1) == 0)
    def _():
        wb_ref[...] = b_ref[...].astype(BF16)

    acc = jnp.dot(a_ref[...], wb_ref[...], preferred_element_type=F32)
    if has_res:
        acc = acc + res_ref[...]
    if head_out:
        for hh in range(acc.shape[1] // LANES):
            part = acc[:, hh * LANES:(hh + 1) * LANES]
            if rope:
                part = part * cos_ref[...] + pltpu.roll(part, LANES // 2, 1) * sin_ref[...]
            o_ref[hh] = part.astype(o_ref.dtype)
    else:
        o_ref[...] = acc.astype(o_ref.dtype)


def _mm(a, w, n_cols, *, col0=0, out_dtype=BF16, tm=1024, tn=512, res=None, heads=None, rope_tabs=None,
        name="mm"):
    m, k = a.shape
    tm = min(tm, m if heads is None else heads[1])
    assert n_cols % tn == 0 and col0 % tn == 0 and m % tm == 0
    cb = col0 // tn
    in_specs = [pl.BlockSpec((tm, k), lambda j, i: (i, 0)),
                pl.BlockSpec((k, tn), lambda j, i: (0, j + cb))]
    args = [a, w]
    if res is not None:
        in_specs.append(pl.BlockSpec((tm, tn), lambda j, i: (i, j)))
        args.append(res)
    if heads is not None:
        bsz, s = heads
        assert s % tm == 0
        spt = s // tm
        if rope_tabs is not None:
            in_specs += [pl.BlockSpec((tm, LANES), lambda j, i: (i % spt, 0))] * 2
            args += list(rope_tabs)
        hpt = tn // LANES
        out_spec = pl.BlockSpec((None, hpt, tm, LANES), lambda j, i: (i // spt, j, i % spt, 0))
        out_shape = jax.ShapeDtypeStruct((bsz, n_cols // LANES, s, LANES), out_dtype)
    else:
        out_spec = pl.BlockSpec((tm, tn), lambda j, i: (i, j))
        out_shape = jax.ShapeDtypeStruct((m, n_cols), out_dtype)
    kern = functools.partial(_mm_kernel, has_res=res is not None, head_out=heads is not None,
                             rope=rope_tabs is not None)
    return pl.pallas_call(
        kern,
        grid=(n_cols // tn, m // tm),
        in_specs=in_specs,
        out_specs=out_spec,
        out_shape=out_shape,
        scratch_shapes=[pltpu.VMEM((k, tn), BF16)],
        compiler_params=_params("arbitrary", "arbitrary"),
        name=name,
    )(*args)


def _rope_tables(s, d):
    inv = jnp.power(ROPE_THETA, -jnp.arange(0, d, 2, dtype=F32) / d)
    ang = jnp.arange(s, dtype=jnp.int32).astype(F32)[:, None] * inv[None, :]
    cos, sin = jnp.cos(ang), jnp.sin(ang)
    cos_t = jnp.tile(jnp.concatenate([cos, cos], axis=-1), (1, LANES // d))
    sin_t = jnp.tile(jnp.concatenate([-sin, sin], axis=-1), (1, LANES // d))
    return cos_t, sin_t


def _rot_half64(x):
    lane = lax.broadcasted_iota(jnp.int32, x.shape, 1)
    return jnp.where((lane % 64) < 32, pltpu.roll(x, 96, 1), pltpu.roll(x, 32, 1))


_C_CQ = 0
_C_CKV = Q_LORA
_C_IK = _C_CKV + KV_LORA
_C_KR = _C_IK + IDX_DIM
_C_FA = _C_KR + LANES
_C_IW = _C_FA + LANES
SMALL_W = _C_IW + LANES


def _prep_kernel(s_ref, gq_ref, gkv_ref, gik_ref, bik_ref, c64_ref, s64_ref, c128_ref, s128_ref,
                 cq_ref, kv_ref, kidx_ref, widx_ref, fa_ref):
    cq = s_ref[:, _C_CQ:_C_CQ + Q_LORA]
    cq_ref[...] = (cq * lax.rsqrt(jnp.mean(cq * cq, axis=-1, keepdims=True) + EPS) * gq_ref[...]).astype(BF16)
    ckv = s_ref[:, _C_CKV:_C_CKV + KV_LORA]
    kv_ref[:, :KV_LORA] = (ckv * lax.rsqrt(jnp.mean(ckv * ckv, axis=-1, keepdims=True) + EPS)
                           * gkv_ref[...]).astype(BF16)
    kr = s_ref[:, _C_KR:_C_KR + LANES]
    kv_ref[:, KV_LORA:] = (kr * c64_ref[...] + _rot_half64(kr) * s64_ref[...]).astype(BF16)
    ik = s_ref[:, _C_IK:_C_IK + IDX_DIM]
    mu = jnp.mean(ik, axis=-1, keepdims=True)
    xc = ik - mu
    ikn = xc * lax.rsqrt(jnp.mean(xc * xc, axis=-1, keepdims=True) + EPS) * gik_ref[...] + bik_ref[...]
    kidx_ref[...] = (ikn * c128_ref[...] + pltpu.roll(ikn, LANES // 2, 1) * s128_ref[...]).astype(BF16)
    widx_ref[...] = s_ref[:, _C_IW:_C_IW + LANES] * (IDX_HEADS * IDX_DIM) ** -0.5
    fa_ref[...] = s_ref[:, _C_FA:_C_FA + LANES]


def _prep(small, g_q, g_kv, g_ik, b_ik, tabs64, tabs128, s, tm=256):
    n = small.shape[0]
    spt = s // tm
    row = lambda w: pl.BlockSpec((tm, w), lambda i: (i, 0))
    vec = lambda w: pl.BlockSpec((1, w), lambda i: (0, 0))
    tab = pl.BlockSpec((tm, LANES), lambda i: (i % spt, 0))
    return pl.pallas_call(
        _prep_kernel,
        grid=(n // tm,),
        in_specs=[row(SMALL_W), vec(Q_LORA), vec(KV_LORA), vec(IDX_DIM), vec(IDX_DIM), tab, tab, tab, tab],
        out_specs=[row(Q_LORA), row(KV_PAD), row(IDX_DIM), row(LANES), row(LANES)],
        out_shape=[jax.ShapeDtypeStruct((n, Q_LORA), BF16), jax.ShapeDtypeStruct((n, KV_PAD), BF16),
                   jax.ShapeDtypeStruct((n, IDX_DIM), BF16), jax.ShapeDtypeStruct((n, LANES), F32),
                   jax.ShapeDtypeStruct((n, LANES), F32)],
        compiler_params=_params("arbitrary"),
        name="mixer_prep",
    )(small, g_q.reshape(1, -1), g_kv.reshape(1, -1), g_ik.reshape(1, -1), b_ik.reshape(1, -1),
      tabs64[0], tabs64[1], tabs128[0], tabs128[1])


def _cum_kernel(fa_ref, bf_ref, cum_ref, carry_ref):
    @pl.when(pl.program_id(1) == 0)
    def _():
        carry_ref[...] = jnp.zeros_like(carry_ref)

    z = fa_ref[...] + bf_ref[...]
    logf = jnp.minimum(z, 0.0) - jnp.log1p(jnp.exp(-jnp.abs(z)))
    t = z.shape[0]
    tri = (lax.broadcasted_iota(jnp.int32, (t, t), 0) >= lax.broadcasted_iota(jnp.int32, (t, t), 1)).astype(F32)
    c = jnp.dot(tri, logf, precision=lax.Precision.HIGHEST, preferred_element_type=F32) + carry_ref[...]
    cum_ref[...] = c
    carry_ref[...] = c[t - 1:t, :]


def _fox_cum(fa, b_f_pad, bsz, s, t=256):
    return pl.pallas_call(
        _cum_kernel,
        grid=(bsz, s // t),
        in_specs=[pl.BlockSpec((None, t, LANES), lambda b, i: (b, i, 0)),
                  pl.BlockSpec((1, LANES), lambda b, i: (0, 0))],
        out_specs=pl.BlockSpec((None, t, LANES), lambda b, i: (b, i, 0)),
        out_shape=jax.ShapeDtypeStruct((bsz, s, LANES), F32),
        scratch_shapes=[pltpu.VMEM((1, LANES), F32)],
        compiler_params=_params("arbitrary", "arbitrary"),
        name="fox_cumsum",
    )(fa.reshape(bsz, s, LANES), b_f_pad)


def _fox_kernel(q_ref, k_ref, v_ref, cum_ref, cumt_ref, o_ref, *, tq, tk):
    h = pl.program_id(1)
    i = pl.program_id(2)
    q = q_ref[...]
    lane = lax.broadcasted_iota(jnp.int32, (tq, LANES), 1)
    cq = jnp.sum(jnp.where(lane == h, cum_ref[...], 0.0), axis=1, keepdims=True)
    scale = FOX_HEAD_DIM ** -0.5
    qpos = i * tq + lax.broadcasted_iota(jnp.int32, (tq, tk), 0)
    kidx = lax.broadcasted_iota(jnp.int32, (tq, tk), 1)

    def body(kc, carry):
        m, l, acc = carry
        k0 = pl.multiple_of(kc * tk, tk)
        kb = k_ref[pl.ds(k0, tk), :]
        vb = v_ref[pl.ds(k0, tk), :]
        s = lax.dot_general(q, kb, (((1,), (1,)), ((), ())), preferred_element_type=F32) * scale
        s = s + (cq - cumt_ref[pl.ds(kc, 1), :])
        s = jnp.where(qpos >= k0 + kidx, s, NEG_BIG)
        m_new = jnp.maximum(m, jnp.max(s, axis=1, keepdims=True))
        p = jnp.exp(s - m_new)
        alpha = jnp.exp(m - m_new)
        l = alpha * l + jnp.sum(p, axis=1, keepdims=True)
        acc = alpha * acc + jnp.dot(p.astype(BF16), vb, preferred_element_type=F32)
        return m_new, l, acc

    nk = (i + 1) * (tq // tk)
    m0 = jnp.full((tq, 1), NEG_BIG, F32)
    l0 = jnp.zeros((tq, 1), F32)
    a0 = jnp.zeros((tq, FOX_HEAD_DIM), F32)
    _, l, acc = lax.fori_loop(0, nk, body, (m0, l0, a0))
    o_ref[...] = (acc / l).astype(o_ref.dtype)


def _fox_attention(qkv, cum, cum_t, bsz, s, tq=256, tk=256):
    n = bsz * s
    nq = s // tq
    hh = FOX_HEADS
    kern = functools.partial(_fox_kernel, tq=tq, tk=tk)
    return pl.pallas_call(
        kern,
        grid=(bsz, hh, nq),
        in_specs=[pl.BlockSpec((tq, LANES), lambda b, h, i: (b * nq + i, h)),
                  pl.BlockSpec((s, LANES), lambda b, h, i: (b, hh + h)),
                  pl.BlockSpec((s, LANES), lambda b, h, i: (b, 2 * hh + h)),
                  pl.BlockSpec((None, tq, LANES), lambda b, h, i: (b, i, 0)),
                  pl.BlockSpec((None, s // tk, tk), lambda b, h, i: (b * hh + h, 0, 0))],
        out_specs=pl.BlockSpec((tq, LANES), lambda b, h, i: (b * nq + i, h)),
        out_shape=jax.ShapeDtypeStruct((n, FOX_WIDTH), BF16),
        compiler_params=_params("arbitrary", "arbitrary", "arbitrary"),
        name="fox_attention",
    )(qkv, qkv, qkv, cum, cum_t)


def _qabs_kernel(qn_ref, qr_ref, wuk_ref, c64_ref, s64_ref, o_ref):
    lat = jnp.dot(qn_ref[...], wuk_ref[...].astype(BF16), preferred_element_type=F32)
    o_ref[:, :KV_LORA] = lat.astype(BF16)
    qr = qr_ref[...].astype(F32)
    o_ref[:, KV_LORA:] = (qr * c64_ref[...] + _rot_half64(qr) * s64_ref[...]).astype(BF16)


def _q_absorb(q_nope, q_rope, w_uk, tabs64, bsz, s, tm=512):
    spt = s // tm
    return pl.pallas_call(
        _qabs_kernel,
        grid=(bsz * spt, DSA_HEADS),
        in_specs=[pl.BlockSpec((tm, LANES), lambda i, h: (i, h)),
                  pl.BlockSpec((tm, LANES), lambda i, h: (i, h)),
                  pl.BlockSpec((None, DSA_NOPE_DIM, KV_LORA), lambda i, h: (h, 0, 0)),
                  pl.BlockSpec((tm, LANES), lambda i, h: (i % spt, 0)),
                  pl.BlockSpec((tm, LANES), lambda i, h: (i % spt, 0))],
        out_specs=pl.BlockSpec((None, None, tm, KV_PAD), lambda i, h: (i // spt, h, i % spt, 0)),
        out_shape=jax.ShapeDtypeStruct((bsz, DSA_HEADS, s, KV_PAD), BF16),
        compiler_params=_params("arbitrary", "arbitrary"),
        name="q_absorb",
    )(q_nope, q_rope, w_uk, tabs64[0], tabs64[1])


def _indexer_kernel(q_ref, kt_ref, w_ref, o_ref, score_ref, *, tq, s, topk, hg):
    i = pl.program_id(1)
    kt = kt_ref[...]
    for g in range(IDX_HEADS // hg):
        q2 = q_ref[g * hg:(g + 1) * hg].reshape(hg * tq, IDX_DIM)
        d = jnp.dot(q2, kt, preferred_element_type=F32)
        part = None
        for hh in range(hg):
            hidx = g * hg + hh
            term = w_ref[:, hidx:hidx + 1] * jnp.maximum(d[hh * tq:(hh + 1) * tq], 0.0)
            part = term if part is None else part + term
        if g == 0:
            score_ref[...] = part
        else:
            score_ref[...] += part
    qpos = i * tq + lax.broadcasted_iota(jnp.int32, (tq, s), 0)
    kpos = lax.broadcasted_iota(jnp.int32, (tq, s), 1)
    causal = kpos <= qpos
    sc = score_ref[...]
    sc = jnp.where(sc == 0.0, 0.0, sc)
    sc = jnp.where(causal, sc, -jnp.inf)
    bits = pltpu.bitcast(sc, jnp.int32)
    key = jnp.where(bits < 0, bits ^ jnp.int32(0x7FFFFFFF), bits)
    kf = float(topk)

    def count_ge(t):
        return jnp.sum(jnp.where(key >= t, 1.0, 0.0), axis=1, keepdims=True)

    int_min = jnp.int32(-2 ** 31)
    t0 = jnp.where(count_ge(jnp.zeros((tq, 1), jnp.int32)) >= kf, jnp.int32(0), int_min)

    def body(it, t):
        cand = t | lax.shift_left(jnp.int32(1), jnp.int32(30) - it)
        return jnp.where(count_ge(cand) >= kf, cand, t)

    thr = lax.fori_loop(0, 31, body, t0)
    keep = (key >= thr) & causal
    o_ref[...] = jnp.where(keep, 0.0, NEG_BIG).astype(o_ref.dtype)


def _indexer_mask(q_idx, k_idx_t, w_idx, bsz, s, topk, tq=128, hg=4):
    kern = functools.partial(_indexer_kernel, tq=tq, s=s, topk=topk, hg=hg)
    return pl.pallas_call(
        kern,
        grid=(bsz, s // tq),
        in_specs=[pl.BlockSpec((None, IDX_HEADS, tq, IDX_DIM), lambda b, i: (b, 0, i, 0)),
                  pl.BlockSpec((None, IDX_DIM, s), lambda b, i: (b, 0, 0)),
                  pl.BlockSpec((None, tq, LANES), lambda b, i: (b, i, 0))],
        out_specs=pl.BlockSpec((None, tq, s), lambda b, i: (b, i, 0)),
        out_shape=jax.ShapeDtypeStruct((bsz, s, s), BF16),
        scratch_shapes=[pltpu.VMEM((tq, s), F32)],
        compiler_params=_params("arbitrary", "arbitrary"),
        name="indexer_topk_mask",
    )(q_idx, k_idx_t, w_idx)


def _dsa_kernel(q_ref, kvt_ref, kv_ref, bias_ref, o_ref, *, tq, s):
    hh = DSA_HEADS
    q2 = q_ref[...].reshape(hh * tq, KV_PAD)
    scale = (DSA_NOPE_DIM + DSA_ROPE_DIM) ** -0.5
    sc = jnp.dot(q2, kvt_ref[...], preferred_element_type=F32) * scale
    sc = sc.reshape(hh, tq, s) + bias_ref[...].astype(F32)[None]
    m = jnp.max(sc, axis=-1, keepdims=True)
    p = jnp.exp(sc - m)
    l = jnp.sum(p, axis=-1, keepdims=True)
    pb = (p / l).astype(BF16).reshape(hh * tq, s)
    o = jnp.dot(pb, kv_ref[:, :KV_LORA], preferred_element_type=F32)
    o_ref[...] = o.reshape(hh, tq, KV_LORA).astype(o_ref.dtype)


def _dsa_attention(q_cat, kv_t, kv, bias, bsz, s, tq=32):
    kern = functools.partial(_dsa_kernel, tq=tq, s=s)
    return pl.pallas_call(
        kern,
        grid=(bsz, s // tq),
        in_specs=[pl.BlockSpec((None, DSA_HEADS, tq, KV_PAD), lambda b, i: (b, 0, i, 0)),
                  pl.BlockSpec((None, KV_PAD, s), lambda b, i: (b, 0, 0)),
                  pl.BlockSpec((None, s, KV_PAD), lambda b, i: (b, 0, 0)),
                  pl.BlockSpec((None, tq, s), lambda b, i: (b, i, 0))],
        out_specs=pl.BlockSpec((None, DSA_HEADS, tq, KV_LORA), lambda b, i: (b, 0, i, 0)),
        out_shape=jax.ShapeDtypeStruct((bsz, DSA_HEADS, s, KV_LORA), BF16),
        compiler_params=_params("arbitrary", "arbitrary"),
        name="dsa_attention",
    )(q_cat, kv_t, kv, bias)


def _uv_kernel(o_ref, w_ref, y_ref):
    y_ref[...] = jnp.dot(o_ref[...], w_ref[...].astype(BF16), preferred_element_type=F32).astype(y_ref.dtype)


def _uv_project(o_lat, w_uv, bsz, s, tm=512):
    spt = s // tm
    return pl.pallas_call(
        _uv_kernel,
        grid=(bsz * spt, DSA_HEADS),
        in_specs=[pl.BlockSpec((None, None, tm, KV_LORA), lambda i, h: (i // spt, h, i % spt, 0)),
                  pl.BlockSpec((None, KV_LORA, DSA_V_DIM), lambda i, h: (h, 0, 0))],
        out_specs=pl.BlockSpec((tm, DSA_V_DIM), lambda i, h: (i, h)),
        out_shape=jax.ShapeDtypeStruct((bsz * s, DSA_HEADS * DSA_V_DIM), BF16),
        compiler_params=_params("arbitrary", "arbitrary"),
        name="uv_project",
    )(o_lat, w_uv)


def _merge_kernel(h_ref, a_ref, b_ref, wga_ref, wgb_ref, wua_ref, wub_ref, o_ref,
                  cga_ref, cgb_ref, cua_ref, cub_ref):
    @pl.when(pl.program_id(1) == 0)
    def _():
        cga_ref[...] = wga_ref[...].astype(BF16)
        cgb_ref[...] = wgb_ref[...].astype(BF16)
        cua_ref[...] = wua_ref[...].astype(BF16)
        cub_ref[...] = wub_ref[...].astype(BF16)

    h = h_ref[...]
    ga = jnp.dot(h, cga_ref[...], preferred_element_type=F32)
    gb = jnp.dot(h, cgb_ref[...], preferred_element_type=F32)
    ya = jnp.dot(a_ref[...], cua_ref[...], preferred_element_type=F32)
    yb = jnp.dot(b_ref[...], cub_ref[...], preferred_element_type=F32)
    o_ref[...] = (jax.nn.sigmoid(ga) * ya + jax.nn.sigmoid(gb) * yb).astype(o_ref.dtype)


def _gated_merge(h1, att_a, att_b, w_ga, w_gb, w_up_a, w_up_b, tm=512, tn=256):
    n, d = h1.shape
    ka, kb = att_a.shape[1], att_b.shape[1]
    return pl.pallas_call(
        _merge_kernel,
        grid=(d // tn, n // tm),
        in_specs=[pl.BlockSpec((tm, d), lambda j, i: (i, 0)),
                  pl.BlockSpec((tm, ka), lambda j, i: (i, 0)),
                  pl.BlockSpec((tm, kb), lambda j, i: (i, 0)),
                  pl.BlockSpec((d, tn), lambda j, i: (0, j)),
                  pl.BlockSpec((d, tn), lambda j, i: (0, j)),
                  pl.BlockSpec((ka, tn), lambda j, i: (0, j)),
                  pl.BlockSpec((kb, tn), lambda j, i: (0, j))],
        out_specs=pl.BlockSpec((tm, tn), lambda j, i: (i, j)),
        out_shape=jax.ShapeDtypeStruct((n, d), BF16),
        scratch_shapes=[pltpu.VMEM((d, tn), BF16), pltpu.VMEM((d, tn), BF16),
                        pltpu.VMEM((ka, tn), BF16), pltpu.VMEM((kb, tn), BF16)],
        compiler_params=_params("arbitrary", "arbitrary"),
        name="gated_merge",
    )(h1, att_a, att_b, w_ga, w_gb, w_up_a, w_up_b)


def _mem_kernel(q_ref, kt_ref, v_ref, wom_ref, x_ref, o_ref):
    scale = MEM_HEAD_DIM ** -0.5
    outs = []
    for h in range(MEM_HEADS):
        sl = slice(h * MEM_HEAD_DIM, (h + 1) * MEM_HEAD_DIM)
        s = jnp.dot(q_ref[:, sl], kt_ref[sl, :], preferred_element_type=F32) * scale
        m = jnp.max(s, axis=-1, keepdims=True)
        p = jnp.exp(s - m)
        p = p / jnp.sum(p, axis=-1, keepdims=True)
        outs.append(jnp.dot(p.astype(BF16), v_ref[:, sl], preferred_element_type=F32).astype(BF16))
    o = jnp.concatenate(outs, axis=-1)
    o_ref[...] = x_ref[...] + jnp.dot(o, wom_ref[...], preferred_element_type=F32)


def _mem_attention(qm, km_t, kv_m, w_om_b, x1, bsz, s, n_mem, tm=512):
    spt = s // tm
    hd = MEM_HEADS * MEM_HEAD_DIM
    d = x1.shape[1]
    return pl.pallas_call(
        _mem_kernel,
        grid=(bsz, spt),
        in_specs=[pl.BlockSpec((tm, hd), lambda b, i: (b * spt + i, 0)),
                  pl.BlockSpec((None, hd, n_mem), lambda b, i: (b, 0, 0)),
                  pl.BlockSpec((n_mem, hd), lambda b, i: (b, 1)),
                  pl.BlockSpec((hd, d), lambda b, i: (0, 0)),
                  pl.BlockSpec((tm, d), lambda b, i: (b * spt + i, 0))],
        out_specs=pl.BlockSpec((tm, d), lambda b, i: (b * spt + i, 0)),
        out_shape=jax.ShapeDtypeStruct(x1.shape, F32),
        compiler_params=_params("arbitrary", "arbitrary"),
        name="memory_cross_attention",
    )(qm, km_t, kv_m, w_om_b, x1)


def _router_kernel(x_ref, g_ref, wr_ref, br_ref, h_ref, idx_ref, wt_ref, cnt_ref, carry_ref):
    @pl.when(pl.program_id(0) == 0)
    def _():
        carry_ref[...] = jnp.zeros_like(carry_ref)

    x = x_ref[...]
    hn = x * lax.rsqrt(jnp.mean(x * x, axis=-1, keepdims=True) + EPS) * g_ref[...]
    h_ref[...] = hn
    logits = jnp.dot(hn, wr_ref[...], precision=lax.Precision.HIGHEST, preferred_element_type=F32) + br_ref[...]
    tm = x.shape[0]
    lane_i = lax.broadcasted_iota(jnp.int32, (tm, LANES), 1)
    lane = lane_i.astype(F32)
    lane_grp = lax.shift_right_logical(lane_i, EXPERTS_PER_GROUP.bit_length() - 1).astype(F32)
    big = 1e9
    is_g = (lane_i >= N_EXPERTS) & (lane_i < N_EXPERTS + N_GROUPS)
    lg = jnp.where(is_g, logits, -jnp.inf)
    mg = jnp.max(lg, axis=1, keepdims=True)
    gate_g = 1.0 / jnp.sum(jnp.exp(lg - mg), axis=1, keepdims=True)
    grp = jnp.min(jnp.where(lg == mg, lane, big), axis=1, keepdims=True) - N_EXPERTS
    in_grp = (lane_i < N_EXPERTS) & (lane_grp == grp)
    le = jnp.where(in_grp, logits, -jnp.inf)
    m1 = jnp.max(le, axis=1, keepdims=True)
    e1 = jnp.min(jnp.where(le == m1, lane, big), axis=1, keepdims=True)
    le2 = jnp.where(lane == e1, -jnp.inf, le)
    m2 = jnp.max(le2, axis=1, keepdims=True)
    e2 = jnp.min(jnp.where(le2 == m2, lane, big), axis=1, keepdims=True)
    r = jnp.exp(m2 - m1)
    w1 = gate_g * (1.0 / (1.0 + r))
    w2 = gate_g * (r / (1.0 + r))
    oh1 = (lane == e1).astype(F32)
    oh2 = (lane == e2).astype(F32)
    oh = oh1 + oh2
    tri = (lax.broadcasted_iota(jnp.int32, (tm, tm), 0) > lax.broadcasted_iota(jnp.int32, (tm, tm), 1)).astype(BF16)
    prefix = jnp.dot(tri, oh.astype(BF16), preferred_element_type=F32) + carry_ref[...]
    rank1 = jnp.sum(prefix * oh1, axis=1, keepdims=True)
    rank2 = jnp.sum(prefix * oh2, axis=1, keepdims=True)
    new_cnt = carry_ref[...] + jnp.sum(oh, axis=0, keepdims=True)
    carry_ref[...] = new_cnt
    cnt_ref[...] = new_cnt
    idx_ref[...] = jnp.where(lane_i == 0, e1, jnp.where(lane_i == 1, e2, jnp.where(
        lane_i == 2, rank1, jnp.where(lane_i == 3, rank2, 0.0)))).astype(jnp.int32)
    wt_ref[...] = jnp.where(lane_i == 0, w1, jnp.where(lane_i == 1, w2, 0.0))


def _router(x2, g, w_r, b_r, tm=256):
    n, d = x2.shape
    row = lambda w: pl.BlockSpec((tm, w), lambda i: (i, 0))
    return pl.pallas_call(
        _router_kernel,
        grid=(n // tm,),
        in_specs=[row(d), pl.BlockSpec((1, d), lambda i: (0, 0)), pl.BlockSpec((d, LANES), lambda i: (0, 0)),
                  pl.BlockSpec((1, LANES), lambda i: (0, 0))],
        out_specs=[row(d), row(LANES), row(LANES), pl.BlockSpec((1, LANES), lambda i: (0, 0))],
        out_shape=[jax.ShapeDtypeStruct((n, d), F32), jax.ShapeDtypeStruct((n, LANES), jnp.int32),
                   jax.ShapeDtypeStruct((n, LANES), F32), jax.ShapeDtypeStruct((1, LANES), F32)],
        scratch_shapes=[pltpu.VMEM((1, LANES), F32)],
        compiler_params=_params("arbitrary"),
        name="moe_router",
    )(x2, g.reshape(1, d), w_r, b_r)


def _row_copy(src_hbm, row, dst, r, sem):
    return pltpu.make_async_copy(src_hbm.at[pl.ds(row, 1), :], dst.at[pl.ds(r, 1), :], sem)


def _expert_kernel(ie_ref, is_ref, in_ref, ni_ref, tok_ref,
                   h_hbm, wg_ref, wu_ref, wd_ref, y_ref, xf_ref, xb_ref, sem):
    i = pl.program_id(0)
    j = pl.program_id(1)
    valid = i < ni_ref[0]
    n = in_ref[i]
    start = is_ref[i]

    @pl.when((i == 0) & (j == 0))
    def _():
        xf_ref[...] = jnp.zeros_like(xf_ref)

    @pl.when(valid & (j == 0))
    def _():
        def issue(r, c):
            _row_copy(h_hbm, tok_ref[start + r], xf_ref, r, sem).start()
            return c

        lax.fori_loop(0, n, issue, 0)

        def wait(r, c):
            _row_copy(h_hbm, 0, xf_ref, 0, sem).wait()
            return c

        lax.fori_loop(0, n, wait, 0)
        xb_ref[...] = xf_ref[...].astype(BF16)
        y_ref[...] = jnp.zeros_like(y_ref)

    @pl.when(valid)
    def _():
        wg = wg_ref[...].astype(BF16)
        wu = wu_ref[...].astype(BF16)
        wd = wd_ref[...].astype(BF16)
        for sub in range(ITEM_ROWS // ITEM_SUB):
            @pl.when(sub * ITEM_SUB < n)
            def _():
                rows = slice(sub * ITEM_SUB, (sub + 1) * ITEM_SUB)
                xb = xb_ref[rows, :]
                g = jnp.dot(xb, wg, preferred_element_type=F32)
                u = jnp.dot(xb, wu, preferred_element_type=F32)
                hm = (g * jax.nn.sigmoid(g) * u).astype(BF16)
                y_ref[rows, :] += jnp.dot(hm, wd, preferred_element_type=F32)


def _experts(item_e, item_start, item_n, n_items, tok_sorted, h3, w_gate, w_up, w_down, max_items):
    d = h3.shape[1]
    nj = D_EXPERT // F_CHUNK

    def jj(i, j, ni):
        return jnp.where(i < ni[0], j, nj - 1)

    def blk(i, ni):
        return jnp.minimum(i, ni[0] - 1)

    grid_spec = pltpu.PrefetchScalarGridSpec(
        num_scalar_prefetch=5,
        grid=(max_items, nj),
        in_specs=[pl.BlockSpec(memory_space=pl.ANY),
                  pl.BlockSpec((None, d, F_CHUNK), lambda i, j, ie, s, n, ni, t: (ie[i], 0, jj(i, j, ni))),
                  pl.BlockSpec((None, d, F_CHUNK), lambda i, j, ie, s, n, ni, t: (ie[i], 0, jj(i, j, ni))),
                  pl.BlockSpec((None, F_CHUNK, d), lambda i, j, ie, s, n, ni, t: (ie[i], jj(i, j, ni), 0))],
        out_specs=pl.BlockSpec((ITEM_ROWS, d), lambda i, j, ie, s, n, ni, t: (blk(i, ni), 0)),
        scratch_shapes=[pltpu.VMEM((ITEM_ROWS, d), F32), pltpu.VMEM((ITEM_ROWS, d), BF16),
                        pltpu.SemaphoreType.DMA(())],
    )
    return pl.pallas_call(
        _expert_kernel,
        grid_spec=grid_spec,
        out_shape=jax.ShapeDtypeStruct((max_items * ITEM_ROWS, d), F32),
        compiler_params=_params("arbitrary", "arbitrary"),
        name="moe_experts",
    )(item_e, item_start, item_n, n_items, tok_sorted, h3, w_gate, w_up, w_down)


def _combine_kernel(dest_ref, y_hbm, x_ref, wt_ref, g_ref, o_ref, yv_ref, sem, *, tm):
    i = pl.program_id(0)

    def issue(r, c):
        base = 2 * (i * tm + r)
        _row_copy(y_hbm, dest_ref[base], yv_ref.at[0], r, sem).start()
        _row_copy(y_hbm, dest_ref[base + 1], yv_ref.at[1], r, sem).start()
        return c

    lax.fori_loop(0, tm, issue, 0)

    def wait(r, c):
        _row_copy(y_hbm, 0, yv_ref.at[0], 0, sem).wait()
        return c

    lax.fori_loop(0, 2 * tm, wait, 0)
    w = wt_ref[...]
    x = x_ref[...] + (yv_ref[0] * w[:, 0:1] + yv_ref[1] * w[:, 1:2])
    o_ref[...] = x * lax.rsqrt(jnp.mean(x * x, axis=-1, keepdims=True) + EPS) * g_ref[...]


def _combine(dest, y_buf, x2, wts, g_final, tm=128):
    n, d = x2.shape
    grid_spec = pltpu.PrefetchScalarGridSpec(
        num_scalar_prefetch=1,
        grid=(n // tm,),
        in_specs=[pl.BlockSpec(memory_space=pl.ANY),
                  pl.BlockSpec((tm, d), lambda i, dr: (i, 0)),
                  pl.BlockSpec((tm, LANES), lambda i, dr: (i, 0)),
                  pl.BlockSpec((1, d), lambda i, dr: (0, 0))],
        out_specs=pl.BlockSpec((tm, d), lambda i, dr: (i, 0)),
        scratch_shapes=[pltpu.VMEM((2, tm, d), F32), pltpu.SemaphoreType.DMA(())],
    )
    return pl.pallas_call(
        functools.partial(_combine_kernel, tm=tm),
        grid_spec=grid_spec,
        out_shape=jax.ShapeDtypeStruct((n, d), F32),
        compiler_params=_params("arbitrary"),
        name="moe_combine_final_norm",
    )(dest, y_buf, x2, wts, g_final.reshape(1, d))


def _pad_cols(w, width):
    return jnp.pad(w, ((0, 0), (0, width - w.shape[1])))


def _mixer(x2d, bsz, s, g_norm, w_in, b_f, g_q_lat, g_kv_lat, g_idx_k, b_idx_k, w_uq, w_idx_q, w_uk, w_uv,
           w_up_a, w_up_b, w_out):
    d = x2d.shape[1]
    topk = min(DSA_TOPK_MAX, s // 4)
    h1 = _rmsnorm(x2d, g_norm, BF16)
    c_small = 3 * FOX_WIDTH
    o_fa, o_cq = c_small, c_small + FOX_HEADS
    o_ckv = o_cq + Q_LORA
    o_kr = o_ckv + KV_LORA
    o_ik = o_kr + DSA_ROPE_DIM
    o_iw = o_ik + IDX_DIM
    o_ga = o_iw + IDX_HEADS
    o_gb = o_ga + d
    w_small = jnp.concatenate([
        w_in[:, o_cq:o_ckv], w_in[:, o_ckv:o_kr], w_in[:, o_ik:o_iw],
        _pad_cols(w_in[:, o_kr:o_ik], LANES), _pad_cols(w_in[:, o_fa:o_cq], LANES),
        _pad_cols(w_in[:, o_iw:o_ga], LANES)], axis=1)
    w_ga = w_in[:, o_ga:o_gb]
    w_gb = w_in[:, o_gb:o_gb + d]

    qkv = _mm(h1, w_in, 3 * FOX_WIDTH, name="proj_qkv")
    small = _mm(h1, w_small, SMALL_W, out_dtype=F32, tn=SMALL_W // 7, name="proj_small")

    tabs64 = _rope_tables(s, DSA_ROPE_DIM)
    tabs128 = _rope_tables(s, IDX_DIM)
    cq_n, kv_cat, k_idx, w_idx, fa = _prep(small, g_q_lat, g_kv_lat, g_idx_k, b_idx_k, tabs64, tabs128, s)

    cum = _fox_cum(fa, _pad_cols(b_f.reshape(1, -1), LANES), bsz, s)
    tk = 256
    cum_t = jnp.transpose(cum[:, :, :FOX_HEADS], (0, 2, 1)).reshape(bsz * FOX_HEADS, s // tk, tk)
    att_a = _fox_attention(qkv, cum, cum_t, bsz, s, tk=tk)

    w_uq3 = w_uq.reshape(Q_LORA, DSA_HEADS, DSA_NOPE_DIM + DSA_ROPE_DIM)
    w_uq_nope = w_uq3[:, :, :DSA_NOPE_DIM].reshape(Q_LORA, DSA_HEADS * DSA_NOPE_DIM)
    w_uq_rope = jnp.pad(w_uq3[:, :, DSA_NOPE_DIM:], ((0, 0), (0, 0), (0, LANES - DSA_ROPE_DIM))).reshape(
        Q_LORA, DSA_HEADS * LANES)
    q_nope = _mm(cq_n, w_uq_nope, DSA_HEADS * DSA_NOPE_DIM, name="proj_q_nope")
    q_rope = _mm(cq_n, w_uq_rope, DSA_HEADS * LANES, name="proj_q_rope")
    q_cat = _q_absorb(q_nope, q_rope, w_uk, tabs64, bsz, s)
    q_idx = _mm(cq_n, w_idx_q, IDX_HEADS * IDX_DIM, heads=(bsz, s), rope_tabs=tabs128, name="proj_q_idx")
    k_idx_t = jnp.transpose(k_idx.reshape(bsz, s, IDX_DIM), (0, 2, 1))
    bias = _indexer_mask(q_idx, k_idx_t, w_idx.reshape(bsz, s, LANES), bsz, s, topk)
    kv3 = kv_cat.reshape(bsz, s, KV_PAD)
    o_lat = _dsa_attention(q_cat, jnp.transpose(kv3, (0, 2, 1)), kv3, bias, bsz, s)
    att_b = _uv_project(o_lat, w_uv, bsz, s)

    merged = _gated_merge(h1, att_a, att_b, w_ga, w_gb, w_up_a, w_up_b)
    return _mm(merged, w_out, d, out_dtype=F32, res=x2d, name="proj_out")


def _memory_layer(x1, mem, bsz, s, g_x, g_m, w_qm, w_km, w_vm, w_om):
    n_mem = mem.shape[1]
    d = x1.shape[1]
    hd = MEM_HEADS * MEM_HEAD_DIM
    h2 = _rmsnorm(x1, g_x, BF16)
    mem_n = _rmsnorm(mem.reshape(bsz * n_mem, d), g_m, BF16)
    qm = _mm(h2, w_qm, hd, name="proj_q_mem")
    kv_m = _mm(mem_n, jnp.concatenate([w_km, w_vm], axis=1), 2 * hd, tm=bsz * n_mem, name="proj_kv_mem")
    km_t = jnp.transpose(kv_m[:, :hd].reshape(bsz, n_mem, hd), (0, 2, 1))
    return _mem_attention(qm, km_t, kv_m, w_om.astype(BF16), x1, bsz, s, n_mem)


def _moe_layer(x2, g_ffn, w_rg, b_rg, w_re, b_re, w_gate, w_up, w_down, g_final):
    n, d = x2.shape
    m = n * TOPK_IN_GROUP
    w_r = _pad_cols(jnp.concatenate([w_re, w_rg], axis=1), LANES)
    b_r = _pad_cols(jnp.concatenate([b_re, b_rg]).reshape(1, -1), LANES)
    h3, ridx, wts, cnt = _router(x2, g_ffn, w_r, b_r)
    counts = cnt[0, :N_EXPERTS].astype(jnp.int32)
    e_flat = ridx[:, :TOPK_IN_GROUP].reshape(m)
    rank_flat = ridx[:, TOPK_IN_GROUP:2 * TOPK_IN_GROUP].reshape(m)
    starts = jnp.cumsum(counts) - counts
    n_it = (counts + ITEM_ROWS - 1) // ITEM_ROWS
    it_end = jnp.cumsum(n_it)
    it_first = it_end - n_it
    max_items = m // ITEM_ROWS + N_EXPERTS
    t = jnp.arange(max_items, dtype=jnp.int32)
    item_e = jnp.minimum(jnp.searchsorted(it_end, t, side='right'), N_EXPERTS - 1).astype(jnp.int32)
    k_in = t - it_first[item_e]
    item_start = (starts[item_e] + k_in * ITEM_ROWS).astype(jnp.int32)
    item_n = jnp.clip(counts[item_e] - k_in * ITEM_ROWS, 0, ITEM_ROWS).astype(jnp.int32)
    n_items = it_end[-1:].astype(jnp.int32)
    last = jnp.maximum(n_items[0] - 1, 0)
    item_e = jnp.where(t < n_items[0], item_e, item_e[last])
    item_start = jnp.where(t < n_items[0], item_start, 0)
    item_n = jnp.where(t < n_items[0], item_n, 0)
    tok_flat = jnp.arange(m, dtype=jnp.int32) // TOPK_IN_GROUP
    tok_sorted = jnp.zeros((m,), jnp.int32).at[starts[e_flat] + rank_flat].set(tok_flat)
    dest = (it_first[e_flat] * ITEM_ROWS + rank_flat).astype(jnp.int32)
    y_buf = _experts(item_e, item_start, item_n, n_items, tok_sorted, h3, w_gate, w_up, w_down, max_items)
    return _combine(dest, y_buf, x2, wts, g_final)


def kernel(x, mem, g_norm_mix, w_in, b_f, g_q_lat, g_kv_lat, g_idx_k, b_idx_k, w_uq, w_idx_q, w_uk, w_uv,
           w_up_a, w_up_b, w_out, g_norm_mem_x, g_mem, w_qm, w_km, w_vm, w_om, g_norm_ffn,
           w_rg, b_rg, w_re, b_re, w_gate, w_up, w_down, g_final):
    bsz, s, d = x.shape
    depth = w_in.shape[0]
    assert depth == 1, "the final norm is fused into the last MoE combine"
    l = 0
    x2d = x.reshape(bsz * s, d)
    x1 = _mixer(x2d, bsz, s, g_norm_mix[l], w_in[l], b_f[l], g_q_lat[l], g_kv_lat[l], g_idx_k[l], b_idx_k[l],
                w_uq[l], w_idx_q[l], w_uk[l], w_uv[l], w_up_a[l], w_up_b[l], w_out[l])
    x2 = _memory_layer(x1, mem, bsz, s, g_norm_mem_x[l], g_mem[l], w_qm[l], w_km[l], w_vm[l], w_om[l])
    out = _moe_layer(x2, g_norm_ffn[l], w_rg[l], b_rg[l], w_re[l], b_re[l], w_gate[l], w_up[l], w_down[l], g_final)
    return out.reshape(bsz, s, d)
```

```python
import functools

import jax
import jax.numpy as jnp
import numpy as np
from jax import lax
from jax.experimental import pallas as pl
from jax.experimental.pallas import tpu as pltpu

F32 = jnp.float32
BF16 = jnp.bfloat16

EPS = 1e-6
ROPE_THETA = 10000.0
FOX_HEADS = 16
FOX_HEAD_DIM = 128
DSA_HEADS = 16
DSA_NOPE_DIM = 128
DSA_ROPE_DIM = 64
DSA_V_DIM = 128
Q_LORA = 1024
KV_LORA = 256
IDX_HEADS = 32
IDX_DIM = 128
DSA_TOPK_MAX = 256
MEM_HEADS = 4
MEM_HEAD_DIM = 128
N_GROUPS = 8
EXPERTS_PER_GROUP = 8
N_EXPERTS = N_GROUPS * EXPERTS_PER_GROUP
TOPK_IN_GROUP = 2
D_EXPERT = 512

FOX_WIDTH = FOX_HEADS * FOX_HEAD_DIM
LANES = 128
SUBLANES = 8
KV_PAD = KV_LORA + LANES
NEG_BIG = -1e30
VMEM_LIMIT_BYTES = 56 * 1024 * 1024
ITEM_ROWS = 512
ITEM_SUB = 128
F_CHUNK = 128


def _params(*sem):
    return pltpu.CompilerParams(dimension_semantics=sem, vmem_limit_bytes=VMEM_LIMIT_BYTES)


def _rmsnorm_kernel(x_ref, g_ref, o_ref):
    x = x_ref[...].astype(F32)
    ms = jnp.mean(x * x, axis=-1, keepdims=True)
    o_ref[...] = (x * lax.rsqrt(ms + EPS) * g_ref[...]).astype(o_ref.dtype)


def _rmsnorm(x2d, g, out_dtype, tm=256):
    n, d = x2d.shape
    return pl.pallas_call(
        _rmsnorm_kernel,
        grid=(n // tm,),
        in_specs=[pl.BlockSpec((tm, d), lambda i: (i, 0)), pl.BlockSpec((1, d), lambda i: (0, 0))],
        out_specs=pl.BlockSpec((tm, d), lambda i: (i, 0)),
        out_shape=jax.ShapeDtypeStruct((n, d), out_dtype),
        compiler_params=_params("arbitrary"),
        name="rmsnorm",
    )(x2d, g.reshape(1, d).astype(F32))


_NT_DIMS = (((1,), (1,)), ((), ()))


def _w_rows_spec(tn, k, row0, index_of_j):
    if row0 % tn == 0:
        return pl.BlockSpec((tn, k), lambda *g: (index_of_j(*g) + row0 // tn, 0))
    assert row0 % SUBLANES == 0
    return pl.BlockSpec((pl.Element(tn), pl.Element(k)),
                        lambda *g: (pl.multiple_of(index_of_j(*g) * tn + row0, SUBLANES), 0))


def _mm_kernel(a_ref, b_ref, *rest, has_res, head_out, rope, w_rows):
    pos = 0
    res_ref = cos_ref = sin_ref = None
    if has_res:
        res_ref = rest[pos]
        pos += 1
    if rope:
        cos_ref, sin_ref = rest[pos], rest[pos + 1]
        pos += 2
    o_ref = rest[pos]
    wb_ref = rest[pos + 1]

    @pl.when(pl.program_id(1) == 0)
    def _():
        wb_ref[...] = b_ref[...].astype(BF16)

    if w_rows:
        acc = lax.dot_general(a_ref[...], wb_ref[...], _NT_DIMS, preferred_element_type=F32)
    else:
        acc = jnp.dot(a_ref[...], wb_ref[...], preferred_element_type=F32)
    if has_res:
        acc = acc + res_ref[...]
    if head_out:
        for hh in range(acc.shape[1] // LANES):
            part = acc[:, hh * LANES:(hh + 1) * LANES]
            if rope:
                part = part * cos_ref[...] + pltpu.roll(part, LANES // 2, 1) * sin_ref[...]
            o_ref[hh] = part.astype(o_ref.dtype)
    else:
        o_ref[...] = acc.astype(o_ref.dtype)


def _mm(a, w, n_cols, *, col0=0, out_dtype=BF16, tm=1024, tn=512, res=None, heads=None, rope_tabs=None,
        w_rows=False, name="mm"):
    m, k = a.shape
    tm = min(tm, m if heads is None else heads[1])
    assert n_cols % tn == 0 and m % tm == 0
    if w_rows:
        w_spec = _w_rows_spec(tn, k, col0, lambda j, i: j)
    else:
        assert col0 % tn == 0
        w_spec = pl.BlockSpec((k, tn), lambda j, i: (0, j + col0 // tn))
    in_specs = [pl.BlockSpec((tm, k), lambda j, i: (i, 0)), w_spec]
    args = [a, w]
    if res is not None:
        in_specs.append(pl.BlockSpec((tm, tn), lambda j, i: (i, j)))
        args.append(res)
    if heads is not None:
        bsz, s = heads
        assert s % tm == 0
        spt = s // tm
        if rope_tabs is not None:
            in_specs += [pl.BlockSpec((tm, LANES), lambda j, i: (i % spt, 0))] * 2
            args += list(rope_tabs)
        hpt = tn // LANES
        out_spec = pl.BlockSpec((None, hpt, tm, LANES), lambda j, i: (i // spt, j, i % spt, 0))
        out_shape = jax.ShapeDtypeStruct((bsz, n_cols // LANES, s, LANES), out_dtype)
    else:
        out_spec = pl.BlockSpec((tm, tn), lambda j, i: (i, j))
        out_shape = jax.ShapeDtypeStruct((m, n_cols), out_dtype)
    kern = functools.partial(_mm_kernel, has_res=res is not None, head_out=heads is not None,
                             rope=rope_tabs is not None, w_rows=w_rows)
    return pl.pallas_call(
        kern,
        grid=(n_cols // tn, m // tm),
        in_specs=in_specs,
        out_specs=out_spec,
        out_shape=out_shape,
        scratch_shapes=[pltpu.VMEM((tn, k) if w_rows else (k, tn), BF16)],
        compiler_params=_params("arbitrary", "arbitrary"),
        name=name,
    )(*args)


def _rope_tables(s, d):
    inv = jnp.power(ROPE_THETA, -jnp.arange(0, d, 2, dtype=F32) / d)
    ang = jnp.arange(s, dtype=jnp.int32).astype(F32)[:, None] * inv[None, :]
    cos, sin = jnp.cos(ang), jnp.sin(ang)
    cos_t = jnp.tile(jnp.concatenate([cos, cos], axis=-1), (1, LANES // d))
    sin_t = jnp.tile(jnp.concatenate([-sin, sin], axis=-1), (1, LANES // d))
    return cos_t, sin_t


def _rot_half64(x):
    lane = lax.broadcasted_iota(jnp.int32, x.shape, 1)
    return jnp.where((lane % 64) < 32, pltpu.roll(x, 96, 1), pltpu.roll(x, 32, 1))


_C_CQ = 0
_C_CKV = Q_LORA
_C_IK = _C_CKV + KV_LORA
_C_KR = _C_IK + IDX_DIM
_C_FA = _C_KR + LANES
_C_IW = _C_FA + LANES
SMALL_W = _C_IW + LANES


def _prep_kernel(s_ref, gq_ref, gkv_ref, gik_ref, bik_ref, c64_ref, s64_ref, c128_ref, s128_ref,
                 cq_ref, kv_ref, kidx_ref, widx_ref, fa_ref):
    cq = s_ref[:, _C_CQ:_C_CQ + Q_LORA]
    cq_ref[...] = (cq * lax.rsqrt(jnp.mean(cq * cq, axis=-1, keepdims=True) + EPS) * gq_ref[...]).astype(BF16)
    ckv = s_ref[:, _C_CKV:_C_CKV + KV_LORA]
    kv_ref[:, :KV_LORA] = (ckv * lax.rsqrt(jnp.mean(ckv * ckv, axis=-1, keepdims=True) + EPS)
                           * gkv_ref[...]).astype(BF16)
    kr = s_ref[:, _C_KR:_C_KR + LANES]
    kv_ref[:, KV_LORA:] = (kr * c64_ref[...] + _rot_half64(kr) * s64_ref[...]).astype(BF16)
    ik = s_ref[:, _C_IK:_C_IK + IDX_DIM]
    mu = jnp.mean(ik, axis=-1, keepdims=True)
    xc = ik - mu
    ikn = xc * lax.rsqrt(jnp.mean(xc * xc, axis=-1, keepdims=True) + EPS) * gik_ref[...] + bik_ref[...]
    kidx_ref[...] = (ikn * c128_ref[...] + pltpu.roll(ikn, LANES // 2, 1) * s128_ref[...]).astype(BF16)
    widx_ref[...] = s_ref[:, _C_IW:_C_IW + LANES] * (IDX_HEADS * IDX_DIM) ** -0.5
    fa_ref[...] = s_ref[:, _C_FA:_C_FA + LANES]


def _prep(small, g_q, g_kv, g_ik, b_ik, tabs64, tabs128, s, tm=256):
    n = small.shape[0]
    spt = s // tm
    row = lambda w: pl.BlockSpec((tm, w), lambda i: (i, 0))
    vec = lambda w: pl.BlockSpec((1, w), lambda i: (0, 0))
    tab = pl.BlockSpec((tm, LANES), lambda i: (i % spt, 0))
    return pl.pallas_call(
        _prep_kernel,
        grid=(n // tm,),
        in_specs=[row(SMALL_W), vec(Q_LORA), vec(KV_LORA), vec(IDX_DIM), vec(IDX_DIM), tab, tab, tab, tab],
        out_specs=[row(Q_LORA), row(KV_PAD), row(IDX_DIM), row(LANES), row(LANES)],
        out_shape=[jax.ShapeDtypeStruct((n, Q_LORA), BF16), jax.ShapeDtypeStruct((n, KV_PAD), BF16),
                   jax.ShapeDtypeStruct((n, IDX_DIM), BF16), jax.ShapeDtypeStruct((n, LANES), F32),
                   jax.ShapeDtypeStruct((n, LANES), F32)],
        compiler_params=_params("arbitrary"),
        name="mixer_prep",
    )(small, g_q.reshape(1, -1), g_kv.reshape(1, -1), g_ik.reshape(1, -1), b_ik.reshape(1, -1),
      tabs64[0], tabs64[1], tabs128[0], tabs128[1])


def _cum_kernel(fa_ref, bf_ref, cum_ref, carry_ref):
    @pl.when(pl.program_id(1) == 0)
    def _():
        carry_ref[...] = jnp.zeros_like(carry_ref)

    z = fa_ref[...] + bf_ref[...]
    logf = jnp.minimum(z, 0.0) - jnp.log1p(jnp.exp(-jnp.abs(z)))
    t = z.shape[0]
    tri = (lax.broadcasted_iota(jnp.int32, (t, t), 0) >= lax.broadcasted_iota(jnp.int32, (t, t), 1)).astype(F32)
    c = jnp.dot(tri, logf, precision=lax.Precision.HIGHEST, preferred_element_type=F32) + carry_ref[...]
    cum_ref[...] = c
    carry_ref[...] = c[t - 1:t, :]


def _fox_cum(fa, b_f_pad, bsz, s, t=256):
    return pl.pallas_call(
        _cum_kernel,
        grid=(bsz, s // t),
        in_specs=[pl.BlockSpec((None, t, LANES), lambda b, i: (b, i, 0)),
                  pl.BlockSpec((1, LANES), lambda b, i: (0, 0))],
        out_specs=pl.BlockSpec((None, t, LANES), lambda b, i: (b, i, 0)),
        out_shape=jax.ShapeDtypeStruct((bsz, s, LANES), F32),
        scratch_shapes=[pltpu.VMEM((1, LANES), F32)],
        compiler_params=_params("arbitrary", "arbitrary"),
        name="fox_cumsum",
    )(fa.reshape(bsz, s, LANES), b_f_pad)


def _lane_fold(x, op):
    out = x[:, :LANES]
    for c in range(1, x.shape[1] // LANES):
        out = op(out, x[:, c * LANES:(c + 1) * LANES])
    return out


def _fox_kernel(q_ref, k_ref, v_ref, cum_ref, cumt_ref, o_ref, s_ref, *, t):
    h = pl.program_id(1)
    i = pl.program_id(2)
    q = q_ref[...]
    lane = lax.broadcasted_iota(jnp.int32, (t, LANES), 1)
    cq = jnp.sum(jnp.where(lane == h, cum_ref[...], 0.0), axis=1, keepdims=True)
    scale = FOX_HEAD_DIM ** -0.5

    def logits(kc):
        kb = k_ref[pl.ds(pl.multiple_of(kc * t, t), t), :]
        s = lax.dot_general(q, kb, (((1,), (1,)), ((), ())), preferred_element_type=F32) * scale
        return s + cq - cumt_ref[pl.ds(kc, 1), :]

    def pass1(kc, m_lane):
        s = logits(kc)
        s_ref[kc] = s
        return jnp.maximum(m_lane, _lane_fold(s, jnp.maximum))

    m_lane = lax.fori_loop(0, i, pass1, jnp.full((t, LANES), NEG_BIG, F32))
    causal = lax.broadcasted_iota(jnp.int32, (t, t), 0) >= lax.broadcasted_iota(jnp.int32, (t, t), 1)
    s = jnp.where(causal, logits(i), NEG_BIG)
    s_ref[i] = s
    m_lane = jnp.maximum(m_lane, _lane_fold(s, jnp.maximum))
    m = jnp.max(m_lane, axis=1, keepdims=True)

    def pass2(kc, carry):
        l_lane, acc = carry
        p = jnp.exp(s_ref[kc] - m)
        vb = v_ref[pl.ds(pl.multiple_of(kc * t, t), t), :]
        return (l_lane + _lane_fold(p, jnp.add),
                acc + jnp.dot(p.astype(BF16), vb, preferred_element_type=F32))

    l_lane, acc = lax.fori_loop(0, i + 1, pass2, (jnp.zeros((t, LANES), F32),
                                                  jnp.zeros((t, FOX_HEAD_DIM), F32)))
    o_ref[...] = (acc / jnp.sum(l_lane, axis=1, keepdims=True)).astype(o_ref.dtype)


def _fox_attention(qkv, cum, cum_t, bsz, s, t=512):
    tq = tk = t
    n = bsz * s
    nq = s // tq
    hh = FOX_HEADS
    kern = functools.partial(_fox_kernel, t=t)
    return pl.pallas_call(
        kern,
        grid=(bsz, hh, nq),
        in_specs=[pl.BlockSpec((tq, LANES), lambda b, h, i: (b * nq + i, h)),
                  pl.BlockSpec((s, LANES), lambda b, h, i: (b, hh + h)),
                  pl.BlockSpec((s, LANES), lambda b, h, i: (b, 2 * hh + h)),
                  pl.BlockSpec((None, tq, LANES), lambda b, h, i: (b, i, 0)),
                  pl.BlockSpec((None, s // tk, tk), lambda b, h, i: (b * hh + h, 0, 0))],
        out_specs=pl.BlockSpec((tq, LANES), lambda b, h, i: (b * nq + i, h)),
        out_shape=jax.ShapeDtypeStruct((n, FOX_WIDTH), BF16),
        scratch_shapes=[pltpu.VMEM((s // tk, tq, tk), F32)],
        compiler_params=_params("arbitrary", "arbitrary", "arbitrary"),
        name="fox_attention",
    )(qkv, qkv, qkv, cum, cum_t)


def _qabs_kernel(qn_ref, qr_ref, wuk_ref, c64_ref, s64_ref, o_ref):
    lat = jnp.dot(qn_ref[...], wuk_ref[...].astype(BF16), preferred_element_type=F32)
    o_ref[:, :KV_LORA] = lat.astype(BF16)
    qr = qr_ref[...].astype(F32)
    o_ref[:, KV_LORA:] = (qr * c64_ref[...] + _rot_half64(qr) * s64_ref[...]).astype(BF16)


def _q_absorb(q_nope, q_rope, w_uk, tabs64, bsz, s, tm=512):
    spt = s // tm
    return pl.pallas_call(
        _qabs_kernel,
        grid=(bsz * spt, DSA_HEADS),
        in_specs=[pl.BlockSpec((tm, LANES), lambda i, h: (i, h)),
                  pl.BlockSpec((tm, LANES), lambda i, h: (i, h)),
                  pl.BlockSpec((None, DSA_NOPE_DIM, KV_LORA), lambda i, h: (h, 0, 0)),
                  pl.BlockSpec((tm, LANES), lambda i, h: (i % spt, 0)),
                  pl.BlockSpec((tm, LANES), lambda i, h: (i % spt, 0))],
        out_specs=pl.BlockSpec((None, None, tm, KV_PAD), lambda i, h: (i // spt, h, i % spt, 0)),
        out_shape=jax.ShapeDtypeStruct((bsz, DSA_HEADS, s, KV_PAD), BF16),
        compiler_params=_params("arbitrary", "arbitrary"),
        name="q_absorb",
    )(q_nope, q_rope, w_uk, tabs64[0], tabs64[1])


def _indexer_kernel(q_ref, kt_ref, w_ref, o_ref, score_ref, *, tq, s, topk, hg):
    i = pl.program_id(1)
    kt = kt_ref[...]
    for g in range(IDX_HEADS // hg):
        q2 = q_ref[g * hg:(g + 1) * hg].reshape(hg * tq, IDX_DIM)
        d = jnp.dot(q2, kt, preferred_element_type=F32)
        part = None
        for hh in range(hg):
            hidx = g * hg + hh
            term = w_ref[:, hidx:hidx + 1] * jnp.maximum(d[hh * tq:(hh + 1) * tq], 0.0)
            part = term if part is None else part + term
        if g == 0:
            score_ref[...] = part
        else:
            score_ref[...] += part
    qpos = i * tq + lax.broadcasted_iota(jnp.int32, (tq, s), 0)
    kpos = lax.broadcasted_iota(jnp.int32, (tq, s), 1)
    causal = kpos <= qpos
    sc = score_ref[...]
    sc = jnp.where(sc == 0.0, 0.0, sc)
    sc = jnp.where(causal, sc, -jnp.inf)
    bits = pltpu.bitcast(sc, jnp.int32)
    key = jnp.where(bits < 0, bits ^ jnp.int32(0x7FFFFFFF), bits)
    kf = float(topk)

    def count_ge(t):
        return jnp.sum(jnp.where(key >= t, 1.0, 0.0), axis=1, keepdims=True)

    int_min = jnp.int32(-2 ** 31)
    t0 = jnp.where(count_ge(jnp.zeros((tq, 1), jnp.int32)) >= kf, jnp.int32(0), int_min)

    def body(it, t):
        cand = t | lax.shift_left(jnp.int32(1), jnp.int32(30) - it)
        return jnp.where(count_ge(cand) >= kf, cand, t)

    thr = lax.fori_loop(0, 31, body, t0)
    keep = (key >= thr) & causal
    o_ref[...] = jnp.where(keep, 0.0, NEG_BIG).astype(o_ref.dtype)


def _indexer_mask(q_idx, k_idx_t, w_idx, bsz, s, topk, tq=128, hg=4):
    kern = functools.partial(_indexer_kernel, tq=tq, s=s, topk=topk, hg=hg)
    return pl.pallas_call(
        kern,
        grid=(bsz, s // tq),
        in_specs=[pl.BlockSpec((None, IDX_HEADS, tq, IDX_DIM), lambda b, i: (b, 0, i, 0)),
                  pl.BlockSpec((None, IDX_DIM, s), lambda b, i: (b, 0, 0)),
                  pl.BlockSpec((None, tq, LANES), lambda b, i: (b, i, 0))],
        out_specs=pl.BlockSpec((None, tq, s), lambda b, i: (b, i, 0)),
        out_shape=jax.ShapeDtypeStruct((bsz, s, s), BF16),
        scratch_shapes=[pltpu.VMEM((tq, s), F32)],
        compiler_params=_params("arbitrary", "arbitrary"),
        name="indexer_topk_mask",
    )(q_idx, k_idx_t, w_idx)


def _dsa_kernel(q_ref, kvt_ref, kv_ref, bias_ref, o_ref, *, tq, s):
    hh = DSA_HEADS
    q2 = q_ref[...].reshape(hh * tq, KV_PAD)
    scale = (DSA_NOPE_DIM + DSA_ROPE_DIM) ** -0.5
    sc = jnp.dot(q2, kvt_ref[...], preferred_element_type=F32) * scale
    sc = sc.reshape(hh, tq, s) + bias_ref[...].astype(F32)[None]
    m = jnp.max(sc, axis=-1, keepdims=True)
    p = jnp.exp(sc - m)
    l = jnp.sum(p, axis=-1, keepdims=True)
    pb = (p / l).astype(BF16).reshape(hh * tq, s)
    o = jnp.dot(pb, kv_ref[:, :KV_LORA], preferred_element_type=F32)
    o_ref[...] = o.reshape(hh, tq, KV_LORA).astype(o_ref.dtype)


def _dsa_attention(q_cat, kv_t, kv, bias, bsz, s, tq=32):
    kern = functools.partial(_dsa_kernel, tq=tq, s=s)
    return pl.pallas_call(
        kern,
        grid=(bsz, s // tq),
        in_specs=[pl.BlockSpec((None, DSA_HEADS, tq, KV_PAD), lambda b, i: (b, 0, i, 0)),
                  pl.BlockSpec((None, KV_PAD, s), lambda b, i: (b, 0, 0)),
                  pl.BlockSpec((None, s, KV_PAD), lambda b, i: (b, 0, 0)),
                  pl.BlockSpec((None, tq, s), lambda b, i: (b, i, 0))],
        out_specs=pl.BlockSpec((None, DSA_HEADS, tq, KV_LORA), lambda b, i: (b, 0, i, 0)),
        out_shape=jax.ShapeDtypeStruct((bsz, DSA_HEADS, s, KV_LORA), BF16),
        compiler_params=_params("arbitrary", "arbitrary"),
        name="dsa_attention",
    )(q_cat, kv_t, kv, bias)


def _uv_kernel(o_ref, w_ref, y_ref):
    y_ref[...] = jnp.dot(o_ref[...], w_ref[...].astype(BF16), preferred_element_type=F32).astype(y_ref.dtype)


def _uv_project(o_lat, w_uv, bsz, s, tm=512):
    spt = s // tm
    return pl.pallas_call(
        _uv_kernel,
        grid=(bsz * spt, DSA_HEADS),
        in_specs=[pl.BlockSpec((None, None, tm, KV_LORA), lambda i, h: (i // spt, h, i % spt, 0)),
                  pl.BlockSpec((None, KV_LORA, DSA_V_DIM), lambda i, h: (h, 0, 0))],
        out_specs=pl.BlockSpec((tm, DSA_V_DIM), lambda i, h: (i, h)),
        out_shape=jax.ShapeDtypeStruct((bsz * s, DSA_HEADS * DSA_V_DIM), BF16),
        compiler_params=_params("arbitrary", "arbitrary"),
        name="uv_project",
    )(o_lat, w_uv)


def _merge_kernel(h_ref, a_ref, b_ref, wga_ref, wgb_ref, wua_ref, wub_ref, o_ref,
                  cga_ref, cgb_ref, cua_ref, cub_ref):
    @pl.when(pl.program_id(1) == 0)
    def _():
        cga_ref[...] = wga_ref[...].astype(BF16)
        cgb_ref[...] = wgb_ref[...].astype(BF16)
        cua_ref[...] = wua_ref[...].astype(BF16)
        cub_ref[...] = wub_ref[...].astype(BF16)

    h = h_ref[...]
    ga = lax.dot_general(h, cga_ref[...], _NT_DIMS, preferred_element_type=F32)
    gb = lax.dot_general(h, cgb_ref[...], _NT_DIMS, preferred_element_type=F32)
    ya = jnp.dot(a_ref[...], cua_ref[...], preferred_element_type=F32)
    yb = jnp.dot(b_ref[...], cub_ref[...], preferred_element_type=F32)
    o_ref[...] = (jax.nn.sigmoid(ga) * ya + jax.nn.sigmoid(gb) * yb).astype(o_ref.dtype)


def _gated_merge(h1, att_a, att_b, w_t, row_ga, row_gb, w_up_a, w_up_b, tm=512, tn=256):
    n, d = h1.shape
    ka, kb = att_a.shape[1], att_b.shape[1]
    return pl.pallas_call(
        _merge_kernel,
        grid=(d // tn, n // tm),
        in_specs=[pl.BlockSpec((tm, d), lambda j, i: (i, 0)),
                  pl.BlockSpec((tm, ka), lambda j, i: (i, 0)),
                  pl.BlockSpec((tm, kb), lambda j, i: (i, 0)),
                  _w_rows_spec(tn, d, row_ga, lambda j, i: j),
                  _w_rows_spec(tn, d, row_gb, lambda j, i: j),
                  pl.BlockSpec((ka, tn), lambda j, i: (0, j)),
                  pl.BlockSpec((kb, tn), lambda j, i: (0, j))],
        out_specs=pl.BlockSpec((tm, tn), lambda j, i: (i, j)),
        out_shape=jax.ShapeDtypeStruct((n, d), BF16),
        scratch_shapes=[pltpu.VMEM((tn, d), BF16), pltpu.VMEM((tn, d), BF16),
                        pltpu.VMEM((ka, tn), BF16), pltpu.VMEM((kb, tn), BF16)],
        compiler_params=_params("arbitrary", "arbitrary"),
        name="gated_merge",
    )(h1, att_a, att_b, w_t, w_t, w_up_a, w_up_b)


def _mem_kernel(q_ref, kt_ref, v_ref, wom_ref, x_ref, o_ref):
    scale = MEM_HEAD_DIM ** -0.5
    outs = []
    for h in range(MEM_HEADS):
        sl = slice(h * MEM_HEAD_DIM, (h + 1) * MEM_HEAD_DIM)
        s = jnp.dot(q_ref[:, sl], kt_ref[sl, :], preferred_element_type=F32) * scale
        m = jnp.max(s, axis=-1, keepdims=True)
        p = jnp.exp(s - m)
        p = p / jnp.sum(p, axis=-1, keepdims=True)
        outs.append(jnp.dot(p.astype(BF16), v_ref[:, sl], preferred_element_type=F32).astype(BF16))
    o = jnp.concatenate(outs, axis=-1)
    o_ref[...] = x_ref[...] + jnp.dot(o, wom_ref[...], preferred_element_type=F32)


def _mem_attention(qm, km_t, kv_m, w_om_b, x1, bsz, s, n_mem, tm=512):
    spt = s // tm
    hd = MEM_HEADS * MEM_HEAD_DIM
    d = x1.shape[1]
    return pl.pallas_call(
        _mem_kernel,
        grid=(bsz, spt),
        in_specs=[pl.BlockSpec((tm, hd), lambda b, i: (b * spt + i, 0)),
                  pl.BlockSpec((None, hd, n_mem), lambda b, i: (b, 0, 0)),
                  pl.BlockSpec((n_mem, hd), lambda b, i: (b, 1)),
                  pl.BlockSpec((hd, d), lambda b, i: (0, 0)),
                  pl.BlockSpec((tm, d), lambda b, i: (b * spt + i, 0))],
        out_specs=pl.BlockSpec((tm, d), lambda b, i: (b * spt + i, 0)),
        out_shape=jax.ShapeDtypeStruct(x1.shape, F32),
        compiler_params=_params("arbitrary", "arbitrary"),
        name="memory_cross_attention",
    )(qm, km_t, kv_m, w_om_b, x1)


def _router_kernel(x_ref, g_ref, wr_ref, br_ref, h_ref, idx_ref, wt_ref, cnt_ref, carry_ref):
    @pl.when(pl.program_id(0) == 0)
    def _():
        carry_ref[...] = jnp.zeros_like(carry_ref)

    x = x_ref[...]
    hn = x * lax.rsqrt(jnp.mean(x * x, axis=-1, keepdims=True) + EPS) * g_ref[...]
    h_ref[...] = hn
    logits = jnp.dot(hn, wr_ref[...], precision=lax.Precision.HIGHEST, preferred_element_type=F32) + br_ref[...]
    tm = x.shape[0]
    lane_i = lax.broadcasted_iota(jnp.int32, (tm, LANES), 1)
    lane = lane_i.astype(F32)
    lane_grp = lax.shift_right_logical(lane_i, EXPERTS_PER_GROUP.bit_length() - 1).astype(F32)
    big = 1e9
    is_g = (lane_i >= N_EXPERTS) & (lane_i < N_EXPERTS + N_GROUPS)
    lg = jnp.where(is_g, logits, -jnp.inf)
    mg = jnp.max(lg, axis=1, keepdims=True)
    gate_g = 1.0 / jnp.sum(jnp.exp(lg - mg), axis=1, keepdims=True)
    grp = jnp.min(jnp.where(lg == mg, lane, big), axis=1, keepdims=True) - N_EXPERTS
    in_grp = (lane_i < N_EXPERTS) & (lane_grp == grp)
    le = jnp.where(in_grp, logits, -jnp.inf)
    m1 = jnp.max(le, axis=1, keepdims=True)
    e1 = jnp.min(jnp.where(le == m1, lane, big), axis=1, keepdims=True)
    le2 = jnp.where(lane == e1, -jnp.inf, le)
    m2 = jnp.max(le2, axis=1, keepdims=True)
    e2 = jnp.min(jnp.where(le2 == m2, lane, big), axis=1, keepdims=True)
    r = jnp.exp(m2 - m1)
    w1 = gate_g * (1.0 / (1.0 + r))
    w2 = gate_g * (r / (1.0 + r))
    oh1 = (lane == e1).astype(F32)
    oh2 = (lane == e2).astype(F32)
    oh = oh1 + oh2
    tri = (lax.broadcasted_iota(jnp.int32, (tm, tm), 0) > lax.broadcasted_iota(jnp.int32, (tm, tm), 1)).astype(BF16)
    prefix = jnp.dot(tri, oh.astype(BF16), preferred_element_type=F32) + carry_ref[...]
    rank1 = jnp.sum(prefix * oh1, axis=1, keepdims=True)
    rank2 = jnp.sum(prefix * oh2, axis=1, keepdims=True)
    new_cnt = carry_ref[...] + jnp.sum(oh, axis=0, keepdims=True)
    carry_ref[...] = new_cnt
    cnt_ref[...] = new_cnt
    idx_ref[...] = jnp.where(lane_i == 0, e1, jnp.where(lane_i == 1, e2, jnp.where(
        lane_i == 2, rank1, jnp.where(lane_i == 3, rank2, 0.0)))).astype(jnp.int32)
    wt_ref[...] = jnp.where(lane_i == 0, w1, jnp.where(lane_i == 1, w2, 0.0))


def _router(x2, g, w_r, b_r, tm=256):
    n, d = x2.shape
    row = lambda w: pl.BlockSpec((tm, w), lambda i: (i, 0))
    return pl.pallas_call(
        _router_kernel,
        grid=(n // tm,),
        in_specs=[row(d), pl.BlockSpec((1, d), lambda i: (0, 0)), pl.BlockSpec((d, LANES), lambda i: (0, 0)),
                  pl.BlockSpec((1, LANES), lambda i: (0, 0))],
        out_specs=[row(d), row(LANES), row(LANES), pl.BlockSpec((1, LANES), lambda i: (0, 0))],
        out_shape=[jax.ShapeDtypeStruct((n, d), F32), jax.ShapeDtypeStruct((n, LANES), jnp.int32),
                   jax.ShapeDtypeStruct((n, LANES), F32), jax.ShapeDtypeStruct((1, LANES), F32)],
        scratch_shapes=[pltpu.VMEM((1, LANES), F32)],
        compiler_params=_params("arbitrary"),
        name="moe_router",
    )(x2, g.reshape(1, d), w_r, b_r)


def _row_copy(src_hbm, row, dst, r, sem):
    return pltpu.make_async_copy(src_hbm.at[pl.ds(row, 1), :], dst.at[pl.ds(r, 1), :], sem)


def _expert_kernel(ie_ref, is_ref, in_ref, ni_ref, tok_ref,
                   h_hbm, wg_ref, wu_ref, wd_ref, y_ref, xf_ref, xb_ref, sem):
    i = pl.program_id(0)
    j = pl.program_id(1)
    valid = i < ni_ref[0]
    n = in_ref[i]
    start = is_ref[i]

    @pl.when((i == 0) & (j == 0))
    def _():
        xf_ref[...] = jnp.zeros_like(xf_ref)

    @pl.when(valid & (j == 0))
    def _():
        def issue(r, c):
            _row_copy(h_hbm, tok_ref[start + r], xf_ref, r, sem).start()
            return c

        lax.fori_loop(0, n, issue, 0)

        def wait(r, c):
            _row_copy(h_hbm, 0, xf_ref, 0, sem).wait()
            return c

        lax.fori_loop(0, n, wait, 0)
        xb_ref[...] = xf_ref[...].astype(BF16)
        y_ref[...] = jnp.zeros_like(y_ref)

    @pl.when(valid)
    def _():
        wg = wg_ref[...].astype(BF16)
        wu = wu_ref[...].astype(BF16)
        wd = wd_ref[...].astype(BF16)
        for sub in range(ITEM_ROWS // ITEM_SUB):
            @pl.when(sub * ITEM_SUB < n)
            def _():
                rows = slice(sub * ITEM_SUB, (sub + 1) * ITEM_SUB)
                xb = xb_ref[rows, :]
                g = jnp.dot(xb, wg, preferred_element_type=F32)
                u = jnp.dot(xb, wu, preferred_element_type=F32)
                hm = (g * jax.nn.sigmoid(g) * u).astype(BF16)
                y_ref[rows, :] += jnp.dot(hm, wd, preferred_element_type=F32)


def _experts(item_e, item_start, item_n, n_items, tok_sorted, h3, w_gate, w_up, w_down, max_items):
    d = h3.shape[1]
    nj = D_EXPERT // F_CHUNK

    def jj(i, j, ni):
        return jnp.where(i < ni[0], j, nj - 1)

    def blk(i, ni):
        return jnp.minimum(i, ni[0] - 1)

    grid_spec = pltpu.PrefetchScalarGridSpec(
        num_scalar_prefetch=5,
        grid=(max_items, nj),
        in_specs=[pl.BlockSpec(memory_space=pl.ANY),
                  pl.BlockSpec((None, d, F_CHUNK), lambda i, j, ie, s, n, ni, t: (ie[i], 0, jj(i, j, ni))),
                  pl.BlockSpec((None, d, F_CHUNK), lambda i, j, ie, s, n, ni, t: (ie[i], 0, jj(i, j, ni))),
                  pl.BlockSpec((None, F_CHUNK, d), lambda i, j, ie, s, n, ni, t: (ie[i], jj(i, j, ni), 0))],
        out_specs=pl.BlockSpec((ITEM_ROWS, d), lambda i, j, ie, s, n, ni, t: (blk(i, ni), 0)),
        scratch_shapes=[pltpu.VMEM((ITEM_ROWS, d), F32), pltpu.VMEM((ITEM_ROWS, d), BF16),
                        pltpu.SemaphoreType.DMA(())],
    )
    return pl.pallas_call(
        _expert_kernel,
        grid_spec=grid_spec,
        out_shape=jax.ShapeDtypeStruct((max_items * ITEM_ROWS, d), F32),
        compiler_params=_params("arbitrary", "arbitrary"),
        name="moe_experts",
    )(item_e, item_start, item_n, n_items, tok_sorted, h3, w_gate, w_up, w_down)


def _combine_kernel(dest_ref, y_hbm, x_ref, wt_ref, g_ref, o_ref, yv_ref, sem, *, tm):
    i = pl.program_id(0)

    def issue(r, c):
        base = 2 * (i * tm + r)
        _row_copy(y_hbm, dest_ref[base], yv_ref.at[0], r, sem).start()
        _row_copy(y_hbm, dest_ref[base + 1], yv_ref.at[1], r, sem).start()
        return c

    lax.fori_loop(0, tm, issue, 0)

    def wait(r, c):
        _row_copy(y_hbm, 0, yv_ref.at[0], 0, sem).wait()
        return c

    lax.fori_loop(0, 2 * tm, wait, 0)
    w = wt_ref[...]
    x = x_ref[...] + (yv_ref[0] * w[:, 0:1] + yv_ref[1] * w[:, 1:2])
    o_ref[...] = x * lax.rsqrt(jnp.mean(x * x, axis=-1, keepdims=True) + EPS) * g_ref[...]


def _combine(dest, y_buf, x2, wts, g_final, tm=128):
    n, d = x2.shape
    grid_spec = pltpu.PrefetchScalarGridSpec(
        num_scalar_prefetch=1,
        grid=(n // tm,),
        in_specs=[pl.BlockSpec(memory_space=pl.ANY),
                  pl.BlockSpec((tm, d), lambda i, dr: (i, 0)),
                  pl.BlockSpec((tm, LANES), lambda i, dr: (i, 0)),
                  pl.BlockSpec((1, d), lambda i, dr: (0, 0))],
        out_specs=pl.BlockSpec((tm, d), lambda i, dr: (i, 0)),
        scratch_shapes=[pltpu.VMEM((2, tm, d), F32), pltpu.SemaphoreType.DMA(())],
    )
    return pl.pallas_call(
        functools.partial(_combine_kernel, tm=tm),
        grid_spec=grid_spec,
        out_shape=jax.ShapeDtypeStruct((n, d), F32),
        compiler_params=_params("arbitrary"),
        name="moe_combine_final_norm",
    )(dest, y_buf, x2, wts, g_final.reshape(1, d))


def _pad_cols(w, width):
    return jnp.pad(w, ((0, 0), (0, width - w.shape[1])))


def _mixer(x2d, bsz, s, g_norm, w_in, b_f, g_q_lat, g_kv_lat, g_idx_k, b_idx_k, w_uq, w_idx_q, w_uk, w_uv,
           w_up_a, w_up_b, w_out):
    d = x2d.shape[1]
    topk = min(DSA_TOPK_MAX, s // 4)
    h1 = _rmsnorm(x2d, g_norm, BF16)
    c_small = 3 * FOX_WIDTH
    o_fa, o_cq = c_small, c_small + FOX_HEADS
    o_ckv = o_cq + Q_LORA
    o_kr = o_ckv + KV_LORA
    o_ik = o_kr + DSA_ROPE_DIM
    o_iw = o_ik + IDX_DIM
    o_ga = o_iw + IDX_HEADS
    o_gb = o_ga + d
    w_t = jnp.swapaxes(w_in, 0, 1)
    pad_rows = lambda w: jnp.pad(w, ((0, LANES - w.shape[0]), (0, 0)))
    w_small_t = jnp.concatenate([
        w_t[o_cq:o_ckv], w_t[o_ckv:o_kr], w_t[o_ik:o_iw],
        pad_rows(w_t[o_kr:o_ik]), pad_rows(w_t[o_fa:o_cq]), pad_rows(w_t[o_iw:o_ga])], axis=0)

    qkv = _mm(h1, w_t, 3 * FOX_WIDTH, w_rows=True, name="proj_qkv")
    small = _mm(h1, w_small_t, SMALL_W, out_dtype=F32, tn=SMALL_W // 7, w_rows=True, name="proj_small")

    tabs64 = _rope_tables(s, DSA_ROPE_DIM)
    tabs128 = _rope_tables(s, IDX_DIM)
    cq_n, kv_cat, k_idx, w_idx, fa = _prep(small, g_q_lat, g_kv_lat, g_idx_k, b_idx_k, tabs64, tabs128, s)

    cum = _fox_cum(fa, _pad_cols(b_f.reshape(1, -1), LANES), bsz, s)
    tk = 512
    cum_t = jnp.transpose(cum[:, :, :FOX_HEADS], (0, 2, 1)).reshape(bsz * FOX_HEADS, s // tk, tk)
    att_a = _fox_attention(qkv, cum, cum_t, bsz, s, t=tk)

    w_uq3 = w_uq.reshape(Q_LORA, DSA_HEADS, DSA_NOPE_DIM + DSA_ROPE_DIM)
    w_uq_nope = w_uq3[:, :, :DSA_NOPE_DIM].reshape(Q_LORA, DSA_HEADS * DSA_NOPE_DIM)
    w_uq_rope = jnp.pad(w_uq3[:, :, DSA_NOPE_DIM:], ((0, 0), (0, 0), (0, LANES - DSA_ROPE_DIM))).reshape(
        Q_LORA, DSA_HEADS * LANES)
    q_nope = _mm(cq_n, w_uq_nope, DSA_HEADS * DSA_NOPE_DIM, name="proj_q_nope")
    q_rope = _mm(cq_n, w_uq_rope, DSA_HEADS * LANES, name="proj_q_rope")
    q_cat = _q_absorb(q_nope, q_rope, w_uk, tabs64, bsz, s)
    q_idx = _mm(cq_n, w_idx_q, IDX_HEADS * IDX_DIM, heads=(bsz, s), rope_tabs=tabs128, name="proj_q_idx")
    k_idx_t = jnp.transpose(k_idx.reshape(bsz, s, IDX_DIM), (0, 2, 1))
    bias = _indexer_mask(q_idx, k_idx_t, w_idx.reshape(bsz, s, LANES), bsz, s, topk)
    kv3 = kv_cat.reshape(bsz, s, KV_PAD)
    o_lat = _dsa_attention(q_cat, jnp.transpose(kv3, (0, 2, 1)), kv3, bias, bsz, s)
    att_b = _uv_project(o_lat, w_uv, bsz, s)

    merged = _gated_merge(h1, att_a, att_b, w_t, o_ga, o_gb, w_up_a, w_up_b)
    return _mm(merged, w_out, d, out_dtype=F32, res=x2d, name="proj_out")


def _memory_layer(x1, mem, bsz, s, g_x, g_m, w_qm, w_km, w_vm, w_om):
    n_mem = mem.shape[1]
    d = x1.shape[1]
    hd = MEM_HEADS * MEM_HEAD_DIM
    h2 = _rmsnorm(x1, g_x, BF16)
    mem_n = _rmsnorm(mem.reshape(bsz * n_mem, d), g_m, BF16)
    qm = _mm(h2, w_qm, hd, name="proj_q_mem")
    kv_m = _mm(mem_n, jnp.concatenate([w_km, w_vm], axis=1), 2 * hd, tm=bsz * n_mem, name="proj_kv_mem")
    km_t = jnp.transpose(kv_m[:, :hd].reshape(bsz, n_mem, hd), (0, 2, 1))
    return _mem_attention(qm, km_t, kv_m, w_om.astype(BF16), x1, bsz, s, n_mem)


def _moe_layer(x2, g_ffn, w_rg, b_rg, w_re, b_re, w_gate, w_up, w_down, g_final):
    n, d = x2.shape
    m = n * TOPK_IN_GROUP
    w_r = _pad_cols(jnp.concatenate([w_re, w_rg], axis=1), LANES)
    b_r = _pad_cols(jnp.concatenate([b_re, b_rg]).reshape(1, -1), LANES)
    h3, ridx, wts, cnt = _router(x2, g_ffn, w_r, b_r)
    counts = cnt[0, :N_EXPERTS].astype(jnp.int32)
    e_flat = ridx[:, :TOPK_IN_GROUP].reshape(m)
    rank_flat = ridx[:, TOPK_IN_GROUP:2 * TOPK_IN_GROUP].reshape(m)
    starts = jnp.cumsum(counts) - counts
    n_it = (counts + ITEM_ROWS - 1) // ITEM_ROWS
    it_end = jnp.cumsum(n_it)
    it_first = it_end - n_it
    max_items = m // ITEM_ROWS + N_EXPERTS
    t = jnp.arange(max_items, dtype=jnp.int32)
    item_e = jnp.minimum(jnp.searchsorted(it_end, t, side='right'), N_EXPERTS - 1).astype(jnp.int32)
    k_in = t - it_first[item_e]
    item_start = (starts[item_e] + k_in * ITEM_ROWS).astype(jnp.int32)
    item_n = jnp.clip(counts[item_e] - k_in * ITEM_ROWS, 0, ITEM_ROWS).astype(jnp.int32)
    n_items = it_end[-1:].astype(jnp.int32)
    last = jnp.maximum(n_items[0] - 1, 0)
    item_e = jnp.where(t < n_items[0], item_e, item_e[last])
    item_start = jnp.where(t < n_items[0], item_start, 0)
    item_n = jnp.where(t < n_items[0], item_n, 0)
    tok_flat = jnp.arange(m, dtype=jnp.int32) // TOPK_IN_GROUP
    tok_sorted = jnp.zeros((m,), jnp.int32).at[starts[e_flat] + rank_flat].set(tok_flat)
    dest = (it_first[e_flat] * ITEM_ROWS + rank_flat).astype(jnp.int32)
    y_buf = _experts(item_e, item_start, item_n, n_items, tok_sorted, h3, w_gate, w_up, w_down, max_items)
    return _combine(dest, y_buf, x2, wts, g_final)


def kernel(x, mem, g_norm_mix, w_in, b_f, g_q_lat, g_kv_lat, g_idx_k, b_idx_k, w_uq, w_idx_q, w_uk, w_uv,
           w_up_a, w_up_b, w_out, g_norm_mem_x, g_mem, w_qm, w_km, w_vm, w_om, g_norm_ffn,
           w_rg, b_rg, w_re, b_re, w_gate, w_up, w_down, g_final):
    bsz, s, d = x.shape
    depth = w_in.shape[0]
    assert depth == 1, "the final norm is fused into the last MoE combine"
    l = 0
    x2d = x.reshape(bsz * s, d)
    x1 = _mixer(x2d, bsz, s, g_norm_mix[l], w_in[l], b_f[l], g_q_lat[l], g_kv_lat[l], g_idx_k[l], b_idx_k[l],
                w_uq[l], w_idx_q[l], w_uk[l], w_uv[l], w_up_a[l], w_up_b[l], w_out[l])
    x2 = _memory_layer(x1, mem, bsz, s, g_norm_mem_x[l], g_mem[l], w_qm[l], w_km[l], w_vm[l], w_om[l])
    out = _moe_layer(x2, g_norm_ffn[l], w_rg[l], b_rg[l], w_re[l], b_re[l], w_gate[l], w_up[l], w_down[l], g_final)
    return out.reshape(bsz, s, d)
```

```python
import functools

import jax
import jax.numpy as jnp
import numpy as np
from jax import lax
from jax.experimental import pallas as pl
from jax.experimental.pallas import tpu as pltpu

F32 = jnp.float32
BF16 = jnp.bfloat16

EPS = 1e-6
ROPE_THETA = 10000.0
FOX_HEADS = 16
FOX_HEAD_DIM = 128
DSA_HEADS = 16
DSA_NOPE_DIM = 128
DSA_ROPE_DIM = 64
DSA_V_DIM = 128
Q_LORA = 1024
KV_LORA = 256
IDX_HEADS = 32
IDX_DIM = 128
DSA_TOPK_MAX = 256
MEM_HEADS = 4
MEM_HEAD_DIM = 128
N_GROUPS = 8
EXPERTS_PER_GROUP = 8
N_EXPERTS = N_GROUPS * EXPERTS_PER_GROUP
TOPK_IN_GROUP = 2
D_EXPERT = 512

FOX_WIDTH = FOX_HEADS * FOX_HEAD_DIM
LANES = 128
SUBLANES = 8
KV_PAD = KV_LORA + LANES
NEG_BIG = -1e30
VMEM_LIMIT_BYTES = 56 * 1024 * 1024
ITEM_ROWS = 512
ITEM_SUB = 128
F_CHUNK = 128
KEY_CHUNK = 256


def _params(*sem):
    return pltpu.CompilerParams(dimension_semantics=sem, vmem_limit_bytes=VMEM_LIMIT_BYTES)


def _rmsnorm_kernel(x_ref, g_ref, o_ref):
    x = x_ref[...].astype(F32)
    ms = jnp.mean(x * x, axis=-1, keepdims=True)
    o_ref[...] = (x * lax.rsqrt(ms + EPS) * g_ref[...]).astype(o_ref.dtype)


def _rmsnorm(x2d, g, out_dtype, tm=256):
    n, d = x2d.shape
    return pl.pallas_call(
        _rmsnorm_kernel,
        grid=(n // tm,),
        in_specs=[pl.BlockSpec((tm, d), lambda i: (i, 0)), pl.BlockSpec((1, d), lambda i: (0, 0))],
        out_specs=pl.BlockSpec((tm, d), lambda i: (i, 0)),
        out_shape=jax.ShapeDtypeStruct((n, d), out_dtype),
        compiler_params=_params("arbitrary"),
        name="rmsnorm",
    )(x2d, g.reshape(1, d).astype(F32))


_NT_DIMS = (((1,), (1,)), ((), ()))


def _w_rows_spec(tn, k, row0, index_of_j):
    if row0 % tn == 0:
        return pl.BlockSpec((tn, k), lambda *g: (index_of_j(*g) + row0 // tn, 0))
    assert row0 % SUBLANES == 0
    return pl.BlockSpec((pl.Element(tn), pl.Element(k)),
                        lambda *g: (pl.multiple_of(index_of_j(*g) * tn + row0, SUBLANES), 0))


def _mm_kernel(a_ref, b_ref, *rest, has_res, head_out, rope, w_rows):
    pos = 0
    res_ref = cos_ref = sin_ref = None
    if has_res:
        res_ref = rest[pos]
        pos += 1
    if rope:
        cos_ref, sin_ref = rest[pos], rest[pos + 1]
        pos += 2
    o_ref = rest[pos]
    wb_ref = rest[pos + 1]

    @pl.when(pl.program_id(1) == 0)
    def _():
        wb_ref[...] = b_ref[...].astype(BF16)

    if w_rows:
        acc = lax.dot_general(a_ref[...], wb_ref[...], _NT_DIMS, preferred_element_type=F32)
    else:
        acc = jnp.dot(a_ref[...], wb_ref[...], preferred_element_type=F32)
    if has_res:
        acc = acc + res_ref[...]
    if head_out:
        for hh in range(acc.shape[1] // LANES):
            part = acc[:, hh * LANES:(hh + 1) * LANES]
            if rope:
                part = part * cos_ref[...] + pltpu.roll(part, LANES // 2, 1) * sin_ref[...]
            o_ref[hh] = part.astype(o_ref.dtype)
    else:
        o_ref[...] = acc.astype(o_ref.dtype)


def _mm(a, w, n_cols, *, col0=0, out_dtype=BF16, tm=1024, tn=512, res=None, heads=None, rope_tabs=None,
        w_rows=False, name="mm"):
    m, k = a.shape
    tm = min(tm, m if heads is None else heads[1])
    assert n_cols % tn == 0 and m % tm == 0
    if w_rows:
        w_spec = _w_rows_spec(tn, k, col0, lambda j, i: j)
    else:
        assert col0 % tn == 0
        w_spec = pl.BlockSpec((k, tn), lambda j, i: (0, j + col0 // tn))
    in_specs = [pl.BlockSpec((tm, k), lambda j, i: (i, 0)), w_spec]
    args = [a, w]
    if res is not None:
        in_specs.append(pl.BlockSpec((tm, tn), lambda j, i: (i, j)))
        args.append(res)
    if heads is not None:
        bsz, s = heads
        assert s % tm == 0
        spt = s // tm
        if rope_tabs is not None:
            in_specs += [pl.BlockSpec((tm, LANES), lambda j, i: (i % spt, 0))] * 2
            args += list(rope_tabs)
        hpt = tn // LANES
        out_spec = pl.BlockSpec((None, hpt, tm, LANES), lambda j, i: (i // spt, j, i % spt, 0))
        out_shape = jax.ShapeDtypeStruct((bsz, n_cols // LANES, s, LANES), out_dtype)
    else:
        out_spec = pl.BlockSpec((tm, tn), lambda j, i: (i, j))
        out_shape = jax.ShapeDtypeStruct((m, n_cols), out_dtype)
    kern = functools.partial(_mm_kernel, has_res=res is not None, head_out=heads is not None,
                             rope=rope_tabs is not None, w_rows=w_rows)
    return pl.pallas_call(
        kern,
        grid=(n_cols // tn, m // tm),
        in_specs=in_specs,
        out_specs=out_spec,
        out_shape=out_shape,
        scratch_shapes=[pltpu.VMEM((tn, k) if w_rows else (k, tn), BF16)],
        compiler_params=_params("arbitrary", "arbitrary"),
        name=name,
    )(*args)


def _rope_tables(s, d):
    inv = jnp.power(ROPE_THETA, -jnp.arange(0, d, 2, dtype=F32) / d)
    ang = jnp.arange(s, dtype=jnp.int32).astype(F32)[:, None] * inv[None, :]
    cos, sin = jnp.cos(ang), jnp.sin(ang)
    cos_t = jnp.tile(jnp.concatenate([cos, cos], axis=-1), (1, LANES // d))
    sin_t = jnp.tile(jnp.concatenate([-sin, sin], axis=-1), (1, LANES // d))
    return cos_t, sin_t


def _rot_half64(x):
    lane = lax.broadcasted_iota(jnp.int32, x.shape, 1)
    return jnp.where((lane % 64) < 32, pltpu.roll(x, 96, 1), pltpu.roll(x, 32, 1))


_C_CQ = 0
_C_CKV = Q_LORA
_C_IK = _C_CKV + KV_LORA
_C_KR = _C_IK + IDX_DIM
_C_FA = _C_KR + LANES
_C_IW = _C_FA + LANES
SMALL_W = _C_IW + LANES


def _prep_kernel(s_ref, gq_ref, gkv_ref, gik_ref, bik_ref, c64_ref, s64_ref, c128_ref, s128_ref,
                 cq_ref, kv_ref, kidx_ref, widx_ref, fa_ref):
    cq = s_ref[:, _C_CQ:_C_CQ + Q_LORA]
    cq_ref[...] = (cq * lax.rsqrt(jnp.mean(cq * cq, axis=-1, keepdims=True) + EPS) * gq_ref[...]).astype(BF16)
    ckv = s_ref[:, _C_CKV:_C_CKV + KV_LORA]
    kv_ref[:, :KV_LORA] = (ckv * lax.rsqrt(jnp.mean(ckv * ckv, axis=-1, keepdims=True) + EPS)
                           * gkv_ref[...]).astype(BF16)
    kr = s_ref[:, _C_KR:_C_KR + LANES]
    kv_ref[:, KV_LORA:] = (kr * c64_ref[...] + _rot_half64(kr) * s64_ref[...]).astype(BF16)
    ik = s_ref[:, _C_IK:_C_IK + IDX_DIM]
    mu = jnp.mean(ik, axis=-1, keepdims=True)
    xc = ik - mu
    ikn = xc * lax.rsqrt(jnp.mean(xc * xc, axis=-1, keepdims=True) + EPS) * gik_ref[...] + bik_ref[...]
    kidx_ref[...] = (ikn * c128_ref[...] + pltpu.roll(ikn, LANES // 2, 1) * s128_ref[...]).astype(BF16)
    widx_ref[...] = s_ref[:, _C_IW:_C_IW + LANES] * (IDX_HEADS * IDX_DIM) ** -0.5
    fa_ref[...] = s_ref[:, _C_FA:_C_FA + LANES]


def _prep(small, g_q, g_kv, g_ik, b_ik, tabs64, tabs128, s, tm=256):
    n = small.shape[0]
    spt = s // tm
    row = lambda w: pl.BlockSpec((tm, w), lambda i: (i, 0))
    vec = lambda w: pl.BlockSpec((1, w), lambda i: (0, 0))
    tab = pl.BlockSpec((tm, LANES), lambda i: (i % spt, 0))
    return pl.pallas_call(
        _prep_kernel,
        grid=(n // tm,),
        in_specs=[row(SMALL_W), vec(Q_LORA), vec(KV_LORA), vec(IDX_DIM), vec(IDX_DIM), tab, tab, tab, tab],
        out_specs=[row(Q_LORA), row(KV_PAD), row(IDX_DIM), row(LANES), row(LANES)],
        out_shape=[jax.ShapeDtypeStruct((n, Q_LORA), BF16), jax.ShapeDtypeStruct((n, KV_PAD), BF16),
                   jax.ShapeDtypeStruct((n, IDX_DIM), BF16), jax.ShapeDtypeStruct((n, LANES), F32),
                   jax.ShapeDtypeStruct((n, LANES), F32)],
        compiler_params=_params("arbitrary"),
        name="mixer_prep",
    )(small, g_q.reshape(1, -1), g_kv.reshape(1, -1), g_ik.reshape(1, -1), b_ik.reshape(1, -1),
      tabs64[0], tabs64[1], tabs128[0], tabs128[1])


def _cum_kernel(fa_ref, bf_ref, cum_ref, carry_ref):
    @pl.when(pl.program_id(1) == 0)
    def _():
        carry_ref[...] = jnp.zeros_like(carry_ref)

    z = fa_ref[...] + bf_ref[...]
    logf = jnp.minimum(z, 0.0) - jnp.log1p(jnp.exp(-jnp.abs(z)))
    t = z.shape[0]
    tri = (lax.broadcasted_iota(jnp.int32, (t, t), 0) >= lax.broadcasted_iota(jnp.int32, (t, t), 1)).astype(F32)
    c = jnp.dot(tri, logf, precision=lax.Precision.HIGHEST, preferred_element_type=F32) + carry_ref[...]
    cum_ref[...] = c
    carry_ref[...] = c[t - 1:t, :]


def _fox_cum(fa, b_f_pad, bsz, s, t=256):
    return pl.pallas_call(
        _cum_kernel,
        grid=(bsz, s // t),
        in_specs=[pl.BlockSpec((None, t, LANES), lambda b, i: (b, i, 0)),
                  pl.BlockSpec((1, LANES), lambda b, i: (0, 0))],
        out_specs=pl.BlockSpec((None, t, LANES), lambda b, i: (b, i, 0)),
        out_shape=jax.ShapeDtypeStruct((bsz, s, LANES), F32),
        scratch_shapes=[pltpu.VMEM((1, LANES), F32)],
        compiler_params=_params("arbitrary", "arbitrary"),
        name="fox_cumsum",
    )(fa.reshape(bsz, s, LANES), b_f_pad)


def _lane_fold(x, op):
    out = x[:, :LANES]
    for c in range(1, x.shape[1] // LANES):
        out = op(out, x[:, c * LANES:(c + 1) * LANES])
    return out


def _fox_kernel(q_ref, k_ref, v_ref, cum_ref, cumt_ref, o_ref, s_ref, *, t):
    h = pl.program_id(1)
    i = pl.program_id(2)
    q = q_ref[...]
    lane = lax.broadcasted_iota(jnp.int32, (t, LANES), 1)
    cq = jnp.sum(jnp.where(lane == h, cum_ref[...], 0.0), axis=1, keepdims=True)
    scale = FOX_HEAD_DIM ** -0.5

    def logits(kc):
        kb = k_ref[pl.ds(pl.multiple_of(kc * t, t), t), :]
        s = lax.dot_general(q, kb, (((1,), (1,)), ((), ())), preferred_element_type=F32) * scale
        return s + cq - cumt_ref[pl.ds(kc, 1), :]

    def pass1(kc, m_lane):
        s = logits(kc)
        s_ref[kc] = s
        return jnp.maximum(m_lane, _lane_fold(s, jnp.maximum))

    m_lane = lax.fori_loop(0, i, pass1, jnp.full((t, LANES), NEG_BIG, F32))
    causal = lax.broadcasted_iota(jnp.int32, (t, t), 0) >= lax.broadcasted_iota(jnp.int32, (t, t), 1)
    s = jnp.where(causal, logits(i), NEG_BIG)
    s_ref[i] = s
    m_lane = jnp.maximum(m_lane, _lane_fold(s, jnp.maximum))
    m = jnp.max(m_lane, axis=1, keepdims=True)

    def pass2(kc, carry):
        l_lane, acc = carry
        p = jnp.exp(s_ref[kc] - m)
        vb = v_ref[pl.ds(pl.multiple_of(kc * t, t), t), :]
        return (l_lane + _lane_fold(p, jnp.add),
                acc + jnp.dot(p.astype(BF16), vb, preferred_element_type=F32))

    l_lane, acc = lax.fori_loop(0, i + 1, pass2, (jnp.zeros((t, LANES), F32),
                                                  jnp.zeros((t, FOX_HEAD_DIM), F32)))
    o_ref[...] = (acc / jnp.sum(l_lane, axis=1, keepdims=True)).astype(o_ref.dtype)


def _fox_attention(qkv, cum, cum_t, bsz, s, t=512):
    tq = tk = t
    n = bsz * s
    nq = s // tq
    hh = FOX_HEADS
    kern = functools.partial(_fox_kernel, t=t)
    return pl.pallas_call(
        kern,
        grid=(bsz, hh, nq),
        in_specs=[pl.BlockSpec((tq, LANES), lambda b, h, i: (b * nq + i, h)),
                  pl.BlockSpec((s, LANES), lambda b, h, i: (b, hh + h)),
                  pl.BlockSpec((s, LANES), lambda b, h, i: (b, 2 * hh + h)),
                  pl.BlockSpec((None, tq, LANES), lambda b, h, i: (b, i, 0)),
                  pl.BlockSpec((None, s // tk, tk), lambda b, h, i: (b * hh + h, 0, 0))],
        out_specs=pl.BlockSpec((tq, LANES), lambda b, h, i: (b * nq + i, h)),
        out_shape=jax.ShapeDtypeStruct((n, FOX_WIDTH), BF16),
        scratch_shapes=[pltpu.VMEM((s // tk, tq, tk), F32)],
        compiler_params=_params("arbitrary", "arbitrary", "arbitrary"),
        name="fox_attention",
    )(qkv, qkv, qkv, cum, cum_t)


def _qabs_kernel(qn_ref, qr_ref, wuk_ref, c64_ref, s64_ref, o_ref):
    lat = jnp.dot(qn_ref[...], wuk_ref[...].astype(BF16), preferred_element_type=F32)
    o_ref[:, :KV_LORA] = lat.astype(BF16)
    qr = qr_ref[...].astype(F32)
    o_ref[:, KV_LORA:] = (qr * c64_ref[...] + _rot_half64(qr) * s64_ref[...]).astype(BF16)


def _q_absorb(q_nope, q_rope, w_uk, tabs64, bsz, s, tm=512):
    spt = s // tm
    return pl.pallas_call(
        _qabs_kernel,
        grid=(bsz * spt, DSA_HEADS),
        in_specs=[pl.BlockSpec((tm, LANES), lambda i, h: (i, h)),
                  pl.BlockSpec((tm, LANES), lambda i, h: (i, h)),
                  pl.BlockSpec((None, DSA_NOPE_DIM, KV_LORA), lambda i, h: (h, 0, 0)),
                  pl.BlockSpec((tm, LANES), lambda i, h: (i % spt, 0)),
                  pl.BlockSpec((tm, LANES), lambda i, h: (i % spt, 0))],
        out_specs=pl.BlockSpec((None, None, tm, KV_PAD), lambda i, h: (i // spt, h, i % spt, 0)),
        out_shape=jax.ShapeDtypeStruct((bsz, DSA_HEADS, s, KV_PAD), BF16),
        compiler_params=_params("arbitrary", "arbitrary"),
        name="q_absorb",
    )(q_nope, q_rope, w_uk, tabs64[0], tabs64[1])


def _needed_chunks(i, tq):
    return ((i + 1) * tq + KEY_CHUNK - 1) // KEY_CHUNK


_NEG_INF_KEY = int(np.array(-np.inf, np.float32).view(np.int32) ^ np.int32(0x7FFFFFFF))


def _indexer_kernel(q_ref, k_ref, wt_ref, o_ref, key_ref, thr_ref, *, tq, topk, hg):
    i = pl.program_id(1)
    tk = KEY_CHUNK
    n_chunks = key_ref.shape[0]
    nk = _needed_chunks(i, tq)
    qpos = i * tq + lax.broadcasted_iota(jnp.int32, (tk, tq), 1)
    krow = lax.broadcasted_iota(jnp.int32, (tk, tq), 0)

    def score_chunk(kc, c):
        k0 = pl.multiple_of(kc * tk, tk)
        kb = k_ref[pl.ds(k0, tk), :]
        sc = jnp.zeros((tk, tq), F32)
        for g in range(IDX_HEADS // hg):
            q2 = q_ref[g * hg:(g + 1) * hg].reshape(hg * tq, IDX_DIM)
            d = lax.dot_general(kb, q2, _NT_DIMS, preferred_element_type=F32)
            for hh in range(hg):
                h = g * hg + hh
                sc = sc + wt_ref[h:h + 1, :] * jnp.maximum(d[:, hh * tq:(hh + 1) * tq], 0.0)
        sc = jnp.where(sc == 0.0, 0.0, sc)
        sc = jnp.where(k0 + krow <= qpos, sc, -jnp.inf)
        bits = pltpu.bitcast(sc, jnp.int32)
        key_ref[kc] = jnp.where(bits < 0, bits ^ jnp.int32(0x7FFFFFFF), bits)
        return c

    lax.fori_loop(0, nk, score_chunk, 0)

    def pad_chunk(kc, c):
        key_ref[kc] = jnp.full((tk, tq), _NEG_INF_KEY, jnp.int32)
        return c

    lax.fori_loop(nk, n_chunks, pad_chunk, 0)
    kf = float(topk)

    def bisection(n_counted):
        def count_ge(t):
            acc = jnp.zeros((SUBLANES, tq), F32)
            for kc in range(n_counted):
                ind = jnp.where(key_ref[kc] >= t, 1.0, 0.0)
                acc = acc + jnp.sum(ind.reshape(tk // SUBLANES, SUBLANES, tq), axis=0)
            return jnp.sum(acc, axis=0, keepdims=True)

        int_min = jnp.int32(-2 ** 31)
        t0 = jnp.where(count_ge(jnp.zeros((1, tq), jnp.int32)) >= kf, jnp.int32(0), int_min)

        def bisect(it, t):
            cand = t | lax.shift_left(jnp.int32(1), jnp.int32(30) - it)
            return jnp.where(count_ge(cand) >= kf, cand, t)

        thr_ref[...] = lax.fori_loop(0, 31, bisect, t0)

    half = max(n_chunks // 2, 1)
    if half < n_chunks:
        pl.when(nk <= half)(lambda: bisection(half))
        pl.when(nk > half)(lambda: bisection(n_chunks))
    else:
        bisection(n_chunks)
    thr = thr_ref[...]

    def write_chunk(kc, c):
        keep = (key_ref[kc] >= thr) & (kc * tk + krow <= qpos)
        o_ref[kc] = jnp.where(keep, 0.0, NEG_BIG).T.astype(o_ref.dtype)
        return c

    lax.fori_loop(0, nk, write_chunk, 0)

    def fill_chunk(kc, c):
        o_ref[kc] = jnp.full((tq, tk), NEG_BIG, o_ref.dtype)
        return c

    lax.fori_loop(nk, n_chunks, fill_chunk, 0)


def _indexer_mask(q_idx, k_idx, w_idx_t, bsz, s, topk, tq=128, hg=4):
    kern = functools.partial(_indexer_kernel, tq=tq, topk=topk, hg=hg)
    nc = s // KEY_CHUNK
    return pl.pallas_call(
        kern,
        grid=(bsz, s // tq),
        in_specs=[pl.BlockSpec((None, IDX_HEADS, tq, IDX_DIM), lambda b, i: (b, 0, i, 0)),
                  pl.BlockSpec((None, s, IDX_DIM), lambda b, i: (b, 0, 0)),
                  pl.BlockSpec((None, IDX_HEADS, tq), lambda b, i: (b, 0, i))],
        out_specs=pl.BlockSpec((None, nc, tq, KEY_CHUNK), lambda b, i: (b, 0, i, 0)),
        out_shape=jax.ShapeDtypeStruct((bsz, nc, s, KEY_CHUNK), BF16),
        scratch_shapes=[pltpu.VMEM((nc, KEY_CHUNK, tq), jnp.int32), pltpu.VMEM((1, tq), jnp.int32)],
        compiler_params=_params("arbitrary", "arbitrary"),
        name="indexer_topk_mask",
    )(q_idx, k_idx, w_idx_t)


def _dsa_kernel(q_ref, kvt_ref, kv_ref, bias_ref, o_ref, s_ref, m_ref, l_ref, acc_ref, *, tq):
    i = pl.program_id(1)
    tk = KEY_CHUNK
    rows = DSA_HEADS * tq
    nk = _needed_chunks(i, tq)
    q2 = q_ref[...].reshape(rows, KV_PAD)
    scale = (DSA_NOPE_DIM + DSA_ROPE_DIM) ** -0.5
    m_ref[...] = jnp.full(m_ref.shape, NEG_BIG, F32)

    def pass1(kc, c):
        sc = jnp.dot(q2, kvt_ref[kc], preferred_element_type=F32) * scale
        sc = (sc.reshape(DSA_HEADS, tq, tk) + bias_ref[kc].astype(F32)[None]).reshape(rows, tk)
        s_ref[kc] = sc
        m_ref[...] = jnp.maximum(m_ref[...], _lane_fold(sc, jnp.maximum))
        return c

    lax.fori_loop(0, nk, pass1, 0)
    m_ref[...] = jnp.broadcast_to(jnp.max(m_ref[...], axis=1, keepdims=True), m_ref.shape)
    l_ref[...] = jnp.zeros(l_ref.shape, F32)
    acc_ref[...] = jnp.zeros(acc_ref.shape, F32)

    def pass2(kc, c):
        m = m_ref[...]
        p = jnp.concatenate([jnp.exp(s_ref[kc, :, g * LANES:(g + 1) * LANES] - m)
                             for g in range(tk // LANES)], axis=1)
        l_ref[...] += _lane_fold(p, jnp.add)
        kvb = kv_ref[pl.ds(pl.multiple_of(kc * tk, tk), tk), :KV_LORA]
        acc_ref[...] += jnp.dot(p.astype(BF16), kvb, preferred_element_type=F32)
        return c

    lax.fori_loop(0, nk, pass2, 0)
    o = acc_ref[...] / jnp.sum(l_ref[...], axis=1, keepdims=True)
    o_ref[...] = o.reshape(DSA_HEADS, tq, KV_LORA).astype(o_ref.dtype)


def _dsa_attention(q_cat, kv_t, kv, bias, bsz, s, tq=64):
    kern = functools.partial(_dsa_kernel, tq=tq)
    nc = s // KEY_CHUNK
    rows = DSA_HEADS * tq
    return pl.pallas_call(
        kern,
        grid=(bsz, s // tq),
        in_specs=[pl.BlockSpec((None, DSA_HEADS, tq, KV_PAD), lambda b, i: (b, 0, i, 0)),
                  pl.BlockSpec((None, nc, KV_PAD, KEY_CHUNK), lambda b, i: (b, 0, 0, 0)),
                  pl.BlockSpec((None, s, KV_PAD), lambda b, i: (b, 0, 0)),
                  pl.BlockSpec((None, nc, tq, KEY_CHUNK), lambda b, i: (b, 0, i, 0))],
        out_specs=pl.BlockSpec((None, DSA_HEADS, tq, KV_LORA), lambda b, i: (b, 0, i, 0)),
        out_shape=jax.ShapeDtypeStruct((bsz, DSA_HEADS, s, KV_LORA), BF16),
        scratch_shapes=[pltpu.VMEM((nc, rows, KEY_CHUNK), F32), pltpu.VMEM((rows, LANES), F32),
                        pltpu.VMEM((rows, LANES), F32), pltpu.VMEM((rows, KV_LORA), F32)],
        compiler_params=_params("arbitrary", "arbitrary"),
        name="dsa_attention",
    )(q_cat, kv_t, kv, bias)


def _uv_kernel(o_ref, w_ref, y_ref):
    y_ref[...] = jnp.dot(o_ref[...], w_ref[...].astype(BF16), preferred_element_type=F32).astype(y_ref.dtype)


def _uv_project(o_lat, w_uv, bsz, s, tm=512):
    spt = s // tm
    return pl.pallas_call(
        _uv_kernel,
        grid=(bsz * spt, DSA_HEADS),
        in_specs=[pl.BlockSpec((None, None, tm, KV_LORA), lambda i, h: (i // spt, h, i % spt, 0)),
                  pl.BlockSpec((None, KV_LORA, DSA_V_DIM), lambda i, h: (h, 0, 0))],
        out_specs=pl.BlockSpec((tm, DSA_V_DIM), lambda i, h: (i, h)),
        out_shape=jax.ShapeDtypeStruct((bsz * s, DSA_HEADS * DSA_V_DIM), BF16),
        compiler_params=_params("arbitrary", "arbitrary"),
        name="uv_project",
    )(o_lat, w_uv)


def _merge_kernel(h_ref, a_ref, b_ref, wga_ref, wgb_ref, wua_ref, wub_ref, o_ref,
                  cga_ref, cgb_ref, cua_ref, cub_ref):
    @pl.when(pl.program_id(1) == 0)
    def _():
        cga_ref[...] = wga_ref[...].astype(BF16)
        cgb_ref[...] = wgb_ref[...].astype(BF16)
        cua_ref[...] = wua_ref[...].astype(BF16)
        cub_ref[...] = wub_ref[...].astype(BF16)

    h = h_ref[...]
    ga = lax.dot_general(h, cga_ref[...], _NT_DIMS, preferred_element_type=F32)
    gb = lax.dot_general(h, cgb_ref[...], _NT_DIMS, preferred_element_type=F32)
    ya = jnp.dot(a_ref[...], cua_ref[...], preferred_element_type=F32)
    yb = jnp.dot(b_ref[...], cub_ref[...], preferred_element_type=F32)
    o_ref[...] = (jax.nn.sigmoid(ga) * ya + jax.nn.sigmoid(gb) * yb).astype(o_ref.dtype)


def _gated_merge(h1, att_a, att_b, w_t, row_ga, row_gb, w_up_a, w_up_b, tm=512, tn=256):
    n, d = h1.shape
    ka, kb = att_a.shape[1], att_b.shape[1]
    return pl.pallas_call(
        _merge_kernel,
        grid=(d // tn, n // tm),
        in_specs=[pl.BlockSpec((tm, d), lambda j, i: (i, 0)),
                  pl.BlockSpec((tm, ka), lambda j, i: (i, 0)),
                  pl.BlockSpec((tm, kb), lambda j, i: (i, 0)),
                  _w_rows_spec(tn, d, row_ga, lambda j, i: j),
                  _w_rows_spec(tn, d, row_gb, lambda j, i: j),
                  pl.BlockSpec((ka, tn), lambda j, i: (0, j)),
                  pl.BlockSpec((kb, tn), lambda j, i: (0, j))],
        out_specs=pl.BlockSpec((tm, tn), lambda j, i: (i, j)),
        out_shape=jax.ShapeDtypeStruct((n, d), BF16),
        scratch_shapes=[pltpu.VMEM((tn, d), BF16), pltpu.VMEM((tn, d), BF16),
                        pltpu.VMEM((ka, tn), BF16), pltpu.VMEM((kb, tn), BF16)],
        compiler_params=_params("arbitrary", "arbitrary"),
        name="gated_merge",
    )(h1, att_a, att_b, w_t, w_t, w_up_a, w_up_b)


def _mem_kernel(q_ref, kt_ref, v_ref, wom_ref, x_ref, o_ref):
    scale = MEM_HEAD_DIM ** -0.5
    outs = []
    for h in range(MEM_HEADS):
        sl = slice(h * MEM_HEAD_DIM, (h + 1) * MEM_HEAD_DIM)
        s = jnp.dot(q_ref[:, sl], kt_ref[sl, :], preferred_element_type=F32) * scale
        m = jnp.max(s, axis=-1, keepdims=True)
        p = jnp.exp(s - m)
        p = p / jnp.sum(p, axis=-1, keepdims=True)
        outs.append(jnp.dot(p.astype(BF16), v_ref[:, sl], preferred_element_type=F32).astype(BF16))
    o = jnp.concatenate(outs, axis=-1)
    o_ref[...] = x_ref[...] + jnp.dot(o, wom_ref[...], preferred_element_type=F32)


def _mem_attention(qm, km_t, kv_m, w_om_b, x1, bsz, s, n_mem, tm=512):
    spt = s // tm
    hd = MEM_HEADS * MEM_HEAD_DIM
    d = x1.shape[1]
    return pl.pallas_call(
        _mem_kernel,
        grid=(bsz, spt),
        in_specs=[pl.BlockSpec((tm, hd), lambda b, i: (b * spt + i, 0)),
                  pl.BlockSpec((None, hd, n_mem), lambda b, i: (b, 0, 0)),
                  pl.BlockSpec((n_mem, hd), lambda b, i: (b, 1)),
                  pl.BlockSpec((hd, d), lambda b, i: (0, 0)),
                  pl.BlockSpec((tm, d), lambda b, i: (b * spt + i, 0))],
        out_specs=pl.BlockSpec((tm, d), lambda b, i: (b * spt + i, 0)),
        out_shape=jax.ShapeDtypeStruct(x1.shape, F32),
        compiler_params=_params("arbitrary", "arbitrary"),
        name="memory_cross_attention",
    )(qm, km_t, kv_m, w_om_b, x1)


def _pack_bf16_pairs(x):
    c = x.shape[1] // 2
    lo = pltpu.bitcast(x[:, :c].astype(BF16).astype(F32), jnp.uint32)
    hi = pltpu.bitcast(x[:, c:].astype(BF16).astype(F32), jnp.uint32)
    return lax.shift_right_logical(lo, jnp.uint32(16)) | hi


def _unpack_bf16_pairs(w):
    lo = pltpu.bitcast(lax.shift_left(w, jnp.uint32(16)), F32).astype(BF16)
    hi = pltpu.bitcast(w & jnp.uint32(0xFFFF0000), F32).astype(BF16)
    return lo, hi


def _router_kernel(x_ref, g_ref, wr_ref, br_ref, h_ref, idx_ref, wt_ref, cnt_ref, carry_ref):
    @pl.when(pl.program_id(0) == 0)
    def _():
        carry_ref[...] = jnp.zeros_like(carry_ref)

    x = x_ref[...]
    hn = x * lax.rsqrt(jnp.mean(x * x, axis=-1, keepdims=True) + EPS) * g_ref[...]
    h_ref[...] = _pack_bf16_pairs(hn)
    logits = jnp.dot(hn, wr_ref[...], precision=lax.Precision.HIGHEST, preferred_element_type=F32) + br_ref[...]
    tm = x.shape[0]
    lane_i = lax.broadcasted_iota(jnp.int32, (tm, LANES), 1)
    lane = lane_i.astype(F32)
    lane_grp = lax.shift_right_logical(lane_i, EXPERTS_PER_GROUP.bit_length() - 1).astype(F32)
    big = 1e9
    is_g = (lane_i >= N_EXPERTS) & (lane_i < N_EXPERTS + N_GROUPS)
    lg = jnp.where(is_g, logits, -jnp.inf)
    mg = jnp.max(lg, axis=1, keepdims=True)
    gate_g = 1.0 / jnp.sum(jnp.exp(lg - mg), axis=1, keepdims=True)
    grp = jnp.min(jnp.where(lg == mg, lane, big), axis=1, keepdims=True) - N_EXPERTS
    in_grp = (lane_i < N_EXPERTS) & (lane_grp == grp)
    le = jnp.where(in_grp, logits, -jnp.inf)
    m1 = jnp.max(le, axis=1, keepdims=True)
    e1 = jnp.min(jnp.where(le == m1, lane, big), axis=1, keepdims=True)
    le2 = jnp.where(lane == e1, -jnp.inf, le)
    m2 = jnp.max(le2, axis=1, keepdims=True)
    e2 = jnp.min(jnp.where(le2 == m2, lane, big), axis=1, keepdims=True)
    r = jnp.exp(m2 - m1)
    w1 = gate_g * (1.0 / (1.0 + r))
    w2 = gate_g * (r / (1.0 + r))
    oh1 = (lane == e1).astype(F32)
    oh2 = (lane == e2).astype(F32)
    oh = oh1 + oh2
    tri = (lax.broadcasted_iota(jnp.int32, (tm, tm), 0) > lax.broadcasted_iota(jnp.int32, (tm, tm), 1)).astype(BF16)
    prefix = jnp.dot(tri, oh.astype(BF16), preferred_element_type=F32) + carry_ref[...]
    rank1 = jnp.sum(prefix * oh1, axis=1, keepdims=True)
    rank2 = jnp.sum(prefix * oh2, axis=1, keepdims=True)
    new_cnt = carry_ref[...] + jnp.sum(oh, axis=0, keepdims=True)
    carry_ref[...] = new_cnt
    cnt_ref[...] = new_cnt
    idx_ref[...] = jnp.where(lane_i == 0, e1, jnp.where(lane_i == 1, e2, jnp.where(
        lane_i == 2, rank1, jnp.where(lane_i == 3, rank2, 0.0)))).astype(jnp.int32)
    wt_ref[...] = jnp.where(lane_i == 0, w1, jnp.where(lane_i == 1, w2, 0.0))


def _router(x2, g, w_r, b_r, tm=256):
    n, d = x2.shape
    row = lambda w: pl.BlockSpec((tm, w), lambda i: (i, 0))
    return pl.pallas_call(
        _router_kernel,
        grid=(n // tm,),
        in_specs=[row(d), pl.BlockSpec((1, d), lambda i: (0, 0)), pl.BlockSpec((d, LANES), lambda i: (0, 0)),
                  pl.BlockSpec((1, LANES), lambda i: (0, 0))],
        out_specs=[row(d // 2), row(LANES), row(LANES), pl.BlockSpec((1, LANES), lambda i: (0, 0))],
        out_shape=[jax.ShapeDtypeStruct((n, d // 2), jnp.uint32), jax.ShapeDtypeStruct((n, LANES), jnp.int32),
                   jax.ShapeDtypeStruct((n, LANES), F32), jax.ShapeDtypeStruct((1, LANES), F32)],
        scratch_shapes=[pltpu.VMEM((1, LANES), F32)],
        compiler_params=_params("arbitrary"),
        name="moe_router",
    )(x2, g.reshape(1, d), w_r, b_r)


def _row_copy(src_hbm, row, dst, r, sem):
    return pltpu.make_async_copy(src_hbm.at[pl.ds(row, 1), :], dst.at[pl.ds(r, 1), :], sem)


def _sorted_rows(m):
    return m + N_EXPERTS * SUBLANES + ITEM_ROWS


def _dispatch_kernel(dest_ref, seg0_ref, cnt_ref, h_hbm, x_hbm, zero_ref, sem, *, tm, m):
    i = pl.program_id(0)

    @pl.when(i == 0)
    def _():
        zero_ref[...] = jnp.zeros_like(zero_ref)
        tail = [pltpu.make_async_copy(zero_ref, x_hbm.at[pl.ds(r0, ITEM_ROWS), :], sem)
                for r0 in range(m, _sorted_rows(m), ITEM_ROWS)]
        for c in tail:
            c.start()
        for c in tail:
            c.wait()

        def fill_gap(e, total):
            gap = (-cnt_ref[e]) & (SUBLANES - 1)

            def one(r, c):
                _row_copy(zero_ref, 0, x_hbm, seg0_ref[e] + cnt_ref[e] + r, sem).start()
                return c

            lax.fori_loop(0, gap, one, 0)
            return total + gap

        n_gap = lax.fori_loop(0, N_EXPERTS, fill_gap, 0)

        def wait_gap(r, c):
            _row_copy(zero_ref, 0, x_hbm, 0, sem).wait()
            return c

        lax.fori_loop(0, n_gap, wait_gap, 0)

    def issue(r, c):
        t = i * tm + r
        for sl in range(TOPK_IN_GROUP):
            _row_copy(h_hbm, t, x_hbm, dest_ref[TOPK_IN_GROUP * t + sl], sem).start()
        return c

    lax.fori_loop(0, tm, issue, 0)

    def wait(r, c):
        _row_copy(h_hbm, 0, x_hbm, 0, sem).wait()
        return c

    lax.fori_loop(0, TOPK_IN_GROUP * tm, wait, 0)


def _dispatch(dest, seg_start, counts, h3p, tm=256):
    n, c = h3p.shape
    m = dest.shape[0]
    assert (_sorted_rows(m) - m) % ITEM_ROWS == 0
    grid_spec = pltpu.PrefetchScalarGridSpec(
        num_scalar_prefetch=3,
        grid=(n // tm,),
        in_specs=[pl.BlockSpec(memory_space=pl.ANY)],
        out_specs=pl.BlockSpec(memory_space=pl.ANY),
        scratch_shapes=[pltpu.VMEM((ITEM_ROWS, c), h3p.dtype), pltpu.SemaphoreType.DMA(())],
    )
    return pl.pallas_call(
        functools.partial(_dispatch_kernel, tm=tm, m=m),
        grid_spec=grid_spec,
        out_shape=jax.ShapeDtypeStruct((_sorted_rows(m), c), h3p.dtype),
        compiler_params=_params("arbitrary"),
        name="moe_dispatch",
    )(dest, seg_start, counts, h3p)


def _expert_kernel(ie_ref, is_ref, in_ref, ni_ref, x_hbm, wg_ref, wu_ref, wd_ref, y_hbm,
                   xin_ref, xb_ref, acc_ref, in_sem, out_sem, *, m, nj):
    i = pl.program_id(0)
    j = pl.program_id(1)
    n_items = ni_ref[0]
    valid = i < n_items
    n = in_ref[i]
    slot = lax.rem(i, 2)
    half = xb_ref.shape[1] // 2
    n_sub = ITEM_ROWS // ITEM_SUB

    def first_row(item):
        return pl.multiple_of(is_ref[item], SUBLANES)

    def in_copy(item, sl):
        return pltpu.make_async_copy(x_hbm.at[pl.ds(first_row(item), ITEM_ROWS), :], xin_ref.at[sl], in_sem.at[sl])

    def out_each(item, sl, fn):
        for sub in range(n_sub):
            @pl.when(sub * ITEM_SUB < in_ref[item])
            def _():
                fn(pltpu.make_async_copy(
                    acc_ref.at[sl, pl.ds(sub * ITEM_SUB, ITEM_SUB), :],
                    y_hbm.at[pl.ds(first_row(item) + sub * ITEM_SUB, ITEM_SUB), :], out_sem.at[sl]))

    @pl.when((i == 0) & (j == 0))
    def _():
        acc_ref[1] = jnp.zeros(acc_ref.shape[1:], F32)
        tail = [pltpu.make_async_copy(acc_ref.at[1], y_hbm.at[pl.ds(r0, ITEM_ROWS), :], out_sem.at[1])
                for r0 in range(m, _sorted_rows(m), ITEM_ROWS)]
        for c in tail:
            c.start()
        for c in tail:
            c.wait()
        in_copy(0, 0).start()

    @pl.when(valid & (j == 0))
    def _():
        in_copy(i, slot).wait()

        @pl.when(i + 1 < n_items)
        def _():
            in_copy(i + 1, 1 - slot).start()

        lo, hi = _unpack_bf16_pairs(xin_ref[slot])
        xb_ref[:, :half] = lo
        xb_ref[:, half:] = hi
        acc_ref[slot] = jnp.zeros(acc_ref.shape[1:], F32)

    @pl.when(valid)
    def _():
        wg = wg_ref[...].astype(BF16)
        wu = wu_ref[...].astype(BF16)
        wd = wd_ref[...].astype(BF16)
        n_blocks = (n + ITEM_SUB - 1) // ITEM_SUB
        for nb in range(1, n_sub + 1):
            @pl.when(n_blocks == nb)
            def _():
                rows = pl.ds(0, nb * ITEM_SUB)
                xb = xb_ref[rows, :]
                g = jnp.dot(xb, wg, preferred_element_type=F32)
                u = jnp.dot(xb, wu, preferred_element_type=F32)
                hm = (g * jax.nn.sigmoid(g) * u).astype(BF16)
                acc_ref[slot, rows, :] += jnp.dot(hm, wd, preferred_element_type=F32)

    @pl.when(valid & (j == nj - 1))
    def _():
        @pl.when(i > 0)
        def _():
            out_each(i - 1, 1 - slot, lambda c: c.wait())

        out_each(i, slot, lambda c: c.start())

        @pl.when(i == n_items - 1)
        def _():
            out_each(i, slot, lambda c: c.wait())


def _experts(item_e, item_start, item_n, n_items, x_sorted, w_gate, w_up, w_down, max_items, m):
    d = w_gate.shape[1]
    nj = D_EXPERT // F_CHUNK

    def jj(i, j, ni):
        return jnp.where(i < ni[0], j, nj - 1)

    grid_spec = pltpu.PrefetchScalarGridSpec(
        num_scalar_prefetch=4,
        grid=(max_items, nj),
        in_specs=[pl.BlockSpec(memory_space=pl.ANY),
                  pl.BlockSpec((None, d, F_CHUNK), lambda i, j, ie, s, n, ni: (ie[i], 0, jj(i, j, ni))),
                  pl.BlockSpec((None, d, F_CHUNK), lambda i, j, ie, s, n, ni: (ie[i], 0, jj(i, j, ni))),
                  pl.BlockSpec((None, F_CHUNK, d), lambda i, j, ie, s, n, ni: (ie[i], jj(i, j, ni), 0))],
        out_specs=pl.BlockSpec(memory_space=pl.ANY),
        scratch_shapes=[pltpu.VMEM((2, ITEM_ROWS, d // 2), jnp.uint32), pltpu.VMEM((ITEM_ROWS, d), BF16),
                        pltpu.VMEM((2, ITEM_ROWS, d), F32),
                        pltpu.SemaphoreType.DMA((2,)), pltpu.SemaphoreType.DMA((2,))],
    )
    return pl.pallas_call(
        functools.partial(_expert_kernel, m=m, nj=nj),
        grid_spec=grid_spec,
        out_shape=jax.ShapeDtypeStruct((_sorted_rows(m), d), F32),
        compiler_params=_params("arbitrary", "arbitrary"),
        name="moe_experts",
    )(item_e, item_start, item_n, n_items, x_sorted, w_gate, w_up, w_down)


def _combine_kernel(dest_ref, y_hbm, x_ref, wt_ref, g_ref, o_ref, yv_ref, sem, *, tm):
    i = pl.program_id(0)
    slot = lax.rem(i, 2)

    def issue_tile(tile, sl):
        def issue(r, c):
            base = TOPK_IN_GROUP * (tile * tm + r)
            for k in range(TOPK_IN_GROUP):
                _row_copy(y_hbm, dest_ref[base + k], yv_ref.at[sl, k], r, sem.at[sl]).start()
            return c

        lax.fori_loop(0, tm, issue, 0)

    @pl.when(i == 0)
    def _():
        issue_tile(0, 0)

    @pl.when(i + 1 < pl.num_programs(0))
    def _():
        issue_tile(i + 1, 1 - slot)

    def wait(r, c):
        _row_copy(y_hbm, 0, yv_ref.at[slot, 0], 0, sem.at[slot]).wait()
        return c

    lax.fori_loop(0, TOPK_IN_GROUP * tm, wait, 0)
    w = wt_ref[...]
    moe = yv_ref[slot, 0] * w[:, 0:1]
    for k in range(1, TOPK_IN_GROUP):
        moe = moe + yv_ref[slot, k] * w[:, k:k + 1]
    x = x_ref[...] + moe
    o_ref[...] = x * lax.rsqrt(jnp.mean(x * x, axis=-1, keepdims=True) + EPS) * g_ref[...]


def _combine(dest, y_sorted, x2, wts, g_final, tm=128):
    n, d = x2.shape
    grid_spec = pltpu.PrefetchScalarGridSpec(
        num_scalar_prefetch=1,
        grid=(n // tm,),
        in_specs=[pl.BlockSpec(memory_space=pl.ANY),
                  pl.BlockSpec((tm, d), lambda i, dr: (i, 0)),
                  pl.BlockSpec((tm, LANES), lambda i, dr: (i, 0)),
                  pl.BlockSpec((1, d), lambda i, dr: (0, 0))],
        out_specs=pl.BlockSpec((tm, d), lambda i, dr: (i, 0)),
        scratch_shapes=[pltpu.VMEM((2, TOPK_IN_GROUP, tm, d), F32), pltpu.SemaphoreType.DMA((2,))],
    )
    return pl.pallas_call(
        functools.partial(_combine_kernel, tm=tm),
        grid_spec=grid_spec,
        out_shape=jax.ShapeDtypeStruct((n, d), F32),
        compiler_params=_params("arbitrary"),
        name="moe_combine_final_norm",
    )(dest, y_sorted, x2, wts, g_final.reshape(1, d))


def _pad_cols(w, width):
    return jnp.pad(w, ((0, 0), (0, width - w.shape[1])))


def _mixer(x2d, bsz, s, g_norm, w_in, b_f, g_q_lat, g_kv_lat, g_idx_k, b_idx_k, w_uq, w_idx_q, w_uk, w_uv,
           w_up_a, w_up_b, w_out):
    d = x2d.shape[1]
    topk = min(DSA_TOPK_MAX, s // 4)
    h1 = _rmsnorm(x2d, g_norm, BF16)
    c_small = 3 * FOX_WIDTH
    o_fa, o_cq = c_small, c_small + FOX_HEADS
    o_ckv = o_cq + Q_LORA
    o_kr = o_ckv + KV_LORA
    o_ik = o_kr + DSA_ROPE_DIM
    o_iw = o_ik + IDX_DIM
    o_ga = o_iw + IDX_HEADS
    o_gb = o_ga + d
    w_t = jnp.swapaxes(w_in, 0, 1)
    pad_rows = lambda w: jnp.pad(w, ((0, LANES - w.shape[0]), (0, 0)))
    w_small_t = jnp.concatenate([
        w_t[o_cq:o_ckv], w_t[o_ckv:o_kr], w_t[o_ik:o_iw],
        pad_rows(w_t[o_kr:o_ik]), pad_rows(w_t[o_fa:o_cq]), pad_rows(w_t[o_iw:o_ga])], axis=0)

    qkv = _mm(h1, w_t, 3 * FOX_WIDTH, w_rows=True, name="proj_qkv")
    small = _mm(h1, w_small_t, SMALL_W, out_dtype=F32, tn=SMALL_W // 7, w_rows=True, name="proj_small")

    tabs64 = _rope_tables(s, DSA_ROPE_DIM)
    tabs128 = _rope_tables(s, IDX_DIM)
    cq_n, kv_cat, k_idx, w_idx, fa = _prep(small, g_q_lat, g_kv_lat, g_idx_k, b_idx_k, tabs64, tabs128, s)

    cum = _fox_cum(fa, _pad_cols(b_f.reshape(1, -1), LANES), bsz, s)
    tk = 512
    cum_t = jnp.transpose(cum[:, :, :FOX_HEADS], (0, 2, 1)).reshape(bsz * FOX_HEADS, s // tk, tk)
    att_a = _fox_attention(qkv, cum, cum_t, bsz, s, t=tk)

    w_uq3 = w_uq.reshape(Q_LORA, DSA_HEADS, DSA_NOPE_DIM + DSA_ROPE_DIM)
    w_uq_nope = w_uq3[:, :, :DSA_NOPE_DIM].reshape(Q_LORA, DSA_HEADS * DSA_NOPE_DIM)
    w_uq_rope = jnp.pad(w_uq3[:, :, DSA_NOPE_DIM:], ((0, 0), (0, 0), (0, LANES - DSA_ROPE_DIM))).reshape(
        Q_LORA, DSA_HEADS * LANES)
    q_nope = _mm(cq_n, w_uq_nope, DSA_HEADS * DSA_NOPE_DIM, name="proj_q_nope")
    q_rope = _mm(cq_n, w_uq_rope, DSA_HEADS * LANES, name="proj_q_rope")
    q_cat = _q_absorb(q_nope, q_rope, w_uk, tabs64, bsz, s)
    q_idx = _mm(cq_n, w_idx_q, IDX_HEADS * IDX_DIM, heads=(bsz, s), rope_tabs=tabs128, name="proj_q_idx")
    w_idx_t = jnp.transpose(w_idx.reshape(bsz, s, LANES)[:, :, :IDX_HEADS], (0, 2, 1))
    bias = _indexer_mask(q_idx, k_idx.reshape(bsz, s, IDX_DIM), w_idx_t, bsz, s, topk)
    kv3 = kv_cat.reshape(bsz, s, KV_PAD)
    kv_t = jnp.transpose(kv3.reshape(bsz, s // KEY_CHUNK, KEY_CHUNK, KV_PAD), (0, 1, 3, 2))
    o_lat = _dsa_attention(q_cat, kv_t, kv3, bias, bsz, s)
    att_b = _uv_project(o_lat, w_uv, bsz, s)

    merged = _gated_merge(h1, att_a, att_b, w_t, o_ga, o_gb, w_up_a, w_up_b)
    return _mm(merged, w_out, d, out_dtype=F32, res=x2d, name="proj_out")


def _memory_layer(x1, mem, bsz, s, g_x, g_m, w_qm, w_km, w_vm, w_om):
    n_mem = mem.shape[1]
    d = x1.shape[1]
    hd = MEM_HEADS * MEM_HEAD_DIM
    h2 = _rmsnorm(x1, g_x, BF16)
    mem_n = _rmsnorm(mem.reshape(bsz * n_mem, d), g_m, BF16)
    qm = _mm(h2, w_qm, hd, name="proj_q_mem")
    kv_m = _mm(mem_n, jnp.concatenate([w_km, w_vm], axis=1), 2 * hd, tm=bsz * n_mem, name="proj_kv_mem")
    km_t = jnp.transpose(kv_m[:, :hd].reshape(bsz, n_mem, hd), (0, 2, 1))
    return _mem_attention(qm, km_t, kv_m, w_om.astype(BF16), x1, bsz, s, n_mem)


def _moe_layer(x2, g_ffn, w_rg, b_rg, w_re, b_re, w_gate, w_up, w_down, g_final):
    n, d = x2.shape
    m = n * TOPK_IN_GROUP
    w_r = _pad_cols(jnp.concatenate([w_re, w_rg], axis=1), LANES)
    b_r = _pad_cols(jnp.concatenate([b_re, b_rg]).reshape(1, -1), LANES)
    h3p, ridx, wts, cnt = _router(x2, g_ffn, w_r, b_r)
    counts = cnt[0, :N_EXPERTS].astype(jnp.int32)
    e_flat = ridx[:, :TOPK_IN_GROUP].reshape(m)
    rank_flat = ridx[:, TOPK_IN_GROUP:2 * TOPK_IN_GROUP].reshape(m)
    seg = (counts + SUBLANES - 1) // SUBLANES * SUBLANES
    starts = (jnp.cumsum(seg) - seg).astype(jnp.int32)
    n_it = (counts + ITEM_ROWS - 1) // ITEM_ROWS
    it_end = jnp.cumsum(n_it)
    it_first = it_end - n_it
    max_items = m // ITEM_ROWS + N_EXPERTS
    t = jnp.arange(max_items, dtype=jnp.int32)
    item_e = jnp.minimum(jnp.searchsorted(it_end, t, side='right'), N_EXPERTS - 1).astype(jnp.int32)
    k_in = t - it_first[item_e]
    item_start = (starts[item_e] + k_in * ITEM_ROWS).astype(jnp.int32)
    item_n = jnp.clip(counts[item_e] - k_in * ITEM_ROWS, 0, ITEM_ROWS).astype(jnp.int32)
    n_items = it_end[-1:].astype(jnp.int32)
    last = jnp.maximum(n_items[0] - 1, 0)
    item_e = jnp.where(t < n_items[0], item_e, item_e[last])
    item_start = jnp.where(t < n_items[0], item_start, 0)
    item_n = jnp.where(t < n_items[0], item_n, 0)
    dest = (starts[e_flat] + rank_flat).astype(jnp.int32)
    x_sorted = _dispatch(dest, starts, counts, h3p)
    y_sorted = _experts(item_e, item_start, item_n, n_items, x_sorted, w_gate, w_up, w_down, max_items, m)
    return _combine(dest, y_sorted, x2, wts, g_final)


def kernel(x, mem, g_norm_mix, w_in, b_f, g_q_lat, g_kv_lat, g_idx_k, b_idx_k, w_uq, w_idx_q, w_uk, w_uv,
           w_up_a, w_up_b, w_out, g_norm_mem_x, g_mem, w_qm, w_km, w_vm, w_om, g_norm_ffn,
           w_rg, b_rg, w_re, b_re, w_gate, w_up, w_down, g_final):
    bsz, s, d = x.shape
    depth = w_in.shape[0]
    assert depth == 1, "the final norm is fused into the last MoE combine"
    l = 0
    x2d = x.reshape(bsz * s, d)
    x1 = _mixer(x2d, bsz, s, g_norm_mix[l], w_in[l], b_f[l], g_q_lat[l], g_kv_lat[l], g_idx_k[l], b_idx_k[l],
                w_uq[l], w_idx_q[l], w_uk[l], w_uv[l], w_up_a[l], w_up_b[l], w_out[l])
    x2 = _memory_layer(x1, mem, bsz, s, g_norm_mem_x[l], g_mem[l], w_qm[l], w_km[l], w_vm[l], w_om[l])
    out = _moe_layer(x2, g_norm_ffn[l], w_rg[l], b_rg[l], w_re[l], b_re[l], w_gate[l], w_up[l], w_down[l], g_final)
    return out.reshape(bsz, s, d)
```

```python
import functools

import jax
import jax.numpy as jnp
import numpy as np
from jax import lax
from jax.experimental import pallas as pl
from jax.experimental.pallas import tpu as pltpu

F32 = jnp.float32
BF16 = jnp.bfloat16

EPS = 1e-6
ROPE_THETA = 10000.0
FOX_HEADS = 16
FOX_HEAD_DIM = 128
DSA_HEADS = 16
DSA_NOPE_DIM = 128
DSA_ROPE_DIM = 64
DSA_V_DIM = 128
Q_LORA = 1024
KV_LORA = 256
IDX_HEADS = 32
IDX_DIM = 128
DSA_TOPK_MAX = 256
MEM_HEADS = 4
MEM_HEAD_DIM = 128
N_GROUPS = 8
EXPERTS_PER_GROUP = 8
N_EXPERTS = N_GROUPS * EXPERTS_PER_GROUP
TOPK_IN_GROUP = 2
D_EXPERT = 512

FOX_WIDTH = FOX_HEADS * FOX_HEAD_DIM
LANES = 128
SUBLANES = 8
KV_PAD = KV_LORA + LANES
NEG_BIG = -1e30
VMEM_LIMIT_BYTES = 56 * 1024 * 1024
ITEM_ROWS = 384
ITEM_SUB = 128
F_CHUNK = 256
KEY_CHUNK = 256


def _params(*sem):
    return pltpu.CompilerParams(dimension_semantics=sem, vmem_limit_bytes=VMEM_LIMIT_BYTES)


def _rmsnorm_kernel(x_ref, g_ref, o_ref):
    x = x_ref[...].astype(F32)
    ms = jnp.mean(x * x, axis=-1, keepdims=True)
    o_ref[...] = (x * lax.rsqrt(ms + EPS) * g_ref[...]).astype(o_ref.dtype)


def _rmsnorm(x2d, g, out_dtype, tm=256):
    n, d = x2d.shape
    return pl.pallas_call(
        _rmsnorm_kernel,
        grid=(n // tm,),
        in_specs=[pl.BlockSpec((tm, d), lambda i: (i, 0)), pl.BlockSpec((1, d), lambda i: (0, 0))],
        out_specs=pl.BlockSpec((tm, d), lambda i: (i, 0)),
        out_shape=jax.ShapeDtypeStruct((n, d), out_dtype),
        compiler_params=_params("arbitrary"),
        name="rmsnorm",
    )(x2d, g.reshape(1, d).astype(F32))


_NT_DIMS = (((1,), (1,)), ((), ()))


def _w_rows_spec(tn, k, row0, index_of_j):
    if row0 % tn == 0:
        return pl.BlockSpec((tn, k), lambda *g: (index_of_j(*g) + row0 // tn, 0))
    assert row0 % SUBLANES == 0
    return pl.BlockSpec((pl.Element(tn), pl.Element(k)),
                        lambda *g: (pl.multiple_of(index_of_j(*g) * tn + row0, SUBLANES), 0))


def _mm_kernel(a_ref, b_ref, *rest, has_res, head_out, rope, w_rows):
    pos = 0
    res_ref = cos_ref = sin_ref = None
    if has_res:
        res_ref = rest[pos]
        pos += 1
    if rope:
        cos_ref, sin_ref = rest[pos], rest[pos + 1]
        pos += 2
    o_ref = rest[pos]
    wb_ref = rest[pos + 1]

    @pl.when(pl.program_id(1) == 0)
    def _():
        wb_ref[...] = b_ref[...].astype(BF16)

    if w_rows:
        acc = lax.dot_general(a_ref[...], wb_ref[...], _NT_DIMS, preferred_element_type=F32)
    else:
        acc = jnp.dot(a_ref[...], wb_ref[...], preferred_element_type=F32)
    if has_res:
        acc = acc + res_ref[...]
    if head_out:
        for hh in range(acc.shape[1] // LANES):
            part = acc[:, hh * LANES:(hh + 1) * LANES]
            if rope:
                part = part * cos_ref[...] + pltpu.roll(part, LANES // 2, 1) * sin_ref[...]
            o_ref[hh] = part.astype(o_ref.dtype)
    else:
        o_ref[...] = acc.astype(o_ref.dtype)


def _mm(a, w, n_cols, *, col0=0, out_dtype=BF16, tm=1024, tn=512, res=None, heads=None, rope_tabs=None,
        w_rows=False, name="mm"):
    m, k = a.shape
    tm = min(tm, m if heads is None else heads[1])
    assert n_cols % tn == 0 and m % tm == 0
    if w_rows:
        w_spec = _w_rows_spec(tn, k, col0, lambda j, i: j)
    else:
        assert col0 % tn == 0
        w_spec = pl.BlockSpec((k, tn), lambda j, i: (0, j + col0 // tn))
    in_specs = [pl.BlockSpec((tm, k), lambda j, i: (i, 0)), w_spec]
    args = [a, w]
    if res is not None:
        in_specs.append(pl.BlockSpec((tm, tn), lambda j, i: (i, j)))
        args.append(res)
    if heads is not None:
        bsz, s = heads
        assert s % tm == 0
        spt = s // tm
        if rope_tabs is not None:
            in_specs += [pl.BlockSpec((tm, LANES), lambda j, i: (i % spt, 0))] * 2
            args += list(rope_tabs)
        hpt = tn // LANES
        out_spec = pl.BlockSpec((None, hpt, tm, LANES), lambda j, i: (i // spt, j, i % spt, 0))
        out_shape = jax.ShapeDtypeStruct((bsz, n_cols // LANES, s, LANES), out_dtype)
    else:
        out_spec = pl.BlockSpec((tm, tn), lambda j, i: (i, j))
        out_shape = jax.ShapeDtypeStruct((m, n_cols), out_dtype)
    kern = functools.partial(_mm_kernel, has_res=res is not None, head_out=heads is not None,
                             rope=rope_tabs is not None, w_rows=w_rows)
    return pl.pallas_call(
        kern,
        grid=(n_cols // tn, m // tm),
        in_specs=in_specs,
        out_specs=out_spec,
        out_shape=out_shape,
        scratch_shapes=[pltpu.VMEM((tn, k) if w_rows else (k, tn), BF16)],
        compiler_params=_params("arbitrary", "arbitrary"),
        name=name,
    )(*args)


def _rope_tables(s, d):
    inv = jnp.power(ROPE_THETA, -jnp.arange(0, d, 2, dtype=F32) / d)
    ang = jnp.arange(s, dtype=jnp.int32).astype(F32)[:, None] * inv[None, :]
    cos, sin = jnp.cos(ang), jnp.sin(ang)
    cos_t = jnp.tile(jnp.concatenate([cos, cos], axis=-1), (1, LANES // d))
    sin_t = jnp.tile(jnp.concatenate([-sin, sin], axis=-1), (1, LANES // d))
    return cos_t, sin_t


def _rot_half64(x):
    lane = lax.broadcasted_iota(jnp.int32, x.shape, 1)
    return jnp.where((lane % 64) < 32, pltpu.roll(x, 96, 1), pltpu.roll(x, 32, 1))


_C_CQ = 0
_C_CKV = Q_LORA
_C_IK = _C_CKV + KV_LORA
_C_KR = _C_IK + IDX_DIM
_C_FA = _C_KR + LANES
_C_IW = _C_FA + LANES
SMALL_W = _C_IW + LANES


def _prep_kernel(s_ref, gq_ref, gkv_ref, gik_ref, bik_ref, c64_ref, s64_ref, c128_ref, s128_ref,
                 cq_ref, kv_ref, kidx_ref, widx_ref, fa_ref):
    cq = s_ref[:, _C_CQ:_C_CQ + Q_LORA]
    cq_ref[...] = (cq * lax.rsqrt(jnp.mean(cq * cq, axis=-1, keepdims=True) + EPS) * gq_ref[...]).astype(BF16)
    ckv = s_ref[:, _C_CKV:_C_CKV + KV_LORA]
    kv_ref[:, :KV_LORA] = (ckv * lax.rsqrt(jnp.mean(ckv * ckv, axis=-1, keepdims=True) + EPS)
                           * gkv_ref[...]).astype(BF16)
    kr = s_ref[:, _C_KR:_C_KR + LANES]
    kv_ref[:, KV_LORA:] = (kr * c64_ref[...] + _rot_half64(kr) * s64_ref[...]).astype(BF16)
    ik = s_ref[:, _C_IK:_C_IK + IDX_DIM]
    mu = jnp.mean(ik, axis=-1, keepdims=True)
    xc = ik - mu
    ikn = xc * lax.rsqrt(jnp.mean(xc * xc, axis=-1, keepdims=True) + EPS) * gik_ref[...] + bik_ref[...]
    kidx_ref[...] = (ikn * c128_ref[...] + pltpu.roll(ikn, LANES // 2, 1) * s128_ref[...]).astype(BF16)
    widx_ref[...] = s_ref[:, _C_IW:_C_IW + LANES] * (IDX_HEADS * IDX_DIM) ** -0.5
    fa_ref[...] = s_ref[:, _C_FA:_C_FA + LANES]


def _prep(small, g_q, g_kv, g_ik, b_ik, tabs64, tabs128, s, tm=256):
    n = small.shape[0]
    spt = s // tm
    row = lambda w: pl.BlockSpec((tm, w), lambda i: (i, 0))
    vec = lambda w: pl.BlockSpec((1, w), lambda i: (0, 0))
    tab = pl.BlockSpec((tm, LANES), lambda i: (i % spt, 0))
    return pl.pallas_call(
        _prep_kernel,
        grid=(n // tm,),
        in_specs=[row(SMALL_W), vec(Q_LORA), vec(KV_LORA), vec(IDX_DIM), vec(IDX_DIM), tab, tab, tab, tab],
        out_specs=[row(Q_LORA), row(KV_PAD), row(IDX_DIM), row(LANES), row(LANES)],
        out_shape=[jax.ShapeDtypeStruct((n, Q_LORA), BF16), jax.ShapeDtypeStruct((n, KV_PAD), BF16),
                   jax.ShapeDtypeStruct((n, IDX_DIM), BF16), jax.ShapeDtypeStruct((n, LANES), F32),
                   jax.ShapeDtypeStruct((n, LANES), F32)],
        compiler_params=_params("arbitrary"),
        name="mixer_prep",
    )(small, g_q.reshape(1, -1), g_kv.reshape(1, -1), g_ik.reshape(1, -1), b_ik.reshape(1, -1),
      tabs64[0], tabs64[1], tabs128[0], tabs128[1])


def _cum_kernel(fa_ref, bf_ref, cum_ref, carry_ref):
    @pl.when(pl.program_id(1) == 0)
    def _():
        carry_ref[...] = jnp.zeros_like(carry_ref)

    z = fa_ref[...] + bf_ref[...]
    logf = jnp.minimum(z, 0.0) - jnp.log1p(jnp.exp(-jnp.abs(z)))
    t = z.shape[0]
    tri = (lax.broadcasted_iota(jnp.int32, (t, t), 0) >= lax.broadcasted_iota(jnp.int32, (t, t), 1)).astype(F32)
    c = jnp.dot(tri, logf, precision=lax.Precision.HIGHEST, preferred_element_type=F32) + carry_ref[...]
    cum_ref[...] = c
    carry_ref[...] = c[t - 1:t, :]


def _fox_cum(fa, b_f_pad, bsz, s, t=256):
    return pl.pallas_call(
        _cum_kernel,
        grid=(bsz, s // t),
        in_specs=[pl.BlockSpec((None, t, LANES), lambda b, i: (b, i, 0)),
                  pl.BlockSpec((1, LANES), lambda b, i: (0, 0))],
        out_specs=pl.BlockSpec((None, t, LANES), lambda b, i: (b, i, 0)),
        out_shape=jax.ShapeDtypeStruct((bsz, s, LANES), F32),
        scratch_shapes=[pltpu.VMEM((1, LANES), F32)],
        compiler_params=_params("arbitrary", "arbitrary"),
        name="fox_cumsum",
    )(fa.reshape(bsz, s, LANES), b_f_pad)


def _lane_fold(x, op):
    out = x[:, :LANES]
    for c in range(1, x.shape[1] // LANES):
        out = op(out, x[:, c * LANES:(c + 1) * LANES])
    return out


def _fox_kernel(q_ref, k_ref, v_ref, cum_ref, cumt_ref, o_ref, s_ref, *, t):
    h = pl.program_id(1)
    i = pl.program_id(2)
    q = q_ref[...]
    lane = lax.broadcasted_iota(jnp.int32, (t, LANES), 1)
    cq = jnp.sum(jnp.where(lane == h, cum_ref[...], 0.0), axis=1, keepdims=True)
    scale = FOX_HEAD_DIM ** -0.5

    def logits(kc):
        kb = k_ref[pl.ds(pl.multiple_of(kc * t, t), t), :]
        s = lax.dot_general(q, kb, (((1,), (1,)), ((), ())), preferred_element_type=F32) * scale
        return s + cq - cumt_ref[pl.ds(kc, 1), :]

    def pass1(kc, m_lane):
        s = logits(kc)
        s_ref[kc] = s
        return jnp.maximum(m_lane, _lane_fold(s, jnp.maximum))

    m_lane = lax.fori_loop(0, i, pass1, jnp.full((t, LANES), NEG_BIG, F32))
    causal = lax.broadcasted_iota(jnp.int32, (t, t), 0) >= lax.broadcasted_iota(jnp.int32, (t, t), 1)
    s = jnp.where(causal, logits(i), NEG_BIG)
    s_ref[i] = s
    m_lane = jnp.maximum(m_lane, _lane_fold(s, jnp.maximum))
    m = jnp.max(m_lane, axis=1, keepdims=True)

    def pass2(kc, carry):
        l_lane, acc = carry
        p = jnp.exp(s_ref[kc] - m)
        vb = v_ref[pl.ds(pl.multiple_of(kc * t, t), t), :]
        return (l_lane + _lane_fold(p, jnp.add),
                acc + jnp.dot(p.astype(BF16), vb, preferred_element_type=F32))

    l_lane, acc = lax.fori_loop(0, i + 1, pass2, (jnp.zeros((t, LANES), F32),
                                                  jnp.zeros((t, FOX_HEAD_DIM), F32)))
    o_ref[...] = (acc / jnp.sum(l_lane, axis=1, keepdims=True)).astype(o_ref.dtype)


def _fox_attention(qkv, cum, cum_t, bsz, s, t=512):
    tq = tk = t
    n = bsz * s
    nq = s // tq
    hh = FOX_HEADS
    kern = functools.partial(_fox_kernel, t=t)
    return pl.pallas_call(
        kern,
        grid=(bsz, hh, nq),
        in_specs=[pl.BlockSpec((tq, LANES), lambda b, h, i: (b * nq + i, h)),
                  pl.BlockSpec((s, LANES), lambda b, h, i: (b, hh + h)),
                  pl.BlockSpec((s, LANES), lambda b, h, i: (b, 2 * hh + h)),
                  pl.BlockSpec((None, tq, LANES), lambda b, h, i: (b, i, 0)),
                  pl.BlockSpec((None, s // tk, tk), lambda b, h, i: (b * hh + h, 0, 0))],
        out_specs=pl.BlockSpec((tq, LANES), lambda b, h, i: (b * nq + i, h)),
        out_shape=jax.ShapeDtypeStruct((n, FOX_WIDTH), BF16),
        scratch_shapes=[pltpu.VMEM((s // tk, tq, tk), F32)],
        compiler_params=_params("arbitrary", "arbitrary", "arbitrary"),
        name="fox_attention",
    )(qkv, qkv, qkv, cum, cum_t)


def _qlat_kernel(cq_ref, wn_ref, wr_ref, wuk_ref, c64_ref, s64_ref, o_ref):
    cq = cq_ref[...]
    qn = jnp.dot(cq, wn_ref[...], preferred_element_type=F32).astype(BF16)
    qr = jnp.dot(cq, wr_ref[...], preferred_element_type=F32)
    for h in range(DSA_HEADS):
        cols = slice(h * LANES, (h + 1) * LANES)
        o_ref[h, :, :KV_LORA] = jnp.dot(qn[:, cols], wuk_ref[h], preferred_element_type=F32).astype(BF16)
        r = qr[:, cols]
        o_ref[h, :, KV_LORA:] = (r * c64_ref[...] + _rot_half64(r) * s64_ref[...]).astype(BF16)


def _q_latent(cq_n, w_nope_b, w_rope_b, w_uk_b, tabs64, bsz, s, tm=256):
    spt = s // tm
    kq = cq_n.shape[1]
    wide = DSA_HEADS * LANES
    return pl.pallas_call(
        _qlat_kernel,
        grid=(bsz * spt,),
        in_specs=[pl.BlockSpec((tm, kq), lambda i: (i, 0)),
                  pl.BlockSpec((kq, wide), lambda i: (0, 0)),
                  pl.BlockSpec((kq, wide), lambda i: (0, 0)),
                  pl.BlockSpec((DSA_HEADS, DSA_NOPE_DIM, KV_LORA), lambda i: (0, 0, 0)),
                  pl.BlockSpec((tm, LANES), lambda i: (i % spt, 0)),
                  pl.BlockSpec((tm, LANES), lambda i: (i % spt, 0))],
        out_specs=pl.BlockSpec((None, DSA_HEADS, tm, KV_PAD), lambda i: (i // spt, 0, i % spt, 0)),
        out_shape=jax.ShapeDtypeStruct((bsz, DSA_HEADS, s, KV_PAD), BF16),
        compiler_params=_params("arbitrary"),
        name="q_latent",
    )(cq_n, w_nope_b, w_rope_b, w_uk_b, tabs64[0], tabs64[1])


def _needed_chunks(i, tq):
    return ((i + 1) * tq + KEY_CHUNK - 1) // KEY_CHUNK


_NEG_INF_KEY = int(np.array(-np.inf, np.float32).view(np.int32) ^ np.int32(0x7FFFFFFF))


def _indexer_kernel(q_ref, k_ref, wt_ref, o_ref, key_ref, thr_ref, *, tq, topk, hg):
    i = pl.program_id(1)
    tk = KEY_CHUNK
    n_chunks = key_ref.shape[0]
    nk = _needed_chunks(i, tq)
    qpos = i * tq + lax.broadcasted_iota(jnp.int32, (tk, tq), 1)
    krow = lax.broadcasted_iota(jnp.int32, (tk, tq), 0)

    def score_chunk(kc, c):
        k0 = pl.multiple_of(kc * tk, tk)
        kb = k_ref[pl.ds(k0, tk), :]
        sc = jnp.zeros((tk, tq), F32)
        for g in range(IDX_HEADS // hg):
            q2 = q_ref[g * hg:(g + 1) * hg].reshape(hg * tq, IDX_DIM)
            d = lax.dot_general(kb, q2, _NT_DIMS, preferred_element_type=F32)
            for hh in range(hg):
                h = g * hg + hh
                sc = sc + wt_ref[h:h + 1, :] * jnp.maximum(d[:, hh * tq:(hh + 1) * tq], 0.0)
        sc = jnp.where(sc == 0.0, 0.0, sc)
        sc = jnp.where(k0 + krow <= qpos, sc, -jnp.inf)
        bits = pltpu.bitcast(sc, jnp.int32)
        key_ref[kc] = jnp.where(bits < 0, bits ^ jnp.int32(0x7FFFFFFF), bits)
        return c

    lax.fori_loop(0, nk, score_chunk, 0)

    def pad_chunk(kc, c):
        key_ref[kc] = jnp.full((tk, tq), _NEG_INF_KEY, jnp.int32)
        return c

    lax.fori_loop(nk, n_chunks, pad_chunk, 0)
    kf = float(topk)

    def bisection(n_counted):
        def count_ge(t):
            acc = jnp.zeros((SUBLANES, tq), F32)
            for kc in range(n_counted):
                ind = jnp.where(key_ref[kc] >= t, 1.0, 0.0)
                acc = acc + jnp.sum(ind.reshape(tk // SUBLANES, SUBLANES, tq), axis=0)
            return jnp.sum(acc, axis=0, keepdims=True)

        int_min = jnp.int32(-2 ** 31)
        t0 = jnp.where(count_ge(jnp.zeros((1, tq), jnp.int32)) >= kf, jnp.int32(0), int_min)

        def bisect(it, t):
            cand = t | lax.shift_left(jnp.int32(1), jnp.int32(30) - it)
            return jnp.where(count_ge(cand) >= kf, cand, t)

        thr_ref[...] = lax.fori_loop(0, 31, bisect, t0)

    half = max(n_chunks // 2, 1)
    if half < n_chunks:
        pl.when(nk <= half)(lambda: bisection(half))
        pl.when(nk > half)(lambda: bisection(n_chunks))
    else:
        bisection(n_chunks)
    thr = thr_ref[...]

    def write_chunk(kc, c):
        keep = (key_ref[kc] >= thr) & (kc * tk + krow <= qpos)
        o_ref[kc] = jnp.where(keep, 0.0, NEG_BIG).T.astype(o_ref.dtype)
        return c

    lax.fori_loop(0, nk, write_chunk, 0)

    def fill_chunk(kc, c):
        o_ref[kc] = jnp.full((tq, tk), NEG_BIG, o_ref.dtype)
        return c

    lax.fori_loop(nk, n_chunks, fill_chunk, 0)


def _indexer_mask(q_idx, k_idx, w_idx_t, bsz, s, topk, tq=128, hg=4):
    kern = functools.partial(_indexer_kernel, tq=tq, topk=topk, hg=hg)
    nc = s // KEY_CHUNK
    return pl.pallas_call(
        kern,
        grid=(bsz, s // tq),
        in_specs=[pl.BlockSpec((None, IDX_HEADS, tq, IDX_DIM), lambda b, i: (b, 0, i, 0)),
                  pl.BlockSpec((None, s, IDX_DIM), lambda b, i: (b, 0, 0)),
                  pl.BlockSpec((None, IDX_HEADS, tq), lambda b, i: (b, 0, i))],
        out_specs=pl.BlockSpec((None, nc, tq, KEY_CHUNK), lambda b, i: (b, 0, i, 0)),
        out_shape=jax.ShapeDtypeStruct((bsz, nc, s, KEY_CHUNK), BF16),
        scratch_shapes=[pltpu.VMEM((nc, KEY_CHUNK, tq), jnp.int32), pltpu.VMEM((1, tq), jnp.int32)],
        compiler_params=_params("arbitrary", "arbitrary"),
        name="indexer_topk_mask",
    )(q_idx, k_idx, w_idx_t)


def _dsa_kernel(q_ref, kvt_ref, kv_ref, bias_ref, o_ref, s_ref, m_ref, l_ref, acc_ref, *, tq):
    i = pl.program_id(1)
    tk = KEY_CHUNK
    rows = DSA_HEADS * tq
    nk = _needed_chunks(i, tq)
    q2 = q_ref[...].reshape(rows, KV_PAD)
    scale = (DSA_NOPE_DIM + DSA_ROPE_DIM) ** -0.5
    m_ref[...] = jnp.full(m_ref.shape, NEG_BIG, F32)

    def pass1(kc, c):
        sc = jnp.dot(q2, kvt_ref[kc], preferred_element_type=F32) * scale
        sc = (sc.reshape(DSA_HEADS, tq, tk) + bias_ref[kc].astype(F32)[None]).reshape(rows, tk)
        s_ref[kc] = sc
        m_ref[...] = jnp.maximum(m_ref[...], _lane_fold(sc, jnp.maximum))
        return c

    lax.fori_loop(0, nk, pass1, 0)
    m_ref[...] = jnp.broadcast_to(jnp.max(m_ref[...], axis=1, keepdims=True), m_ref.shape)
    l_ref[...] = jnp.zeros(l_ref.shape, F32)
    acc_ref[...] = jnp.zeros(acc_ref.shape, F32)

    def pass2(kc, c):
        m = m_ref[...]
        p = jnp.concatenate([jnp.exp(s_ref[kc, :, g * LANES:(g + 1) * LANES] - m)
                             for g in range(tk // LANES)], axis=1)
        l_ref[...] += _lane_fold(p, jnp.add)
        kvb = kv_ref[pl.ds(pl.multiple_of(kc * tk, tk), tk), :KV_LORA]
        acc_ref[...] += jnp.dot(p.astype(BF16), kvb, preferred_element_type=F32)
        return c

    lax.fori_loop(0, nk, pass2, 0)
    o = acc_ref[...] / jnp.sum(l_ref[...], axis=1, keepdims=True)
    o_ref[...] = o.reshape(DSA_HEADS, tq, KV_LORA).astype(o_ref.dtype)


def _dsa_attention(q_cat, kv_t, kv, bias, bsz, s, tq=64):
    kern = functools.partial(_dsa_kernel, tq=tq)
    nc = s // KEY_CHUNK
    rows = DSA_HEADS * tq
    return pl.pallas_call(
        kern,
        grid=(bsz, s // tq),
        in_specs=[pl.BlockSpec((None, DSA_HEADS, tq, KV_PAD), lambda b, i: (b, 0, i, 0)),
                  pl.BlockSpec((None, nc, KV_PAD, KEY_CHUNK), lambda b, i: (b, 0, 0, 0)),
                  pl.BlockSpec((None, s, KV_PAD), lambda b, i: (b, 0, 0)),
                  pl.BlockSpec((None, nc, tq, KEY_CHUNK), lambda b, i: (b, 0, i, 0))],
        out_specs=pl.BlockSpec((None, DSA_HEADS, tq, KV_LORA), lambda b, i: (b, 0, i, 0)),
        out_shape=jax.ShapeDtypeStruct((bsz, DSA_HEADS, s, KV_LORA), BF16),
        scratch_shapes=[pltpu.VMEM((nc, rows, KEY_CHUNK), F32), pltpu.VMEM((rows, LANES), F32),
                        pltpu.VMEM((rows, LANES), F32), pltpu.VMEM((rows, KV_LORA), F32)],
        compiler_params=_params("arbitrary", "arbitrary"),
        name="dsa_attention",
    )(q_cat, kv_t, kv, bias)


def _uv_kernel(o_ref, w_ref, y_ref):
    for h in range(DSA_HEADS):
        y_ref[:, h * DSA_V_DIM:(h + 1) * DSA_V_DIM] = jnp.dot(
            o_ref[h], w_ref[h], preferred_element_type=F32).astype(y_ref.dtype)


def _uv_project(o_lat, w_uv_b, bsz, s, tm=512):
    spt = s // tm
    return pl.pallas_call(
        _uv_kernel,
        grid=(bsz * spt,),
        in_specs=[pl.BlockSpec((None, DSA_HEADS, tm, KV_LORA), lambda i: (i // spt, 0, i % spt, 0)),
                  pl.BlockSpec((DSA_HEADS, KV_LORA, DSA_V_DIM), lambda i: (0, 0, 0))],
        out_specs=pl.BlockSpec((tm, DSA_HEADS * DSA_V_DIM), lambda i: (i, 0)),
        out_shape=jax.ShapeDtypeStruct((bsz * s, DSA_HEADS * DSA_V_DIM), BF16),
        compiler_params=_params("arbitrary"),
        name="uv_project",
    )(o_lat, w_uv_b)


def _merge_kernel(h_ref, a_ref, b_ref, wga_ref, wgb_ref, wua_ref, wub_ref, o_ref,
                  cga_ref, cgb_ref, cua_ref, cub_ref):
    @pl.when(pl.program_id(1) == 0)
    def _():
        cga_ref[...] = wga_ref[...].astype(BF16)
        cgb_ref[...] = wgb_ref[...].astype(BF16)
        cua_ref[...] = wua_ref[...].astype(BF16)
        cub_ref[...] = wub_ref[...].astype(BF16)

    h = h_ref[...]
    ga = lax.dot_general(h, cga_ref[...], _NT_DIMS, preferred_element_type=F32)
    gb = lax.dot_general(h, cgb_ref[...], _NT_DIMS, preferred_element_type=F32)
    ya = jnp.dot(a_ref[...], cua_ref[...], preferred_element_type=F32)
    yb = jnp.dot(b_ref[...], cub_ref[...], preferred_element_type=F32)
    o_ref[...] = (jax.nn.sigmoid(ga) * ya + jax.nn.sigmoid(gb) * yb).astype(o_ref.dtype)


def _gated_merge(h1, att_a, att_b, w_t, row_ga, row_gb, w_up_a, w_up_b, tm=512, tn=256):
    n, d = h1.shape
    ka, kb = att_a.shape[1], att_b.shape[1]
    return pl.pallas_call(
        _merge_kernel,
        grid=(d // tn, n // tm),
        in_specs=[pl.BlockSpec((tm, d), lambda j, i: (i, 0)),
                  pl.BlockSpec((tm, ka), lambda j, i: (i, 0)),
                  pl.BlockSpec((tm, kb), lambda j, i: (i, 0)),
                  _w_rows_spec(tn, d, row_ga, lambda j, i: j),
                  _w_rows_spec(tn, d, row_gb, lambda j, i: j),
                  pl.BlockSpec((ka, tn), lambda j, i: (0, j)),
                  pl.BlockSpec((kb, tn), lambda j, i: (0, j))],
        out_specs=pl.BlockSpec((tm, tn), lambda j, i: (i, j)),
        out_shape=jax.ShapeDtypeStruct((n, d), BF16),
        scratch_shapes=[pltpu.VMEM((tn, d), BF16), pltpu.VMEM((tn, d), BF16),
                        pltpu.VMEM((ka, tn), BF16), pltpu.VMEM((kb, tn), BF16)],
        compiler_params=_params("arbitrary", "arbitrary"),
        name="gated_merge",
    )(h1, att_a, att_b, w_t, w_t, w_up_a, w_up_b)


def _mem_kernel(q_ref, kt_ref, v_ref, wom_ref, x_ref, o_ref):
    scale = MEM_HEAD_DIM ** -0.5
    outs = []
    for h in range(MEM_HEADS):
        sl = slice(h * MEM_HEAD_DIM, (h + 1) * MEM_HEAD_DIM)
        s = jnp.dot(q_ref[:, sl], kt_ref[sl, :], preferred_element_type=F32) * scale
        m = jnp.max(s, axis=-1, keepdims=True)
        p = jnp.exp(s - m)
        p = p / jnp.sum(p, axis=-1, keepdims=True)
        outs.append(jnp.dot(p.astype(BF16), v_ref[:, sl], preferred_element_type=F32).astype(BF16))
    o = jnp.concatenate(outs, axis=-1)
    o_ref[...] = x_ref[...] + jnp.dot(o, wom_ref[...], preferred_element_type=F32)


def _mem_attention(qm, km_t, kv_m, w_om_b, x1, bsz, s, n_mem, tm=512):
    spt = s // tm
    hd = MEM_HEADS * MEM_HEAD_DIM
    d = x1.shape[1]
    return pl.pallas_call(
        _mem_kernel,
        grid=(bsz, spt),
        in_specs=[pl.BlockSpec((tm, hd), lambda b, i: (b * spt + i, 0)),
                  pl.BlockSpec((None, hd, n_mem), lambda b, i: (b, 0, 0)),
                  pl.BlockSpec((n_mem, hd), lambda b, i: (b, 1)),
                  pl.BlockSpec((hd, d), lambda b, i: (0, 0)),
                  pl.BlockSpec((tm, d), lambda b, i: (b * spt + i, 0))],
        out_specs=pl.BlockSpec((tm, d), lambda b, i: (b * spt + i, 0)),
        out_shape=jax.ShapeDtypeStruct(x1.shape, F32),
        compiler_params=_params("arbitrary", "arbitrary"),
        name="memory_cross_attention",
    )(qm, km_t, kv_m, w_om_b, x1)


def _pack_bf16_pairs(x):
    c = x.shape[1] // 2
    lo = pltpu.bitcast(x[:, :c].astype(BF16).astype(F32), jnp.uint32)
    hi = pltpu.bitcast(x[:, c:].astype(BF16).astype(F32), jnp.uint32)
    return lax.shift_right_logical(lo, jnp.uint32(16)) | hi


def _unpack_bf16_pairs(w):
    lo = pltpu.bitcast(lax.shift_left(w, jnp.uint32(16)), F32).astype(BF16)
    hi = pltpu.bitcast(w & jnp.uint32(0xFFFF0000), F32).astype(BF16)
    return lo, hi


def _router_kernel(x_ref, g_ref, wr_ref, br_ref, h_ref, idx_ref, wt_ref, cnt_ref, carry_ref):
    @pl.when(pl.program_id(0) == 0)
    def _():
        carry_ref[...] = jnp.zeros_like(carry_ref)

    x = x_ref[...]
    hn = x * lax.rsqrt(jnp.mean(x * x, axis=-1, keepdims=True) + EPS) * g_ref[...]
    h_ref[...] = _pack_bf16_pairs(hn)
    logits = jnp.dot(hn, wr_ref[...], precision=lax.Precision.HIGHEST, preferred_element_type=F32) + br_ref[...]
    tm = x.shape[0]
    lane_i = lax.broadcasted_iota(jnp.int32, (tm, LANES), 1)
    lane = lane_i.astype(F32)
    lane_grp = lax.shift_right_logical(lane_i, EXPERTS_PER_GROUP.bit_length() - 1).astype(F32)
    big = 1e9
    is_g = (lane_i >= N_EXPERTS) & (lane_i < N_EXPERTS + N_GROUPS)
    lg = jnp.where(is_g, logits, -jnp.inf)
    mg = jnp.max(lg, axis=1, keepdims=True)
    gate_g = 1.0 / jnp.sum(jnp.exp(lg - mg), axis=1, keepdims=True)
    grp = jnp.min(jnp.where(lg == mg, lane, big), axis=1, keepdims=True) - N_EXPERTS
    in_grp = (lane_i < N_EXPERTS) & (lane_grp == grp)
    le = jnp.where(in_grp, logits, -jnp.inf)
    m1 = jnp.max(le, axis=1, keepdims=True)
    e1 = jnp.min(jnp.where(le == m1, lane, big), axis=1, keepdims=True)
    le2 = jnp.where(lane == e1, -jnp.inf, le)
    m2 = jnp.max(le2, axis=1, keepdims=True)
    e2 = jnp.min(jnp.where(le2 == m2, lane, big), axis=1, keepdims=True)
    r = jnp.exp(m2 - m1)
    w1 = gate_g * (1.0 / (1.0 + r))
    w2 = gate_g * (r / (1.0 + r))
    oh1 = (lane == e1).astype(F32)
    oh2 = (lane == e2).astype(F32)
    oh = oh1 + oh2
    tri = (lax.broadcasted_iota(jnp.int32, (tm, tm), 0) > lax.broadcasted_iota(jnp.int32, (tm, tm), 1)).astype(BF16)
    prefix = jnp.dot(tri, oh.astype(BF16), preferred_element_type=F32) + carry_ref[...]
    rank1 = jnp.sum(prefix * oh1, axis=1, keepdims=True)
    rank2 = jnp.sum(prefix * oh2, axis=1, keepdims=True)
    new_cnt = carry_ref[...] + jnp.sum(oh, axis=0, keepdims=True)
    carry_ref[...] = new_cnt
    cnt_ref[...] = new_cnt
    idx_ref[...] = jnp.where(lane_i == 0, e1, jnp.where(lane_i == 1, e2, jnp.where(
        lane_i == 2, rank1, jnp.where(lane_i == 3, rank2, 0.0)))).astype(jnp.int32)
    wt_ref[...] = jnp.where(lane_i == 0, w1, jnp.where(lane_i == 1, w2, 0.0))


def _router(x2, g, w_r, b_r, tm=256):
    n, d = x2.shape
    row = lambda w: pl.BlockSpec((tm, w), lambda i: (i, 0))
    return pl.pallas_call(
        _router_kernel,
        grid=(n // tm,),
        in_specs=[row(d), pl.BlockSpec((1, d), lambda i: (0, 0)), pl.BlockSpec((d, LANES), lambda i: (0, 0)),
                  pl.BlockSpec((1, LANES), lambda i: (0, 0))],
        out_specs=[row(d // 2), row(LANES), row(LANES), pl.BlockSpec((1, LANES), lambda i: (0, 0))],
        out_shape=[jax.ShapeDtypeStruct((n, d // 2), jnp.uint32), jax.ShapeDtypeStruct((n, LANES), jnp.int32),
                   jax.ShapeDtypeStruct((n, LANES), F32), jax.ShapeDtypeStruct((1, LANES), F32)],
        scratch_shapes=[pltpu.VMEM((1, LANES), F32)],
        compiler_params=_params("arbitrary"),
        name="moe_router",
    )(x2, g.reshape(1, d), w_r, b_r)


def _row_copy(src_hbm, row, dst, r, sem):
    return pltpu.make_async_copy(src_hbm.at[pl.ds(row, 1), :], dst.at[pl.ds(r, 1), :], sem)


def _sorted_rows(m):
    return m + N_EXPERTS * SUBLANES + ITEM_ROWS


def _dispatch_kernel(dest_ref, seg0_ref, cnt_ref, h_ref, x_hbm, zero_ref, sem, *, tm, m):
    i = pl.program_id(0)

    @pl.when(i == 0)
    def _():
        zero_ref[...] = jnp.zeros_like(zero_ref)
        tail = [pltpu.make_async_copy(zero_ref, x_hbm.at[pl.ds(r0, ITEM_SUB), :], sem)
                for r0 in range(m, _sorted_rows(m), ITEM_SUB)]
        for c in tail:
            c.start()
        for c in tail:
            c.wait()

        def fill_gap(e, total):
            gap = (-cnt_ref[e]) & (SUBLANES - 1)

            def one(r, c):
                _row_copy(zero_ref, 0, x_hbm, seg0_ref[e] + cnt_ref[e] + r, sem).start()
                return c

            lax.fori_loop(0, gap, one, 0)
            return total + gap

        n_gap = lax.fori_loop(0, N_EXPERTS, fill_gap, 0)

        def wait_gap(r, c):
            _row_copy(zero_ref, 0, x_hbm, 0, sem).wait()
            return c

        lax.fori_loop(0, n_gap, wait_gap, 0)

    def issue(r, c):
        t = i * tm + r
        for sl in range(TOPK_IN_GROUP):
            _row_copy(h_ref, r, x_hbm, dest_ref[TOPK_IN_GROUP * t + sl], sem).start()
        return c

    lax.fori_loop(0, tm, issue, 0)

    def wait(r, c):
        _row_copy(h_ref, 0, x_hbm, 0, sem).wait()
        return c

    lax.fori_loop(0, TOPK_IN_GROUP * tm, wait, 0)


def _dispatch(dest, seg_start, counts, h3p, tm=256):
    n, c = h3p.shape
    m = dest.shape[0]
    assert (_sorted_rows(m) - m) % ITEM_SUB == 0
    grid_spec = pltpu.PrefetchScalarGridSpec(
        num_scalar_prefetch=3,
        grid=(n // tm,),
        in_specs=[pl.BlockSpec((tm, c), lambda i, d, s0, cn: (i, 0))],
        out_specs=pl.BlockSpec(memory_space=pl.ANY),
        scratch_shapes=[pltpu.VMEM((ITEM_SUB, c), h3p.dtype), pltpu.SemaphoreType.DMA(())],
    )
    return pl.pallas_call(
        functools.partial(_dispatch_kernel, tm=tm, m=m),
        grid_spec=grid_spec,
        out_shape=jax.ShapeDtypeStruct((_sorted_rows(m), c), h3p.dtype),
        compiler_params=_params("arbitrary"),
        name="moe_dispatch",
    )(dest, seg_start, counts, h3p)


def _expert_kernel(ie_ref, is_ref, in_ref, ni_ref, x_hbm, wg_ref, wu_ref, wd_ref, y_hbm,
                   xin_ref, xb_ref, acc_ref, in_sem, out_sem, *, m, nj):
    i = pl.program_id(0)
    j = pl.program_id(1)
    n_items = ni_ref[0]
    valid = i < n_items
    n = in_ref[i]
    slot = lax.rem(i, 2)
    half = xb_ref.shape[1] // 2
    n_sub = ITEM_ROWS // ITEM_SUB

    def first_row(item):
        return pl.multiple_of(is_ref[item], SUBLANES)

    def in_copy(item):
        return pltpu.make_async_copy(x_hbm.at[pl.ds(first_row(item), ITEM_ROWS), :], xin_ref, in_sem)

    def out_each(item, sl, fn):
        for sub in range(n_sub):
            @pl.when(sub * ITEM_SUB < in_ref[item])
            def _():
                fn(pltpu.make_async_copy(
                    acc_ref.at[sl, pl.ds(sub * ITEM_SUB, ITEM_SUB), :],
                    y_hbm.at[pl.ds(first_row(item) + sub * ITEM_SUB, ITEM_SUB), :], out_sem.at[sl]))

    @pl.when((i == 0) & (j == 0))
    def _():
        acc_ref[1, :ITEM_SUB, :] = jnp.zeros((ITEM_SUB, acc_ref.shape[2]), F32)
        tail = [pltpu.make_async_copy(acc_ref.at[1, pl.ds(0, ITEM_SUB), :],
                                      y_hbm.at[pl.ds(r0, ITEM_SUB), :], out_sem.at[1])
                for r0 in range(m, _sorted_rows(m), ITEM_SUB)]
        for c in tail:
            c.start()
        for c in tail:
            c.wait()
        in_copy(0).start()

    @pl.when(valid & (j == 0))
    def _():
        in_copy(i).wait()
        lo, hi = _unpack_bf16_pairs(xin_ref[...])
        xb_ref[:, :half] = lo
        xb_ref[:, half:] = hi

        @pl.when(i + 1 < n_items)
        def _():
            in_copy(i + 1).start()

        acc_ref[slot] = jnp.zeros(acc_ref.shape[1:], F32)

    @pl.when(valid)
    def _():
        wg = wg_ref[...].astype(BF16)
        wu = wu_ref[...].astype(BF16)
        wd = wd_ref[...].astype(BF16)
        n_blocks = (n + ITEM_SUB - 1) // ITEM_SUB
        for nb in range(1, n_sub + 1):
            @pl.when(n_blocks == nb)
            def _():
                rows = pl.ds(0, nb * ITEM_SUB)
                xb = xb_ref[rows, :]
                g = jnp.dot(xb, wg, preferred_element_type=F32)
                u = jnp.dot(xb, wu, preferred_element_type=F32)
                hm = (g * jax.nn.sigmoid(g) * u).astype(BF16)
                acc_ref[slot, rows, :] += jnp.dot(hm, wd, preferred_element_type=F32)

    @pl.when(valid & (j == nj - 1))
    def _():
        @pl.when(i > 0)
        def _():
            out_each(i - 1, 1 - slot, lambda c: c.wait())

        out_each(i, slot, lambda c: c.start())

        @pl.when(i == n_items - 1)
        def _():
            out_each(i, slot, lambda c: c.wait())


def _experts(item_e, item_start, item_n, n_items, x_sorted, w_gate, w_up, w_down, max_items, m):
    d = w_gate.shape[1]
    nj = D_EXPERT // F_CHUNK

    def jj(i, j, ni):
        return jnp.where(i < ni[0], j, nj - 1)

    grid_spec = pltpu.PrefetchScalarGridSpec(
        num_scalar_prefetch=4,
        grid=(max_items, nj),
        in_specs=[pl.BlockSpec(memory_space=pl.ANY),
                  pl.BlockSpec((None, d, F_CHUNK), lambda i, j, ie, s, n, ni: (ie[i], 0, jj(i, j, ni))),
                  pl.BlockSpec((None, d, F_CHUNK), lambda i, j, ie, s, n, ni: (ie[i], 0, jj(i, j, ni))),
                  pl.BlockSpec((None, F_CHUNK, d), lambda i, j, ie, s, n, ni: (ie[i], jj(i, j, ni), 0))],
        out_specs=pl.BlockSpec(memory_space=pl.ANY),
        scratch_shapes=[pltpu.VMEM((ITEM_ROWS, d // 2), jnp.uint32), pltpu.VMEM((ITEM_ROWS, d), BF16),
                        pltpu.VMEM((2, ITEM_ROWS, d), F32),
                        pltpu.SemaphoreType.DMA(()), pltpu.SemaphoreType.DMA((2,))],
    )
    return pl.pallas_call(
        functools.partial(_expert_kernel, m=m, nj=nj),
        grid_spec=grid_spec,
        out_shape=jax.ShapeDtypeStruct((_sorted_rows(m), d), F32),
        compiler_params=_params("arbitrary", "arbitrary"),
        name="moe_experts",
    )(item_e, item_start, item_n, n_items, x_sorted, w_gate, w_up, w_down)


def _combine_kernel(dest_ref, y_hbm, x_ref, wt_ref, g_ref, o_ref, yv_ref, sem, *, tm):
    i = pl.program_id(0)
    slot = lax.rem(i, 2)

    def issue_tile(tile, sl):
        def issue(r, c):
            base = TOPK_IN_GROUP * (tile * tm + r)
            for k in range(TOPK_IN_GROUP):
                _row_copy(y_hbm, dest_ref[base + k], yv_ref.at[sl, k], r, sem.at[sl]).start()
            return c

        lax.fori_loop(0, tm, issue, 0)

    @pl.when(i == 0)
    def _():
        issue_tile(0, 0)

    @pl.when(i + 1 < pl.num_programs(0))
    def _():
        issue_tile(i + 1, 1 - slot)

    def wait(r, c):
        _row_copy(y_hbm, 0, yv_ref.at[slot, 0], 0, sem.at[slot]).wait()
        return c

    lax.fori_loop(0, TOPK_IN_GROUP * tm, wait, 0)
    w = wt_ref[...]
    moe = yv_ref[slot, 0] * w[:, 0:1]
    for k in range(1, TOPK_IN_GROUP):
        moe = moe + yv_ref[slot, k] * w[:, k:k + 1]
    x = x_ref[...] + moe
    o_ref[...] = x * lax.rsqrt(jnp.mean(x * x, axis=-1, keepdims=True) + EPS) * g_ref[...]


def _combine(dest, y_sorted, x2, wts, g_final, tm=128):
    n, d = x2.shape
    grid_spec = pltpu.PrefetchScalarGridSpec(
        num_scalar_prefetch=1,
        grid=(n // tm,),
        in_specs=[pl.BlockSpec(memory_space=pl.ANY),
                  pl.BlockSpec((tm, d), lambda i, dr: (i, 0)),
                  pl.BlockSpec((tm, LANES), lambda i, dr: (i, 0)),
                  pl.BlockSpec((1, d), lambda i, dr: (0, 0))],
        out_specs=pl.BlockSpec((tm, d), lambda i, dr: (i, 0)),
        scratch_shapes=[pltpu.VMEM((2, TOPK_IN_GROUP, tm, d), F32), pltpu.SemaphoreType.DMA((2,))],
    )
    return pl.pallas_call(
        functools.partial(_combine_kernel, tm=tm),
        grid_spec=grid_spec,
        out_shape=jax.ShapeDtypeStruct((n, d), F32),
        compiler_params=_params("arbitrary"),
        name="moe_combine_final_norm",
    )(dest, y_sorted, x2, wts, g_final.reshape(1, d))


def _pad_cols(w, width):
    return jnp.pad(w, ((0, 0), (0, width - w.shape[1])))


def _mixer(x2d, bsz, s, g_norm, w_in, b_f, g_q_lat, g_kv_lat, g_idx_k, b_idx_k, w_uq, w_idx_q, w_uk, w_uv,
           w_up_a, w_up_b, w_out):
    d = x2d.shape[1]
    topk = min(DSA_TOPK_MAX, s // 4)
    h1 = _rmsnorm(x2d, g_norm, BF16)
    c_small = 3 * FOX_WIDTH
    o_fa, o_cq = c_small, c_small + FOX_HEADS
    o_ckv = o_cq + Q_LORA
    o_kr = o_ckv + KV_LORA
    o_ik = o_kr + DSA_ROPE_DIM
    o_iw = o_ik + IDX_DIM
    o_ga = o_iw + IDX_HEADS
    o_gb = o_ga + d
    w_t = jnp.swapaxes(w_in, 0, 1)
    pad_rows = lambda w: jnp.pad(w, ((0, LANES - w.shape[0]), (0, 0)))
    w_small_t = jnp.concatenate([
        w_t[o_cq:o_ckv], w_t[o_ckv:o_kr], w_t[o_ik:o_iw],
        pad_rows(w_t[o_kr:o_ik]), pad_rows(w_t[o_fa:o_cq]), pad_rows(w_t[o_iw:o_ga])], axis=0)

    qkv = _mm(h1, w_t, 3 * FOX_WIDTH, w_rows=True, name="proj_qkv")
    small = _mm(h1, w_small_t, SMALL_W, out_dtype=F32, tn=SMALL_W // 7, w_rows=True, name="proj_small")

    tabs64 = _rope_tables(s, DSA_ROPE_DIM)
    tabs128 = _rope_tables(s, IDX_DIM)
    cq_n, kv_cat, k_idx, w_idx, fa = _prep(small, g_q_lat, g_kv_lat, g_idx_k, b_idx_k, tabs64, tabs128, s)

    cum = _fox_cum(fa, _pad_cols(b_f.reshape(1, -1), LANES), bsz, s)
    tk = 512
    cum_t = jnp.transpose(cum[:, :, :FOX_HEADS], (0, 2, 1)).reshape(bsz * FOX_HEADS, s // tk, tk)
    att_a = _fox_attention(qkv, cum, cum_t, bsz, s, t=tk)

    w_uq3 = w_uq.reshape(Q_LORA, DSA_HEADS, DSA_NOPE_DIM + DSA_ROPE_DIM)
    w_uq_nope = w_uq3[:, :, :DSA_NOPE_DIM].reshape(Q_LORA, DSA_HEADS * DSA_NOPE_DIM).astype(BF16)
    w_uq_rope = jnp.pad(w_uq3[:, :, DSA_NOPE_DIM:], ((0, 0), (0, 0), (0, LANES - DSA_ROPE_DIM))).reshape(
        Q_LORA, DSA_HEADS * LANES).astype(BF16)
    q_cat = _q_latent(cq_n, w_uq_nope, w_uq_rope, w_uk.astype(BF16), tabs64, bsz, s)
    q_idx = _mm(cq_n, w_idx_q, IDX_HEADS * IDX_DIM, heads=(bsz, s), rope_tabs=tabs128, name="proj_q_idx")
    w_idx_t = jnp.transpose(w_idx.reshape(bsz, s, LANES)[:, :, :IDX_HEADS], (0, 2, 1))
    bias = _indexer_mask(q_idx, k_idx.reshape(bsz, s, IDX_DIM), w_idx_t, bsz, s, topk)
    kv3 = kv_cat.reshape(bsz, s, KV_PAD)
    kv_t = jnp.transpose(kv3.reshape(bsz, s // KEY_CHUNK, KEY_CHUNK, KV_PAD), (0, 1, 3, 2))
    o_lat = _dsa_attention(q_cat, kv_t, kv3, bias, bsz, s)
    att_b = _uv_project(o_lat, w_uv.astype(BF16), bsz, s)

    merged = _gated_merge(h1, att_a, att_b, w_t, o_ga, o_gb, w_up_a, w_up_b)
    return _mm(merged, w_out, d, out_dtype=F32, res=x2d, name="proj_out")


def _memory_layer(x1, mem, bsz, s, g_x, g_m, w_qm, w_km, w_vm, w_om):
    n_mem = mem.shape[1]
    d = x1.shape[1]
    hd = MEM_HEADS * MEM_HEAD_DIM
    h2 = _rmsnorm(x1, g_x, BF16)
    mem_n = _rmsnorm(mem.reshape(bsz * n_mem, d), g_m, BF16)
    qm = _mm(h2, w_qm, hd, name="proj_q_mem")
    kv_m = _mm(mem_n, jnp.concatenate([w_km, w_vm], axis=1), 2 * hd, tm=bsz * n_mem, name="proj_kv_mem")
    km_t = jnp.transpose(kv_m[:, :hd].reshape(bsz, n_mem, hd), (0, 2, 1))
    return _mem_attention(qm, km_t, kv_m, w_om.astype(BF16), x1, bsz, s, n_mem)


def _moe_layer(x2, g_ffn, w_rg, b_rg, w_re, b_re, w_gate, w_up, w_down, g_final):
    n, d = x2.shape
    m = n * TOPK_IN_GROUP
    w_r = _pad_cols(jnp.concatenate([w_re, w_rg], axis=1), LANES)
    b_r = _pad_cols(jnp.concatenate([b_re, b_rg]).reshape(1, -1), LANES)
    h3p, ridx, wts, cnt = _router(x2, g_ffn, w_r, b_r)
    counts = cnt[0, :N_EXPERTS].astype(jnp.int32)
    e_flat = ridx[:, :TOPK_IN_GROUP].reshape(m)
    rank_flat = ridx[:, TOPK_IN_GROUP:2 * TOPK_IN_GROUP].reshape(m)
    seg = (counts + SUBLANES - 1) // SUBLANES * SUBLANES
    starts = (jnp.cumsum(seg) - seg).astype(jnp.int32)
    n_it = (counts + ITEM_ROWS - 1) // ITEM_ROWS
    it_end = jnp.cumsum(n_it)
    it_first = it_end - n_it
    max_items = (m + N_EXPERTS * (ITEM_ROWS - 1)) // ITEM_ROWS
    t = jnp.arange(max_items, dtype=jnp.int32)
    item_e = jnp.minimum(jnp.searchsorted(it_end, t, side='right'), N_EXPERTS - 1).astype(jnp.int32)
    k_in = t - it_first[item_e]
    item_start = (starts[item_e] + k_in * ITEM_ROWS).astype(jnp.int32)
    item_n = jnp.clip(counts[item_e] - k_in * ITEM_ROWS, 0, ITEM_ROWS).astype(jnp.int32)
    n_items = it_end[-1:].astype(jnp.int32)
    last = jnp.maximum(n_items[0] - 1, 0)
    item_e = jnp.where(t < n_items[0], item_e, item_e[last])
    item_start = jnp.where(t < n_items[0], item_start, 0)
    item_n = jnp.where(t < n_items[0], item_n, 0)
    dest = (starts[e_flat] + rank_flat).astype(jnp.int32)
    x_sorted = _dispatch(dest, starts, counts, h3p)
    y_sorted = _experts(item_e, item_start, item_n, n_items, x_sorted, w_gate, w_up, w_down, max_items, m)
    return _combine(dest, y_sorted, x2, wts, g_final)


def kernel(x, mem, g_norm_mix, w_in, b_f, g_q_lat, g_kv_lat, g_idx_k, b_idx_k, w_uq, w_idx_q, w_uk, w_uv,
           w_up_a, w_up_b, w_out, g_norm_mem_x, g_mem, w_qm, w_km, w_vm, w_om, g_norm_ffn,
           w_rg, b_rg, w_re, b_re, w_gate, w_up, w_down, g_final):
    bsz, s, d = x.shape
    depth = w_in.shape[0]
    assert depth == 1, "the final norm is fused into the last MoE combine"
    l = 0
    x2d = x.reshape(bsz * s, d)
    x1 = _mixer(x2d, bsz, s, g_norm_mix[l], w_in[l], b_f[l], g_q_lat[l], g_kv_lat[l], g_idx_k[l], b_idx_k[l],
                w_uq[l], w_idx_q[l], w_uk[l], w_uv[l], w_up_a[l], w_up_b[l], w_out[l])
    x2 = _memory_layer(x1, mem, bsz, s, g_norm_mem_x[l], g_mem[l], w_qm[l], w_km[l], w_vm[l], w_om[l])
    out = _moe_layer(x2, g_norm_ffn[l], w_rg[l], b_rg[l], w_re[l], b_re[l], w_gate[l], w_up[l], w_down[l], g_final)
    return out.reshape(bsz, s, d)
```

```python
import functools

import jax
import jax.numpy as jnp
import numpy as np
from jax import lax
from jax.experimental import pallas as pl
from jax.experimental.pallas import tpu as pltpu

F32 = jnp.float32
BF16 = jnp.bfloat16

EPS = 1e-6
ROPE_THETA = 10000.0
FOX_HEADS = 16
FOX_HEAD_DIM = 128
DSA_HEADS = 16
DSA_NOPE_DIM = 128
DSA_ROPE_DIM = 64
DSA_V_DIM = 128
Q_LORA = 1024
KV_LORA = 256
IDX_HEADS = 32
IDX_DIM = 128
DSA_TOPK_MAX = 256
MEM_HEADS = 4
MEM_HEAD_DIM = 128
N_GROUPS = 8
EXPERTS_PER_GROUP = 8
N_EXPERTS = N_GROUPS * EXPERTS_PER_GROUP
TOPK_IN_GROUP = 2
D_EXPERT = 512

FOX_WIDTH = FOX_HEADS * FOX_HEAD_DIM
LANES = 128
SUBLANES = 8
KV_PAD = KV_LORA + LANES
NEG_BIG = -1e30
LOG2E = 1.4426950408889634
FOX_QSCALE = FOX_HEAD_DIM ** -0.5 * LOG2E
FOX_HB = 2
DSA_QSCALE = (DSA_NOPE_DIM + DSA_ROPE_DIM) ** -0.5 * LOG2E
VMEM_LIMIT_BYTES = 56 * 1024 * 1024
ITEM_ROWS = 384
ITEM_SUB = 128
F_CHUNK = 256
KEY_CHUNK = 256


def _params(*sem):
    return pltpu.CompilerParams(dimension_semantics=sem, vmem_limit_bytes=VMEM_LIMIT_BYTES)


def _rmsnorm_kernel(x_ref, g_ref, o_ref):
    x = x_ref[...].astype(F32)
    ms = jnp.mean(x * x, axis=-1, keepdims=True)
    o_ref[...] = (x * lax.rsqrt(ms + EPS) * g_ref[...]).astype(o_ref.dtype)


def _rmsnorm(x2d, g, out_dtype, tm=256):
    n, d = x2d.shape
    return pl.pallas_call(
        _rmsnorm_kernel,
        grid=(n // tm,),
        in_specs=[pl.BlockSpec((tm, d), lambda i: (i, 0)), pl.BlockSpec((1, d), lambda i: (0, 0))],
        out_specs=pl.BlockSpec((tm, d), lambda i: (i, 0)),
        out_shape=jax.ShapeDtypeStruct((n, d), out_dtype),
        compiler_params=_params("arbitrary"),
        name="rmsnorm",
    )(x2d, g.reshape(1, d).astype(F32))


_NT_DIMS = (((1,), (1,)), ((), ()))


def _w_rows_spec(tn, k, row0, index_of_j):
    if row0 % tn == 0:
        return pl.BlockSpec((tn, k), lambda *g: (index_of_j(*g) + row0 // tn, 0))
    assert row0 % SUBLANES == 0
    return pl.BlockSpec((pl.Element(tn), pl.Element(k)),
                        lambda *g: (pl.multiple_of(index_of_j(*g) * tn + row0, SUBLANES), 0))


def _mm_kernel(a_ref, b_ref, *rest, has_res, head_out, rope, w_rows, col_scale):
    pos = 0
    res_ref = cos_ref = sin_ref = None
    if has_res:
        res_ref = rest[pos]
        pos += 1
    if rope:
        cos_ref, sin_ref = rest[pos], rest[pos + 1]
        pos += 2
    o_ref = rest[pos]
    if b_ref.dtype == BF16:
        wb_ref = b_ref
    else:
        wb_ref = rest[pos + 1]

        @pl.when(pl.program_id(1) == 0)
        def _():
            wb_ref[...] = b_ref[...].astype(BF16)

    if w_rows:
        acc = lax.dot_general(a_ref[...], wb_ref[...], _NT_DIMS, preferred_element_type=F32)
    else:
        acc = jnp.dot(a_ref[...], wb_ref[...], preferred_element_type=F32)
    if col_scale is not None:
        acc = acc * jnp.where(pl.program_id(0) < col_scale[0], col_scale[1], 1.0)
    if has_res:
        acc = acc + res_ref[...]
    if head_out:
        for hh in range(acc.shape[1] // LANES):
            part = acc[:, hh * LANES:(hh + 1) * LANES]
            if rope:
                part = part * cos_ref[...] + pltpu.roll(part, LANES // 2, 1) * sin_ref[...]
            o_ref[hh] = part.astype(o_ref.dtype)
    else:
        o_ref[...] = acc.astype(o_ref.dtype)


def _mm(a, w, n_cols, *, col0=0, out_dtype=BF16, tm=1024, tn=512, res=None, heads=None, rope_tabs=None,
        w_rows=False, scaled_cols=None, name="mm"):
    m, k = a.shape
    tm = min(tm, m if heads is None else heads[1])
    assert n_cols % tn == 0 and m % tm == 0
    if w_rows:
        w_spec = _w_rows_spec(tn, k, col0, lambda j, i: j)
    else:
        assert col0 % tn == 0
        w_spec = pl.BlockSpec((k, tn), lambda j, i: (0, j + col0 // tn))
    in_specs = [pl.BlockSpec((tm, k), lambda j, i: (i, 0)), w_spec]
    args = [a, w]
    if res is not None:
        in_specs.append(pl.BlockSpec((tm, tn), lambda j, i: (i, j)))
        args.append(res)
    if heads is not None:
        bsz, s = heads
        assert s % tm == 0
        spt = s // tm
        if rope_tabs is not None:
            in_specs += [pl.BlockSpec((tm, LANES), lambda j, i: (i % spt, 0))] * 2
            args += list(rope_tabs)
        hpt = tn // LANES
        out_spec = pl.BlockSpec((None, hpt, tm, LANES), lambda j, i: (i // spt, j, i % spt, 0))
        out_shape = jax.ShapeDtypeStruct((bsz, n_cols // LANES, s, LANES), out_dtype)
    else:
        out_spec = pl.BlockSpec((tm, tn), lambda j, i: (i, j))
        out_shape = jax.ShapeDtypeStruct((m, n_cols), out_dtype)
    col_scale = None
    if scaled_cols is not None:
        assert scaled_cols[0] % tn == 0
        col_scale = (scaled_cols[0] // tn, scaled_cols[1])
    kern = functools.partial(_mm_kernel, has_res=res is not None, head_out=heads is not None,
                             rope=rope_tabs is not None, w_rows=w_rows, col_scale=col_scale)
    return pl.pallas_call(
        kern,
        grid=(n_cols // tn, m // tm),
        in_specs=in_specs,
        out_specs=out_spec,
        out_shape=out_shape,
        scratch_shapes=[] if w.dtype == BF16 else [pltpu.VMEM((tn, k) if w_rows else (k, tn), BF16)],
        compiler_params=_params("arbitrary", "arbitrary"),
        name=name,
    )(*args)


def _rope_tables(s, d):
    inv = jnp.power(ROPE_THETA, -jnp.arange(0, d, 2, dtype=F32) / d)
    ang = jnp.arange(s, dtype=jnp.int32).astype(F32)[:, None] * inv[None, :]
    cos, sin = jnp.cos(ang), jnp.sin(ang)
    cos_t = jnp.tile(jnp.concatenate([cos, cos], axis=-1), (1, LANES // d))
    sin_t = jnp.tile(jnp.concatenate([-sin, sin], axis=-1), (1, LANES // d))
    return cos_t, sin_t


def _rot_half64(x):
    lane = lax.broadcasted_iota(jnp.int32, x.shape, 1)
    return jnp.where((lane % 64) < 32, pltpu.roll(x, 96, 1), pltpu.roll(x, 32, 1))


_C_CQ = 0
_C_CKV = Q_LORA
_C_IK = _C_CKV + KV_LORA
_C_KR = _C_IK + IDX_DIM
_C_FA = _C_KR + LANES
_C_IW = _C_FA + LANES
SMALL_W = _C_IW + LANES


def _prep_kernel(s_ref, gq_ref, gkv_ref, gik_ref, bik_ref, c64_ref, s64_ref, c128_ref, s128_ref,
                 cq_ref, kv_ref, kidx_ref, widx_ref, fa_ref):
    cq = s_ref[:, _C_CQ:_C_CQ + Q_LORA]
    cq_ref[...] = (cq * lax.rsqrt(jnp.mean(cq * cq, axis=-1, keepdims=True) + EPS) * gq_ref[...]).astype(BF16)
    ckv = s_ref[:, _C_CKV:_C_CKV + KV_LORA]
    kv_ref[:, :KV_LORA] = (ckv * lax.rsqrt(jnp.mean(ckv * ckv, axis=-1, keepdims=True) + EPS)
                           * gkv_ref[...]).astype(BF16)
    kr = s_ref[:, _C_KR:_C_KR + LANES]
    kv_ref[:, KV_LORA:] = (kr * c64_ref[...] + _rot_half64(kr) * s64_ref[...]).astype(BF16)
    ik = s_ref[:, _C_IK:_C_IK + IDX_DIM]
    mu = jnp.mean(ik, axis=-1, keepdims=True)
    xc = ik - mu
    ikn = xc * lax.rsqrt(jnp.mean(xc * xc, axis=-1, keepdims=True) + EPS) * gik_ref[...] + bik_ref[...]
    kidx_ref[...] = (ikn * c128_ref[...] + pltpu.roll(ikn, LANES // 2, 1) * s128_ref[...]).astype(BF16)
    widx_ref[...] = s_ref[:, _C_IW:_C_IW + LANES] * (IDX_HEADS * IDX_DIM) ** -0.5
    fa_ref[...] = s_ref[:, _C_FA:_C_FA + LANES]


def _prep(small, g_q, g_kv, g_ik, b_ik, tabs64, tabs128, s, tm=256):
    n = small.shape[0]
    spt = s // tm
    row = lambda w: pl.BlockSpec((tm, w), lambda i: (i, 0))
    vec = lambda w: pl.BlockSpec((1, w), lambda i: (0, 0))
    tab = pl.BlockSpec((tm, LANES), lambda i: (i % spt, 0))
    return pl.pallas_call(
        _prep_kernel,
        grid=(n // tm,),
        in_specs=[row(SMALL_W), vec(Q_LORA), vec(KV_LORA), vec(IDX_DIM), vec(IDX_DIM), tab, tab, tab, tab],
        out_specs=[row(Q_LORA), row(KV_PAD), row(IDX_DIM), row(LANES), row(LANES)],
        out_shape=[jax.ShapeDtypeStruct((n, Q_LORA), BF16), jax.ShapeDtypeStruct((n, KV_PAD), BF16),
                   jax.ShapeDtypeStruct((n, IDX_DIM), BF16), jax.ShapeDtypeStruct((n, LANES), F32),
                   jax.ShapeDtypeStruct((n, LANES), F32)],
        compiler_params=_params("arbitrary"),
        name="mixer_prep",
    )(small, g_q.reshape(1, -1), g_kv.reshape(1, -1), g_ik.reshape(1, -1), b_ik.reshape(1, -1),
      tabs64[0], tabs64[1], tabs128[0], tabs128[1])


def _cum_kernel(fa_ref, bf_ref, cum_ref, carry_ref):
    @pl.when(pl.program_id(1) == 0)
    def _():
        carry_ref[...] = jnp.zeros_like(carry_ref)

    z = fa_ref[...] + bf_ref[...]
    logf = jnp.minimum(z, 0.0) - jnp.log1p(jnp.exp(-jnp.abs(z)))
    t = z.shape[0]
    tri = (lax.broadcasted_iota(jnp.int32, (t, t), 0) >= lax.broadcasted_iota(jnp.int32, (t, t), 1)).astype(F32)
    c = jnp.dot(tri, logf, precision=lax.Precision.HIGHEST, preferred_element_type=F32) + carry_ref[...]
    cum_ref[...] = c
    carry_ref[...] = c[t - 1:t, :]


def _fox_cum(fa, b_f_pad, bsz, s, t=256):
    return pl.pallas_call(
        _cum_kernel,
        grid=(bsz, s // t),
        in_specs=[pl.BlockSpec((None, t, LANES), lambda b, i: (b, i, 0)),
                  pl.BlockSpec((1, LANES), lambda b, i: (0, 0))],
        out_specs=pl.BlockSpec((None, t, LANES), lambda b, i: (b, i, 0)),
        out_shape=jax.ShapeDtypeStruct((bsz, s, LANES), F32),
        scratch_shapes=[pltpu.VMEM((1, LANES), F32)],
        compiler_params=_params("arbitrary", "arbitrary"),
        name="fox_cumsum",
    )(fa.reshape(bsz, s, LANES), b_f_pad)


def _lane_fold(x, op):
    out = x[:, :LANES]
    for c in range(1, x.shape[1] // LANES):
        out = op(out, x[:, c * LANES:(c + 1) * LANES])
    return out


def _split3(x):
    t1 = x.astype(BF16)
    r = x - t1.astype(F32)
    t2 = r.astype(BF16)
    t3 = (r - t2.astype(F32)).astype(BF16)
    return t1, t2, t3


def _gate_lanes(c, own_first):
    t1, t2, t3 = (t.astype(F32) for t in _split3(c))
    lane = lax.broadcasted_iota(jnp.int32, c.shape, 1)
    a, b = (0, 3) if own_first else (3, 0)
    ones3 = jnp.where((lane >= b) & (lane < b + 3), 1.0, 0.0)
    out = jnp.where(lane == a, t1, jnp.where(lane == a + 1, t2, jnp.where(lane == a + 2, t3, ones3)))
    return out.astype(BF16)


def _fox_kernel(q_ref, k_ref, v_ref, cum_ref, o_ref, kf_ref, vf_ref, s_ref, *, t):
    hp = pl.program_id(1)
    i = pl.program_id(2)
    d = FOX_HEAD_DIM
    heads = range(FOX_HB)
    row = lax.broadcasted_iota(jnp.int32, (LANES, LANES), 0)

    def gate_column(c, hh):
        pick = (row == hp * FOX_HB + hh).astype(F32)
        return jnp.dot(c, pick, precision=lax.Precision.HIGHEST, preferred_element_type=F32) * LOG2E

    @pl.when(i == 0)
    def _():
        for hh in heads:
            kf_ref[hh, :, :d] = k_ref[:, hh * d:(hh + 1) * d]
            kf_ref[hh, :, d:] = _gate_lanes(-gate_column(cum_ref[...], hh), True)
            vf_ref[hh, :, :d] = v_ref[:, hh * d:(hh + 1) * d]
            vf_ref[hh, :, d:] = jnp.ones((vf_ref.shape[1], LANES), BF16)

    cum_q = cum_ref[pl.ds(pl.multiple_of(i * t, t), t), :]
    qs = [jnp.concatenate([q_ref[:, hh * d:(hh + 1) * d], _gate_lanes(gate_column(cum_q, hh), False)], axis=1)
          for hh in heads]

    def logits(hh, kc):
        kb = kf_ref[hh, pl.ds(pl.multiple_of(kc * t, t), t), :]
        return lax.dot_general(qs[hh], kb, _NT_DIMS, preferred_element_type=F32)

    def pass1(kc, m_lanes):
        out = []
        for hh in heads:
            s = logits(hh, kc)
            s_ref[hh, kc] = s
            out.append(jnp.maximum(m_lanes[hh], _lane_fold(s, jnp.maximum)))
        return tuple(out)

    m_lanes = lax.fori_loop(0, i, pass1, tuple(jnp.full((t, LANES), NEG_BIG, F32) for _ in heads))
    causal = lax.broadcasted_iota(jnp.int32, (t, t), 0) >= lax.broadcasted_iota(jnp.int32, (t, t), 1)
    ms = []
    for hh in heads:
        s = jnp.where(causal, logits(hh, i), NEG_BIG)
        s_ref[hh, i] = s
        m_lane = jnp.maximum(m_lanes[hh], _lane_fold(s, jnp.maximum))
        m = jnp.broadcast_to(jnp.max(m_lane, axis=1, keepdims=True), (t, LANES))
        ms.append(jnp.concatenate([m] * (t // LANES), axis=1))

    def pass2(kc, accs):
        out = []
        for hh in heads:
            p = jnp.exp2(s_ref[hh, kc] - ms[hh])
            vb = vf_ref[hh, pl.ds(pl.multiple_of(kc * t, t), t), :]
            out.append(accs[hh] + jnp.dot(p.astype(BF16), vb, preferred_element_type=F32))
        return tuple(out)

    accs = lax.fori_loop(0, i + 1, pass2, tuple(jnp.zeros((t, d + LANES), F32) for _ in heads))
    for hh in heads:
        o_ref[:, hh * d:(hh + 1) * d] = (accs[hh][:, :d] / accs[hh][:, d:]).astype(o_ref.dtype)


def _fox_attention(qkv, cum, bsz, s, t=512):
    n = bsz * s
    nq = s // t
    hp = FOX_HEADS // FOX_HB
    w = FOX_HB * LANES
    assert FOX_HEAD_DIM == LANES and FOX_HEADS % FOX_HB == 0
    kern = functools.partial(_fox_kernel, t=t)
    return pl.pallas_call(
        kern,
        grid=(bsz, hp, nq),
        in_specs=[pl.BlockSpec((t, w), lambda b, h, i: (b * nq + i, h)),
                  pl.BlockSpec((s, w), lambda b, h, i: (b, hp + h)),
                  pl.BlockSpec((s, w), lambda b, h, i: (b, 2 * hp + h)),
                  pl.BlockSpec((None, s, LANES), lambda b, h, i: (b, 0, 0))],
        out_specs=pl.BlockSpec((t, w), lambda b, h, i: (b * nq + i, h)),
        out_shape=jax.ShapeDtypeStruct((n, FOX_WIDTH), BF16),
        scratch_shapes=[pltpu.VMEM((FOX_HB, s, 2 * LANES), BF16), pltpu.VMEM((FOX_HB, s, 2 * LANES), BF16),
                        pltpu.VMEM((FOX_HB, nq, t, t), F32)],
        compiler_params=_params("arbitrary", "arbitrary", "arbitrary"),
        name="fox_attention",
    )(qkv, qkv, qkv, cum)


def _qlat_kernel(cq_ref, wn_ref, wr_ref, wuk_ref, c64_ref, s64_ref, o_ref):
    cq = cq_ref[...]
    qn = jnp.dot(cq, wn_ref[...], preferred_element_type=F32).astype(BF16)
    qr = jnp.dot(cq, wr_ref[...], preferred_element_type=F32)
    for h in range(DSA_HEADS):
        cols = slice(h * LANES, (h + 1) * LANES)
        lat = jnp.dot(qn[:, cols], wuk_ref[h], preferred_element_type=F32)
        o_ref[h, :, :KV_LORA] = (lat * DSA_QSCALE).astype(BF16)
        r = qr[:, cols]
        o_ref[h, :, KV_LORA:] = ((r * c64_ref[...] + _rot_half64(r) * s64_ref[...]) * DSA_QSCALE).astype(BF16)


def _q_latent(cq_n, w_nope_b, w_rope_b, w_uk_b, tabs64, bsz, s, tm=256):
    spt = s // tm
    kq = cq_n.shape[1]
    wide = DSA_HEADS * LANES
    return pl.pallas_call(
        _qlat_kernel,
        grid=(bsz * spt,),
        in_specs=[pl.BlockSpec((tm, kq), lambda i: (i, 0)),
                  pl.BlockSpec((kq, wide), lambda i: (0, 0)),
                  pl.BlockSpec((kq, wide), lambda i: (0, 0)),
                  pl.BlockSpec((DSA_HEADS, DSA_NOPE_DIM, KV_LORA), lambda i: (0, 0, 0)),
                  pl.BlockSpec((tm, LANES), lambda i: (i % spt, 0)),
                  pl.BlockSpec((tm, LANES), lambda i: (i % spt, 0))],
        out_specs=pl.BlockSpec((None, DSA_HEADS, tm, KV_PAD), lambda i: (i // spt, 0, i % spt, 0)),
        out_shape=jax.ShapeDtypeStruct((bsz, DSA_HEADS, s, KV_PAD), BF16),
        compiler_params=_params("arbitrary"),
        name="q_latent",
    )(cq_n, w_nope_b, w_rope_b, w_uk_b, tabs64[0], tabs64[1])


def _needed_chunks(i, tq):
    return ((i + 1) * tq + KEY_CHUNK - 1) // KEY_CHUNK


_NEG_INF_KEY = int(np.array(-np.inf, np.float32).view(np.int32) ^ np.int32(0x7FFFFFFF))


def _indexer_kernel(q_ref, k_ref, wt_ref, o_ref, key_ref, thr_ref, *, tq, topk, hg):
    i = pl.program_id(1)
    tk = KEY_CHUNK
    n_chunks = key_ref.shape[0]
    nk = _needed_chunks(i, tq)
    qpos = i * tq + lax.broadcasted_iota(jnp.int32, (tk, tq), 1)
    krow = lax.broadcasted_iota(jnp.int32, (tk, tq), 0)

    def score_chunk(kc, c):
        k0 = pl.multiple_of(kc * tk, tk)
        kb = k_ref[pl.ds(k0, tk), :]
        sc = jnp.zeros((tk, tq), F32)
        for g in range(IDX_HEADS // hg):
            q2 = q_ref[g * hg:(g + 1) * hg].reshape(hg * tq, IDX_DIM)
            d = lax.dot_general(kb, q2, _NT_DIMS, preferred_element_type=F32)
            for hh in range(hg):
                h = g * hg + hh
                sc = sc + wt_ref[h:h + 1, :] * jnp.maximum(d[:, hh * tq:(hh + 1) * tq], 0.0)
        sc = jnp.where(sc == 0.0, 0.0, sc)
        sc = jnp.where(k0 + krow <= qpos, sc, -jnp.inf)
        bits = pltpu.bitcast(sc, jnp.int32)
        key_ref[kc] = jnp.where(bits < 0, bits ^ jnp.int32(0x7FFFFFFF), bits)
        return c

    lax.fori_loop(0, nk, score_chunk, 0)

    def pad_chunk(kc, c):
        key_ref[kc] = jnp.full((tk, tq), _NEG_INF_KEY, jnp.int32)
        return c

    lax.fori_loop(nk, n_chunks, pad_chunk, 0)
    kf = float(topk)

    def bisection(n_counted):
        def count_ge(t):
            acc = jnp.zeros((SUBLANES, tq), F32)
            for kc in range(n_counted):
                ind = jnp.where(key_ref[kc] >= t, 1.0, 0.0)
                acc = acc + jnp.sum(ind.reshape(tk // SUBLANES, SUBLANES, tq), axis=0)
            return jnp.sum(acc, axis=0, keepdims=True)

        int_min = jnp.int32(-2 ** 31)
        t0 = jnp.where(count_ge(jnp.zeros((1, tq), jnp.int32)) >= kf, jnp.int32(0), int_min)

        def bisect(it, t):
            cand = t | lax.shift_left(jnp.int32(1), jnp.int32(30) - it)
            return jnp.where(count_ge(cand) >= kf, cand, t)

        thr_ref[...] = lax.fori_loop(0, 31, bisect, t0)

    half = max(n_chunks // 2, 1)
    if half < n_chunks:
        pl.when(nk <= half)(lambda: bisection(half))
        pl.when(nk > half)(lambda: bisection(n_chunks))
    else:
        bisection(n_chunks)
    thr = thr_ref[...]

    def write_chunk(kc, c):
        keep = (key_ref[kc] >= thr) & (kc * tk + krow <= qpos)
        o_ref[kc] = jnp.where(keep, 0.0, NEG_BIG).T.astype(o_ref.dtype)
        return c

    lax.fori_loop(0, nk, write_chunk, 0)

    def fill_chunk(kc, c):
        o_ref[kc] = jnp.full((tq, tk), NEG_BIG, o_ref.dtype)
        return c

    lax.fori_loop(nk, n_chunks, fill_chunk, 0)


def _indexer_mask(q_idx, k_idx, w_idx_t, bsz, s, topk, tq=128, hg=4):
    kern = functools.partial(_indexer_kernel, tq=tq, topk=topk, hg=hg)
    nc = s // KEY_CHUNK
    return pl.pallas_call(
        kern,
        grid=(bsz, s // tq),
        in_specs=[pl.BlockSpec((None, IDX_HEADS, tq, IDX_DIM), lambda b, i: (b, 0, i, 0)),
                  pl.BlockSpec((None, s, IDX_DIM), lambda b, i: (b, 0, 0)),
                  pl.BlockSpec((None, IDX_HEADS, tq), lambda b, i: (b, 0, i))],
        out_specs=pl.BlockSpec((None, nc, tq, KEY_CHUNK), lambda b, i: (b, 0, i, 0)),
        out_shape=jax.ShapeDtypeStruct((bsz, nc, s, KEY_CHUNK), BF16),
        scratch_shapes=[pltpu.VMEM((nc, KEY_CHUNK, tq), jnp.int32), pltpu.VMEM((1, tq), jnp.int32)],
        compiler_params=_params("arbitrary", "arbitrary"),
        name="indexer_topk_mask",
    )(q_idx, k_idx, w_idx_t)


def _dsa_kernel(q_ref, kvt_ref, kv_ref, bias_ref, o_ref, s_ref, m_ref, l_ref, acc_ref, *, tq):
    i = pl.program_id(1)
    tk = KEY_CHUNK
    rows = DSA_HEADS * tq
    nk = _needed_chunks(i, tq)
    q2 = q_ref[...].reshape(rows, KV_PAD)
    m_ref[...] = jnp.full(m_ref.shape, NEG_BIG, F32)

    n_pairs = (nk + 1) // 2

    def logits(kc):
        sc = jnp.dot(q2, kvt_ref[kc], preferred_element_type=F32)
        sc = (sc.reshape(DSA_HEADS, tq, tk) + bias_ref[kc].astype(F32)[None]).reshape(rows, tk)
        s_ref[kc] = sc
        return _lane_fold(sc, jnp.maximum)

    def pass1(pr, c):
        m_ref[...] = jnp.maximum(m_ref[...], jnp.maximum(logits(2 * pr), logits(2 * pr + 1)))
        return c

    lax.fori_loop(0, n_pairs, pass1, 0)
    m_ref[...] = jnp.broadcast_to(jnp.max(m_ref[...], axis=1, keepdims=True), m_ref.shape)
    l_ref[...] = jnp.zeros(l_ref.shape, F32)
    acc_ref[...] = jnp.zeros(acc_ref.shape, F32)

    def probs(kc):
        m = m_ref[...]
        p = jnp.concatenate([jnp.exp2(s_ref[kc, :, g * LANES:(g + 1) * LANES] - m)
                             for g in range(tk // LANES)], axis=1)
        kvb = kv_ref[pl.ds(pl.multiple_of(kc * tk, tk), tk), :KV_LORA]
        return _lane_fold(p, jnp.add), jnp.dot(p.astype(BF16), kvb, preferred_element_type=F32)

    def pass2(pr, c):
        l0, a0 = probs(2 * pr)
        l1, a1 = probs(2 * pr + 1)
        l_ref[...] += l0 + l1
        acc_ref[...] += a0 + a1
        return c

    lax.fori_loop(0, n_pairs, pass2, 0)
    l = jnp.dot(l_ref[...], jnp.ones((LANES, LANES), F32), precision=lax.Precision.HIGHEST,
                preferred_element_type=F32)
    o = acc_ref[...] / jnp.concatenate([l] * (KV_LORA // LANES), axis=1)
    o_ref[...] = o.reshape(DSA_HEADS, tq, KV_LORA).astype(o_ref.dtype)


def _dsa_attention(q_cat, kv_t, kv, bias, bsz, s, tq=64):
    kern = functools.partial(_dsa_kernel, tq=tq)
    nc = s // KEY_CHUNK
    assert nc % 2 == 0, "the kernel walks key chunks in pairs"
    rows = DSA_HEADS * tq
    return pl.pallas_call(
        kern,
        grid=(bsz, s // tq),
        in_specs=[pl.BlockSpec((None, DSA_HEADS, tq, KV_PAD), lambda b, i: (b, 0, i, 0)),
                  pl.BlockSpec((None, nc, KV_PAD, KEY_CHUNK), lambda b, i: (b, 0, 0, 0)),
                  pl.BlockSpec((None, s, KV_PAD), lambda b, i: (b, 0, 0)),
                  pl.BlockSpec((None, nc, tq, KEY_CHUNK), lambda b, i: (b, 0, i, 0))],
        out_specs=pl.BlockSpec((None, DSA_HEADS, tq, KV_LORA), lambda b, i: (b, 0, i, 0)),
        out_shape=jax.ShapeDtypeStruct((bsz, DSA_HEADS, s, KV_LORA), BF16),
        scratch_shapes=[pltpu.VMEM((nc, rows, KEY_CHUNK), F32), pltpu.VMEM((rows, LANES), F32),
                        pltpu.VMEM((rows, LANES), F32), pltpu.VMEM((rows, KV_LORA), F32)],
        compiler_params=_params("arbitrary", "arbitrary"),
        name="dsa_attention",
    )(q_cat, kv_t, kv, bias)


def _uv_kernel(o_ref, w_ref, y_ref):
    for h in range(DSA_HEADS):
        y_ref[:, h * DSA_V_DIM:(h + 1) * DSA_V_DIM] = jnp.dot(
            o_ref[h], w_ref[h], preferred_element_type=F32).astype(y_ref.dtype)


def _uv_project(o_lat, w_uv_b, bsz, s, tm=512):
    spt = s // tm
    return pl.pallas_call(
        _uv_kernel,
        grid=(bsz * spt,),
        in_specs=[pl.BlockSpec((None, DSA_HEADS, tm, KV_LORA), lambda i: (i // spt, 0, i % spt, 0)),
                  pl.BlockSpec((DSA_HEADS, KV_LORA, DSA_V_DIM), lambda i: (0, 0, 0))],
        out_specs=pl.BlockSpec((tm, DSA_HEADS * DSA_V_DIM), lambda i: (i, 0)),
        out_shape=jax.ShapeDtypeStruct((bsz * s, DSA_HEADS * DSA_V_DIM), BF16),
        compiler_params=_params("arbitrary"),
        name="uv_project",
    )(o_lat, w_uv_b)


def _merge_kernel(h_ref, a_ref, b_ref, wga_ref, wgb_ref, wua_ref, wub_ref, o_ref,
                  cga_ref, cgb_ref, cua_ref, cub_ref):
    @pl.when(pl.program_id(1) == 0)
    def _():
        cga_ref[...] = wga_ref[...].astype(BF16)
        cgb_ref[...] = wgb_ref[...].astype(BF16)
        cua_ref[...] = wua_ref[...].astype(BF16)
        cub_ref[...] = wub_ref[...].astype(BF16)

    h = h_ref[...]
    ga = lax.dot_general(h, cga_ref[...], _NT_DIMS, preferred_element_type=F32)
    gb = lax.dot_general(h, cgb_ref[...], _NT_DIMS, preferred_element_type=F32)
    ya = jnp.dot(a_ref[...], cua_ref[...], preferred_element_type=F32)
    yb = jnp.dot(b_ref[...], cub_ref[...], preferred_element_type=F32)
    o_ref[...] = (jax.nn.sigmoid(ga) * ya + jax.nn.sigmoid(gb) * yb).astype(o_ref.dtype)


def _gated_merge(h1, att_a, att_b, w_t, row_ga, row_gb, w_up_a, w_up_b, tm=512, tn=256):
    n, d = h1.shape
    ka, kb = att_a.shape[1], att_b.shape[1]
    return pl.pallas_call(
        _merge_kernel,
        grid=(d // tn, n // tm),
        in_specs=[pl.BlockSpec((tm, d), lambda j, i: (i, 0)),
                  pl.BlockSpec((tm, ka), lambda j, i: (i, 0)),
                  pl.BlockSpec((tm, kb), lambda j, i: (i, 0)),
                  _w_rows_spec(tn, d, row_ga, lambda j, i: j),
                  _w_rows_spec(tn, d, row_gb, lambda j, i: j),
                  pl.BlockSpec((ka, tn), lambda j, i: (0, j)),
                  pl.BlockSpec((kb, tn), lambda j, i: (0, j))],
        out_specs=pl.BlockSpec((tm, tn), lambda j, i: (i, j)),
        out_shape=jax.ShapeDtypeStruct((n, d), BF16),
        scratch_shapes=[pltpu.VMEM((tn, d), BF16), pltpu.VMEM((tn, d), BF16),
                        pltpu.VMEM((ka, tn), BF16), pltpu.VMEM((kb, tn), BF16)],
        compiler_params=_params("arbitrary", "arbitrary"),
        name="gated_merge",
    )(h1, att_a, att_b, w_t, w_t, w_up_a, w_up_b)


def _mem_kernel(x_ref, g_ref, wq_ref, kt_ref, v_ref, wom_ref, o_ref):
    x = x_ref[...]
    hn = (x * lax.rsqrt(jnp.mean(x * x, axis=-1, keepdims=True) + EPS) * g_ref[...]).astype(BF16)
    q = jnp.dot(hn, wq_ref[...], preferred_element_type=F32).astype(BF16)
    scale = MEM_HEAD_DIM ** -0.5
    outs = []
    for h in range(MEM_HEADS):
        sl = slice(h * MEM_HEAD_DIM, (h + 1) * MEM_HEAD_DIM)
        s = jnp.dot(q[:, sl], kt_ref[sl, :], preferred_element_type=F32) * scale
        m = jnp.max(s, axis=-1, keepdims=True)
        p = jnp.exp(s - m)
        p = p / jnp.sum(p, axis=-1, keepdims=True)
        outs.append(jnp.dot(p.astype(BF16), v_ref[:, sl], preferred_element_type=F32).astype(BF16))
    o = jnp.concatenate(outs, axis=-1)
    o_ref[...] = x + jnp.dot(o, wom_ref[...], preferred_element_type=F32)


def _mem_attention(x1, g_x, w_qm_b, km_t, kv_m, w_om_b, bsz, s, n_mem, tm=256):
    spt = s // tm
    hd = MEM_HEADS * MEM_HEAD_DIM
    d = x1.shape[1]
    return pl.pallas_call(
        _mem_kernel,
        grid=(bsz, spt),
        in_specs=[pl.BlockSpec((tm, d), lambda b, i: (b * spt + i, 0)),
                  pl.BlockSpec((1, d), lambda b, i: (0, 0)),
                  pl.BlockSpec((d, hd), lambda b, i: (0, 0)),
                  pl.BlockSpec((None, hd, n_mem), lambda b, i: (b, 0, 0)),
                  pl.BlockSpec((n_mem, hd), lambda b, i: (b, 1)),
                  pl.BlockSpec((hd, d), lambda b, i: (0, 0))],
        out_specs=pl.BlockSpec((tm, d), lambda b, i: (b * spt + i, 0)),
        out_shape=jax.ShapeDtypeStruct(x1.shape, F32),
        compiler_params=_params("arbitrary", "arbitrary"),
        name="memory_cross_attention",
    )(x1, g_x.reshape(1, d), w_qm_b, km_t, kv_m, w_om_b)


def _pack_bf16_pairs(x):
    c = x.shape[1] // 2
    lo = pltpu.bitcast(x[:, :c].astype(BF16).astype(F32), jnp.uint32)
    hi = pltpu.bitcast(x[:, c:].astype(BF16).astype(F32), jnp.uint32)
    return lax.shift_right_logical(lo, jnp.uint32(16)) | hi


def _unpack_bf16_pairs(w):
    lo = pltpu.bitcast(lax.shift_left(w, jnp.uint32(16)), F32).astype(BF16)
    hi = pltpu.bitcast(w & jnp.uint32(0xFFFF0000), F32).astype(BF16)
    return lo, hi


def _router_kernel(x_ref, g_ref, wr_ref, br_ref, h_ref, idx_ref, wt_ref, cnt_ref, carry_ref):
    @pl.when(pl.program_id(0) == 0)
    def _():
        carry_ref[...] = jnp.zeros_like(carry_ref)

    x = x_ref[...]
    hn = x * lax.rsqrt(jnp.mean(x * x, axis=-1, keepdims=True) + EPS) * g_ref[...]
    h_ref[...] = _pack_bf16_pairs(hn)
    logits = jnp.dot(hn, wr_ref[...], precision=lax.Precision.HIGHEST, preferred_element_type=F32) + br_ref[...]
    tm = x.shape[0]
    lane_i = lax.broadcasted_iota(jnp.int32, (tm, LANES), 1)
    lane = lane_i.astype(F32)
    lane_grp = lax.shift_right_logical(lane_i, EXPERTS_PER_GROUP.bit_length() - 1).astype(F32)
    big = 1e9
    is_g = (lane_i >= N_EXPERTS) & (lane_i < N_EXPERTS + N_GROUPS)
    lg = jnp.where(is_g, logits, -jnp.inf)
    mg = jnp.max(lg, axis=1, keepdims=True)
    gate_g = 1.0 / jnp.sum(jnp.exp(lg - mg), axis=1, keepdims=True)
    grp = jnp.min(jnp.where(lg == mg, lane, big), axis=1, keepdims=True) - N_EXPERTS
    in_grp = (lane_i < N_EXPERTS) & (lane_grp == grp)
    le = jnp.where(in_grp, logits, -jnp.inf)
    m1 = jnp.max(le, axis=1, keepdims=True)
    e1 = jnp.min(jnp.where(le == m1, lane, big), axis=1, keepdims=True)
    le2 = jnp.where(lane == e1, -jnp.inf, le)
    m2 = jnp.max(le2, axis=1, keepdims=True)
    e2 = jnp.min(jnp.where(le2 == m2, lane, big), axis=1, keepdims=True)
    r = jnp.exp(m2 - m1)
    w1 = gate_g * (1.0 / (1.0 + r))
    w2 = gate_g * (r / (1.0 + r))
    oh1 = (lane == e1).astype(F32)
    oh2 = (lane == e2).astype(F32)
    oh = oh1 + oh2
    tri = (lax.broadcasted_iota(jnp.int32, (tm, tm), 0) > lax.broadcasted_iota(jnp.int32, (tm, tm), 1)).astype(BF16)
    prefix = jnp.dot(tri, oh.astype(BF16), preferred_element_type=F32) + carry_ref[...]
    rank1 = jnp.sum(prefix * oh1, axis=1, keepdims=True)
    rank2 = jnp.sum(prefix * oh2, axis=1, keepdims=True)
    new_cnt = carry_ref[...] + jnp.sum(oh, axis=0, keepdims=True)
    carry_ref[...] = new_cnt
    cnt_ref[...] = new_cnt
    idx_ref[...] = jnp.where(lane_i == 0, e1, jnp.where(lane_i == 1, e2, jnp.where(
        lane_i == 2, rank1, jnp.where(lane_i == 3, rank2, 0.0)))).astype(jnp.int32)
    wt_ref[...] = jnp.where(lane_i == 0, w1, jnp.where(lane_i == 1, w2, 0.0))


def _router(x2, g, w_r, b_r, tm=256):
    n, d = x2.shape
    row = lambda w: pl.BlockSpec((tm, w), lambda i: (i, 0))
    return pl.pallas_call(
        _router_kernel,
        grid=(n // tm,),
        in_specs=[row(d), pl.BlockSpec((1, d), lambda i: (0, 0)), pl.BlockSpec((d, LANES), lambda i: (0, 0)),
                  pl.BlockSpec((1, LANES), lambda i: (0, 0))],
        out_specs=[row(d // 2), row(LANES), row(LANES), pl.BlockSpec((1, LANES), lambda i: (0, 0))],
        out_shape=[jax.ShapeDtypeStruct((n, d // 2), jnp.uint32), jax.ShapeDtypeStruct((n, LANES), jnp.int32),
                   jax.ShapeDtypeStruct((n, LANES), F32), jax.ShapeDtypeStruct((1, LANES), F32)],
        scratch_shapes=[pltpu.VMEM((1, LANES), F32)],
        compiler_params=_params("arbitrary"),
        name="moe_router",
    )(x2, g.reshape(1, d), w_r, b_r)


def _row_copy(src_hbm, row, dst, r, sem):
    return pltpu.make_async_copy(src_hbm.at[pl.ds(row, 1), :], dst.at[pl.ds(r, 1), :], sem)


def _sorted_rows(m):
    return m + N_EXPERTS * SUBLANES + ITEM_ROWS


def _dispatch_kernel(dest_ref, seg0_ref, cnt_ref, h_ref, x_hbm, zero_ref, sem, *, tm, m):
    i = pl.program_id(0)

    @pl.when(i == 0)
    def _():
        zero_ref[...] = jnp.zeros_like(zero_ref)
        tail = [pltpu.make_async_copy(zero_ref, x_hbm.at[pl.ds(r0, ITEM_SUB), :], sem)
                for r0 in range(m, _sorted_rows(m), ITEM_SUB)]
        for c in tail:
            c.start()
        for c in tail:
            c.wait()

        def fill_gap(e, total):
            gap = (-cnt_ref[e]) & (SUBLANES - 1)

            def one(r, c):
                _row_copy(zero_ref, 0, x_hbm, seg0_ref[e] + cnt_ref[e] + r, sem).start()
                return c

            lax.fori_loop(0, gap, one, 0)
            return total + gap

        n_gap = lax.fori_loop(0, N_EXPERTS, fill_gap, 0)

        def wait_gap(r, c):
            _row_copy(zero_ref, 0, x_hbm, 0, sem).wait()
            return c

        lax.fori_loop(0, n_gap, wait_gap, 0)

    def issue(r, c):
        t = i * tm + r
        for sl in range(TOPK_IN_GROUP):
            _row_copy(h_ref, r, x_hbm, dest_ref[TOPK_IN_GROUP * t + sl], sem).start()
        return c

    lax.fori_loop(0, tm, issue, 0)

    def wait(r, c):
        _row_copy(h_ref, 0, x_hbm, 0, sem).wait()
        return c

    lax.fori_loop(0, TOPK_IN_GROUP * tm, wait, 0)


def _dispatch(dest, seg_start, counts, h3p, tm=256):
    n, c = h3p.shape
    m = dest.shape[0]
    assert (_sorted_rows(m) - m) % ITEM_SUB == 0
    grid_spec = pltpu.PrefetchScalarGridSpec(
        num_scalar_prefetch=3,
        grid=(n // tm,),
        in_specs=[pl.BlockSpec((tm, c), lambda i, d, s0, cn: (i, 0))],
        out_specs=pl.BlockSpec(memory_space=pl.ANY),
        scratch_shapes=[pltpu.VMEM((ITEM_SUB, c), h3p.dtype), pltpu.SemaphoreType.DMA(())],
    )
    return pl.pallas_call(
        functools.partial(_dispatch_kernel, tm=tm, m=m),
        grid_spec=grid_spec,
        out_shape=jax.ShapeDtypeStruct((_sorted_rows(m), c), h3p.dtype),
        compiler_params=_params("arbitrary"),
        name="moe_dispatch",
    )(dest, seg_start, counts, h3p)


def _expert_kernel(ie_ref, is_ref, in_ref, ni_ref, x_hbm, wg_ref, wu_ref, wd_ref, y_hbm,
                   xin_ref, xb_ref, acc_ref, in_sem, out_sem, *, m, nj):
    i = pl.program_id(0)
    j = pl.program_id(1)
    n_items = ni_ref[0]
    valid = i < n_items
    n = in_ref[i]
    slot = lax.rem(i, 2)
    half = xb_ref.shape[1] // 2
    n_sub = ITEM_ROWS // ITEM_SUB

    def first_row(item):
        return pl.multiple_of(is_ref[item], SUBLANES)

    def in_copy(item):
        return pltpu.make_async_copy(x_hbm.at[pl.ds(first_row(item), ITEM_ROWS), :], xin_ref, in_sem)

    def out_each(item, sl, fn):
        for sub in range(n_sub):
            @pl.when(sub * ITEM_SUB < in_ref[item])
            def _():
                fn(pltpu.make_async_copy(
                    acc_ref.at[sl, pl.ds(sub * ITEM_SUB, ITEM_SUB), :],
                    y_hbm.at[pl.ds(first_row(item) + sub * ITEM_SUB, ITEM_SUB), :], out_sem.at[sl]))

    @pl.when((i == 0) & (j == 0))
    def _():
        acc_ref[1, :ITEM_SUB, :] = jnp.zeros((ITEM_SUB, acc_ref.shape[2]), F32)
        tail = [pltpu.make_async_copy(acc_ref.at[1, pl.ds(0, ITEM_SUB), :],
                                      y_hbm.at[pl.ds(r0, ITEM_SUB), :], out_sem.at[1])
                for r0 in range(m, _sorted_rows(m), ITEM_SUB)]
        for c in tail:
            c.start()
        for c in tail:
            c.wait()
        in_copy(0).start()

    @pl.when(valid & (j == 0))
    def _():
        in_copy(i).wait()
        lo, hi = _unpack_bf16_pairs(xin_ref[...])
        xb_ref[:, :half] = lo
        xb_ref[:, half:] = hi

        @pl.when(i + 1 < n_items)
        def _():
            in_copy(i + 1).start()

        acc_ref[slot] = jnp.zeros(acc_ref.shape[1:], F32)

    @pl.when(valid)
    def _():
        wg = wg_ref[...].astype(BF16)
        wu = wu_ref[...].astype(BF16)
        wd = wd_ref[...].astype(BF16)
        n_blocks = (n + ITEM_SUB - 1) // ITEM_SUB
        for nb in range(1, n_sub + 1):
            @pl.when(n_blocks == nb)
            def _():
                rows = pl.ds(0, nb * ITEM_SUB)
                xb = xb_ref[rows, :]
                g = jnp.dot(xb, wg, preferred_element_type=F32)
                u = jnp.dot(xb, wu, preferred_element_type=F32)
                hm = (g * jax.nn.sigmoid(g) * u).astype(BF16)
                acc_ref[slot, rows, :] += jnp.dot(hm, wd, preferred_element_type=F32)

    @pl.when(valid & (j == nj - 1))
    def _():
        @pl.when(i > 0)
        def _():
            out_each(i - 1, 1 - slot, lambda c: c.wait())

        out_each(i, slot, lambda c: c.start())

        @pl.when(i == n_items - 1)
        def _():
            out_each(i, slot, lambda c: c.wait())


def _experts(item_e, item_start, item_n, n_items, x_sorted, w_gate, w_up, w_down, max_items, m):
    d = w_gate.shape[1]
    nj = D_EXPERT // F_CHUNK

    def jj(i, j, ni):
        return jnp.where(i < ni[0], j, nj - 1)

    grid_spec = pltpu.PrefetchScalarGridSpec(
        num_scalar_prefetch=4,
        grid=(max_items, nj),
        in_specs=[pl.BlockSpec(memory_space=pl.ANY),
                  pl.BlockSpec((None, d, F_CHUNK), lambda i, j, ie, s, n, ni: (ie[i], 0, jj(i, j, ni))),
                  pl.BlockSpec((None, d, F_CHUNK), lambda i, j, ie, s, n, ni: (ie[i], 0, jj(i, j, ni))),
                  pl.BlockSpec((None, F_CHUNK, d), lambda i, j, ie, s, n, ni: (ie[i], jj(i, j, ni), 0))],
        out_specs=pl.BlockSpec(memory_space=pl.ANY),
        scratch_shapes=[pltpu.VMEM((ITEM_ROWS, d // 2), jnp.uint32), pltpu.VMEM((ITEM_ROWS, d), BF16),
                        pltpu.VMEM((2, ITEM_ROWS, d), F32),
                        pltpu.SemaphoreType.DMA(()), pltpu.SemaphoreType.DMA((2,))],
    )
    return pl.pallas_call(
        functools.partial(_expert_kernel, m=m, nj=nj),
        grid_spec=grid_spec,
        out_shape=jax.ShapeDtypeStruct((_sorted_rows(m), d), F32),
        compiler_params=_params("arbitrary", "arbitrary"),
        name="moe_experts",
    )(item_e, item_start, item_n, n_items, x_sorted, w_gate, w_up, w_down)


def _combine_kernel(dest_ref, y_hbm, x_ref, wt_ref, g_ref, o_ref, yv_ref, sem, *, tm):
    i = pl.program_id(0)
    slot = lax.rem(i, 2)

    def issue_tile(tile, sl):
        def issue(r, c):
            base = TOPK_IN_GROUP * (tile * tm + r)
            for k in range(TOPK_IN_GROUP):
                _row_copy(y_hbm, dest_ref[base + k], yv_ref.at[sl, k], r, sem.at[sl]).start()
            return c

        lax.fori_loop(0, tm, issue, 0)

    @pl.when(i == 0)
    def _():
        issue_tile(0, 0)

    @pl.when(i + 1 < pl.num_programs(0))
    def _():
        issue_tile(i + 1, 1 - slot)

    def wait(r, c):
        _row_copy(y_hbm, 0, yv_ref.at[slot, 0], 0, sem.at[slot]).wait()
        return c

    lax.fori_loop(0, TOPK_IN_GROUP * tm, wait, 0)
    w = wt_ref[...]
    moe = yv_ref[slot, 0] * w[:, 0:1]
    for k in range(1, TOPK_IN_GROUP):
        moe = moe + yv_ref[slot, k] * w[:, k:k + 1]
    x = x_ref[...] + moe
    o_ref[...] = x * lax.rsqrt(jnp.mean(x * x, axis=-1, keepdims=True) + EPS) * g_ref[...]


def _combine(dest, y_sorted, x2, wts, g_final, tm=128):
    n, d = x2.shape
    grid_spec = pltpu.PrefetchScalarGridSpec(
        num_scalar_prefetch=1,
        grid=(n // tm,),
        in_specs=[pl.BlockSpec(memory_space=pl.ANY),
                  pl.BlockSpec((tm, d), lambda i, dr: (i, 0)),
                  pl.BlockSpec((tm, LANES), lambda i, dr: (i, 0)),
                  pl.BlockSpec((1, d), lambda i, dr: (0, 0))],
        out_specs=pl.BlockSpec((tm, d), lambda i, dr: (i, 0)),
        scratch_shapes=[pltpu.VMEM((2, TOPK_IN_GROUP, tm, d), F32), pltpu.SemaphoreType.DMA((2,))],
    )
    return pl.pallas_call(
        functools.partial(_combine_kernel, tm=tm),
        grid_spec=grid_spec,
        out_shape=jax.ShapeDtypeStruct((n, d), F32),
        compiler_params=_params("arbitrary"),
        name="moe_combine_final_norm",
    )(dest, y_sorted, x2, wts, g_final.reshape(1, d))


def _pad_cols(w, width):
    return jnp.pad(w, ((0, 0), (0, width - w.shape[1])))


def _mixer(x2d, bsz, s, g_norm, w_in, b_f, g_q_lat, g_kv_lat, g_idx_k, b_idx_k, w_uq, w_idx_q, w_uk, w_uv,
           w_up_a, w_up_b, w_out):
    d = x2d.shape[1]
    topk = min(DSA_TOPK_MAX, s // 4)
    h1 = _rmsnorm(x2d, g_norm, BF16)
    c_small = 3 * FOX_WIDTH
    o_fa, o_cq = c_small, c_small + FOX_HEADS
    o_ckv = o_cq + Q_LORA
    o_kr = o_ckv + KV_LORA
    o_ik = o_kr + DSA_ROPE_DIM
    o_iw = o_ik + IDX_DIM
    o_ga = o_iw + IDX_HEADS
    o_gb = o_ga + d
    w_t = jnp.swapaxes(w_in, 0, 1)
    pad_rows = lambda w: jnp.pad(w, ((0, LANES - w.shape[0]), (0, 0)))
    w_small_t = jnp.concatenate([
        w_t[o_cq:o_ckv], w_t[o_ckv:o_kr], w_t[o_ik:o_iw],
        pad_rows(w_t[o_kr:o_ik]), pad_rows(w_t[o_fa:o_cq]), pad_rows(w_t[o_iw:o_ga])], axis=0).astype(BF16)

    qkv = _mm(h1, w_t, 3 * FOX_WIDTH, w_rows=True, scaled_cols=(FOX_WIDTH, FOX_QSCALE), name="proj_qkv")
    small = _mm(h1, w_small_t, SMALL_W, out_dtype=F32, tm=512, tn=SMALL_W, w_rows=True, name="proj_small")

    tabs64 = _rope_tables(s, DSA_ROPE_DIM)
    tabs128 = _rope_tables(s, IDX_DIM)
    cq_n, kv_cat, k_idx, w_idx, fa = _prep(small, g_q_lat, g_kv_lat, g_idx_k, b_idx_k, tabs64, tabs128, s)

    cum = _fox_cum(fa, _pad_cols(b_f.reshape(1, -1), LANES), bsz, s)
    att_a = _fox_attention(qkv, cum, bsz, s)

    w_uq3 = w_uq.reshape(Q_LORA, DSA_HEADS, DSA_NOPE_DIM + DSA_ROPE_DIM)
    w_uq_nope = w_uq3[:, :, :DSA_NOPE_DIM].reshape(Q_LORA, DSA_HEADS * DSA_NOPE_DIM).astype(BF16)
    w_uq_rope = jnp.pad(w_uq3[:, :, DSA_NOPE_DIM:], ((0, 0), (0, 0), (0, LANES - DSA_ROPE_DIM))).reshape(
        Q_LORA, DSA_HEADS * LANES).astype(BF16)
    q_cat = _q_latent(cq_n, w_uq_nope, w_uq_rope, w_uk.astype(BF16), tabs64, bsz, s)
    q_idx = _mm(cq_n, w_idx_q, IDX_HEADS * IDX_DIM, heads=(bsz, s), rope_tabs=tabs128, name="proj_q_idx")
    w_idx_t = jnp.transpose(w_idx.reshape(bsz, s, LANES)[:, :, :IDX_HEADS], (0, 2, 1))
    bias = _indexer_mask(q_idx, k_idx.reshape(bsz, s, IDX_DIM), w_idx_t, bsz, s, topk)
    kv3 = kv_cat.reshape(bsz, s, KV_PAD)
    kv_t = jnp.transpose(kv3.reshape(bsz, s // KEY_CHUNK, KEY_CHUNK, KV_PAD), (0, 1, 3, 2))
    o_lat = _dsa_attention(q_cat, kv_t, kv3, bias, bsz, s)
    att_b = _uv_project(o_lat, w_uv.astype(BF16), bsz, s)

    merged = _gated_merge(h1, att_a, att_b, w_t, o_ga, o_gb, w_up_a, w_up_b)
    return _mm(merged, w_out, d, out_dtype=F32, res=x2d, name="proj_out")


def _memory_layer(x1, mem, bsz, s, g_x, g_m, w_qm, w_km, w_vm, w_om):
    n_mem = mem.shape[1]
    d = x1.shape[1]
    hd = MEM_HEADS * MEM_HEAD_DIM
    mem_n = _rmsnorm(mem.reshape(bsz * n_mem, d), g_m, BF16)
    kv_m = _mm(mem_n, jnp.concatenate([w_km, w_vm], axis=1), 2 * hd, tm=bsz * n_mem, name="proj_kv_mem")
    km_t = jnp.transpose(kv_m[:, :hd].reshape(bsz, n_mem, hd), (0, 2, 1))
    return _mem_attention(x1, g_x, w_qm.astype(BF16), km_t, kv_m, w_om.astype(BF16), bsz, s, n_mem)


def _moe_layer(x2, g_ffn, w_rg, b_rg, w_re, b_re, w_gate, w_up, w_down, g_final):
    n, d = x2.shape
    m = n * TOPK_IN_GROUP
    w_r = _pad_cols(jnp.concatenate([w_re, w_rg], axis=1), LANES)
    b_r = _pad_cols(jnp.concatenate([b_re, b_rg]).reshape(1, -1), LANES)
    h3p, ridx, wts, cnt = _router(x2, g_ffn, w_r, b_r)
    counts = cnt[0, :N_EXPERTS].astype(jnp.int32)
    e_flat = ridx[:, :TOPK_IN_GROUP].reshape(m)
    rank_flat = ridx[:, TOPK_IN_GROUP:2 * TOPK_IN_GROUP].reshape(m)
    seg = (counts + SUBLANES - 1) // SUBLANES * SUBLANES
    starts = (jnp.cumsum(seg) - seg).astype(jnp.int32)
    n_it = (counts + ITEM_ROWS - 1) // ITEM_ROWS
    it_end = jnp.cumsum(n_it)
    it_first = it_end - n_it
    max_items = (m + N_EXPERTS * (ITEM_ROWS - 1)) // ITEM_ROWS
    t = jnp.arange(max_items, dtype=jnp.int32)
    item_e = jnp.minimum(jnp.searchsorted(it_end, t, side='right'), N_EXPERTS - 1).astype(jnp.int32)
    k_in = t - it_first[item_e]
    item_start = (starts[item_e] + k_in * ITEM_ROWS).astype(jnp.int32)
    item_n = jnp.clip(counts[item_e] - k_in * ITEM_ROWS, 0, ITEM_ROWS).astype(jnp.int32)
    n_items = it_end[-1:].astype(jnp.int32)
    last = jnp.maximum(n_items[0] - 1, 0)
    item_e = jnp.where(t < n_items[0], item_e, item_e[last])
    item_start = jnp.where(t < n_items[0], item_start, 0)
    item_n = jnp.where(t < n_items[0], item_n, 0)
    dest = (starts[e_flat] + rank_flat).astype(jnp.int32)
    x_sorted = _dispatch(dest, starts, counts, h3p)
    y_sorted = _experts(item_e, item_start, item_n, n_items, x_sorted, w_gate, w_up, w_down, max_items, m)
    return _combine(dest, y_sorted, x2, wts, g_final)


def kernel(x, mem, g_norm_mix, w_in, b_f, g_q_lat, g_kv_lat, g_idx_k, b_idx_k, w_uq, w_idx_q, w_uk, w_uv,
           w_up_a, w_up_b, w_out, g_norm_mem_x, g_mem, w_qm, w_km, w_vm, w_om, g_norm_ffn,
           w_rg, b_rg, w_re, b_re, w_gate, w_up, w_down, g_final):
    bsz, s, d = x.shape
    depth = w_in.shape[0]
    assert depth == 1, "the final norm is fused into the last MoE combine"
    l = 0
    x2d = x.reshape(bsz * s, d)
    x1 = _mixer(x2d, bsz, s, g_norm_mix[l], w_in[l], b_f[l], g_q_lat[l], g_kv_lat[l], g_idx_k[l], b_idx_k[l],
                w_uq[l], w_idx_q[l], w_uk[l], w_uv[l], w_up_a[l], w_up_b[l], w_out[l])
    x2 = _memory_layer(x1, mem, bsz, s, g_norm_mem_x[l], g_mem[l], w_qm[l], w_km[l], w_vm[l], w_om[l])
    out = _moe_layer(x2, g_norm_ffn[l], w_rg[l], b_rg[l], w_re[l], b_re[l], w_gate[l], w_up[l], w_down[l], g_final)
    return out.reshape(bsz, s, d)
```

```python
import functools

import jax
import jax.numpy as jnp
import numpy as np
from jax import lax
from jax.experimental import pallas as pl
from jax.experimental.pallas import tpu as pltpu

F32 = jnp.float32
BF16 = jnp.bfloat16

EPS = 1e-6
ROPE_THETA = 10000.0
FOX_HEADS = 16
FOX_HEAD_DIM = 128
DSA_HEADS = 16
DSA_NOPE_DIM = 128
DSA_ROPE_DIM = 64
DSA_V_DIM = 128
Q_LORA = 1024
KV_LORA = 256
IDX_HEADS = 32
IDX_DIM = 128
DSA_TOPK_MAX = 256
MEM_HEADS = 4
MEM_HEAD_DIM = 128
N_GROUPS = 8
EXPERTS_PER_GROUP = 8
N_EXPERTS = N_GROUPS * EXPERTS_PER_GROUP
TOPK_IN_GROUP = 2
D_EXPERT = 512

FOX_WIDTH = FOX_HEADS * FOX_HEAD_DIM
LANES = 128
SUBLANES = 8
KV_PAD = KV_LORA + LANES
NEG_BIG = -1e30
LOG2E = 1.4426950408889634
FOX_QSCALE = FOX_HEAD_DIM ** -0.5 * LOG2E
FOX_HB = 2
DSA_QSCALE = (DSA_NOPE_DIM + DSA_ROPE_DIM) ** -0.5 * LOG2E
VMEM_LIMIT_BYTES = 56 * 1024 * 1024
ITEM_ROWS = 384
ITEM_SUB = 128
F_CHUNK = 256
KEY_CHUNK = 256


def _params(*sem):
    return pltpu.CompilerParams(dimension_semantics=sem, vmem_limit_bytes=VMEM_LIMIT_BYTES)


def _rmsnorm_kernel(x_ref, g_ref, o_ref):
    x = x_ref[...].astype(F32)
    ms = jnp.mean(x * x, axis=-1, keepdims=True)
    o_ref[...] = (x * lax.rsqrt(ms + EPS) * g_ref[...]).astype(o_ref.dtype)


def _rmsnorm(x2d, g, out_dtype, tm=256):
    n, d = x2d.shape
    return pl.pallas_call(
        _rmsnorm_kernel,
        grid=(n // tm,),
        in_specs=[pl.BlockSpec((tm, d), lambda i: (i, 0)), pl.BlockSpec((1, d), lambda i: (0, 0))],
        out_specs=pl.BlockSpec((tm, d), lambda i: (i, 0)),
        out_shape=jax.ShapeDtypeStruct((n, d), out_dtype),
        compiler_params=_params("arbitrary"),
        name="rmsnorm",
    )(x2d, g.reshape(1, d).astype(F32))


_NT_DIMS = (((1,), (1,)), ((), ()))


def _w_rows_spec(tn, k, row0, index_of_j):
    if row0 % tn == 0:
        return pl.BlockSpec((tn, k), lambda *g: (index_of_j(*g) + row0 // tn, 0))
    assert row0 % SUBLANES == 0
    return pl.BlockSpec((pl.Element(tn), pl.Element(k)),
                        lambda *g: (pl.multiple_of(index_of_j(*g) * tn + row0, SUBLANES), 0))


def _mm_kernel(a_ref, b_ref, *rest, has_res, head_out, rope, w_rows, col_scale):
    pos = 0
    res_ref = cos_ref = sin_ref = None
    if has_res:
        res_ref = rest[pos]
        pos += 1
    if rope:
        cos_ref, sin_ref = rest[pos], rest[pos + 1]
        pos += 2
    o_ref = rest[pos]
    if b_ref.dtype == BF16:
        wb_ref = b_ref
    else:
        wb_ref = rest[pos + 1]

        @pl.when(pl.program_id(1) == 0)
        def _():
            wb_ref[...] = b_ref[...].astype(BF16)

    if w_rows:
        acc = lax.dot_general(a_ref[...], wb_ref[...], _NT_DIMS, preferred_element_type=F32)
    else:
        acc = jnp.dot(a_ref[...], wb_ref[...], preferred_element_type=F32)
    if col_scale is not None:
        acc = acc * jnp.where(pl.program_id(0) < col_scale[0], col_scale[1], 1.0)
    if has_res:
        acc = acc + res_ref[...]
    if head_out:
        for hh in range(acc.shape[1] // LANES):
            part = acc[:, hh * LANES:(hh + 1) * LANES]
            if rope:
                part = part * cos_ref[...] + pltpu.roll(part, LANES // 2, 1) * sin_ref[...]
            o_ref[hh] = part.astype(o_ref.dtype)
    else:
        o_ref[...] = acc.astype(o_ref.dtype)


def _mm(a, w, n_cols, *, col0=0, out_dtype=BF16, tm=1024, tn=512, res=None, heads=None, rope_tabs=None,
        w_rows=False, scaled_cols=None, name="mm"):
    m, k = a.shape
    tm = min(tm, m if heads is None else heads[1])
    assert n_cols % tn == 0 and m % tm == 0
    if w_rows:
        w_spec = _w_rows_spec(tn, k, col0, lambda j, i: j)
    else:
        assert col0 % tn == 0
        w_spec = pl.BlockSpec((k, tn), lambda j, i: (0, j + col0 // tn))
    in_specs = [pl.BlockSpec((tm, k), lambda j, i: (i, 0)), w_spec]
    args = [a, w]
    if res is not None:
        in_specs.append(pl.BlockSpec((tm, tn), lambda j, i: (i, j)))
        args.append(res)
    if heads is not None:
        bsz, s = heads
        assert s % tm == 0
        spt = s // tm
        if rope_tabs is not None:
            in_specs += [pl.BlockSpec((tm, LANES), lambda j, i: (i % spt, 0))] * 2
            args += list(rope_tabs)
        hpt = tn // LANES
        out_spec = pl.BlockSpec((None, hpt, tm, LANES), lambda j, i: (i // spt, j, i % spt, 0))
        out_shape = jax.ShapeDtypeStruct((bsz, n_cols // LANES, s, LANES), out_dtype)
    else:
        out_spec = pl.BlockSpec((tm, tn), lambda j, i: (i, j))
        out_shape = jax.ShapeDtypeStruct((m, n_cols), out_dtype)
    col_scale = None
    if scaled_cols is not None:
        assert scaled_cols[0] % tn == 0
        col_scale = (scaled_cols[0] // tn, scaled_cols[1])
    kern = functools.partial(_mm_kernel, has_res=res is not None, head_out=heads is not None,
                             rope=rope_tabs is not None, w_rows=w_rows, col_scale=col_scale)
    return pl.pallas_call(
        kern,
        grid=(n_cols // tn, m // tm),
        in_specs=in_specs,
        out_specs=out_spec,
        out_shape=out_shape,
        scratch_shapes=[] if w.dtype == BF16 else [pltpu.VMEM((tn, k) if w_rows else (k, tn), BF16)],
        compiler_params=_params("arbitrary", "arbitrary"),
        name=name,
    )(*args)


def _rope_tables(s, d):
    inv = jnp.power(ROPE_THETA, -jnp.arange(0, d, 2, dtype=F32) / d)
    ang = jnp.arange(s, dtype=jnp.int32).astype(F32)[:, None] * inv[None, :]
    cos, sin = jnp.cos(ang), jnp.sin(ang)
    cos_t = jnp.tile(jnp.concatenate([cos, cos], axis=-1), (1, LANES // d))
    sin_t = jnp.tile(jnp.concatenate([-sin, sin], axis=-1), (1, LANES // d))
    return cos_t, sin_t


def _rot_half64(x):
    lane = lax.broadcasted_iota(jnp.int32, x.shape, 1)
    return jnp.where((lane % 64) < 32, pltpu.roll(x, 96, 1), pltpu.roll(x, 32, 1))


_C_CQ = 0
_C_CKV = Q_LORA
_C_IK = _C_CKV + KV_LORA
_C_KR = _C_IK + IDX_DIM
_C_FA = _C_KR + LANES
_C_IW = _C_FA + LANES
SMALL_W = _C_IW + LANES


def _prep_kernel(s_ref, gq_ref, gkv_ref, gik_ref, bik_ref, c64_ref, s64_ref, c128_ref, s128_ref,
                 cq_ref, kv_ref, kidx_ref, widx_ref, fa_ref):
    cq = s_ref[:, _C_CQ:_C_CQ + Q_LORA]
    cq_ref[...] = (cq * lax.rsqrt(jnp.mean(cq * cq, axis=-1, keepdims=True) + EPS) * gq_ref[...]).astype(BF16)
    ckv = s_ref[:, _C_CKV:_C_CKV + KV_LORA]
    kv_ref[:, :KV_LORA] = (ckv * lax.rsqrt(jnp.mean(ckv * ckv, axis=-1, keepdims=True) + EPS)
                           * gkv_ref[...]).astype(BF16)
    kr = s_ref[:, _C_KR:_C_KR + LANES]
    kv_ref[:, KV_LORA:] = (kr * c64_ref[...] + _rot_half64(kr) * s64_ref[...]).astype(BF16)
    ik = s_ref[:, _C_IK:_C_IK + IDX_DIM]
    mu = jnp.mean(ik, axis=-1, keepdims=True)
    xc = ik - mu
    ikn = xc * lax.rsqrt(jnp.mean(xc * xc, axis=-1, keepdims=True) + EPS) * gik_ref[...] + bik_ref[...]
    kidx_ref[...] = (ikn * c128_ref[...] + pltpu.roll(ikn, LANES // 2, 1) * s128_ref[...]).astype(BF16)
    widx_ref[...] = s_ref[:, _C_IW:_C_IW + LANES] * (IDX_HEADS * IDX_DIM) ** -0.5
    fa_ref[...] = s_ref[:, _C_FA:_C_FA + LANES]


def _prep(small, g_q, g_kv, g_ik, b_ik, tabs64, tabs128, s, tm=256):
    n = small.shape[0]
    spt = s // tm
    row = lambda w: pl.BlockSpec((tm, w), lambda i: (i, 0))
    vec = lambda w: pl.BlockSpec((1, w), lambda i: (0, 0))
    tab = pl.BlockSpec((tm, LANES), lambda i: (i % spt, 0))
    return pl.pallas_call(
        _prep_kernel,
        grid=(n // tm,),
        in_specs=[row(SMALL_W), vec(Q_LORA), vec(KV_LORA), vec(IDX_DIM), vec(IDX_DIM), tab, tab, tab, tab],
        out_specs=[row(Q_LORA), row(KV_PAD), row(IDX_DIM), row(LANES), row(LANES)],
        out_shape=[jax.ShapeDtypeStruct((n, Q_LORA), BF16), jax.ShapeDtypeStruct((n, KV_PAD), BF16),
                   jax.ShapeDtypeStruct((n, IDX_DIM), BF16), jax.ShapeDtypeStruct((n, LANES), F32),
                   jax.ShapeDtypeStruct((n, LANES), F32)],
        compiler_params=_params("arbitrary"),
        name="mixer_prep",
    )(small, g_q.reshape(1, -1), g_kv.reshape(1, -1), g_ik.reshape(1, -1), b_ik.reshape(1, -1),
      tabs64[0], tabs64[1], tabs128[0], tabs128[1])


def _cum_kernel(fa_ref, bf_ref, cum_ref, carry_ref):
    @pl.when(pl.program_id(1) == 0)
    def _():
        carry_ref[...] = jnp.zeros_like(carry_ref)

    z = fa_ref[...] + bf_ref[...]
    logf = jnp.minimum(z, 0.0) - jnp.log1p(jnp.exp(-jnp.abs(z)))
    t = z.shape[0]
    tri = (lax.broadcasted_iota(jnp.int32, (t, t), 0) >= lax.broadcasted_iota(jnp.int32, (t, t), 1)).astype(F32)
    c = jnp.dot(tri, logf, precision=lax.Precision.HIGHEST, preferred_element_type=F32) + carry_ref[...]
    cum_ref[...] = c
    carry_ref[...] = c[t - 1:t, :]


def _fox_cum(fa, b_f_pad, bsz, s, t=256):
    return pl.pallas_call(
        _cum_kernel,
        grid=(bsz, s // t),
        in_specs=[pl.BlockSpec((None, t, LANES), lambda b, i: (b, i, 0)),
                  pl.BlockSpec((1, LANES), lambda b, i: (0, 0))],
        out_specs=pl.BlockSpec((None, t, LANES), lambda b, i: (b, i, 0)),
        out_shape=jax.ShapeDtypeStruct((bsz, s, LANES), F32),
        scratch_shapes=[pltpu.VMEM((1, LANES), F32)],
        compiler_params=_params("arbitrary", "arbitrary"),
        name="fox_cumsum",
    )(fa.reshape(bsz, s, LANES), b_f_pad)


def _lane_fold(x, op):
    out = x[:, :LANES]
    for c in range(1, x.shape[1] // LANES):
        out = op(out, x[:, c * LANES:(c + 1) * LANES])
    return out


def _split3(x):
    t1 = x.astype(BF16)
    r = x - t1.astype(F32)
    t2 = r.astype(BF16)
    t3 = (r - t2.astype(F32)).astype(BF16)
    return t1, t2, t3


def _gate_lanes(c, own_first):
    t1, t2, t3 = (t.astype(F32) for t in _split3(c))
    lane = lax.broadcasted_iota(jnp.int32, c.shape, 1)
    a, b = (0, 3) if own_first else (3, 0)
    ones3 = jnp.where((lane >= b) & (lane < b + 3), 1.0, 0.0)
    out = jnp.where(lane == a, t1, jnp.where(lane == a + 1, t2, jnp.where(lane == a + 2, t3, ones3)))
    return out.astype(BF16)


def _fox_kernel(q_ref, k_ref, v_ref, cum_ref, o_ref, kf_ref, vf_ref, s_ref, *, t):
    hp = pl.program_id(1)
    i = pl.program_id(2)
    d = FOX_HEAD_DIM
    heads = range(FOX_HB)
    row = lax.broadcasted_iota(jnp.int32, (LANES, LANES), 0)

    def gate_column(c, hh):
        pick = (row == hp * FOX_HB + hh).astype(F32)
        return jnp.dot(c, pick, precision=lax.Precision.HIGHEST, preferred_element_type=F32) * LOG2E

    @pl.when(i == 0)
    def _():
        for hh in heads:
            kf_ref[hh, :, :d] = k_ref[:, hh * d:(hh + 1) * d]
            kf_ref[hh, :, d:] = _gate_lanes(-gate_column(cum_ref[...], hh), True)
            vf_ref[hh, :, :d] = v_ref[:, hh * d:(hh + 1) * d]
            vf_ref[hh, :, d:] = jnp.ones((vf_ref.shape[1], LANES), BF16)

    cum_q = cum_ref[pl.ds(pl.multiple_of(i * t, t), t), :]
    qs = [jnp.concatenate([q_ref[:, hh * d:(hh + 1) * d], _gate_lanes(gate_column(cum_q, hh), False)], axis=1)
          for hh in heads]

    def logits(hh, kc):
        kb = kf_ref[hh, pl.ds(pl.multiple_of(kc * t, t), t), :]
        return lax.dot_general(qs[hh], kb, _NT_DIMS, preferred_element_type=F32)

    def pass1(kc, m_lanes):
        out = []
        for hh in heads:
            s = logits(hh, kc)
            s_ref[hh, kc] = s
            out.append(jnp.maximum(m_lanes[hh], _lane_fold(s, jnp.maximum)))
        return tuple(out)

    m_lanes = lax.fori_loop(0, i, pass1, tuple(jnp.full((t, LANES), NEG_BIG, F32) for _ in heads))
    causal = lax.broadcasted_iota(jnp.int32, (t, t), 0) >= lax.broadcasted_iota(jnp.int32, (t, t), 1)
    ms = []
    for hh in heads:
        s = jnp.where(causal, logits(hh, i), NEG_BIG)
        s_ref[hh, i] = s
        m_lane = jnp.maximum(m_lanes[hh], _lane_fold(s, jnp.maximum))
        m = jnp.broadcast_to(jnp.max(m_lane, axis=1, keepdims=True), (t, LANES))
        ms.append(jnp.concatenate([m] * (t // LANES), axis=1))

    def pass2(kc, accs):
        out = []
        for hh in heads:
            p = jnp.exp2(s_ref[hh, kc] - ms[hh])
            vb = vf_ref[hh, pl.ds(pl.multiple_of(kc * t, t), t), :]
            out.append(accs[hh] + jnp.dot(p.astype(BF16), vb, preferred_element_type=F32))
        return tuple(out)

    accs = lax.fori_loop(0, i + 1, pass2, tuple(jnp.zeros((t, d + LANES), F32) for _ in heads))
    for hh in heads:
        o_ref[:, hh * d:(hh + 1) * d] = (accs[hh][:, :d] / accs[hh][:, d:]).astype(o_ref.dtype)


def _fox_attention(qkv, cum, bsz, s, t=512):
    n = bsz * s
    nq = s // t
    hp = FOX_HEADS // FOX_HB
    w = FOX_HB * LANES
    assert FOX_HEAD_DIM == LANES and FOX_HEADS % FOX_HB == 0
    kern = functools.partial(_fox_kernel, t=t)
    return pl.pallas_call(
        kern,
        grid=(bsz, hp, nq),
        in_specs=[pl.BlockSpec((t, w), lambda b, h, i: (b * nq + i, h)),
                  pl.BlockSpec((s, w), lambda b, h, i: (b, hp + h)),
                  pl.BlockSpec((s, w), lambda b, h, i: (b, 2 * hp + h)),
                  pl.BlockSpec((None, s, LANES), lambda b, h, i: (b, 0, 0))],
        out_specs=pl.BlockSpec((t, w), lambda b, h, i: (b * nq + i, h)),
        out_shape=jax.ShapeDtypeStruct((n, FOX_WIDTH), BF16),
        scratch_shapes=[pltpu.VMEM((FOX_HB, s, 2 * LANES), BF16), pltpu.VMEM((FOX_HB, s, 2 * LANES), BF16),
                        pltpu.VMEM((FOX_HB, nq, t, t), F32)],
        compiler_params=_params("arbitrary", "arbitrary", "arbitrary"),
        name="fox_attention",
    )(qkv, qkv, qkv, cum)


def _qlat_kernel(cq_ref, wn_ref, wr_ref, wuk_ref, c64_ref, s64_ref, o_ref):
    cq = cq_ref[...]
    qn = jnp.dot(cq, wn_ref[...], preferred_element_type=F32).astype(BF16)
    qr = jnp.dot(cq, wr_ref[...], preferred_element_type=F32)
    for h in range(DSA_HEADS):
        cols = slice(h * LANES, (h + 1) * LANES)
        lat = jnp.dot(qn[:, cols], wuk_ref[h], preferred_element_type=F32)
        o_ref[h, :, :KV_LORA] = (lat * DSA_QSCALE).astype(BF16)
        r = qr[:, cols]
        o_ref[h, :, KV_LORA:] = ((r * c64_ref[...] + _rot_half64(r) * s64_ref[...]) * DSA_QSCALE).astype(BF16)


def _q_latent(cq_n, w_nope_b, w_rope_b, w_uk_b, tabs64, bsz, s, tm=256):
    spt = s // tm
    kq = cq_n.shape[1]
    wide = DSA_HEADS * LANES
    return pl.pallas_call(
        _qlat_kernel,
        grid=(bsz * spt,),
        in_specs=[pl.BlockSpec((tm, kq), lambda i: (i, 0)),
                  pl.BlockSpec((kq, wide), lambda i: (0, 0)),
                  pl.BlockSpec((kq, wide), lambda i: (0, 0)),
                  pl.BlockSpec((DSA_HEADS, DSA_NOPE_DIM, KV_LORA), lambda i: (0, 0, 0)),
                  pl.BlockSpec((tm, LANES), lambda i: (i % spt, 0)),
                  pl.BlockSpec((tm, LANES), lambda i: (i % spt, 0))],
        out_specs=pl.BlockSpec((None, DSA_HEADS, tm, KV_PAD), lambda i: (i // spt, 0, i % spt, 0)),
        out_shape=jax.ShapeDtypeStruct((bsz, DSA_HEADS, s, KV_PAD), BF16),
        compiler_params=_params("arbitrary"),
        name="q_latent",
    )(cq_n, w_nope_b, w_rope_b, w_uk_b, tabs64[0], tabs64[1])


def _needed_chunks(i, tq):
    return ((i + 1) * tq + KEY_CHUNK - 1) // KEY_CHUNK


_NEG_INF_KEY = int(np.array(-np.inf, np.float32).view(np.int32) ^ np.int32(0x7FFFFFFF))


def _indexer_kernel(q_ref, k_ref, wt_ref, o_ref, key_ref, thr_ref, *, tq, topk, hg):
    i = pl.program_id(1)
    tk = KEY_CHUNK
    n_chunks = key_ref.shape[0]
    nk = _needed_chunks(i, tq)
    qpos = i * tq + lax.broadcasted_iota(jnp.int32, (tk, tq), 1)
    krow = lax.broadcasted_iota(jnp.int32, (tk, tq), 0)

    def score_chunk(kc, c):
        k0 = pl.multiple_of(kc * tk, tk)
        kb = k_ref[pl.ds(k0, tk), :]
        sc = jnp.zeros((tk, tq), F32)
        for g in range(IDX_HEADS // hg):
            q2 = q_ref[g * hg:(g + 1) * hg].reshape(hg * tq, IDX_DIM)
            d = lax.dot_general(kb, q2, _NT_DIMS, preferred_element_type=F32)
            for hh in range(hg):
                h = g * hg + hh
                sc = sc + wt_ref[h:h + 1, :] * jnp.maximum(d[:, hh * tq:(hh + 1) * tq], 0.0)
        sc = jnp.where(sc == 0.0, 0.0, sc)
        sc = jnp.where(k0 + krow <= qpos, sc, -jnp.inf)
        bits = pltpu.bitcast(sc, jnp.int32)
        key_ref[kc] = jnp.where(bits < 0, bits ^ jnp.int32(0x7FFFFFFF), bits)
        return c

    lax.fori_loop(0, nk, score_chunk, 0)

    def pad_chunk(kc, c):
        key_ref[kc] = jnp.full((tk, tq), _NEG_INF_KEY, jnp.int32)
        return c

    lax.fori_loop(nk, n_chunks, pad_chunk, 0)
    kf = float(topk)

    def bisection(n_counted):
        def count_ge(t):
            acc = jnp.zeros((SUBLANES, tq), F32)
            for kc in range(n_counted):
                ind = jnp.where(key_ref[kc] >= t, 1.0, 0.0)
                acc = acc + jnp.sum(ind.reshape(tk // SUBLANES, SUBLANES, tq), axis=0)
            return jnp.sum(acc, axis=0, keepdims=True)

        int_min = jnp.int32(-2 ** 31)
        t0 = jnp.where(count_ge(jnp.zeros((1, tq), jnp.int32)) >= kf, jnp.int32(0), int_min)

        def bisect(it, t):
            cand = t | lax.shift_left(jnp.int32(1), jnp.int32(30) - it)
            return jnp.where(count_ge(cand) >= kf, cand, t)

        thr_ref[...] = lax.fori_loop(0, 31, bisect, t0)

    half = max(n_chunks // 2, 1)
    if half < n_chunks:
        pl.when(nk <= half)(lambda: bisection(half))
        pl.when(nk > half)(lambda: bisection(n_chunks))
    else:
        bisection(n_chunks)
    thr = thr_ref[...]

    def write_chunk(kc, c):
        keep = (key_ref[kc] >= thr) & (kc * tk + krow <= qpos)
        o_ref[kc] = jnp.where(keep, 0.0, NEG_BIG).T.astype(o_ref.dtype)
        return c

    lax.fori_loop(0, nk, write_chunk, 0)

    def fill_chunk(kc, c):
        o_ref[kc] = jnp.full((tq, tk), NEG_BIG, o_ref.dtype)
        return c

    lax.fori_loop(nk, n_chunks, fill_chunk, 0)


def _indexer_mask(q_idx, k_idx, w_idx_t, bsz, s, topk, tq=128, hg=4):
    kern = functools.partial(_indexer_kernel, tq=tq, topk=topk, hg=hg)
    nc = s // KEY_CHUNK
    return pl.pallas_call(
        kern,
        grid=(bsz, s // tq),
        in_specs=[pl.BlockSpec((None, IDX_HEADS, tq, IDX_DIM), lambda b, i: (b, 0, i, 0)),
                  pl.BlockSpec((None, s, IDX_DIM), lambda b, i: (b, 0, 0)),
                  pl.BlockSpec((None, IDX_HEADS, tq), lambda b, i: (b, 0, i))],
        out_specs=pl.BlockSpec((None, nc, tq, KEY_CHUNK), lambda b, i: (b, 0, i, 0)),
        out_shape=jax.ShapeDtypeStruct((bsz, nc, s, KEY_CHUNK), BF16),
        scratch_shapes=[pltpu.VMEM((nc, KEY_CHUNK, tq), jnp.int32), pltpu.VMEM((1, tq), jnp.int32)],
        compiler_params=_params("arbitrary", "arbitrary"),
        name="indexer_topk_mask",
    )(q_idx, k_idx, w_idx_t)


def _dsa_kernel(q_ref, kvt_ref, kv_ref, bias_ref, o_ref, s_ref, m_ref, l_ref, acc_ref, *, tq):
    i = pl.program_id(1)
    tk = KEY_CHUNK
    rows = DSA_HEADS * tq
    nk = _needed_chunks(i, tq)
    q2 = q_ref[...].reshape(rows, KV_PAD)
    m_ref[...] = jnp.full(m_ref.shape, NEG_BIG, F32)

    n_pairs = (nk + 1) // 2

    def logits(kc):
        sc = jnp.dot(q2, kvt_ref[kc], preferred_element_type=F32)
        sc = (sc.reshape(DSA_HEADS, tq, tk) + bias_ref[kc].astype(F32)[None]).reshape(rows, tk)
        s_ref[kc] = sc
        return _lane_fold(sc, jnp.maximum)

    def pass1(pr, c):
        m_ref[...] = jnp.maximum(m_ref[...], jnp.maximum(logits(2 * pr), logits(2 * pr + 1)))
        return c

    lax.fori_loop(0, n_pairs, pass1, 0)
    m_ref[...] = jnp.broadcast_to(jnp.max(m_ref[...], axis=1, keepdims=True), m_ref.shape)
    l_ref[...] = jnp.zeros(l_ref.shape, F32)
    acc_ref[...] = jnp.zeros(acc_ref.shape, F32)

    def probs(kc):
        m = m_ref[...]
        p = jnp.concatenate([jnp.exp2(s_ref[kc, :, g * LANES:(g + 1) * LANES] - m)
                             for g in range(tk // LANES)], axis=1)
        kvb = kv_ref[pl.ds(pl.multiple_of(kc * tk, tk), tk), :KV_LORA]
        return _lane_fold(p, jnp.add), jnp.dot(p.astype(BF16), kvb, preferred_element_type=F32)

    def pass2(pr, c):
        l0, a0 = probs(2 * pr)
        l1, a1 = probs(2 * pr + 1)
        l_ref[...] += l0 + l1
        acc_ref[...] += a0 + a1
        return c

    lax.fori_loop(0, n_pairs, pass2, 0)
    l = jnp.dot(l_ref[...], jnp.ones((LANES, LANES), F32), precision=lax.Precision.HIGHEST,
                preferred_element_type=F32)
    o = acc_ref[...] / jnp.concatenate([l] * (KV_LORA // LANES), axis=1)
    o_ref[...] = o.reshape(DSA_HEADS, tq, KV_LORA).astype(o_ref.dtype)


def _dsa_attention(q_cat, kv_t, kv, bias, bsz, s, tq=64):
    kern = functools.partial(_dsa_kernel, tq=tq)
    nc = s // KEY_CHUNK
    assert nc % 2 == 0, "the kernel walks key chunks in pairs"
    rows = DSA_HEADS * tq
    return pl.pallas_call(
        kern,
        grid=(bsz, s // tq),
        in_specs=[pl.BlockSpec((None, DSA_HEADS, tq, KV_PAD), lambda b, i: (b, 0, i, 0)),
                  pl.BlockSpec((None, nc, KV_PAD, KEY_CHUNK), lambda b, i: (b, 0, 0, 0)),
                  pl.BlockSpec((None, s, KV_PAD), lambda b, i: (b, 0, 0)),
                  pl.BlockSpec((None, nc, tq, KEY_CHUNK), lambda b, i: (b, 0, i, 0))],
        out_specs=pl.BlockSpec((None, DSA_HEADS, tq, KV_LORA), lambda b, i: (b, 0, i, 0)),
        out_shape=jax.ShapeDtypeStruct((bsz, DSA_HEADS, s, KV_LORA), BF16),
        scratch_shapes=[pltpu.VMEM((nc, rows, KEY_CHUNK), F32), pltpu.VMEM((rows, LANES), F32),
                        pltpu.VMEM((rows, LANES), F32), pltpu.VMEM((rows, KV_LORA), F32)],
        compiler_params=_params("arbitrary", "arbitrary"),
        name="dsa_attention",
    )(q_cat, kv_t, kv, bias)


def _uv_kernel(o_ref, w_ref, y_ref):
    for h in range(DSA_HEADS):
        y_ref[:, h * DSA_V_DIM:(h + 1) * DSA_V_DIM] = jnp.dot(
            o_ref[h], w_ref[h], preferred_element_type=F32).astype(y_ref.dtype)


def _uv_project(o_lat, w_uv_b, bsz, s, tm=512):
    spt = s // tm
    return pl.pallas_call(
        _uv_kernel,
        grid=(bsz * spt,),
        in_specs=[pl.BlockSpec((None, DSA_HEADS, tm, KV_LORA), lambda i: (i // spt, 0, i % spt, 0)),
                  pl.BlockSpec((DSA_HEADS, KV_LORA, DSA_V_DIM), lambda i: (0, 0, 0))],
        out_specs=pl.BlockSpec((tm, DSA_HEADS * DSA_V_DIM), lambda i: (i, 0)),
        out_shape=jax.ShapeDtypeStruct((bsz * s, DSA_HEADS * DSA_V_DIM), BF16),
        compiler_params=_params("arbitrary"),
        name="uv_project",
    )(o_lat, w_uv_b)


def _merge_kernel(h_ref, a_ref, b_ref, wga_ref, wgb_ref, wua_ref, wub_ref, o_ref,
                  cga_ref, cgb_ref, cua_ref, cub_ref):
    @pl.when(pl.program_id(1) == 0)
    def _():
        cga_ref[...] = wga_ref[...].astype(BF16)
        cgb_ref[...] = wgb_ref[...].astype(BF16)
        cua_ref[...] = wua_ref[...].astype(BF16)
        cub_ref[...] = wub_ref[...].astype(BF16)

    h = h_ref[...]
    ga = lax.dot_general(h, cga_ref[...], _NT_DIMS, preferred_element_type=F32)
    gb = lax.dot_general(h, cgb_ref[...], _NT_DIMS, preferred_element_type=F32)
    ya = jnp.dot(a_ref[...], cua_ref[...], preferred_element_type=F32)
    yb = jnp.dot(b_ref[...], cub_ref[...], preferred_element_type=F32)
    o_ref[...] = (jax.nn.sigmoid(ga) * ya + jax.nn.sigmoid(gb) * yb).astype(o_ref.dtype)


def _gated_merge(h1, att_a, att_b, w_t, row_ga, row_gb, w_up_a, w_up_b, tm=512, tn=256):
    n, d = h1.shape
    ka, kb = att_a.shape[1], att_b.shape[1]
    return pl.pallas_call(
        _merge_kernel,
        grid=(d // tn, n // tm),
        in_specs=[pl.BlockSpec((tm, d), lambda j, i: (i, 0)),
                  pl.BlockSpec((tm, ka), lambda j, i: (i, 0)),
                  pl.BlockSpec((tm, kb), lambda j, i: (i, 0)),
                  _w_rows_spec(tn, d, row_ga, lambda j, i: j),
                  _w_rows_spec(tn, d, row_gb, lambda j, i: j),
                  pl.BlockSpec((ka, tn), lambda j, i: (0, j)),
                  pl.BlockSpec((kb, tn), lambda j, i: (0, j))],
        out_specs=pl.BlockSpec((tm, tn), lambda j, i: (i, j)),
        out_shape=jax.ShapeDtypeStruct((n, d), BF16),
        scratch_shapes=[pltpu.VMEM((tn, d), BF16), pltpu.VMEM((tn, d), BF16),
                        pltpu.VMEM((ka, tn), BF16), pltpu.VMEM((kb, tn), BF16)],
        compiler_params=_params("arbitrary", "arbitrary"),
        name="gated_merge",
    )(h1, att_a, att_b, w_t, w_t, w_up_a, w_up_b)


def _mem_kernel(x_ref, g_ref, wq_ref, kt_ref, v_ref, wom_ref, o_ref):
    x = x_ref[...]
    hn = (x * lax.rsqrt(jnp.mean(x * x, axis=-1, keepdims=True) + EPS) * g_ref[...]).astype(BF16)
    q = jnp.dot(hn, wq_ref[...], preferred_element_type=F32).astype(BF16)
    scale = MEM_HEAD_DIM ** -0.5
    outs = []
    for h in range(MEM_HEADS):
        sl = slice(h * MEM_HEAD_DIM, (h + 1) * MEM_HEAD_DIM)
        s = jnp.dot(q[:, sl], kt_ref[sl, :], preferred_element_type=F32) * scale
        m = jnp.max(s, axis=-1, keepdims=True)
        p = jnp.exp(s - m)
        p = p / jnp.sum(p, axis=-1, keepdims=True)
        outs.append(jnp.dot(p.astype(BF16), v_ref[:, sl], preferred_element_type=F32).astype(BF16))
    o = jnp.concatenate(outs, axis=-1)
    o_ref[...] = x + jnp.dot(o, wom_ref[...], preferred_element_type=F32)


def _mem_attention(x1, g_x, w_qm_b, km_t, kv_m, w_om_b, bsz, s, n_mem, tm=256):
    spt = s // tm
    hd = MEM_HEADS * MEM_HEAD_DIM
    d = x1.shape[1]
    return pl.pallas_call(
        _mem_kernel,
        grid=(bsz, spt),
        in_specs=[pl.BlockSpec((tm, d), lambda b, i: (b * spt + i, 0)),
                  pl.BlockSpec((1, d), lambda b, i: (0, 0)),
                  pl.BlockSpec((d, hd), lambda b, i: (0, 0)),
                  pl.BlockSpec((None, hd, n_mem), lambda b, i: (b, 0, 0)),
                  pl.BlockSpec((n_mem, hd), lambda b, i: (b, 1)),
                  pl.BlockSpec((hd, d), lambda b, i: (0, 0))],
        out_specs=pl.BlockSpec((tm, d), lambda b, i: (b * spt + i, 0)),
        out_shape=jax.ShapeDtypeStruct(x1.shape, F32),
        compiler_params=_params("arbitrary", "arbitrary"),
        name="memory_cross_attention",
    )(x1, g_x.reshape(1, d), w_qm_b, km_t, kv_m, w_om_b)


def _pack_bf16_pairs(x):
    c = x.shape[1] // 2
    lo = pltpu.bitcast(x[:, :c].astype(BF16).astype(F32), jnp.uint32)
    hi = pltpu.bitcast(x[:, c:].astype(BF16).astype(F32), jnp.uint32)
    return lax.shift_right_logical(lo, jnp.uint32(16)) | hi


def _unpack_bf16_pairs(w):
    lo = pltpu.bitcast(lax.shift_left(w, jnp.uint32(16)), F32).astype(BF16)
    hi = pltpu.bitcast(w & jnp.uint32(0xFFFF0000), F32).astype(BF16)
    return lo, hi


def _router_kernel(x_ref, g_ref, wr_ref, br_ref, h_ref, idx_ref, wt_ref, cnt_ref, carry_ref):
    @pl.when(pl.program_id(0) == 0)
    def _():
        carry_ref[...] = jnp.zeros_like(carry_ref)

    x = x_ref[...]
    hn = x * lax.rsqrt(jnp.mean(x * x, axis=-1, keepdims=True) + EPS) * g_ref[...]
    h_ref[...] = _pack_bf16_pairs(hn)
    logits = jnp.dot(hn, wr_ref[...], precision=lax.Precision.HIGHEST, preferred_element_type=F32) + br_ref[...]
    tm = x.shape[0]
    lane_i = lax.broadcasted_iota(jnp.int32, (tm, LANES), 1)
    lane = lane_i.astype(F32)
    lane_grp = lax.shift_right_logical(lane_i, EXPERTS_PER_GROUP.bit_length() - 1).astype(F32)
    big = 1e9
    is_g = (lane_i >= N_EXPERTS) & (lane_i < N_EXPERTS + N_GROUPS)
    lg = jnp.where(is_g, logits, -jnp.inf)
    mg = jnp.max(lg, axis=1, keepdims=True)
    gate_g = 1.0 / jnp.sum(jnp.exp(lg - mg), axis=1, keepdims=True)
    grp = jnp.min(jnp.where(lg == mg, lane, big), axis=1, keepdims=True) - N_EXPERTS
    in_grp = (lane_i < N_EXPERTS) & (lane_grp == grp)
    le = jnp.where(in_grp, logits, -jnp.inf)
    m1 = jnp.max(le, axis=1, keepdims=True)
    e1 = jnp.min(jnp.where(le == m1, lane, big), axis=1, keepdims=True)
    le2 = jnp.where(lane == e1, -jnp.inf, le)
    m2 = jnp.max(le2, axis=1, keepdims=True)
    e2 = jnp.min(jnp.where(le2 == m2, lane, big), axis=1, keepdims=True)
    r = jnp.exp(m2 - m1)
    w1 = gate_g * (1.0 / (1.0 + r))
    w2 = gate_g * (r / (1.0 + r))
    oh1 = (lane == e1).astype(F32)
    oh2 = (lane == e2).astype(F32)
    oh = oh1 + oh2
    tri = (lax.broadcasted_iota(jnp.int32, (tm, tm), 0) > lax.broadcasted_iota(jnp.int32, (tm, tm), 1)).astype(BF16)
    prefix = jnp.dot(tri, oh.astype(BF16), preferred_element_type=F32) + carry_ref[...]
    rank1 = jnp.sum(prefix * oh1, axis=1, keepdims=True)
    rank2 = jnp.sum(prefix * oh2, axis=1, keepdims=True)
    new_cnt = carry_ref[...] + jnp.sum(oh, axis=0, keepdims=True)
    carry_ref[...] = new_cnt
    cnt_ref[...] = new_cnt
    idx_ref[...] = jnp.where(lane_i == 0, e1, jnp.where(lane_i == 1, e2, jnp.where(
        lane_i == 2, rank1, jnp.where(lane_i == 3, rank2, 0.0)))).astype(jnp.int32)
    wt_ref[...] = jnp.where(lane_i == 0, w1, jnp.where(lane_i == 1, w2, 0.0))


def _router(x2, g, w_r, b_r, tm=256):
    n, d = x2.shape
    row = lambda w: pl.BlockSpec((tm, w), lambda i: (i, 0))
    return pl.pallas_call(
        _router_kernel,
        grid=(n // tm,),
        in_specs=[row(d), pl.BlockSpec((1, d), lambda i: (0, 0)), pl.BlockSpec((d, LANES), lambda i: (0, 0)),
                  pl.BlockSpec((1, LANES), lambda i: (0, 0))],
        out_specs=[row(d // 2), row(LANES), row(LANES), pl.BlockSpec((1, LANES), lambda i: (0, 0))],
        out_shape=[jax.ShapeDtypeStruct((n, d // 2), jnp.uint32), jax.ShapeDtypeStruct((n, LANES), jnp.int32),
                   jax.ShapeDtypeStruct((n, LANES), F32), jax.ShapeDtypeStruct((1, LANES), F32)],
        scratch_shapes=[pltpu.VMEM((1, LANES), F32)],
        compiler_params=_params("arbitrary"),
        name="moe_router",
    )(x2, g.reshape(1, d), w_r, b_r)


def _row_copy(src_hbm, row, dst, r, sem):
    return pltpu.make_async_copy(src_hbm.at[pl.ds(row, 1), :], dst.at[pl.ds(r, 1), :], sem)


def _sorted_rows(m):
    return m + N_EXPERTS * SUBLANES + ITEM_ROWS


def _dispatch_kernel(dest_ref, seg0_ref, cnt_ref, h_ref, x_hbm, zero_ref, sem, *, tm, m):
    i = pl.program_id(0)

    @pl.when(i == 0)
    def _():
        zero_ref[...] = jnp.zeros_like(zero_ref)
        tail = [pltpu.make_async_copy(zero_ref, x_hbm.at[pl.ds(r0, ITEM_SUB), :], sem)
                for r0 in range(m, _sorted_rows(m), ITEM_SUB)]
        for c in tail:
            c.start()
        for c in tail:
            c.wait()

        def fill_gap(e, total):
            gap = (-cnt_ref[e]) & (SUBLANES - 1)

            def one(r, c):
                _row_copy(zero_ref, 0, x_hbm, seg0_ref[e] + cnt_ref[e] + r, sem).start()
                return c

            lax.fori_loop(0, gap, one, 0)
            return total + gap

        n_gap = lax.fori_loop(0, N_EXPERTS, fill_gap, 0)

        def wait_gap(r, c):
            _row_copy(zero_ref, 0, x_hbm, 0, sem).wait()
            return c

        lax.fori_loop(0, n_gap, wait_gap, 0)

    def issue(r, c):
        t = i * tm + r
        for sl in range(TOPK_IN_GROUP):
            _row_copy(h_ref, r, x_hbm, dest_ref[TOPK_IN_GROUP * t + sl], sem).start()
        return c

    lax.fori_loop(0, tm, issue, 0, unroll=4)
    for _ in range(TOPK_IN_GROUP):
        pltpu.make_async_copy(h_ref, x_hbm.at[pl.ds(0, tm), :], sem).wait()


def _dispatch(dest, seg_start, counts, h3p, tm=256):
    n, c = h3p.shape
    m = dest.shape[0]
    assert (_sorted_rows(m) - m) % ITEM_SUB == 0
    grid_spec = pltpu.PrefetchScalarGridSpec(
        num_scalar_prefetch=3,
        grid=(n // tm,),
        in_specs=[pl.BlockSpec((tm, c), lambda i, d, s0, cn: (i, 0))],
        out_specs=pl.BlockSpec(memory_space=pl.ANY),
        scratch_shapes=[pltpu.VMEM((ITEM_SUB, c), h3p.dtype), pltpu.SemaphoreType.DMA(())],
    )
    return pl.pallas_call(
        functools.partial(_dispatch_kernel, tm=tm, m=m),
        grid_spec=grid_spec,
        out_shape=jax.ShapeDtypeStruct((_sorted_rows(m), c), h3p.dtype),
        compiler_params=_params("arbitrary"),
        name="moe_dispatch",
    )(dest, seg_start, counts, h3p)


def _expert_kernel(ie_ref, is_ref, in_ref, ni_ref, x_hbm, wg_ref, wu_ref, wd_ref, y_hbm,
                   xin_ref, xb_ref, acc_ref, in_sem, out_sem, *, m, nj):
    i = pl.program_id(0)
    j = pl.program_id(1)
    n_items = ni_ref[0]
    valid = i < n_items
    n = in_ref[i]
    slot = lax.rem(i, 2)
    half = xb_ref.shape[1] // 2
    n_sub = ITEM_ROWS // ITEM_SUB

    def first_row(item):
        return pl.multiple_of(is_ref[item], SUBLANES)

    def in_copy(item):
        return pltpu.make_async_copy(x_hbm.at[pl.ds(first_row(item), ITEM_ROWS), :], xin_ref, in_sem)

    def out_each(item, sl, fn):
        for sub in range(n_sub):
            @pl.when(sub * ITEM_SUB < in_ref[item])
            def _():
                fn(pltpu.make_async_copy(
                    acc_ref.at[sl, pl.ds(sub * ITEM_SUB, ITEM_SUB), :],
                    y_hbm.at[pl.ds(first_row(item) + sub * ITEM_SUB, ITEM_SUB), :], out_sem.at[sl]))

    @pl.when((i == 0) & (j == 0))
    def _():
        acc_ref[1, :ITEM_SUB, :] = jnp.zeros((ITEM_SUB, acc_ref.shape[2]), F32)
        tail = [pltpu.make_async_copy(acc_ref.at[1, pl.ds(0, ITEM_SUB), :],
                                      y_hbm.at[pl.ds(r0, ITEM_SUB), :], out_sem.at[1])
                for r0 in range(m, _sorted_rows(m), ITEM_SUB)]
        for c in tail:
            c.start()
        for c in tail:
            c.wait()
        in_copy(0).start()

    @pl.when(valid & (j == 0))
    def _():
        in_copy(i).wait()
        lo, hi = _unpack_bf16_pairs(xin_ref[...])
        xb_ref[:, :half] = lo
        xb_ref[:, half:] = hi

        @pl.when(i + 1 < n_items)
        def _():
            in_copy(i + 1).start()

        acc_ref[slot] = jnp.zeros(acc_ref.shape[1:], F32)

    @pl.when(valid)
    def _():
        wg = wg_ref[...].astype(BF16)
        wu = wu_ref[...].astype(BF16)
        wd = wd_ref[...].astype(BF16)
        n_blocks = (n + ITEM_SUB - 1) // ITEM_SUB
        for nb in range(1, n_sub + 1):
            @pl.when(n_blocks == nb)
            def _():
                rows = pl.ds(0, nb * ITEM_SUB)
                xb = xb_ref[rows, :]
                g = jnp.dot(xb, wg, preferred_element_type=F32)
                u = jnp.dot(xb, wu, preferred_element_type=F32)
                hm = (g * jax.nn.sigmoid(g) * u).astype(BF16)
                acc_ref[slot, rows, :] += jnp.dot(hm, wd, preferred_element_type=F32)

    @pl.when(valid & (j == nj - 1))
    def _():
        @pl.when(i > 0)
        def _():
            out_each(i - 1, 1 - slot, lambda c: c.wait())

        out_each(i, slot, lambda c: c.start())

        @pl.when(i == n_items - 1)
        def _():
            out_each(i, slot, lambda c: c.wait())


def _experts(item_e, item_start, item_n, n_items, x_sorted, w_gate, w_up, w_down, max_items, m):
    d = w_gate.shape[1]
    nj = D_EXPERT // F_CHUNK

    def jj(i, j, ni):
        return jnp.where(i < ni[0], j, nj - 1)

    grid_spec = pltpu.PrefetchScalarGridSpec(
        num_scalar_prefetch=4,
        grid=(max_items, nj),
        in_specs=[pl.BlockSpec(memory_space=pl.ANY),
                  pl.BlockSpec((None, d, F_CHUNK), lambda i, j, ie, s, n, ni: (ie[i], 0, jj(i, j, ni))),
                  pl.BlockSpec((None, d, F_CHUNK), lambda i, j, ie, s, n, ni: (ie[i], 0, jj(i, j, ni))),
                  pl.BlockSpec((None, F_CHUNK, d), lambda i, j, ie, s, n, ni: (ie[i], jj(i, j, ni), 0))],
        out_specs=pl.BlockSpec(memory_space=pl.ANY),
        scratch_shapes=[pltpu.VMEM((ITEM_ROWS, d // 2), jnp.uint32), pltpu.VMEM((ITEM_ROWS, d), BF16),
                        pltpu.VMEM((2, ITEM_ROWS, d), F32),
                        pltpu.SemaphoreType.DMA(()), pltpu.SemaphoreType.DMA((2,))],
    )
    return pl.pallas_call(
        functools.partial(_expert_kernel, m=m, nj=nj),
        grid_spec=grid_spec,
        out_shape=jax.ShapeDtypeStruct((_sorted_rows(m), d), F32),
        compiler_params=_params("arbitrary", "arbitrary"),
        name="moe_experts",
    )(item_e, item_start, item_n, n_items, x_sorted, w_gate, w_up, w_down)


def _combine_kernel(dest_ref, y_hbm, x_ref, wt_ref, g_ref, o_ref, yv_ref, sem, *, tm):
    i = pl.program_id(0)
    slot = lax.rem(i, 2)

    def issue_tile(tile, sl):
        def issue(r, c):
            base = TOPK_IN_GROUP * (tile * tm + r)
            for k in range(TOPK_IN_GROUP):
                _row_copy(y_hbm, dest_ref[base + k], yv_ref.at[sl, k], r, sem.at[sl]).start()
            return c

        lax.fori_loop(0, tm, issue, 0, unroll=4)

    @pl.when(i == 0)
    def _():
        issue_tile(0, 0)

    @pl.when(i + 1 < pl.num_programs(0))
    def _():
        issue_tile(i + 1, 1 - slot)

    for k in range(TOPK_IN_GROUP):
        pltpu.make_async_copy(y_hbm.at[pl.ds(0, tm), :], yv_ref.at[slot, k], sem.at[slot]).wait()
    w = wt_ref[...]
    moe = yv_ref[slot, 0] * w[:, 0:1]
    for k in range(1, TOPK_IN_GROUP):
        moe = moe + yv_ref[slot, k] * w[:, k:k + 1]
    x = x_ref[...] + moe
    o_ref[...] = x * lax.rsqrt(jnp.mean(x * x, axis=-1, keepdims=True) + EPS) * g_ref[...]


def _combine(dest, y_sorted, x2, wts, g_final, tm=128):
    n, d = x2.shape
    grid_spec = pltpu.PrefetchScalarGridSpec(
        num_scalar_prefetch=1,
        grid=(n // tm,),
        in_specs=[pl.BlockSpec(memory_space=pl.ANY),
                  pl.BlockSpec((tm, d), lambda i, dr: (i, 0)),
                  pl.BlockSpec((tm, LANES), lambda i, dr: (i, 0)),
                  pl.BlockSpec((1, d), lambda i, dr: (0, 0))],
        out_specs=pl.BlockSpec((tm, d), lambda i, dr: (i, 0)),
        scratch_shapes=[pltpu.VMEM((2, TOPK_IN_GROUP, tm, d), F32), pltpu.SemaphoreType.DMA((2,))],
    )
    return pl.pallas_call(
        functools.partial(_combine_kernel, tm=tm),
        grid_spec=grid_spec,
        out_shape=jax.ShapeDtypeStruct((n, d), F32),
        compiler_params=_params("arbitrary"),
        name="moe_combine_final_norm",
    )(dest, y_sorted, x2, wts, g_final.reshape(1, d))


def _pad_cols(w, width):
    return jnp.pad(w, ((0, 0), (0, width - w.shape[1])))


def _mixer(x2d, bsz, s, g_norm, w_in, b_f, g_q_lat, g_kv_lat, g_idx_k, b_idx_k, w_uq, w_idx_q, w_uk, w_uv,
           w_up_a, w_up_b, w_out):
    d = x2d.shape[1]
    topk = min(DSA_TOPK_MAX, s // 4)
    h1 = _rmsnorm(x2d, g_norm, BF16)
    c_small = 3 * FOX_WIDTH
    o_fa, o_cq = c_small, c_small + FOX_HEADS
    o_ckv = o_cq + Q_LORA
    o_kr = o_ckv + KV_LORA
    o_ik = o_kr + DSA_ROPE_DIM
    o_iw = o_ik + IDX_DIM
    o_ga = o_iw + IDX_HEADS
    o_gb = o_ga + d
    w_t = jnp.swapaxes(w_in, 0, 1)
    pad_rows = lambda w: jnp.pad(w, ((0, LANES - w.shape[0]), (0, 0)))
    w_small_t = jnp.concatenate([
        w_t[o_cq:o_ckv], w_t[o_ckv:o_kr], w_t[o_ik:o_iw],
        pad_rows(w_t[o_kr:o_ik]), pad_rows(w_t[o_fa:o_cq]), pad_rows(w_t[o_iw:o_ga])], axis=0)

    qkv = _mm(h1, w_t, 3 * FOX_WIDTH, w_rows=True, scaled_cols=(FOX_WIDTH, FOX_QSCALE), name="proj_qkv")
    small = _mm(h1, w_small_t, SMALL_W, out_dtype=F32, tm=512, tn=SMALL_W // 2, w_rows=True, name="proj_small")

    tabs64 = _rope_tables(s, DSA_ROPE_DIM)
    tabs128 = _rope_tables(s, IDX_DIM)
    cq_n, kv_cat, k_idx, w_idx, fa = _prep(small, g_q_lat, g_kv_lat, g_idx_k, b_idx_k, tabs64, tabs128, s)

    cum = _fox_cum(fa, _pad_cols(b_f.reshape(1, -1), LANES), bsz, s)
    att_a = _fox_attention(qkv, cum, bsz, s)

    w_uq3 = w_uq.reshape(Q_LORA, DSA_HEADS, DSA_NOPE_DIM + DSA_ROPE_DIM)
    w_uq_nope = w_uq3[:, :, :DSA_NOPE_DIM].reshape(Q_LORA, DSA_HEADS * DSA_NOPE_DIM).astype(BF16)
    w_uq_rope = jnp.pad(w_uq3[:, :, DSA_NOPE_DIM:], ((0, 0), (0, 0), (0, LANES - DSA_ROPE_DIM))).reshape(
        Q_LORA, DSA_HEADS * LANES).astype(BF16)
    q_cat = _q_latent(cq_n, w_uq_nope, w_uq_rope, w_uk.astype(BF16), tabs64, bsz, s)
    q_idx = _mm(cq_n, w_idx_q, IDX_HEADS * IDX_DIM, heads=(bsz, s), rope_tabs=tabs128, name="proj_q_idx")
    w_idx_t = jnp.transpose(w_idx.reshape(bsz, s, LANES)[:, :, :IDX_HEADS], (0, 2, 1))
    bias = _indexer_mask(q_idx, k_idx.reshape(bsz, s, IDX_DIM), w_idx_t, bsz, s, topk)
    kv3 = kv_cat.reshape(bsz, s, KV_PAD)
    kv_t = jnp.transpose(kv3.reshape(bsz, s // KEY_CHUNK, KEY_CHUNK, KV_PAD), (0, 1, 3, 2))
    o_lat = _dsa_attention(q_cat, kv_t, kv3, bias, bsz, s)
    att_b = _uv_project(o_lat, w_uv.astype(BF16), bsz, s)

    merged = _gated_merge(h1, att_a, att_b, w_t, o_ga, o_gb, w_up_a, w_up_b)
    return _mm(merged, w_out, d, out_dtype=F32, res=x2d, name="proj_out")


def _memory_layer(x1, mem, bsz, s, g_x, g_m, w_qm, w_km, w_vm, w_om):
    n_mem = mem.shape[1]
    d = x1.shape[1]
    hd = MEM_HEADS * MEM_HEAD_DIM
    mem_n = _rmsnorm(mem.reshape(bsz * n_mem, d), g_m, BF16)
    kv_m = _mm(mem_n, jnp.concatenate([w_km, w_vm], axis=1), 2 * hd, tm=bsz * n_mem, name="proj_kv_mem")
    km_t = jnp.transpose(kv_m[:, :hd].reshape(bsz, n_mem, hd), (0, 2, 1))
    return _mem_attention(x1, g_x, w_qm.astype(BF16), km_t, kv_m, w_om.astype(BF16), bsz, s, n_mem)


def _moe_layer(x2, g_ffn, w_rg, b_rg, w_re, b_re, w_gate, w_up, w_down, g_final):
    n, d = x2.shape
    m = n * TOPK_IN_GROUP
    w_r = _pad_cols(jnp.concatenate([w_re, w_rg], axis=1), LANES)
    b_r = _pad_cols(jnp.concatenate([b_re, b_rg]).reshape(1, -1), LANES)
    h3p, ridx, wts, cnt = _router(x2, g_ffn, w_r, b_r)
    counts = cnt[0, :N_EXPERTS].astype(jnp.int32)
    e_flat = ridx[:, :TOPK_IN_GROUP].reshape(m)
    rank_flat = ridx[:, TOPK_IN_GROUP:2 * TOPK_IN_GROUP].reshape(m)
    seg = (counts + SUBLANES - 1) // SUBLANES * SUBLANES
    starts = (jnp.cumsum(seg) - seg).astype(jnp.int32)
    n_it = (counts + ITEM_ROWS - 1) // ITEM_ROWS
    it_end = jnp.cumsum(n_it)
    it_first = it_end - n_it
    max_items = (m + N_EXPERTS * (ITEM_ROWS - 1)) // ITEM_ROWS
    t = jnp.arange(max_items, dtype=jnp.int32)
    item_e = jnp.minimum(jnp.searchsorted(it_end, t, side='right'), N_EXPERTS - 1).astype(jnp.int32)
    k_in = t - it_first[item_e]
    item_start = (starts[item_e] + k_in * ITEM_ROWS).astype(jnp.int32)
    item_n = jnp.clip(counts[item_e] - k_in * ITEM_ROWS, 0, ITEM_ROWS).astype(jnp.int32)
    n_items = it_end[-1:].astype(jnp.int32)
    last = jnp.maximum(n_items[0] - 1, 0)
    item_e = jnp.where(t < n_items[0], item_e, item_e[last])
    item_start = jnp.where(t < n_items[0], item_start, 0)
    item_n = jnp.where(t < n_items[0], item_n, 0)
    dest = (starts[e_flat] + rank_flat).astype(jnp.int32)
    x_sorted = _dispatch(dest, starts, counts, h3p)
    y_sorted = _experts(item_e, item_start, item_n, n_items, x_sorted, w_gate, w_up, w_down, max_items, m)
    return _combine(dest, y_sorted, x2, wts, g_final)


def kernel(x, mem, g_norm_mix, w_in, b_f, g_q_lat, g_kv_lat, g_idx_k, b_idx_k, w_uq, w_idx_q, w_uk, w_uv,
           w_up_a, w_up_b, w_out, g_norm_mem_x, g_mem, w_qm, w_km, w_vm, w_om, g_norm_ffn,
           w_rg, b_rg, w_re, b_re, w_gate, w_up, w_down, g_final):
    bsz, s, d = x.shape
    depth = w_in.shape[0]
    assert depth == 1, "the final norm is fused into the last MoE combine"
    l = 0
    x2d = x.reshape(bsz * s, d)
    x1 = _mixer(x2d, bsz, s, g_norm_mix[l], w_in[l], b_f[l], g_q_lat[l], g_kv_lat[l], g_idx_k[l], b_idx_k[l],
                w_uq[l], w_idx_q[l], w_uk[l], w_uv[l], w_up_a[l], w_up_b[l], w_out[l])
    x2 = _memory_layer(x1, mem, bsz, s, g_norm_mem_x[l], g_mem[l], w_qm[l], w_km[l], w_vm[l], w_om[l])
    out = _moe_layer(x2, g_norm_ffn[l], w_rg[l], b_rg[l], w_re[l], b_re[l], w_gate[l], w_up[l], w_down[l], g_final)
    return out.reshape(bsz, s, d)
```

```python
import functools

import jax
import jax.numpy as jnp
import numpy as np
from jax import lax
from jax.experimental import pallas as pl
from jax.experimental.pallas import tpu as pltpu

F32 = jnp.float32
BF16 = jnp.bfloat16

EPS = 1e-6
ROPE_THETA = 10000.0
FOX_HEADS = 16
FOX_HEAD_DIM = 128
DSA_HEADS = 16
DSA_NOPE_DIM = 128
DSA_ROPE_DIM = 64
DSA_V_DIM = 128
Q_LORA = 1024
KV_LORA = 256
IDX_HEADS = 32
IDX_DIM = 128
DSA_TOPK_MAX = 256
MEM_HEADS = 4
MEM_HEAD_DIM = 128
N_GROUPS = 8
EXPERTS_PER_GROUP = 8
N_EXPERTS = N_GROUPS * EXPERTS_PER_GROUP
TOPK_IN_GROUP = 2
D_EXPERT = 512

FOX_WIDTH = FOX_HEADS * FOX_HEAD_DIM
LANES = 128
SUBLANES = 8
KV_PAD = KV_LORA + LANES
NEG_BIG = -1e30
LOG2E = 1.4426950408889634
FOX_QSCALE = FOX_HEAD_DIM ** -0.5 * LOG2E
FOX_HB = 4
DSA_QSCALE = (DSA_NOPE_DIM + DSA_ROPE_DIM) ** -0.5 * LOG2E
VMEM_LIMIT_BYTES = 56 * 1024 * 1024
ITEM_ROWS = 384
ITEM_SUB = 128
F_CHUNK = 256
KEY_CHUNK = 256


def _params(*sem):
    return pltpu.CompilerParams(dimension_semantics=sem, vmem_limit_bytes=VMEM_LIMIT_BYTES)


def _rmsnorm_kernel(x_ref, g_ref, o_ref):
    x = x_ref[...].astype(F32)
    ms = jnp.mean(x * x, axis=-1, keepdims=True)
    o_ref[...] = (x * lax.rsqrt(ms + EPS) * g_ref[...]).astype(o_ref.dtype)


def _rmsnorm(x2d, g, out_dtype, tm=256):
    n, d = x2d.shape
    return pl.pallas_call(
        _rmsnorm_kernel,
        grid=(n // tm,),
        in_specs=[pl.BlockSpec((tm, d), lambda i: (i, 0)), pl.BlockSpec((1, d), lambda i: (0, 0))],
        out_specs=pl.BlockSpec((tm, d), lambda i: (i, 0)),
        out_shape=jax.ShapeDtypeStruct((n, d), out_dtype),
        compiler_params=_params("arbitrary"),
        name="rmsnorm",
    )(x2d, g.reshape(1, d).astype(F32))


_NT_DIMS = (((1,), (1,)), ((), ()))


def _w_rows_spec(tn, k, row0, index_of_j):
    if row0 % tn == 0:
        return pl.BlockSpec((tn, k), lambda *g: (index_of_j(*g) + row0 // tn, 0))
    assert row0 % SUBLANES == 0
    return pl.BlockSpec((pl.Element(tn), pl.Element(k)),
                        lambda *g: (pl.multiple_of(index_of_j(*g) * tn + row0, SUBLANES), 0))


def _mm_kernel(a_ref, b_ref, *rest, has_res, head_out, rope, w_rows, col_scale):
    pos = 0
    res_ref = cos_ref = sin_ref = None
    if has_res:
        res_ref = rest[pos]
        pos += 1
    if rope:
        cos_ref, sin_ref = rest[pos], rest[pos + 1]
        pos += 2
    o_ref = rest[pos]
    if b_ref.dtype == BF16:
        wb_ref = b_ref
    else:
        wb_ref = rest[pos + 1]

        @pl.when(pl.program_id(1) == 0)
        def _():
            wb_ref[...] = b_ref[...].astype(BF16)

    if w_rows:
        acc = lax.dot_general(a_ref[...], wb_ref[...], _NT_DIMS, preferred_element_type=F32)
    else:
        acc = jnp.dot(a_ref[...], wb_ref[...], preferred_element_type=F32)
    if col_scale is not None:
        acc = acc * jnp.where(pl.program_id(0) < col_scale[0], col_scale[1], 1.0)
    if has_res:
        acc = acc + res_ref[...]
    if head_out:
        for hh in range(acc.shape[1] // LANES):
            part = acc[:, hh * LANES:(hh + 1) * LANES]
            if rope:
                part = part * cos_ref[...] + pltpu.roll(part, LANES // 2, 1) * sin_ref[...]
            o_ref[hh] = part.astype(o_ref.dtype)
    else:
        o_ref[...] = acc.astype(o_ref.dtype)


def _mm(a, w, n_cols, *, col0=0, out_dtype=BF16, tm=1024, tn=512, res=None, heads=None, rope_tabs=None,
        w_rows=False, scaled_cols=None, name="mm"):
    m, k = a.shape
    tm = min(tm, m if heads is None else heads[1])
    assert n_cols % tn == 0 and m % tm == 0
    if w_rows:
        w_spec = _w_rows_spec(tn, k, col0, lambda j, i: j)
    else:
        assert col0 % tn == 0
        w_spec = pl.BlockSpec((k, tn), lambda j, i: (0, j + col0 // tn))
    in_specs = [pl.BlockSpec((tm, k), lambda j, i: (i, 0)), w_spec]
    args = [a, w]
    if res is not None:
        in_specs.append(pl.BlockSpec((tm, tn), lambda j, i: (i, j)))
        args.append(res)
    if heads is not None:
        bsz, s = heads
        assert s % tm == 0
        spt = s // tm
        if rope_tabs is not None:
            in_specs += [pl.BlockSpec((tm, LANES), lambda j, i: (i % spt, 0))] * 2
            args += list(rope_tabs)
        hpt = tn // LANES
        out_spec = pl.BlockSpec((None, hpt, tm, LANES), lambda j, i: (i // spt, j, i % spt, 0))
        out_shape = jax.ShapeDtypeStruct((bsz, n_cols // LANES, s, LANES), out_dtype)
    else:
        out_spec = pl.BlockSpec((tm, tn), lambda j, i: (i, j))
        out_shape = jax.ShapeDtypeStruct((m, n_cols), out_dtype)
    col_scale = None
    if scaled_cols is not None:
        assert scaled_cols[0] % tn == 0
        col_scale = (scaled_cols[0] // tn, scaled_cols[1])
    kern = functools.partial(_mm_kernel, has_res=res is not None, head_out=heads is not None,
                             rope=rope_tabs is not None, w_rows=w_rows, col_scale=col_scale)
    return pl.pallas_call(
        kern,
        grid=(n_cols // tn, m // tm),
        in_specs=in_specs,
        out_specs=out_spec,
        out_shape=out_shape,
        scratch_shapes=[] if w.dtype == BF16 else [pltpu.VMEM((tn, k) if w_rows else (k, tn), BF16)],
        compiler_params=_params("arbitrary", "arbitrary"),
        name=name,
    )(*args)


def _rope_tables(s, d):
    inv = jnp.power(ROPE_THETA, -jnp.arange(0, d, 2, dtype=F32) / d)
    ang = jnp.arange(s, dtype=jnp.int32).astype(F32)[:, None] * inv[None, :]
    cos, sin = jnp.cos(ang), jnp.sin(ang)
    cos_t = jnp.tile(jnp.concatenate([cos, cos], axis=-1), (1, LANES // d))
    sin_t = jnp.tile(jnp.concatenate([-sin, sin], axis=-1), (1, LANES // d))
    return cos_t, sin_t


def _rot_half64(x):
    lane = lax.broadcasted_iota(jnp.int32, x.shape, 1)
    return jnp.where((lane % 64) < 32, pltpu.roll(x, 96, 1), pltpu.roll(x, 32, 1))


_C_CQ = 0
_C_CKV = Q_LORA
_C_IK = _C_CKV + KV_LORA
_C_KR = _C_IK + IDX_DIM
_C_FA = _C_KR + LANES
_C_IW = _C_FA + LANES
SMALL_W = _C_IW + LANES


def _prep_kernel(s_ref, gq_ref, gkv_ref, gik_ref, bik_ref, c64_ref, s64_ref, c128_ref, s128_ref,
                 cq_ref, kv_ref, kidx_ref, widx_ref, fa_ref):
    cq = s_ref[:, _C_CQ:_C_CQ + Q_LORA]
    cq_ref[...] = (cq * lax.rsqrt(jnp.mean(cq * cq, axis=-1, keepdims=True) + EPS) * gq_ref[...]).astype(BF16)
    ckv = s_ref[:, _C_CKV:_C_CKV + KV_LORA]
    kv_ref[:, :KV_LORA] = (ckv * lax.rsqrt(jnp.mean(ckv * ckv, axis=-1, keepdims=True) + EPS)
                           * gkv_ref[...]).astype(BF16)
    kr = s_ref[:, _C_KR:_C_KR + LANES]
    kv_ref[:, KV_LORA:] = (kr * c64_ref[...] + _rot_half64(kr) * s64_ref[...]).astype(BF16)
    ik = s_ref[:, _C_IK:_C_IK + IDX_DIM]
    mu = jnp.mean(ik, axis=-1, keepdims=True)
    xc = ik - mu
    ikn = xc * lax.rsqrt(jnp.mean(xc * xc, axis=-1, keepdims=True) + EPS) * gik_ref[...] + bik_ref[...]
    kidx_ref[...] = (ikn * c128_ref[...] + pltpu.roll(ikn, LANES // 2, 1) * s128_ref[...]).astype(BF16)
    widx_ref[...] = s_ref[:, _C_IW:_C_IW + LANES] * (IDX_HEADS * IDX_DIM) ** -0.5
    fa_ref[...] = s_ref[:, _C_FA:_C_FA + LANES]


def _prep(small, g_q, g_kv, g_ik, b_ik, tabs64, tabs128, s, tm=256):
    n = small.shape[0]
    spt = s // tm
    row = lambda w: pl.BlockSpec((tm, w), lambda i: (i, 0))
    vec = lambda w: pl.BlockSpec((1, w), lambda i: (0, 0))
    tab = pl.BlockSpec((tm, LANES), lambda i: (i % spt, 0))
    return pl.pallas_call(
        _prep_kernel,
        grid=(n // tm,),
        in_specs=[row(SMALL_W), vec(Q_LORA), vec(KV_LORA), vec(IDX_DIM), vec(IDX_DIM), tab, tab, tab, tab],
        out_specs=[row(Q_LORA), row(KV_PAD), row(IDX_DIM), row(LANES), row(LANES)],
        out_shape=[jax.ShapeDtypeStruct((n, Q_LORA), BF16), jax.ShapeDtypeStruct((n, KV_PAD), BF16),
                   jax.ShapeDtypeStruct((n, IDX_DIM), BF16), jax.ShapeDtypeStruct((n, LANES), F32),
                   jax.ShapeDtypeStruct((n, LANES), F32)],
        compiler_params=_params("arbitrary"),
        name="mixer_prep",
    )(small, g_q.reshape(1, -1), g_kv.reshape(1, -1), g_ik.reshape(1, -1), b_ik.reshape(1, -1),
      tabs64[0], tabs64[1], tabs128[0], tabs128[1])


def _cum_kernel(fa_ref, bf_ref, cum_ref, carry_ref):
    @pl.when(pl.program_id(1) == 0)
    def _():
        carry_ref[...] = jnp.zeros_like(carry_ref)

    z = fa_ref[...] + bf_ref[...]
    logf = jnp.minimum(z, 0.0) - jnp.log1p(jnp.exp(-jnp.abs(z)))
    t = z.shape[0]
    tri = (lax.broadcasted_iota(jnp.int32, (t, t), 0) >= lax.broadcasted_iota(jnp.int32, (t, t), 1)).astype(F32)
    c = jnp.dot(tri, logf, precision=lax.Precision.HIGHEST, preferred_element_type=F32) + carry_ref[...]
    cum_ref[...] = c
    carry_ref[...] = c[t - 1:t, :]


def _fox_cum(fa, b_f_pad, bsz, s, t=256):
    return pl.pallas_call(
        _cum_kernel,
        grid=(bsz, s // t),
        in_specs=[pl.BlockSpec((None, t, LANES), lambda b, i: (b, i, 0)),
                  pl.BlockSpec((1, LANES), lambda b, i: (0, 0))],
        out_specs=pl.BlockSpec((None, t, LANES), lambda b, i: (b, i, 0)),
        out_shape=jax.ShapeDtypeStruct((bsz, s, LANES), F32),
        scratch_shapes=[pltpu.VMEM((1, LANES), F32)],
        compiler_params=_params("arbitrary", "arbitrary"),
        name="fox_cumsum",
    )(fa.reshape(bsz, s, LANES), b_f_pad)


def _lane_fold(x, op):
    out = x[:, :LANES]
    for c in range(1, x.shape[1] // LANES):
        out = op(out, x[:, c * LANES:(c + 1) * LANES])
    return out


def _split3(x):
    t1 = x.astype(BF16)
    r = x - t1.astype(F32)
    t2 = r.astype(BF16)
    t3 = (r - t2.astype(F32)).astype(BF16)
    return t1, t2, t3


def _gate_lanes(c, own_first):
    t1, t2, t3 = (t.astype(F32) for t in _split3(c))
    lane = lax.broadcasted_iota(jnp.int32, c.shape, 1)
    a, b = (0, 3) if own_first else (3, 0)
    ones3 = jnp.where((lane >= b) & (lane < b + 3), 1.0, 0.0)
    out = jnp.where(lane == a, t1, jnp.where(lane == a + 1, t2, jnp.where(lane == a + 2, t3, ones3)))
    return out.astype(BF16)


def _fox_kernel(q_ref, k_ref, v_ref, cum_ref, o_ref, kf_ref, vf_ref, s_ref, *, t):
    hp = pl.program_id(1)
    i = pl.program_id(2)
    d = FOX_HEAD_DIM
    heads = range(FOX_HB)
    row = lax.broadcasted_iota(jnp.int32, (LANES, LANES), 0)

    def gate_column(c, hh):
        pick = (row == hp * FOX_HB + hh).astype(F32)
        return jnp.dot(c, pick, precision=lax.Precision.HIGHEST, preferred_element_type=F32) * LOG2E

    @pl.when(i == 0)
    def _():
        for hh in heads:
            kf_ref[hh, :, :d] = k_ref[:, hh * d:(hh + 1) * d]
            kf_ref[hh, :, d:] = _gate_lanes(-gate_column(cum_ref[...], hh), True)
            vf_ref[hh, :, :d] = v_ref[:, hh * d:(hh + 1) * d]
            vf_ref[hh, :, d:] = jnp.ones((vf_ref.shape[1], LANES), BF16)

    cum_q = cum_ref[pl.ds(pl.multiple_of(i * t, t), t), :]
    qs = [jnp.concatenate([q_ref[:, hh * d:(hh + 1) * d], _gate_lanes(gate_column(cum_q, hh), False)], axis=1)
          for hh in heads]

    def logits(hh, kc):
        kb = kf_ref[hh, pl.ds(pl.multiple_of(kc * t, t), t), :]
        return lax.dot_general(qs[hh], kb, _NT_DIMS, preferred_element_type=F32)

    def pass1(kc, m_lanes):
        out = []
        for hh in heads:
            s = logits(hh, kc)
            s_ref[hh, kc] = s
            out.append(jnp.maximum(m_lanes[hh], _lane_fold(s, jnp.maximum)))
        return tuple(out)

    m_lanes = lax.fori_loop(0, i, pass1, tuple(jnp.full((t, LANES), NEG_BIG, F32) for _ in heads))
    causal = lax.broadcasted_iota(jnp.int32, (t, t), 0) >= lax.broadcasted_iota(jnp.int32, (t, t), 1)
    ms = []
    for hh in heads:
        s = jnp.where(causal, logits(hh, i), NEG_BIG)
        s_ref[hh, i] = s
        m_lane = jnp.maximum(m_lanes[hh], _lane_fold(s, jnp.maximum))
        m = jnp.broadcast_to(jnp.max(m_lane, axis=1, keepdims=True), (t, LANES))
        ms.append(jnp.concatenate([m] * (t // LANES), axis=1))

    def pass2(kc, accs):
        out = []
        for hh in heads:
            p = jnp.exp2(s_ref[hh, kc] - ms[hh])
            vb = vf_ref[hh, pl.ds(pl.multiple_of(kc * t, t), t), :]
            out.append(accs[hh] + jnp.dot(p.astype(BF16), vb, preferred_element_type=F32))
        return tuple(out)

    accs = lax.fori_loop(0, i + 1, pass2, tuple(jnp.zeros((t, d + LANES), F32) for _ in heads))
    for hh in heads:
        o_ref[:, hh * d:(hh + 1) * d] = (accs[hh][:, :d] / accs[hh][:, d:]).astype(o_ref.dtype)


def _fox_attention(qkv, cum, bsz, s, t=512):
    n = bsz * s
    nq = s // t
    hp = FOX_HEADS // FOX_HB
    w = FOX_HB * LANES
    assert FOX_HEAD_DIM == LANES and FOX_HEADS % FOX_HB == 0
    kern = functools.partial(_fox_kernel, t=t)
    return pl.pallas_call(
        kern,
        grid=(bsz, hp, nq),
        in_specs=[pl.BlockSpec((t, w), lambda b, h, i: (b * nq + i, h)),
                  pl.BlockSpec((s, w), lambda b, h, i: (b, hp + h)),
                  pl.BlockSpec((s, w), lambda b, h, i: (b, 2 * hp + h)),
                  pl.BlockSpec((None, s, LANES), lambda b, h, i: (b, 0, 0))],
        out_specs=pl.BlockSpec((t, w), lambda b, h, i: (b * nq + i, h)),
        out_shape=jax.ShapeDtypeStruct((n, FOX_WIDTH), BF16),
        scratch_shapes=[pltpu.VMEM((FOX_HB, s, 2 * LANES), BF16), pltpu.VMEM((FOX_HB, s, 2 * LANES), BF16),
                        pltpu.VMEM((FOX_HB, nq, t, t), F32)],
        compiler_params=_params("arbitrary", "arbitrary", "arbitrary"),
        name="fox_attention",
    )(qkv, qkv, qkv, cum)


def _qlat_kernel(cq_ref, wn_ref, wr_ref, wuk_ref, c64_ref, s64_ref, o_ref):
    cq = cq_ref[...]
    qn = jnp.dot(cq, wn_ref[...], preferred_element_type=F32).astype(BF16)
    qr = jnp.dot(cq, wr_ref[...], preferred_element_type=F32)
    for h in range(DSA_HEADS):
        cols = slice(h * LANES, (h + 1) * LANES)
        lat = jnp.dot(qn[:, cols], wuk_ref[h], preferred_element_type=F32)
        o_ref[h, :, :KV_LORA] = (lat * DSA_QSCALE).astype(BF16)
        r = qr[:, cols]
        o_ref[h, :, KV_LORA:] = ((r * c64_ref[...] + _rot_half64(r) * s64_ref[...]) * DSA_QSCALE).astype(BF16)


def _q_latent(cq_n, w_nope_b, w_rope_b, w_uk_b, tabs64, bsz, s, tm=256):
    spt = s // tm
    kq = cq_n.shape[1]
    wide = DSA_HEADS * LANES
    return pl.pallas_call(
        _qlat_kernel,
        grid=(bsz * spt,),
        in_specs=[pl.BlockSpec((tm, kq), lambda i: (i, 0)),
                  pl.BlockSpec((kq, wide), lambda i: (0, 0)),
                  pl.BlockSpec((kq, wide), lambda i: (0, 0)),
                  pl.BlockSpec((DSA_HEADS, DSA_NOPE_DIM, KV_LORA), lambda i: (0, 0, 0)),
                  pl.BlockSpec((tm, LANES), lambda i: (i % spt, 0)),
                  pl.BlockSpec((tm, LANES), lambda i: (i % spt, 0))],
        out_specs=pl.BlockSpec((None, DSA_HEADS, tm, KV_PAD), lambda i: (i // spt, 0, i % spt, 0)),
        out_shape=jax.ShapeDtypeStruct((bsz, DSA_HEADS, s, KV_PAD), BF16),
        compiler_params=_params("arbitrary"),
        name="q_latent",
    )(cq_n, w_nope_b, w_rope_b, w_uk_b, tabs64[0], tabs64[1])


def _needed_chunks(i, tq):
    return ((i + 1) * tq + KEY_CHUNK - 1) // KEY_CHUNK


BISECT_MAX_STEPS = 320
BISECT_UNROLL = 4


def _indexer_kernel(q_ref, k_ref, wt_ref, o_ref, sc_ref, thr_ref, *, tq, topk, hg):
    i = pl.program_id(1)
    tk = KEY_CHUNK
    n_chunks = sc_ref.shape[0]
    nk = _needed_chunks(i, tq)
    qpos = i * tq + lax.broadcasted_iota(jnp.int32, (tk, tq), 1)
    krow = lax.broadcasted_iota(jnp.int32, (tk, tq), 0)
    inf = jnp.inf

    def score_chunk(kc, c):
        k0 = pl.multiple_of(kc * tk, tk)
        kb = k_ref[pl.ds(k0, tk), :]
        sc = jnp.zeros((tk, tq), F32)
        for g in range(IDX_HEADS // hg):
            q2 = q_ref[g * hg:(g + 1) * hg].reshape(hg * tq, IDX_DIM)
            d = lax.dot_general(kb, q2, _NT_DIMS, preferred_element_type=F32)
            for hh in range(hg):
                h = g * hg + hh
                sc = sc + wt_ref[h:h + 1, :] * jnp.maximum(d[:, hh * tq:(hh + 1) * tq], 0.0)
        sc_ref[kc] = jnp.where(k0 + krow <= qpos, sc, -inf)
        return c

    lax.fori_loop(0, nk, score_chunk, 0)

    def pad_chunk(kc, c):
        sc_ref[kc] = jnp.full((tk, tq), -inf, F32)
        return c

    lax.fori_loop(nk, n_chunks, pad_chunk, 0)
    kf = float(topk)
    n_causal = (i * tq + lax.broadcasted_iota(jnp.int32, (1, tq), 1) + 1).astype(F32)

    def fold(x, op):
        return op(x.reshape(tk // SUBLANES, SUBLANES, tq), axis=0)

    def threshold(n_counted):
        def count_ge(t):
            acc = jnp.zeros((SUBLANES, tq), F32)
            for kc in range(n_counted):
                acc = acc + fold(jnp.where(sc_ref[kc] >= t, 1.0, 0.0), jnp.sum)
            return jnp.sum(acc, axis=0, keepdims=True)

        hi = jnp.full((SUBLANES, tq), -inf, F32)
        lo = jnp.full((SUBLANES, tq), inf, F32)
        for kc in range(n_counted):
            s = sc_ref[kc]
            hi = jnp.maximum(hi, fold(s, jnp.max))
            lo = jnp.minimum(lo, fold(jnp.where(s == -inf, inf, s), jnp.min))
        hi = jnp.max(hi, axis=0, keepdims=True)
        lo = jnp.min(lo, axis=0, keepdims=True)
        few = n_causal <= kf
        at_max = count_ge(hi) >= kf
        lo = jnp.where(few, -inf, jnp.where(at_max, hi, lo))
        done = jnp.where(few | at_max, 1.0, 0.0)

        def unsettled(state):
            it, _, _, done = state
            return (it < BISECT_MAX_STEPS) & (jnp.sum(1.0 - done) > 0.0)

        def step(state):
            it, lo, hi, done = state
            for _ in range(BISECT_UNROLL):
                mid = 0.5 * lo + 0.5 * hi
                cnt = count_ge(mid)
                ge = cnt >= kf
                closed = (mid <= lo) | (mid >= hi)
                live = done == 0.0
                lo = jnp.where(live & ge, mid, lo)
                hi = jnp.where(live & jnp.logical_not(ge), mid, hi)
                done = jnp.where(live & ((ge & (cnt == kf)) | closed), 1.0, done)
            return it + BISECT_UNROLL, lo, hi, done

        _, lo, _, _ = lax.while_loop(unsettled, step, (jnp.int32(0), lo, hi, done))
        thr_ref[...] = lo

    half = max(n_chunks // 2, 1)
    if half < n_chunks:
        pl.when(nk <= half)(lambda: threshold(half))
        pl.when(nk > half)(lambda: threshold(n_chunks))
    else:
        threshold(n_chunks)
    thr = thr_ref[...]

    def write_chunk(kc, c):
        keep = (sc_ref[kc] >= thr) & (kc * tk + krow <= qpos)
        o_ref[kc] = jnp.where(keep, 0.0, NEG_BIG).T.astype(o_ref.dtype)
        return c

    lax.fori_loop(0, nk, write_chunk, 0)

    def fill_chunk(kc, c):
        o_ref[kc] = jnp.full((tq, tk), NEG_BIG, o_ref.dtype)
        return c

    lax.fori_loop(nk, n_chunks, fill_chunk, 0)


def _indexer_mask(q_idx, k_idx, w_idx_t, bsz, s, topk, tq=128, hg=4):
    kern = functools.partial(_indexer_kernel, tq=tq, topk=topk, hg=hg)
    nc = s // KEY_CHUNK
    return pl.pallas_call(
        kern,
        grid=(bsz, s // tq),
        in_specs=[pl.BlockSpec((None, IDX_HEADS, tq, IDX_DIM), lambda b, i: (b, 0, i, 0)),
                  pl.BlockSpec((None, s, IDX_DIM), lambda b, i: (b, 0, 0)),
                  pl.BlockSpec((None, IDX_HEADS, tq), lambda b, i: (b, 0, i))],
        out_specs=pl.BlockSpec((None, nc, tq, KEY_CHUNK), lambda b, i: (b, 0, i, 0)),
        out_shape=jax.ShapeDtypeStruct((bsz, nc, s, KEY_CHUNK), BF16),
        scratch_shapes=[pltpu.VMEM((nc, KEY_CHUNK, tq), F32), pltpu.VMEM((1, tq), F32)],
        compiler_params=_params("arbitrary", "arbitrary"),
        name="indexer_topk_mask",
    )(q_idx, k_idx, w_idx_t)


def _dsa_kernel(q_ref, kvt_ref, kv_ref, bias_ref, o_ref, s_ref, m_ref, l_ref, acc_ref, *, tq):
    i = pl.program_id(1)
    tk = KEY_CHUNK
    rows = DSA_HEADS * tq
    nk = _needed_chunks(i, tq)
    q2 = q_ref[...].reshape(rows, KV_PAD)
    m_ref[...] = jnp.full(m_ref.shape, NEG_BIG, F32)

    n_pairs = (nk + 1) // 2

    def logits(kc):
        sc = jnp.dot(q2, kvt_ref[kc], preferred_element_type=F32)
        sc = (sc.reshape(DSA_HEADS, tq, tk) + bias_ref[kc].astype(F32)[None]).reshape(rows, tk)
        s_ref[kc] = sc
        return _lane_fold(sc, jnp.maximum)

    def pass1(pr, c):
        m_ref[...] = jnp.maximum(m_ref[...], jnp.maximum(logits(2 * pr), logits(2 * pr + 1)))
        return c

    lax.fori_loop(0, n_pairs, pass1, 0)
    m_ref[...] = jnp.broadcast_to(jnp.max(m_ref[...], axis=1, keepdims=True), m_ref.shape)
    l_ref[...] = jnp.zeros(l_ref.shape, F32)
    acc_ref[...] = jnp.zeros(acc_ref.shape, F32)

    def probs(kc):
        m = m_ref[...]
        p = jnp.concatenate([jnp.exp2(s_ref[kc, :, g * LANES:(g + 1) * LANES] - m)
                             for g in range(tk // LANES)], axis=1)
        kvb = kv_ref[pl.ds(pl.multiple_of(kc * tk, tk), tk), :KV_LORA]
        return _lane_fold(p, jnp.add), jnp.dot(p.astype(BF16), kvb, preferred_element_type=F32)

    def pass2(pr, c):
        l0, a0 = probs(2 * pr)
        l1, a1 = probs(2 * pr + 1)
        l_ref[...] += l0 + l1
        acc_ref[...] += a0 + a1
        return c

    lax.fori_loop(0, n_pairs, pass2, 0)
    l = jnp.dot(l_ref[...], jnp.ones((LANES, LANES), F32), precision=lax.Precision.HIGHEST,
                preferred_element_type=F32)
    o = acc_ref[...] / jnp.concatenate([l] * (KV_LORA // LANES), axis=1)
    o_ref[...] = o.reshape(DSA_HEADS, tq, KV_LORA).astype(o_ref.dtype)


def _dsa_attention(q_cat, kv_t, kv, bias, bsz, s, tq=128):
    kern = functools.partial(_dsa_kernel, tq=tq)
    nc = s // KEY_CHUNK
    assert nc % 2 == 0, "the kernel walks key chunks in pairs"
    rows = DSA_HEADS * tq
    return pl.pallas_call(
        kern,
        grid=(bsz, s // tq),
        in_specs=[pl.BlockSpec((None, DSA_HEADS, tq, KV_PAD), lambda b, i: (b, 0, i, 0)),
                  pl.BlockSpec((None, nc, KV_PAD, KEY_CHUNK), lambda b, i: (b, 0, 0, 0)),
                  pl.BlockSpec((None, s, KV_PAD), lambda b, i: (b, 0, 0)),
                  pl.BlockSpec((None, nc, tq, KEY_CHUNK), lambda b, i: (b, 0, i, 0))],
        out_specs=pl.BlockSpec((None, DSA_HEADS, tq, KV_LORA), lambda b, i: (b, 0, i, 0)),
        out_shape=jax.ShapeDtypeStruct((bsz, DSA_HEADS, s, KV_LORA), BF16),
        scratch_shapes=[pltpu.VMEM((nc, rows, KEY_CHUNK), F32), pltpu.VMEM((rows, LANES), F32),
                        pltpu.VMEM((rows, LANES), F32), pltpu.VMEM((rows, KV_LORA), F32)],
        compiler_params=_params("arbitrary", "arbitrary"),
        name="dsa_attention",
    )(q_cat, kv_t, kv, bias)


def _uv_kernel(o_ref, w_ref, y_ref):
    for h in range(DSA_HEADS):
        y_ref[:, h * DSA_V_DIM:(h + 1) * DSA_V_DIM] = jnp.dot(
            o_ref[h], w_ref[h], preferred_element_type=F32).astype(y_ref.dtype)


def _uv_project(o_lat, w_uv_b, bsz, s, tm=512):
    spt = s // tm
    return pl.pallas_call(
        _uv_kernel,
        grid=(bsz * spt,),
        in_specs=[pl.BlockSpec((None, DSA_HEADS, tm, KV_LORA), lambda i: (i // spt, 0, i % spt, 0)),
                  pl.BlockSpec((DSA_HEADS, KV_LORA, DSA_V_DIM), lambda i: (0, 0, 0))],
        out_specs=pl.BlockSpec((tm, DSA_HEADS * DSA_V_DIM), lambda i: (i, 0)),
        out_shape=jax.ShapeDtypeStruct((bsz * s, DSA_HEADS * DSA_V_DIM), BF16),
        compiler_params=_params("arbitrary"),
        name="uv_project",
    )(o_lat, w_uv_b)


def _merge_kernel(h_ref, a_ref, b_ref, wga_ref, wgb_ref, wua_ref, wub_ref, o_ref,
                  cga_ref, cgb_ref, cua_ref, cub_ref):
    @pl.when(pl.program_id(1) == 0)
    def _():
        cga_ref[...] = wga_ref[...].astype(BF16)
        cgb_ref[...] = wgb_ref[...].astype(BF16)
        cua_ref[...] = wua_ref[...].astype(BF16)
        cub_ref[...] = wub_ref[...].astype(BF16)

    h = h_ref[...]
    ga = lax.dot_general(h, cga_ref[...], _NT_DIMS, preferred_element_type=F32)
    gb = lax.dot_general(h, cgb_ref[...], _NT_DIMS, preferred_element_type=F32)
    ya = jnp.dot(a_ref[...], cua_ref[...], preferred_element_type=F32)
    yb = jnp.dot(b_ref[...], cub_ref[...], preferred_element_type=F32)
    o_ref[...] = (jax.nn.sigmoid(ga) * ya + jax.nn.sigmoid(gb) * yb).astype(o_ref.dtype)


def _gated_merge(h1, att_a, att_b, w_t, row_ga, row_gb, w_up_a, w_up_b, tm=512, tn=256):
    n, d = h1.shape
    ka, kb = att_a.shape[1], att_b.shape[1]
    return pl.pallas_call(
        _merge_kernel,
        grid=(d // tn, n // tm),
        in_specs=[pl.BlockSpec((tm, d), lambda j, i: (i, 0)),
                  pl.BlockSpec((tm, ka), lambda j, i: (i, 0)),
                  pl.BlockSpec((tm, kb), lambda j, i: (i, 0)),
                  _w_rows_spec(tn, d, row_ga, lambda j, i: j),
                  _w_rows_spec(tn, d, row_gb, lambda j, i: j),
                  pl.BlockSpec((ka, tn), lambda j, i: (0, j)),
                  pl.BlockSpec((kb, tn), lambda j, i: (0, j))],
        out_specs=pl.BlockSpec((tm, tn), lambda j, i: (i, j)),
        out_shape=jax.ShapeDtypeStruct((n, d), BF16),
        scratch_shapes=[pltpu.VMEM((tn, d), BF16), pltpu.VMEM((tn, d), BF16),
                        pltpu.VMEM((ka, tn), BF16), pltpu.VMEM((kb, tn), BF16)],
        compiler_params=_params("arbitrary", "arbitrary"),
        name="gated_merge",
    )(h1, att_a, att_b, w_t, w_t, w_up_a, w_up_b)


def _mem_kernel(x_ref, g_ref, wq_ref, kt_ref, v_ref, wom_ref, o_ref):
    x = x_ref[...]
    hn = (x * lax.rsqrt(jnp.mean(x * x, axis=-1, keepdims=True) + EPS) * g_ref[...]).astype(BF16)
    q = jnp.dot(hn, wq_ref[...], preferred_element_type=F32).astype(BF16)
    scale = MEM_HEAD_DIM ** -0.5
    outs = []
    for h in range(MEM_HEADS):
        sl = slice(h * MEM_HEAD_DIM, (h + 1) * MEM_HEAD_DIM)
        s = jnp.dot(q[:, sl], kt_ref[sl, :], preferred_element_type=F32) * scale
        m = jnp.max(s, axis=-1, keepdims=True)
        p = jnp.exp(s - m)
        p = p / jnp.sum(p, axis=-1, keepdims=True)
        outs.append(jnp.dot(p.astype(BF16), v_ref[:, sl], preferred_element_type=F32).astype(BF16))
    o = jnp.concatenate(outs, axis=-1)
    o_ref[...] = x + jnp.dot(o, wom_ref[...], preferred_element_type=F32)


def _mem_attention(x1, g_x, w_qm_b, km_t, kv_m, w_om_b, bsz, s, n_mem, tm=256):
    spt = s // tm
    hd = MEM_HEADS * MEM_HEAD_DIM
    d = x1.shape[1]
    return pl.pallas_call(
        _mem_kernel,
        grid=(bsz, spt),
        in_specs=[pl.BlockSpec((tm, d), lambda b, i: (b * spt + i, 0)),
                  pl.BlockSpec((1, d), lambda b, i: (0, 0)),
                  pl.BlockSpec((d, hd), lambda b, i: (0, 0)),
                  pl.BlockSpec((None, hd, n_mem), lambda b, i: (b, 0, 0)),
                  pl.BlockSpec((n_mem, hd), lambda b, i: (b, 1)),
                  pl.BlockSpec((hd, d), lambda b, i: (0, 0))],
        out_specs=pl.BlockSpec((tm, d), lambda b, i: (b * spt + i, 0)),
        out_shape=jax.ShapeDtypeStruct(x1.shape, F32),
        compiler_params=_params("arbitrary", "arbitrary"),
        name="memory_cross_attention",
    )(x1, g_x.reshape(1, d), w_qm_b, km_t, kv_m, w_om_b)


def _pack_bf16_pairs(x):
    c = x.shape[1] // 2
    return pltpu.pack_elementwise([x[:, :c], x[:, c:]], packed_dtype=BF16)


def _unpack_bf16_pairs(w):
    lo = pltpu.unpack_elementwise(w, index=0, packed_dtype=BF16, unpacked_dtype=F32)
    hi = pltpu.unpack_elementwise(w, index=1, packed_dtype=BF16, unpacked_dtype=F32)
    return lo.astype(BF16), hi.astype(BF16)


def _router_kernel(x_ref, g_ref, wr_ref, br_ref, h_ref, idx_ref, wt_ref, cnt_ref, carry_ref):
    @pl.when(pl.program_id(0) == 0)
    def _():
        carry_ref[...] = jnp.zeros_like(carry_ref)

    x = x_ref[...]
    hn = x * lax.rsqrt(jnp.mean(x * x, axis=-1, keepdims=True) + EPS) * g_ref[...]
    h_ref[...] = _pack_bf16_pairs(hn)
    logits = jnp.dot(hn, wr_ref[...], precision=lax.Precision.HIGHEST, preferred_element_type=F32) + br_ref[...]
    tm = x.shape[0]
    lane_i = lax.broadcasted_iota(jnp.int32, (tm, LANES), 1)
    lane = lane_i.astype(F32)
    lane_grp = lax.shift_right_logical(lane_i, EXPERTS_PER_GROUP.bit_length() - 1).astype(F32)
    big = 1e9
    is_g = (lane_i >= N_EXPERTS) & (lane_i < N_EXPERTS + N_GROUPS)
    lg = jnp.where(is_g, logits, -jnp.inf)
    mg = jnp.max(lg, axis=1, keepdims=True)
    gate_g = 1.0 / jnp.sum(jnp.exp(lg - mg), axis=1, keepdims=True)
    grp = jnp.min(jnp.where(lg == mg, lane, big), axis=1, keepdims=True) - N_EXPERTS
    in_grp = (lane_i < N_EXPERTS) & (lane_grp == grp)
    le = jnp.where(in_grp, logits, -jnp.inf)
    m1 = jnp.max(le, axis=1, keepdims=True)
    e1 = jnp.min(jnp.where(le == m1, lane, big), axis=1, keepdims=True)
    le2 = jnp.where(lane == e1, -jnp.inf, le)
    m2 = jnp.max(le2, axis=1, keepdims=True)
    e2 = jnp.min(jnp.where(le2 == m2, lane, big), axis=1, keepdims=True)
    r = jnp.exp(m2 - m1)
    w1 = gate_g * (1.0 / (1.0 + r))
    w2 = gate_g * (r / (1.0 + r))
    oh1 = (lane == e1).astype(F32)
    oh2 = (lane == e2).astype(F32)
    oh = oh1 + oh2
    tri = (lax.broadcasted_iota(jnp.int32, (tm, tm), 0) > lax.broadcasted_iota(jnp.int32, (tm, tm), 1)).astype(BF16)
    prefix = jnp.dot(tri, oh.astype(BF16), preferred_element_type=F32) + carry_ref[...]
    rank1 = jnp.sum(prefix * oh1, axis=1, keepdims=True)
    rank2 = jnp.sum(prefix * oh2, axis=1, keepdims=True)
    new_cnt = carry_ref[...] + jnp.sum(oh, axis=0, keepdims=True)
    carry_ref[...] = new_cnt
    cnt_ref[...] = new_cnt
    idx_ref[...] = jnp.where(lane_i == 0, e1, jnp.where(lane_i == 1, e2, jnp.where(
        lane_i == 2, rank1, jnp.where(lane_i == 3, rank2, 0.0)))).astype(jnp.int32)
    wt_ref[...] = jnp.where(lane_i == 0, w1, jnp.where(lane_i == 1, w2, 0.0))


def _router(x2, g, w_r, b_r, tm=256):
    n, d = x2.shape
    row = lambda w: pl.BlockSpec((tm, w), lambda i: (i, 0))
    return pl.pallas_call(
        _router_kernel,
        grid=(n // tm,),
        in_specs=[row(d), pl.BlockSpec((1, d), lambda i: (0, 0)), pl.BlockSpec((d, LANES), lambda i: (0, 0)),
                  pl.BlockSpec((1, LANES), lambda i: (0, 0))],
        out_specs=[row(d // 2), row(LANES), row(LANES), pl.BlockSpec((1, LANES), lambda i: (0, 0))],
        out_shape=[jax.ShapeDtypeStruct((n, d // 2), jnp.uint32), jax.ShapeDtypeStruct((n, LANES), jnp.int32),
                   jax.ShapeDtypeStruct((n, LANES), F32), jax.ShapeDtypeStruct((1, LANES), F32)],
        scratch_shapes=[pltpu.VMEM((1, LANES), F32)],
        compiler_params=_params("arbitrary"),
        name="moe_router",
    )(x2, g.reshape(1, d), w_r, b_r)


def _row_copy(src_hbm, row, dst, r, sem):
    return pltpu.make_async_copy(src_hbm.at[pl.ds(row, 1), :], dst.at[pl.ds(r, 1), :], sem)


def _sorted_rows(m):
    return m + N_EXPERTS * SUBLANES + ITEM_ROWS


def _dispatch_kernel(dest_ref, seg0_ref, cnt_ref, h_ref, x_hbm, zero_ref, sem, *, tm, m):
    i = pl.program_id(0)

    @pl.when(i == 0)
    def _():
        zero_ref[...] = jnp.zeros_like(zero_ref)
        tail = [pltpu.make_async_copy(zero_ref, x_hbm.at[pl.ds(r0, ITEM_SUB), :], sem)
                for r0 in range(m, _sorted_rows(m), ITEM_SUB)]
        for c in tail:
            c.start()
        for c in tail:
            c.wait()

        def fill_gap(e, total):
            gap = (-cnt_ref[e]) & (SUBLANES - 1)

            def one(r, c):
                _row_copy(zero_ref, 0, x_hbm, seg0_ref[e] + cnt_ref[e] + r, sem).start()
                return c

            lax.fori_loop(0, gap, one, 0)
            return total + gap

        n_gap = lax.fori_loop(0, N_EXPERTS, fill_gap, 0)

        def wait_gap(r, c):
            _row_copy(zero_ref, 0, x_hbm, 0, sem).wait()
            return c

        lax.fori_loop(0, n_gap, wait_gap, 0)

    def issue(r, c):
        t = i * tm + r
        for sl in range(TOPK_IN_GROUP):
            _row_copy(h_ref, r, x_hbm, dest_ref[TOPK_IN_GROUP * t + sl], sem).start()
        return c

    lax.fori_loop(0, tm, issue, 0, unroll=4)
    for _ in range(TOPK_IN_GROUP):
        pltpu.make_async_copy(h_ref, x_hbm.at[pl.ds(0, tm), :], sem).wait()


def _dispatch(dest, seg_start, counts, h3p, tm=256):
    n, c = h3p.shape
    m = dest.shape[0]
    assert (_sorted_rows(m) - m) % ITEM_SUB == 0
    grid_spec = pltpu.PrefetchScalarGridSpec(
        num_scalar_prefetch=3,
        grid=(n // tm,),
        in_specs=[pl.BlockSpec((tm, c), lambda i, d, s0, cn: (i, 0))],
        out_specs=pl.BlockSpec(memory_space=pl.ANY),
        scratch_shapes=[pltpu.VMEM((ITEM_SUB, c), h3p.dtype), pltpu.SemaphoreType.DMA(())],
    )
    return pl.pallas_call(
        functools.partial(_dispatch_kernel, tm=tm, m=m),
        grid_spec=grid_spec,
        out_shape=jax.ShapeDtypeStruct((_sorted_rows(m), c), h3p.dtype),
        compiler_params=_params("arbitrary"),
        name="moe_dispatch",
    )(dest, seg_start, counts, h3p)


def _expert_kernel(ie_ref, is_ref, in_ref, ni_ref, x_hbm, wg_ref, wu_ref, wd_ref, y_hbm,
                   xin_ref, xb_ref, acc_ref, yp_ref, in_sem, out_sem, *, m, nj):
    i = pl.program_id(0)
    j = pl.program_id(1)
    n_items = ni_ref[0]
    valid = i < n_items
    slot = lax.rem(i, 2)
    half = xb_ref.shape[1] // 2
    n_sub = ITEM_ROWS // ITEM_SUB

    def first_row(item):
        return pl.multiple_of(is_ref[item], SUBLANES)

    def blocks(item):
        return (in_ref[item] + ITEM_SUB - 1) // ITEM_SUB

    def each_block(item, fn):
        for sub in range(n_sub):
            pl.when(sub * ITEM_SUB < in_ref[item])(functools.partial(fn, sub))

    def in_copy(item, sub):
        rows = pl.ds(sub * ITEM_SUB, ITEM_SUB)
        return pltpu.make_async_copy(x_hbm.at[pl.ds(first_row(item) + sub * ITEM_SUB, ITEM_SUB), :],
                                     xin_ref.at[rows, :], in_sem)

    def out_copy(item, sl, sub):
        rows = pl.ds(sub * ITEM_SUB, ITEM_SUB)
        return pltpu.make_async_copy(yp_ref.at[sl, rows, :],
                                     y_hbm.at[pl.ds(first_row(item) + sub * ITEM_SUB, ITEM_SUB), :], out_sem.at[sl])

    def by_block_count(item, fn):
        for nb in range(1, n_sub + 1):
            pl.when(blocks(item) == nb)(functools.partial(fn, nb))

    @pl.when((i == 0) & (j == 0))
    def _():
        yp_ref[1, :ITEM_SUB, :] = jnp.zeros((ITEM_SUB, yp_ref.shape[2]), yp_ref.dtype)
        tail = [pltpu.make_async_copy(yp_ref.at[1, pl.ds(0, ITEM_SUB), :],
                                      y_hbm.at[pl.ds(r0, ITEM_SUB), :], out_sem.at[1])
                for r0 in range(m, _sorted_rows(m), ITEM_SUB)]
        for c in tail:
            c.start()
        for c in tail:
            c.wait()
        each_block(0, lambda sub: in_copy(0, sub).start())

    @pl.when(valid & (j == 0))
    def _():
        each_block(i, lambda sub: in_copy(i, sub).wait())

        def unpack(nb):
            rows = pl.ds(0, nb * ITEM_SUB)
            lo, hi = _unpack_bf16_pairs(xin_ref[rows, :])
            xb_ref[rows, :half] = lo
            xb_ref[rows, half:] = hi
            acc_ref[rows, :] = jnp.zeros((nb * ITEM_SUB, acc_ref.shape[1]), F32)

        by_block_count(i, unpack)

        @pl.when(i + 1 < n_items)
        def _():
            each_block(i + 1, lambda sub: in_copy(i + 1, sub).start())

    @pl.when(valid)
    def _():
        wg = wg_ref[...].astype(BF16)
        wu = wu_ref[...].astype(BF16)
        wd = wd_ref[...].astype(BF16)

        def mlp(nb):
            rows = pl.ds(0, nb * ITEM_SUB)
            xb = xb_ref[rows, :]
            g = jnp.dot(xb, wg, preferred_element_type=F32)
            u = jnp.dot(xb, wu, preferred_element_type=F32)
            hm = (g * jax.nn.sigmoid(g) * u).astype(BF16)
            acc_ref[rows, :] += jnp.dot(hm, wd, preferred_element_type=F32)

        by_block_count(i, mlp)

    @pl.when(valid & (j == nj - 1))
    def _():
        @pl.when(i > 0)
        def _():
            each_block(i - 1, lambda sub: out_copy(i - 1, 1 - slot, sub).wait())

        def pack(nb):
            rows = pl.ds(0, nb * ITEM_SUB)
            yp_ref[slot, rows, :] = _pack_bf16_pairs(acc_ref[rows, :])

        by_block_count(i, pack)
        each_block(i, lambda sub: out_copy(i, slot, sub).start())

        @pl.when(i == n_items - 1)
        def _():
            each_block(i, lambda sub: out_copy(i, slot, sub).wait())


def _experts(item_e, item_start, item_n, n_items, x_sorted, w_gate, w_up, w_down, max_items, m):
    d = w_gate.shape[1]
    nj = D_EXPERT // F_CHUNK

    def jj(i, j, ni):
        return jnp.where(i < ni[0], j, nj - 1)

    grid_spec = pltpu.PrefetchScalarGridSpec(
        num_scalar_prefetch=4,
        grid=(max_items, nj),
        in_specs=[pl.BlockSpec(memory_space=pl.ANY),
                  pl.BlockSpec((None, d, F_CHUNK), lambda i, j, ie, s, n, ni: (ie[i], 0, jj(i, j, ni))),
                  pl.BlockSpec((None, d, F_CHUNK), lambda i, j, ie, s, n, ni: (ie[i], 0, jj(i, j, ni))),
                  pl.BlockSpec((None, F_CHUNK, d), lambda i, j, ie, s, n, ni: (ie[i], jj(i, j, ni), 0))],
        out_specs=pl.BlockSpec(memory_space=pl.ANY),
        scratch_shapes=[pltpu.VMEM((ITEM_ROWS, d // 2), jnp.uint32), pltpu.VMEM((ITEM_ROWS, d), BF16),
                        pltpu.VMEM((ITEM_ROWS, d), F32), pltpu.VMEM((2, ITEM_ROWS, d // 2), jnp.uint32),
                        pltpu.SemaphoreType.DMA(()), pltpu.SemaphoreType.DMA((2,))],
    )
    return pl.pallas_call(
        functools.partial(_expert_kernel, m=m, nj=nj),
        grid_spec=grid_spec,
        out_shape=jax.ShapeDtypeStruct((_sorted_rows(m), d // 2), jnp.uint32),
        compiler_params=_params("arbitrary", "arbitrary"),
        name="moe_experts",
    )(item_e, item_start, item_n, n_items, x_sorted, w_gate, w_up, w_down)


def _combine_kernel(dest_ref, y_hbm, x_ref, wt_ref, g_ref, o_ref, yv_ref, sem, *, tm):
    i = pl.program_id(0)
    slot = lax.rem(i, 2)

    def issue_tile(tile, sl):
        def issue(r, c):
            base = TOPK_IN_GROUP * (tile * tm + r)
            for k in range(TOPK_IN_GROUP):
                _row_copy(y_hbm, dest_ref[base + k], yv_ref.at[sl, k], r, sem.at[sl]).start()
            return c

        lax.fori_loop(0, tm, issue, 0, unroll=4)

    @pl.when(i == 0)
    def _():
        issue_tile(0, 0)

    @pl.when(i + 1 < pl.num_programs(0))
    def _():
        issue_tile(i + 1, 1 - slot)

    for k in range(TOPK_IN_GROUP):
        pltpu.make_async_copy(y_hbm.at[pl.ds(0, tm), :], yv_ref.at[slot, k], sem.at[slot]).wait()
    w = wt_ref[...]
    half = x_ref.shape[1] // 2
    parts = [x_ref[:, :half], x_ref[:, half:]]
    for k in range(TOPK_IN_GROUP):
        lo, hi = _unpack_bf16_pairs(yv_ref[slot, k])
        parts = [parts[0] + lo.astype(F32) * w[:, k:k + 1], parts[1] + hi.astype(F32) * w[:, k:k + 1]]
    ms = (jnp.sum(parts[0] * parts[0], axis=-1, keepdims=True)
          + jnp.sum(parts[1] * parts[1], axis=-1, keepdims=True)) / x_ref.shape[1]
    r = lax.rsqrt(ms + EPS)
    o_ref[:, :half] = parts[0] * r * g_ref[:, :half]
    o_ref[:, half:] = parts[1] * r * g_ref[:, half:]


def _combine(dest, y_sorted, x2, wts, g_final, tm=128):
    n, d = x2.shape
    grid_spec = pltpu.PrefetchScalarGridSpec(
        num_scalar_prefetch=1,
        grid=(n // tm,),
        in_specs=[pl.BlockSpec(memory_space=pl.ANY),
                  pl.BlockSpec((tm, d), lambda i, dr: (i, 0)),
                  pl.BlockSpec((tm, LANES), lambda i, dr: (i, 0)),
                  pl.BlockSpec((1, d), lambda i, dr: (0, 0))],
        out_specs=pl.BlockSpec((tm, d), lambda i, dr: (i, 0)),
        scratch_shapes=[pltpu.VMEM((2, TOPK_IN_GROUP, tm, d // 2), y_sorted.dtype), pltpu.SemaphoreType.DMA((2,))],
    )
    return pl.pallas_call(
        functools.partial(_combine_kernel, tm=tm),
        grid_spec=grid_spec,
        out_shape=jax.ShapeDtypeStruct((n, d), F32),
        compiler_params=_params("arbitrary"),
        name="moe_combine_final_norm",
    )(dest, y_sorted, x2, wts, g_final.reshape(1, d))


def _pad_cols(w, width):
    return jnp.pad(w, ((0, 0), (0, width - w.shape[1])))


def _mixer(x2d, bsz, s, g_norm, w_in, b_f, g_q_lat, g_kv_lat, g_idx_k, b_idx_k, w_uq, w_idx_q, w_uk, w_uv,
           w_up_a, w_up_b, w_out):
    d = x2d.shape[1]
    topk = min(DSA_TOPK_MAX, s // 4)
    h1 = _rmsnorm(x2d, g_norm, BF16)
    c_small = 3 * FOX_WIDTH
    o_fa, o_cq = c_small, c_small + FOX_HEADS
    o_ckv = o_cq + Q_LORA
    o_kr = o_ckv + KV_LORA
    o_ik = o_kr + DSA_ROPE_DIM
    o_iw = o_ik + IDX_DIM
    o_ga = o_iw + IDX_HEADS
    o_gb = o_ga + d
    w_t = jnp.swapaxes(w_in, 0, 1)
    pad_rows = lambda w: jnp.pad(w, ((0, LANES - w.shape[0]), (0, 0)))
    w_small_t = jnp.concatenate([
        w_t[o_cq:o_ckv], w_t[o_ckv:o_kr], w_t[o_ik:o_iw],
        pad_rows(w_t[o_kr:o_ik]), pad_rows(w_t[o_fa:o_cq]), pad_rows(w_t[o_iw:o_ga])], axis=0)

    qkv = _mm(h1, w_t, 3 * FOX_WIDTH, w_rows=True, scaled_cols=(FOX_WIDTH, FOX_QSCALE), name="proj_qkv")
    small = _mm(h1, w_small_t, SMALL_W, out_dtype=F32, tm=512, tn=SMALL_W // 2, w_rows=True, name="proj_small")

    tabs64 = _rope_tables(s, DSA_ROPE_DIM)
    tabs128 = _rope_tables(s, IDX_DIM)
    cq_n, kv_cat, k_idx, w_idx, fa = _prep(small, g_q_lat, g_kv_lat, g_idx_k, b_idx_k, tabs64, tabs128, s)

    cum = _fox_cum(fa, _pad_cols(b_f.reshape(1, -1), LANES), bsz, s)
    att_a = _fox_attention(qkv, cum, bsz, s)

    w_uq3 = w_uq.reshape(Q_LORA, DSA_HEADS, DSA_NOPE_DIM + DSA_ROPE_DIM)
    w_uq_nope = w_uq3[:, :, :DSA_NOPE_DIM].reshape(Q_LORA, DSA_HEADS * DSA_NOPE_DIM).astype(BF16)
    w_uq_rope = jnp.pad(w_uq3[:, :, DSA_NOPE_DIM:], ((0, 0), (0, 0), (0, LANES - DSA_ROPE_DIM))).reshape(
        Q_LORA, DSA_HEADS * LANES).astype(BF16)
    q_cat = _q_latent(cq_n, w_uq_nope, w_uq_rope, w_uk.astype(BF16), tabs64, bsz, s)
    q_idx = _mm(cq_n, w_idx_q, IDX_HEADS * IDX_DIM, heads=(bsz, s), rope_tabs=tabs128, name="proj_q_idx")
    w_idx_t = jnp.transpose(w_idx.reshape(bsz, s, LANES)[:, :, :IDX_HEADS], (0, 2, 1))
    bias = _indexer_mask(q_idx, k_idx.reshape(bsz, s, IDX_DIM), w_idx_t, bsz, s, topk)
    kv3 = kv_cat.reshape(bsz, s, KV_PAD)
    kv_t = jnp.transpose(kv3.reshape(bsz, s // KEY_CHUNK, KEY_CHUNK, KV_PAD), (0, 1, 3, 2))
    o_lat = _dsa_attention(q_cat, kv_t, kv3, bias, bsz, s)
    att_b = _uv_project(o_lat, w_uv.astype(BF16), bsz, s)

    merged = _gated_merge(h1, att_a, att_b, w_t, o_ga, o_gb, w_up_a, w_up_b)
    return _mm(merged, w_out, d, out_dtype=F32, res=x2d, name="proj_out")


def _memory_layer(x1, mem, bsz, s, g_x, g_m, w_qm, w_km, w_vm, w_om):
    n_mem = mem.shape[1]
    d = x1.shape[1]
    hd = MEM_HEADS * MEM_HEAD_DIM
    mem_n = _rmsnorm(mem.reshape(bsz * n_mem, d), g_m, BF16)
    kv_m = _mm(mem_n, jnp.concatenate([w_km, w_vm], axis=1), 2 * hd, tm=bsz * n_mem, name="proj_kv_mem")
    km_t = jnp.transpose(kv_m[:, :hd].reshape(bsz, n_mem, hd), (0, 2, 1))
    return _mem_attention(x1, g_x, w_qm.astype(BF16), km_t, kv_m, w_om.astype(BF16), bsz, s, n_mem)


def _moe_layer(x2, g_ffn, w_rg, b_rg, w_re, b_re, w_gate, w_up, w_down, g_final):
    n, d = x2.shape
    m = n * TOPK_IN_GROUP
    w_r = _pad_cols(jnp.concatenate([w_re, w_rg], axis=1), LANES)
    b_r = _pad_cols(jnp.concatenate([b_re, b_rg]).reshape(1, -1), LANES)
    h3p, ridx, wts, cnt = _router(x2, g_ffn, w_r, b_r)
    counts = cnt[0, :N_EXPERTS].astype(jnp.int32)
    e_flat = ridx[:, :TOPK_IN_GROUP].reshape(m)
    rank_flat = ridx[:, TOPK_IN_GROUP:2 * TOPK_IN_GROUP].reshape(m)
    seg = (counts + SUBLANES - 1) // SUBLANES * SUBLANES
    starts = (jnp.cumsum(seg) - seg).astype(jnp.int32)
    n_it = (counts + ITEM_ROWS - 1) // ITEM_ROWS
    it_end = jnp.cumsum(n_it)
    it_first = it_end - n_it
    max_items = (m + N_EXPERTS * (ITEM_ROWS - 1)) // ITEM_ROWS
    t = jnp.arange(max_items, dtype=jnp.int32)
    item_e = jnp.minimum(jnp.sum(it_end[None, :] <= t[:, None], axis=1), N_EXPERTS - 1).astype(jnp.int32)
    k_in = t - it_first[item_e]
    item_start = (starts[item_e] + k_in * ITEM_ROWS).astype(jnp.int32)
    item_n = jnp.clip(counts[item_e] - k_in * ITEM_ROWS, 0, ITEM_ROWS).astype(jnp.int32)
    n_items = it_end[-1:].astype(jnp.int32)
    last = jnp.maximum(n_items[0] - 1, 0)
    item_e = jnp.where(t < n_items[0], item_e, item_e[last])
    item_start = jnp.where(t < n_items[0], item_start, 0)
    item_n = jnp.where(t < n_items[0], item_n, 0)
    e_ids = jnp.arange(N_EXPERTS, dtype=jnp.int32)
    dest = rank_flat + jnp.sum(jnp.where(e_flat[:, None] == e_ids[None, :], starts[None, :], 0), axis=1)
    x_sorted = _dispatch(dest, starts, counts, h3p)
    y_sorted = _experts(item_e, item_start, item_n, n_items, x_sorted, w_gate, w_up, w_down, max_items, m)
    return _combine(dest, y_sorted, x2, wts, g_final)


def kernel(x, mem, g_norm_mix, w_in, b_f, g_q_lat, g_kv_lat, g_idx_k, b_idx_k, w_uq, w_idx_q, w_uk, w_uv,
           w_up_a, w_up_b, w_out, g_norm_mem_x, g_mem, w_qm, w_km, w_vm, w_om, g_norm_ffn,
           w_rg, b_rg, w_re, b_re, w_gate, w_up, w_down, g_final):
    bsz, s, d = x.shape
    depth = w_in.shape[0]
    assert depth == 1, "the final norm is fused into the last MoE combine"
    l = 0
    x2d = x.reshape(bsz * s, d)
    x1 = _mixer(x2d, bsz, s, g_norm_mix[l], w_in[l], b_f[l], g_q_lat[l], g_kv_lat[l], g_idx_k[l], b_idx_k[l],
                w_uq[l], w_idx_q[l], w_uk[l], w_uv[l], w_up_a[l], w_up_b[l], w_out[l])
    x2 = _memory_layer(x1, mem, bsz, s, g_norm_mem_x[l], g_mem[l], w_qm[l], w_km[l], w_vm[l], w_om[l])
    out = _moe_layer(x2, g_norm_ffn[l], w_rg[l], b_rg[l], w_re[l], b_re[l], w_gate[l], w_up[l], w_down[l], g_final)
    return out.reshape(bsz, s, d)
```

```python
import functools

import jax
import jax.numpy as jnp
from jax import lax
from jax.experimental import pallas as pl
from jax.experimental.pallas import tpu as pltpu

F32 = jnp.float32
BF16 = jnp.bfloat16

EPS = 1e-6
ROPE_THETA = 10000.0
FOX_HEADS = 16
FOX_HEAD_DIM = 128
DSA_HEADS = 16
DSA_NOPE_DIM = 128
DSA_ROPE_DIM = 64
DSA_V_DIM = 128
Q_LORA = 1024
KV_LORA = 256
IDX_HEADS = 32
IDX_DIM = 128
DSA_TOPK_MAX = 256
MEM_HEADS = 4
MEM_HEAD_DIM = 128
N_GROUPS = 8
EXPERTS_PER_GROUP = 8
N_EXPERTS = N_GROUPS * EXPERTS_PER_GROUP
TOPK_IN_GROUP = 2
D_EXPERT = 512

FOX_WIDTH = FOX_HEADS * FOX_HEAD_DIM
LANES = 128
SUBLANES = 8
KV_PAD = KV_LORA + LANES
NEG_BIG = -1e30
LOG2E = 1.4426950408889634
FOX_QSCALE = FOX_HEAD_DIM ** -0.5 * LOG2E
FOX_HB = 4
DSA_QSCALE = (DSA_NOPE_DIM + DSA_ROPE_DIM) ** -0.5 * LOG2E
VMEM_LIMIT_BYTES = 56 * 1024 * 1024
ITEM_ROWS = 384
ITEM_SUB = 128
F_CHUNK = 256
KEY_CHUNK = 256


def _params(*sem):
    return pltpu.CompilerParams(dimension_semantics=sem, vmem_limit_bytes=VMEM_LIMIT_BYTES)


def _rmsnorm_kernel(x_ref, g_ref, o_ref):
    x = x_ref[...].astype(F32)
    ms = jnp.mean(x * x, axis=-1, keepdims=True)
    o_ref[...] = (x * lax.rsqrt(ms + EPS) * g_ref[...]).astype(o_ref.dtype)


def _rmsnorm(x2d, g, out_dtype, tm=256):
    n, d = x2d.shape
    return pl.pallas_call(
        _rmsnorm_kernel,
        grid=(n // tm,),
        in_specs=[pl.BlockSpec((tm, d), lambda i: (i, 0)), pl.BlockSpec((1, d), lambda i: (0, 0))],
        out_specs=pl.BlockSpec((tm, d), lambda i: (i, 0)),
        out_shape=jax.ShapeDtypeStruct((n, d), out_dtype),
        compiler_params=_params("arbitrary"),
        name="rmsnorm",
    )(x2d, g.reshape(1, d).astype(F32))


_NT_DIMS = (((1,), (1,)), ((), ()))


def _w_rows_spec(tn, k, row0, index_of_j):
    if row0 % tn == 0:
        return pl.BlockSpec((tn, k), lambda *g: (index_of_j(*g) + row0 // tn, 0))
    assert row0 % SUBLANES == 0
    return pl.BlockSpec((pl.Element(tn), pl.Element(k)),
                        lambda *g: (pl.multiple_of(index_of_j(*g) * tn + row0, SUBLANES), 0))


def _mm_kernel(a_ref, b_ref, *rest, has_res, head_out, rope, w_rows, col_scale):
    pos = 0
    res_ref = cos_ref = sin_ref = None
    if has_res:
        res_ref = rest[pos]
        pos += 1
    if rope:
        cos_ref, sin_ref = rest[pos], rest[pos + 1]
        pos += 2
    o_ref = rest[pos]
    if b_ref.dtype == BF16:
        wb_ref = b_ref
    else:
        wb_ref = rest[pos + 1]

        @pl.when(pl.program_id(1) == 0)
        def _():
            wb_ref[...] = b_ref[...].astype(BF16)

    if w_rows:
        acc = lax.dot_general(a_ref[...], wb_ref[...], _NT_DIMS, preferred_element_type=F32)
    else:
        acc = jnp.dot(a_ref[...], wb_ref[...], preferred_element_type=F32)
    if col_scale is not None:
        acc = acc * jnp.where(pl.program_id(0) < col_scale[0], col_scale[1], 1.0)
    if has_res:
        acc = acc + res_ref[...]
    if head_out:
        for hh in range(acc.shape[1] // LANES):
            part = acc[:, hh * LANES:(hh + 1) * LANES]
            if rope:
                part = part * cos_ref[...] + pltpu.roll(part, LANES // 2, 1) * sin_ref[...]
            o_ref[hh] = part.astype(o_ref.dtype)
    else:
        o_ref[...] = acc.astype(o_ref.dtype)


def _mm(a, w, n_cols, *, col0=0, out_dtype=BF16, tm=1024, tn=512, res=None, heads=None, rope_tabs=None,
        w_rows=False, scaled_cols=None, name="mm"):
    m, k = a.shape
    tm = min(tm, m if heads is None else heads[1])
    assert n_cols % tn == 0 and m % tm == 0
    if w_rows:
        w_spec = _w_rows_spec(tn, k, col0, lambda j, i: j)
    else:
        assert col0 % tn == 0
        w_spec = pl.BlockSpec((k, tn), lambda j, i: (0, j + col0 // tn))
    in_specs = [pl.BlockSpec((tm, k), lambda j, i: (i, 0)), w_spec]
    args = [a, w]
    if res is not None:
        in_specs.append(pl.BlockSpec((tm, tn), lambda j, i: (i, j)))
        args.append(res)
    if heads is not None:
        bsz, s = heads
        assert s % tm == 0
        spt = s // tm
        if rope_tabs is not None:
            in_specs += [pl.BlockSpec((tm, LANES), lambda j, i: (i % spt, 0))] * 2
            args += list(rope_tabs)
        hpt = tn // LANES
        out_spec = pl.BlockSpec((None, hpt, tm, LANES), lambda j, i: (i // spt, j, i % spt, 0))
        out_shape = jax.ShapeDtypeStruct((bsz, n_cols // LANES, s, LANES), out_dtype)
    else:
        out_spec = pl.BlockSpec((tm, tn), lambda j, i: (i, j))
        out_shape = jax.ShapeDtypeStruct((m, n_cols), out_dtype)
    col_scale = None
    if scaled_cols is not None:
        assert scaled_cols[0] % tn == 0
        col_scale = (scaled_cols[0] // tn, scaled_cols[1])
    kern = functools.partial(_mm_kernel, has_res=res is not None, head_out=heads is not None,
                             rope=rope_tabs is not None, w_rows=w_rows, col_scale=col_scale)
    return pl.pallas_call(
        kern,
        grid=(n_cols // tn, m // tm),
        in_specs=in_specs,
        out_specs=out_spec,
        out_shape=out_shape,
        scratch_shapes=[] if w.dtype == BF16 else [pltpu.VMEM((tn, k) if w_rows else (k, tn), BF16)],
        compiler_params=_params("arbitrary", "arbitrary"),
        name=name,
    )(*args)


def _rope_tables(s, d):
    inv = jnp.power(ROPE_THETA, -jnp.arange(0, d, 2, dtype=F32) / d)
    ang = jnp.arange(s, dtype=jnp.int32).astype(F32)[:, None] * inv[None, :]
    cos, sin = jnp.cos(ang), jnp.sin(ang)
    cos_t = jnp.tile(jnp.concatenate([cos, cos], axis=-1), (1, LANES // d))
    sin_t = jnp.tile(jnp.concatenate([-sin, sin], axis=-1), (1, LANES // d))
    return cos_t, sin_t


def _rot_half64(x):
    lane = lax.broadcasted_iota(jnp.int32, x.shape, 1)
    return jnp.where((lane % 64) < 32, pltpu.roll(x, 96, 1), pltpu.roll(x, 32, 1))


_C_CQ = 0
_C_CKV = Q_LORA
_C_IK = _C_CKV + KV_LORA
_C_KR = _C_IK + IDX_DIM
_C_FA = _C_KR + LANES
_C_IW = _C_FA + LANES
SMALL_W = _C_IW + LANES


def _prep_kernel(s_ref, gq_ref, gkv_ref, gik_ref, bik_ref, c64_ref, s64_ref, c128_ref, s128_ref,
                 cq_ref, kv_ref, kidx_ref, widx_ref, fa_ref):
    cq = s_ref[:, _C_CQ:_C_CQ + Q_LORA]
    cq_ref[...] = (cq * lax.rsqrt(jnp.mean(cq * cq, axis=-1, keepdims=True) + EPS) * gq_ref[...]).astype(BF16)
    ckv = s_ref[:, _C_CKV:_C_CKV + KV_LORA]
    kv_ref[:, :KV_LORA] = (ckv * lax.rsqrt(jnp.mean(ckv * ckv, axis=-1, keepdims=True) + EPS)
                           * gkv_ref[...]).astype(BF16)
    kr = s_ref[:, _C_KR:_C_KR + LANES]
    kv_ref[:, KV_LORA:] = (kr * c64_ref[...] + _rot_half64(kr) * s64_ref[...]).astype(BF16)
    ik = s_ref[:, _C_IK:_C_IK + IDX_DIM]
    mu = jnp.mean(ik, axis=-1, keepdims=True)
    xc = ik - mu
    ikn = xc * lax.rsqrt(jnp.mean(xc * xc, axis=-1, keepdims=True) + EPS) * gik_ref[...] + bik_ref[...]
    kidx_ref[...] = (ikn * c128_ref[...] + pltpu.roll(ikn, LANES // 2, 1) * s128_ref[...]).astype(BF16)
    widx_ref[...] = s_ref[:, _C_IW:_C_IW + LANES] * (IDX_HEADS * IDX_DIM) ** -0.5
    fa_ref[...] = s_ref[:, _C_FA:_C_FA + LANES]


def _prep(small, g_q, g_kv, g_ik, b_ik, tabs64, tabs128, s, tm=256):
    n = small.shape[0]
    spt = s // tm
    row = lambda w: pl.BlockSpec((tm, w), lambda i: (i, 0))
    vec = lambda w: pl.BlockSpec((1, w), lambda i: (0, 0))
    tab = pl.BlockSpec((tm, LANES), lambda i: (i % spt, 0))
    return pl.pallas_call(
        _prep_kernel,
        grid=(n // tm,),
        in_specs=[row(SMALL_W), vec(Q_LORA), vec(KV_LORA), vec(IDX_DIM), vec(IDX_DIM), tab, tab, tab, tab],
        out_specs=[row(Q_LORA), row(KV_PAD), row(IDX_DIM), row(LANES), row(LANES)],
        out_shape=[jax.ShapeDtypeStruct((n, Q_LORA), BF16), jax.ShapeDtypeStruct((n, KV_PAD), BF16),
                   jax.ShapeDtypeStruct((n, IDX_DIM), BF16), jax.ShapeDtypeStruct((n, LANES), F32),
                   jax.ShapeDtypeStruct((n, LANES), F32)],
        compiler_params=_params("arbitrary"),
        name="mixer_prep",
    )(small, g_q.reshape(1, -1), g_kv.reshape(1, -1), g_ik.reshape(1, -1), b_ik.reshape(1, -1),
      tabs64[0], tabs64[1], tabs128[0], tabs128[1])


def _cum_kernel(fa_ref, bf_ref, cum_ref, carry_ref):
    @pl.when(pl.program_id(1) == 0)
    def _():
        carry_ref[...] = jnp.zeros_like(carry_ref)

    z = fa_ref[...] + bf_ref[...]
    logf = jnp.minimum(z, 0.0) - jnp.log1p(jnp.exp(-jnp.abs(z)))
    t = z.shape[0]
    tri = (lax.broadcasted_iota(jnp.int32, (t, t), 0) >= lax.broadcasted_iota(jnp.int32, (t, t), 1)).astype(F32)
    c = jnp.dot(tri, logf, precision=lax.Precision.HIGHEST, preferred_element_type=F32) + carry_ref[...]
    cum_ref[...] = c
    carry_ref[...] = c[t - 1:t, :]


def _fox_cum(fa, b_f_pad, bsz, s, t=256):
    return pl.pallas_call(
        _cum_kernel,
        grid=(bsz, s // t),
        in_specs=[pl.BlockSpec((None, t, LANES), lambda b, i: (b, i, 0)),
                  pl.BlockSpec((1, LANES), lambda b, i: (0, 0))],
        out_specs=pl.BlockSpec((None, t, LANES), lambda b, i: (b, i, 0)),
        out_shape=jax.ShapeDtypeStruct((bsz, s, LANES), F32),
        scratch_shapes=[pltpu.VMEM((1, LANES), F32)],
        compiler_params=_params("arbitrary", "arbitrary"),
        name="fox_cumsum",
    )(fa.reshape(bsz, s, LANES), b_f_pad)


def _lane_fold(x, op):
    out = x[:, :LANES]
    for c in range(1, x.shape[1] // LANES):
        out = op(out, x[:, c * LANES:(c + 1) * LANES])
    return out


def _split3(x):
    t1 = x.astype(BF16)
    r = x - t1.astype(F32)
    t2 = r.astype(BF16)
    t3 = (r - t2.astype(F32)).astype(BF16)
    return t1, t2, t3


def _gate_lanes(c, own_first):
    t1, t2, t3 = (t.astype(F32) for t in _split3(c))
    lane = lax.broadcasted_iota(jnp.int32, c.shape, 1)
    a, b = (0, 3) if own_first else (3, 0)
    ones3 = jnp.where((lane >= b) & (lane < b + 3), 1.0, 0.0)
    out = jnp.where(lane == a, t1, jnp.where(lane == a + 1, t2, jnp.where(lane == a + 2, t3, ones3)))
    return out.astype(BF16)


def _fox_kernel(q_ref, k_ref, v_ref, cum_ref, o_ref, kf_ref, vf_ref, s_ref, *, t):
    hp = pl.program_id(1)
    i = pl.program_id(2)
    d = FOX_HEAD_DIM
    heads = range(FOX_HB)
    row = lax.broadcasted_iota(jnp.int32, (LANES, LANES), 0)

    def gate_column(c, hh):
        pick = (row == hp * FOX_HB + hh).astype(F32)
        return jnp.dot(c, pick, precision=lax.Precision.HIGHEST, preferred_element_type=F32) * LOG2E

    @pl.when(i == 0)
    def _():
        for hh in heads:
            kf_ref[hh, :, :d] = k_ref[:, hh * d:(hh + 1) * d]
            kf_ref[hh, :, d:] = _gate_lanes(-gate_column(cum_ref[...], hh), True)
            vf_ref[hh, :, :d] = v_ref[:, hh * d:(hh + 1) * d]
            vf_ref[hh, :, d:] = jnp.ones((vf_ref.shape[1], LANES), BF16)

    cum_q = cum_ref[pl.ds(pl.multiple_of(i * t, t), t), :]
    qs = [jnp.concatenate([q_ref[:, hh * d:(hh + 1) * d], _gate_lanes(gate_column(cum_q, hh), False)], axis=1)
          for hh in heads]

    def logits(hh, kc):
        kb = kf_ref[hh, pl.ds(pl.multiple_of(kc * t, t), t), :]
        return lax.dot_general(qs[hh], kb, _NT_DIMS, preferred_element_type=F32)

    def pass1(kc, m_lanes):
        out = []
        for hh in heads:
            s = logits(hh, kc)
            s_ref[hh, kc] = s
            out.append(jnp.maximum(m_lanes[hh], _lane_fold(s, jnp.maximum)))
        return tuple(out)

    m_lanes = lax.fori_loop(0, i, pass1, tuple(jnp.full((t, LANES), NEG_BIG, F32) for _ in heads))
    causal = lax.broadcasted_iota(jnp.int32, (t, t), 0) >= lax.broadcasted_iota(jnp.int32, (t, t), 1)
    ms = []
    for hh in heads:
        s = jnp.where(causal, logits(hh, i), NEG_BIG)
        s_ref[hh, i] = s
        m_lane = jnp.maximum(m_lanes[hh], _lane_fold(s, jnp.maximum))
        m = jnp.broadcast_to(jnp.max(m_lane, axis=1, keepdims=True), (t, LANES))
        ms.append(jnp.concatenate([m] * (t // LANES), axis=1))

    def pass2(kc, accs):
        out = []
        for hh in heads:
            p = jnp.exp2(s_ref[hh, kc] - ms[hh])
            vb = vf_ref[hh, pl.ds(pl.multiple_of(kc * t, t), t), :]
            out.append(accs[hh] + jnp.dot(p.astype(BF16), vb, preferred_element_type=F32))
        return tuple(out)

    accs = lax.fori_loop(0, i + 1, pass2, tuple(jnp.zeros((t, d + LANES), F32) for _ in heads))
    for hh in heads:
        o_ref[:, hh * d:(hh + 1) * d] = (accs[hh][:, :d] / accs[hh][:, d:]).astype(o_ref.dtype)


def _fox_attention(qkv, cum, bsz, s, t=512):
    n = bsz * s
    nq = s // t
    hp = FOX_HEADS // FOX_HB
    w = FOX_HB * LANES
    assert FOX_HEAD_DIM == LANES and FOX_HEADS % FOX_HB == 0
    kern = functools.partial(_fox_kernel, t=t)
    return pl.pallas_call(
        kern,
        grid=(bsz, hp, nq),
        in_specs=[pl.BlockSpec((t, w), lambda b, h, i: (b * nq + i, h)),
                  pl.BlockSpec((s, w), lambda b, h, i: (b, hp + h)),
                  pl.BlockSpec((s, w), lambda b, h, i: (b, 2 * hp + h)),
                  pl.BlockSpec((None, s, LANES), lambda b, h, i: (b, 0, 0))],
        out_specs=pl.BlockSpec((t, w), lambda b, h, i: (b * nq + i, h)),
        out_shape=jax.ShapeDtypeStruct((n, FOX_WIDTH), BF16),
        scratch_shapes=[pltpu.VMEM((FOX_HB, s, 2 * LANES), BF16), pltpu.VMEM((FOX_HB, s, 2 * LANES), BF16),
                        pltpu.VMEM((FOX_HB, nq, t, t), F32)],
        compiler_params=_params("arbitrary", "arbitrary", "arbitrary"),
        name="fox_attention",
    )(qkv, qkv, qkv, cum)


def _qlat_kernel(cq_ref, wn_ref, wr_ref, wuk_ref, c64_ref, s64_ref, o_ref):
    cq = cq_ref[...]
    qn = jnp.dot(cq, wn_ref[...], preferred_element_type=F32).astype(BF16)
    qr = jnp.dot(cq, wr_ref[...], preferred_element_type=F32)
    for h in range(DSA_HEADS):
        cols = slice(h * LANES, (h + 1) * LANES)
        lat = jnp.dot(qn[:, cols], wuk_ref[h], preferred_element_type=F32)
        o_ref[h, :, :KV_LORA] = (lat * DSA_QSCALE).astype(BF16)
        r = qr[:, cols]
        o_ref[h, :, KV_LORA:] = ((r * c64_ref[...] + _rot_half64(r) * s64_ref[...]) * DSA_QSCALE).astype(BF16)


def _q_latent(cq_n, w_nope_b, w_rope_b, w_uk_b, tabs64, bsz, s, tm=256):
    spt = s // tm
    kq = cq_n.shape[1]
    wide = DSA_HEADS * LANES
    return pl.pallas_call(
        _qlat_kernel,
        grid=(bsz * spt,),
        in_specs=[pl.BlockSpec((tm, kq), lambda i: (i, 0)),
                  pl.BlockSpec((kq, wide), lambda i: (0, 0)),
                  pl.BlockSpec((kq, wide), lambda i: (0, 0)),
                  pl.BlockSpec((DSA_HEADS, DSA_NOPE_DIM, KV_LORA), lambda i: (0, 0, 0)),
                  pl.BlockSpec((tm, LANES), lambda i: (i % spt, 0)),
                  pl.BlockSpec((tm, LANES), lambda i: (i % spt, 0))],
        out_specs=pl.BlockSpec((None, DSA_HEADS, tm, KV_PAD), lambda i: (i // spt, 0, i % spt, 0)),
        out_shape=jax.ShapeDtypeStruct((bsz, DSA_HEADS, s, KV_PAD), BF16),
        compiler_params=_params("arbitrary"),
        name="q_latent",
    )(cq_n, w_nope_b, w_rope_b, w_uk_b, tabs64[0], tabs64[1])


def _needed_chunks(i, tq):
    return ((i + 1) * tq + KEY_CHUNK - 1) // KEY_CHUNK


BISECT_MAX_STEPS = 320
BISECT_UNROLL = 4


def _indexer_kernel(q_ref, k_ref, wt_ref, o_ref, sc_ref, thr_ref, cut_ref, *, tq, topk, hg):
    i = pl.program_id(1)
    tk = KEY_CHUNK
    n_chunks = sc_ref.shape[0]
    nk = _needed_chunks(i, tq)
    qpos = i * tq + lax.broadcasted_iota(jnp.int32, (tk, tq), 1)
    krow = lax.broadcasted_iota(jnp.int32, (tk, tq), 0)
    inf = jnp.inf

    def score_chunk(kc, c):
        k0 = pl.multiple_of(kc * tk, tk)
        kb = k_ref[pl.ds(k0, tk), :]
        sc = jnp.zeros((tk, tq), F32)
        for g in range(IDX_HEADS // hg):
            q2 = q_ref[g * hg:(g + 1) * hg].reshape(hg * tq, IDX_DIM)
            d = lax.dot_general(kb, q2, _NT_DIMS, preferred_element_type=F32)
            for hh in range(hg):
                h = g * hg + hh
                sc = sc + wt_ref[h:h + 1, :] * jnp.maximum(d[:, hh * tq:(hh + 1) * tq], 0.0)
        sc_ref[kc] = jnp.where(k0 + krow <= qpos, sc, -inf)
        return c

    lax.fori_loop(0, nk, score_chunk, 0)

    def pad_chunk(kc, c):
        sc_ref[kc] = jnp.full((tk, tq), -inf, F32)
        return c

    lax.fori_loop(nk, n_chunks, pad_chunk, 0)
    kf = float(topk)
    n_causal = (i * tq + lax.broadcasted_iota(jnp.int32, (1, tq), 1) + 1).astype(F32)

    def fold(x, op):
        return op(x.reshape(tk // SUBLANES, SUBLANES, tq), axis=0)

    def key_pos(kc):
        return (kc * tk + krow).astype(F32)

    def threshold(n_counted):
        def count_ge(t):
            acc = jnp.zeros((SUBLANES, tq), F32)
            for kc in range(n_counted):
                acc = acc + fold(jnp.where(sc_ref[kc] >= t, 1.0, 0.0), jnp.sum)
            return jnp.sum(acc, axis=0, keepdims=True)

        hi = jnp.full((SUBLANES, tq), -inf, F32)
        lo = jnp.full((SUBLANES, tq), inf, F32)
        for kc in range(n_counted):
            s = sc_ref[kc]
            hi = jnp.maximum(hi, fold(s, jnp.max))
            lo = jnp.minimum(lo, fold(jnp.where(s == -inf, inf, s), jnp.min))
        hi = jnp.max(hi, axis=0, keepdims=True)
        lo = jnp.min(lo, axis=0, keepdims=True)
        few = n_causal <= kf
        at_max = count_ge(hi) >= kf
        lo = jnp.where(few, -inf, jnp.where(at_max, hi, lo))
        done = jnp.where(few | at_max, 1.0, 0.0)

        def unsettled(state):
            it, _, _, done = state
            return (it < BISECT_MAX_STEPS) & (jnp.sum(1.0 - done) > 0.0)

        def step(state):
            it, lo, hi, done = state
            for _ in range(BISECT_UNROLL):
                mid = 0.5 * lo + 0.5 * hi
                cnt = count_ge(mid)
                ge = cnt >= kf
                closed = (mid <= lo) | (mid >= hi)
                live = done == 0.0
                lo = jnp.where(live & ge, mid, lo)
                hi = jnp.where(live & jnp.logical_not(ge), mid, hi)
                done = jnp.where(live & ((ge & (cnt == kf)) | closed), 1.0, done)
            return it + BISECT_UNROLL, lo, hi, done

        _, lo, _, _ = lax.while_loop(unsettled, step, (jnp.int32(0), lo, hi, done))
        thr_ref[...] = lo
        cut_ref[...] = jnp.full((1, tq), float(n_chunks * tk), F32)

        over = jnp.where((count_ge(lo) > kf) & jnp.logical_not(few), 1.0, 0.0)

        @pl.when(jnp.sum(over) > 0.0)
        def _():
            def count_where(pred):
                def body(kc, acc):
                    hit = pred(sc_ref[kc], key_pos(kc))
                    return acc + fold(jnp.where(hit, 1.0, 0.0), jnp.sum)

                acc = lax.fori_loop(0, n_counted, body, jnp.zeros((SUBLANES, tq), F32))
                return jnp.sum(acc, axis=0, keepdims=True)

            need = kf - count_where(lambda s, kp: s > lo)

            def narrow(_, bracket):
                lo_p, hi_p = bracket
                mid = jnp.floor(0.5 * (lo_p + hi_p))
                ok = count_where(lambda s, kp: (s == lo) & (kp <= mid)) >= need
                return jnp.where(ok, lo_p, mid), jnp.where(ok, mid, hi_p)

            n_pos = n_counted * tk
            _, hi_p = lax.fori_loop(0, n_pos.bit_length(), narrow,
                                    (jnp.full((1, tq), -1.0, F32), jnp.full((1, tq), n_pos - 1.0, F32)))
            cut_ref[...] = jnp.where(over > 0.0, hi_p, cut_ref[...])

    half = max(n_chunks // 2, 1)
    if half < n_chunks:
        pl.when(nk <= half)(lambda: threshold(half))
        pl.when(nk > half)(lambda: threshold(n_chunks))
    else:
        threshold(n_chunks)
    thr = thr_ref[...]
    cut = cut_ref[...]

    def write_chunk(kc, c):
        s = sc_ref[kc]
        keep = ((s > thr) | ((s == thr) & (key_pos(kc) <= cut))) & (kc * tk + krow <= qpos)
        o_ref[kc] = jnp.where(keep, 0.0, NEG_BIG).T.astype(o_ref.dtype)
        return c

    lax.fori_loop(0, nk, write_chunk, 0)

    def fill_chunk(kc, c):
        o_ref[kc] = jnp.full((tq, tk), NEG_BIG, o_ref.dtype)
        return c

    lax.fori_loop(nk, n_chunks, fill_chunk, 0)


def _indexer_mask(q_idx, k_idx, w_idx_t, bsz, s, topk, tq=128, hg=4):
    kern = functools.partial(_indexer_kernel, tq=tq, topk=topk, hg=hg)
    nc = s // KEY_CHUNK
    return pl.pallas_call(
        kern,
        grid=(bsz, s // tq),
        in_specs=[pl.BlockSpec((None, IDX_HEADS, tq, IDX_DIM), lambda b, i: (b, 0, i, 0)),
                  pl.BlockSpec((None, s, IDX_DIM), lambda b, i: (b, 0, 0)),
                  pl.BlockSpec((None, IDX_HEADS, tq), lambda b, i: (b, 0, i))],
        out_specs=pl.BlockSpec((None, nc, tq, KEY_CHUNK), lambda b, i: (b, 0, i, 0)),
        out_shape=jax.ShapeDtypeStruct((bsz, nc, s, KEY_CHUNK), BF16),
        scratch_shapes=[pltpu.VMEM((nc, KEY_CHUNK, tq), F32), pltpu.VMEM((1, tq), F32),
                        pltpu.VMEM((1, tq), F32)],
        compiler_params=_params("arbitrary", "arbitrary"),
        name="indexer_topk_mask",
    )(q_idx, k_idx, w_idx_t)


def _dsa_kernel(q_ref, kvt_ref, kv_ref, bias_ref, o_ref, s_ref, m_ref, l_ref, acc_ref, *, tq):
    i = pl.program_id(1)
    tk = KEY_CHUNK
    rows = DSA_HEADS * tq
    nk = _needed_chunks(i, tq)
    q2 = q_ref[...].reshape(rows, KV_PAD)
    m_ref[...] = jnp.full(m_ref.shape, NEG_BIG, F32)

    n_pairs = (nk + 1) // 2

    def logits(kc):
        sc = jnp.dot(q2, kvt_ref[kc], preferred_element_type=F32)
        sc = (sc.reshape(DSA_HEADS, tq, tk) + bias_ref[kc].astype(F32)[None]).reshape(rows, tk)
        s_ref[kc] = sc
        return _lane_fold(sc, jnp.maximum)

    def pass1(pr, c):
        m_ref[...] = jnp.maximum(m_ref[...], jnp.maximum(logits(2 * pr), logits(2 * pr + 1)))
        return c

    lax.fori_loop(0, n_pairs, pass1, 0)
    m_ref[...] = jnp.broadcast_to(jnp.max(m_ref[...], axis=1, keepdims=True), m_ref.shape)
    l_ref[...] = jnp.zeros(l_ref.shape, F32)
    acc_ref[...] = jnp.zeros(acc_ref.shape, F32)

    def probs(kc):
        m = m_ref[...]
        p = jnp.concatenate([jnp.exp2(s_ref[kc, :, g * LANES:(g + 1) * LANES] - m)
                             for g in range(tk // LANES)], axis=1)
        kvb = kv_ref[pl.ds(pl.multiple_of(kc * tk, tk), tk), :KV_LORA]
        return _lane_fold(p, jnp.add), jnp.dot(p.astype(BF16), kvb, preferred_element_type=F32)

    def pass2(pr, c):
        l0, a0 = probs(2 * pr)
        l1, a1 = probs(2 * pr + 1)
        l_ref[...] += l0 + l1
        acc_ref[...] += a0 + a1
        return c

    lax.fori_loop(0, n_pairs, pass2, 0)
    l = jnp.dot(l_ref[...], jnp.ones((LANES, LANES), F32), precision=lax.Precision.HIGHEST,
                preferred_element_type=F32)
    o = acc_ref[...] / jnp.concatenate([l] * (KV_LORA // LANES), axis=1)
    o_ref[...] = o.reshape(DSA_HEADS, tq, KV_LORA).astype(o_ref.dtype)


def _dsa_attention(q_cat, kv_t, kv, bias, bsz, s, tq=128):
    kern = functools.partial(_dsa_kernel, tq=tq)
    nc = s // KEY_CHUNK
    assert nc % 2 == 0, "the kernel walks key chunks in pairs"
    rows = DSA_HEADS * tq
    return pl.pallas_call(
        kern,
        grid=(bsz, s // tq),
        in_specs=[pl.BlockSpec((None, DSA_HEADS, tq, KV_PAD), lambda b, i: (b, 0, i, 0)),
                  pl.BlockSpec((None, nc, KV_PAD, KEY_CHUNK), lambda b, i: (b, 0, 0, 0)),
                  pl.BlockSpec((None, s, KV_PAD), lambda b, i: (b, 0, 0)),
                  pl.BlockSpec((None, nc, tq, KEY_CHUNK), lambda b, i: (b, 0, i, 0))],
        out_specs=pl.BlockSpec((None, DSA_HEADS, tq, KV_LORA), lambda b, i: (b, 0, i, 0)),
        out_shape=jax.ShapeDtypeStruct((bsz, DSA_HEADS, s, KV_LORA), BF16),
        scratch_shapes=[pltpu.VMEM((nc, rows, KEY_CHUNK), F32), pltpu.VMEM((rows, LANES), F32),
                        pltpu.VMEM((rows, LANES), F32), pltpu.VMEM((rows, KV_LORA), F32)],
        compiler_params=_params("arbitrary", "arbitrary"),
        name="dsa_attention",
    )(q_cat, kv_t, kv, bias)


def _uv_kernel(o_ref, w_ref, y_ref):
    for h in range(DSA_HEADS):
        y_ref[:, h * DSA_V_DIM:(h + 1) * DSA_V_DIM] = jnp.dot(
            o_ref[h], w_ref[h], preferred_element_type=F32).astype(y_ref.dtype)


def _uv_project(o_lat, w_uv_b, bsz, s, tm=512):
    spt = s // tm
    return pl.pallas_call(
        _uv_kernel,
        grid=(bsz * spt,),
        in_specs=[pl.BlockSpec((None, DSA_HEADS, tm, KV_LORA), lambda i: (i // spt, 0, i % spt, 0)),
                  pl.BlockSpec((DSA_HEADS, KV_LORA, DSA_V_DIM), lambda i: (0, 0, 0))],
        out_specs=pl.BlockSpec((tm, DSA_HEADS * DSA_V_DIM), lambda i: (i, 0)),
        out_shape=jax.ShapeDtypeStruct((bsz * s, DSA_HEADS * DSA_V_DIM), BF16),
        compiler_params=_params("arbitrary"),
        name="uv_project",
    )(o_lat, w_uv_b)


def _merge_kernel(h_ref, a_ref, b_ref, wga_ref, wgb_ref, wua_ref, wub_ref, o_ref,
                  cga_ref, cgb_ref, cua_ref, cub_ref):
    @pl.when(pl.program_id(1) == 0)
    def _():
        cga_ref[...] = wga_ref[...].astype(BF16)
        cgb_ref[...] = wgb_ref[...].astype(BF16)
        cua_ref[...] = wua_ref[...].astype(BF16)
        cub_ref[...] = wub_ref[...].astype(BF16)

    h = h_ref[...]
    ga = lax.dot_general(h, cga_ref[...], _NT_DIMS, preferred_element_type=F32)
    gb = lax.dot_general(h, cgb_ref[...], _NT_DIMS, preferred_element_type=F32)
    ya = jnp.dot(a_ref[...], cua_ref[...], preferred_element_type=F32)
    yb = jnp.dot(b_ref[...], cub_ref[...], preferred_element_type=F32)
    o_ref[...] = (jax.nn.sigmoid(ga) * ya + jax.nn.sigmoid(gb) * yb).astype(o_ref.dtype)


def _gated_merge(h1, att_a, att_b, w_t, row_ga, row_gb, w_up_a, w_up_b, tm=512, tn=256):
    n, d = h1.shape
    ka, kb = att_a.shape[1], att_b.shape[1]
    return pl.pallas_call(
        _merge_kernel,
        grid=(d // tn, n // tm),
        in_specs=[pl.BlockSpec((tm, d), lambda j, i: (i, 0)),
                  pl.BlockSpec((tm, ka), lambda j, i: (i, 0)),
                  pl.BlockSpec((tm, kb), lambda j, i: (i, 0)),
                  _w_rows_spec(tn, d, row_ga, lambda j, i: j),
                  _w_rows_spec(tn, d, row_gb, lambda j, i: j),
                  pl.BlockSpec((ka, tn), lambda j, i: (0, j)),
                  pl.BlockSpec((kb, tn), lambda j, i: (0, j))],
        out_specs=pl.BlockSpec((tm, tn), lambda j, i: (i, j)),
        out_shape=jax.ShapeDtypeStruct((n, d), BF16),
        scratch_shapes=[pltpu.VMEM((tn, d), BF16), pltpu.VMEM((tn, d), BF16),
                        pltpu.VMEM((ka, tn), BF16), pltpu.VMEM((kb, tn), BF16)],
        compiler_params=_params("arbitrary", "arbitrary"),
        name="gated_merge",
    )(h1, att_a, att_b, w_t, w_t, w_up_a, w_up_b)


def _mem_kernel(x_ref, g_ref, wq_ref, kt_ref, v_ref, wom_ref, o_ref):
    x = x_ref[...]
    hn = (x * lax.rsqrt(jnp.mean(x * x, axis=-1, keepdims=True) + EPS) * g_ref[...]).astype(BF16)
    q = jnp.dot(hn, wq_ref[...], preferred_element_type=F32).astype(BF16)
    scale = MEM_HEAD_DIM ** -0.5
    outs = []
    for h in range(MEM_HEADS):
        sl = slice(h * MEM_HEAD_DIM, (h + 1) * MEM_HEAD_DIM)
        s = jnp.dot(q[:, sl], kt_ref[sl, :], preferred_element_type=F32) * scale
        m = jnp.max(s, axis=-1, keepdims=True)
        p = jnp.exp(s - m)
        p = p / jnp.sum(p, axis=-1, keepdims=True)
        outs.append(jnp.dot(p.astype(BF16), v_ref[:, sl], preferred_element_type=F32).astype(BF16))
    o = jnp.concatenate(outs, axis=-1)
    o_ref[...] = x + jnp.dot(o, wom_ref[...], preferred_element_type=F32)


def _mem_attention(x1, g_x, w_qm_b, km_t, kv_m, w_om_b, bsz, s, n_mem, tm=256):
    spt = s // tm
    hd = MEM_HEADS * MEM_HEAD_DIM
    d = x1.shape[1]
    return pl.pallas_call(
        _mem_kernel,
        grid=(bsz, spt),
        in_specs=[pl.BlockSpec((tm, d), lambda b, i: (b * spt + i, 0)),
                  pl.BlockSpec((1, d), lambda b, i: (0, 0)),
                  pl.BlockSpec((d, hd), lambda b, i: (0, 0)),
                  pl.BlockSpec((None, hd, n_mem), lambda b, i: (b, 0, 0)),
                  pl.BlockSpec((n_mem, hd), lambda b, i: (b, 1)),
                  pl.BlockSpec((hd, d), lambda b, i: (0, 0))],
        out_specs=pl.BlockSpec((tm, d), lambda b, i: (b * spt + i, 0)),
        out_shape=jax.ShapeDtypeStruct(x1.shape, F32),
        compiler_params=_params("arbitrary", "arbitrary"),
        name="memory_cross_attention",
    )(x1, g_x.reshape(1, d), w_qm_b, km_t, kv_m, w_om_b)


def _pack_bf16_pairs(x):
    c = x.shape[1] // 2
    return pltpu.pack_elementwise([x[:, :c], x[:, c:]], packed_dtype=BF16)


def _unpack_bf16_pairs(w):
    lo = pltpu.unpack_elementwise(w, index=0, packed_dtype=BF16, unpacked_dtype=F32)
    hi = pltpu.unpack_elementwise(w, index=1, packed_dtype=BF16, unpacked_dtype=F32)
    return lo.astype(BF16), hi.astype(BF16)


def _router_kernel(x_ref, g_ref, wr_ref, br_ref, h_ref, idx_ref, wt_ref, cnt_ref, carry_ref):
    @pl.when(pl.program_id(0) == 0)
    def _():
        carry_ref[...] = jnp.zeros_like(carry_ref)

    x = x_ref[...]
    hn = x * lax.rsqrt(jnp.mean(x * x, axis=-1, keepdims=True) + EPS) * g_ref[...]
    h_ref[...] = _pack_bf16_pairs(hn)
    logits = jnp.dot(hn, wr_ref[...], precision=lax.Precision.HIGHEST, preferred_element_type=F32) + br_ref[...]
    tm = x.shape[0]
    lane_i = lax.broadcasted_iota(jnp.int32, (tm, LANES), 1)
    lane = lane_i.astype(F32)
    lane_grp = lax.shift_right_logical(lane_i, EXPERTS_PER_GROUP.bit_length() - 1).astype(F32)
    big = 1e9
    is_g = (lane_i >= N_EXPERTS) & (lane_i < N_EXPERTS + N_GROUPS)
    lg = jnp.where(is_g, logits, -jnp.inf)
    mg = jnp.max(lg, axis=1, keepdims=True)
    gate_g = 1.0 / jnp.sum(jnp.exp(lg - mg), axis=1, keepdims=True)
    grp = jnp.min(jnp.where(lg == mg, lane, big), axis=1, keepdims=True) - N_EXPERTS
    in_grp = (lane_i < N_EXPERTS) & (lane_grp == grp)
    le = jnp.where(in_grp, logits, -jnp.inf)
    m1 = jnp.max(le, axis=1, keepdims=True)
    e1 = jnp.min(jnp.where(le == m1, lane, big), axis=1, keepdims=True)
    le2 = jnp.where(lane == e1, -jnp.inf, le)
    m2 = jnp.max(le2, axis=1, keepdims=True)
    e2 = jnp.min(jnp.where(le2 == m2, lane, big), axis=1, keepdims=True)
    r = jnp.exp(m2 - m1)
    w1 = gate_g * (1.0 / (1.0 + r))
    w2 = gate_g * (r / (1.0 + r))
    oh1 = (lane == e1).astype(F32)
    oh2 = (lane == e2).astype(F32)
    oh = oh1 + oh2
    tri = (lax.broadcasted_iota(jnp.int32, (tm, tm), 0) > lax.broadcasted_iota(jnp.int32, (tm, tm), 1)).astype(BF16)
    prefix = jnp.dot(tri, oh.astype(BF16), preferred_element_type=F32) + carry_ref[...]
    rank1 = jnp.sum(prefix * oh1, axis=1, keepdims=True)
    rank2 = jnp.sum(prefix * oh2, axis=1, keepdims=True)
    new_cnt = carry_ref[...] + jnp.sum(oh, axis=0, keepdims=True)
    carry_ref[...] = new_cnt
    cnt_ref[...] = new_cnt
    idx_ref[...] = jnp.where(lane_i == 0, e1, jnp.where(lane_i == 1, e2, jnp.where(
        lane_i == 2, rank1, jnp.where(lane_i == 3, rank2, 0.0)))).astype(jnp.int32)
    wt_ref[...] = jnp.where(lane_i == 0, w1, jnp.where(lane_i == 1, w2, 0.0))


def _router(x2, g, w_r, b_r, tm=256):
    n, d = x2.shape
    row = lambda w: pl.BlockSpec((tm, w), lambda i: (i, 0))
    return pl.pallas_call(
        _router_kernel,
        grid=(n // tm,),
        in_specs=[row(d), pl.BlockSpec((1, d), lambda i: (0, 0)), pl.BlockSpec((d, LANES), lambda i: (0, 0)),
                  pl.BlockSpec((1, LANES), lambda i: (0, 0))],
        out_specs=[row(d // 2), row(LANES), row(LANES), pl.BlockSpec((1, LANES), lambda i: (0, 0))],
        out_shape=[jax.ShapeDtypeStruct((n, d // 2), jnp.uint32), jax.ShapeDtypeStruct((n, LANES), jnp.int32),
                   jax.ShapeDtypeStruct((n, LANES), F32), jax.ShapeDtypeStruct((1, LANES), F32)],
        scratch_shapes=[pltpu.VMEM((1, LANES), F32)],
        compiler_params=_params("arbitrary"),
        name="moe_router",
    )(x2, g.reshape(1, d), w_r, b_r)


def _row_copy(src_hbm, row, dst, r, sem):
    return pltpu.make_async_copy(src_hbm.at[pl.ds(row, 1), :], dst.at[pl.ds(r, 1), :], sem)


def _sorted_rows(m):
    return m + N_EXPERTS * SUBLANES + ITEM_ROWS


def _dispatch_kernel(dest_ref, seg0_ref, cnt_ref, h_ref, x_hbm, zero_ref, sem, *, tm, m):
    i = pl.program_id(0)

    @pl.when(i == 0)
    def _():
        zero_ref[...] = jnp.zeros_like(zero_ref)
        tail = [pltpu.make_async_copy(zero_ref, x_hbm.at[pl.ds(r0, ITEM_SUB), :], sem)
                for r0 in range(m, _sorted_rows(m), ITEM_SUB)]
        for c in tail:
            c.start()
        for c in tail:
            c.wait()

        def fill_gap(e, total):
            gap = (-cnt_ref[e]) & (SUBLANES - 1)

            def one(r, c):
                _row_copy(zero_ref, 0, x_hbm, seg0_ref[e] + cnt_ref[e] + r, sem).start()
                return c

            lax.fori_loop(0, gap, one, 0)
            return total + gap

        n_gap = lax.fori_loop(0, N_EXPERTS, fill_gap, 0)

        def wait_gap(r, c):
            _row_copy(zero_ref, 0, x_hbm, 0, sem).wait()
            return c

        lax.fori_loop(0, n_gap, wait_gap, 0)

    def issue(r, c):
        t = i * tm + r
        for sl in range(TOPK_IN_GROUP):
            _row_copy(h_ref, r, x_hbm, dest_ref[TOPK_IN_GROUP * t + sl], sem).start()
        return c

    lax.fori_loop(0, tm, issue, 0, unroll=4)
    for _ in range(TOPK_IN_GROUP):
        pltpu.make_async_copy(h_ref, x_hbm.at[pl.ds(0, tm), :], sem).wait()


def _dispatch(dest, seg_start, counts, h3p, tm=256):
    n, c = h3p.shape
    m = dest.shape[0]
    assert (_sorted_rows(m) - m) % ITEM_SUB == 0
    grid_spec = pltpu.PrefetchScalarGridSpec(
        num_scalar_prefetch=3,
        grid=(n // tm,),
        in_specs=[pl.BlockSpec((tm, c), lambda i, d, s0, cn: (i, 0))],
        out_specs=pl.BlockSpec(memory_space=pl.ANY),
        scratch_shapes=[pltpu.VMEM((ITEM_SUB, c), h3p.dtype), pltpu.SemaphoreType.DMA(())],
    )
    return pl.pallas_call(
        functools.partial(_dispatch_kernel, tm=tm, m=m),
        grid_spec=grid_spec,
        out_shape=jax.ShapeDtypeStruct((_sorted_rows(m), c), h3p.dtype),
        compiler_params=_params("arbitrary"),
        name="moe_dispatch",
    )(dest, seg_start, counts, h3p)


def _expert_kernel(ie_ref, is_ref, in_ref, ni_ref, x_hbm, wg_ref, wu_ref, wd_ref, y_hbm,
                   xin_ref, xb_ref, acc_ref, yp_ref, in_sem, out_sem, *, m, nj):
    i = pl.program_id(0)
    j = pl.program_id(1)
    n_items = ni_ref[0]
    valid = i < n_items
    slot = lax.rem(i, 2)
    half = xb_ref.shape[1] // 2
    n_sub = ITEM_ROWS // ITEM_SUB

    def first_row(item):
        return pl.multiple_of(is_ref[item], SUBLANES)

    def blocks(item):
        return (in_ref[item] + ITEM_SUB - 1) // ITEM_SUB

    def each_block(item, fn):
        for sub in range(n_sub):
            pl.when(sub * ITEM_SUB < in_ref[item])(functools.partial(fn, sub))

    def in_copy(item, sub):
        rows = pl.ds(sub * ITEM_SUB, ITEM_SUB)
        return pltpu.make_async_copy(x_hbm.at[pl.ds(first_row(item) + sub * ITEM_SUB, ITEM_SUB), :],
                                     xin_ref.at[rows, :], in_sem)

    def out_copy(item, sl, sub):
        rows = pl.ds(sub * ITEM_SUB, ITEM_SUB)
        return pltpu.make_async_copy(yp_ref.at[sl, rows, :],
                                     y_hbm.at[pl.ds(first_row(item) + sub * ITEM_SUB, ITEM_SUB), :], out_sem.at[sl])

    def by_block_count(item, fn):
        for nb in range(1, n_sub + 1):
            pl.when(blocks(item) == nb)(functools.partial(fn, nb))

    @pl.when((i == 0) & (j == 0))
    def _():
        yp_ref[1, :ITEM_SUB, :] = jnp.zeros((ITEM_SUB, yp_ref.shape[2]), yp_ref.dtype)
        tail = [pltpu.make_async_copy(yp_ref.at[1, pl.ds(0, ITEM_SUB), :],
                                      y_hbm.at[pl.ds(r0, ITEM_SUB), :], out_sem.at[1])
                for r0 in range(m, _sorted_rows(m), ITEM_SUB)]
        for c in tail:
            c.start()
        for c in tail:
            c.wait()
        each_block(0, lambda sub: in_copy(0, sub).start())

    @pl.when(valid & (j == 0))
    def _():
        each_block(i, lambda sub: in_copy(i, sub).wait())

        def unpack(nb):
            rows = pl.ds(0, nb * ITEM_SUB)
            lo, hi = _unpack_bf16_pairs(xin_ref[rows, :])
            xb_ref[rows, :half] = lo
            xb_ref[rows, half:] = hi
            acc_ref[rows, :] = jnp.zeros((nb * ITEM_SUB, acc_ref.shape[1]), F32)

        by_block_count(i, unpack)

        @pl.when(i + 1 < n_items)
        def _():
            each_block(i + 1, lambda sub: in_copy(i + 1, sub).start())

    @pl.when(valid)
    def _():
        wg = wg_ref[...].astype(BF16)
        wu = wu_ref[...].astype(BF16)
        wd = wd_ref[...].astype(BF16)

        def mlp(nb):
            rows = pl.ds(0, nb * ITEM_SUB)
            xb = xb_ref[rows, :]
            g = jnp.dot(xb, wg, preferred_element_type=F32)
            u = jnp.dot(xb, wu, preferred_element_type=F32)
            hm = (g * jax.nn.sigmoid(g) * u).astype(BF16)
            acc_ref[rows, :] += jnp.dot(hm, wd, preferred_element_type=F32)

        by_block_count(i, mlp)

    @pl.when(valid & (j == nj - 1))
    def _():
        @pl.when(i > 0)
        def _():
            each_block(i - 1, lambda sub: out_copy(i - 1, 1 - slot, sub).wait())

        def pack(nb):
            rows = pl.ds(0, nb * ITEM_SUB)
            yp_ref[slot, rows, :] = _pack_bf16_pairs(acc_ref[rows, :])

        by_block_count(i, pack)
        each_block(i, lambda sub: out_copy(i, slot, sub).start())

        @pl.when(i == n_items - 1)
        def _():
            each_block(i, lambda sub: out_copy(i, slot, sub).wait())


def _experts(item_e, item_start, item_n, n_items, x_sorted, w_gate, w_up, w_down, max_items, m):
    d = w_gate.shape[1]
    nj = D_EXPERT // F_CHUNK

    def jj(i, j, ni):
        return jnp.where(i < ni[0], j, nj - 1)

    grid_spec = pltpu.PrefetchScalarGridSpec(
        num_scalar_prefetch=4,
        grid=(max_items, nj),
        in_specs=[pl.BlockSpec(memory_space=pl.ANY),
                  pl.BlockSpec((None, d, F_CHUNK), lambda i, j, ie, s, n, ni: (ie[i], 0, jj(i, j, ni))),
                  pl.BlockSpec((None, d, F_CHUNK), lambda i, j, ie, s, n, ni: (ie[i], 0, jj(i, j, ni))),
                  pl.BlockSpec((None, F_CHUNK, d), lambda i, j, ie, s, n, ni: (ie[i], jj(i, j, ni), 0))],
        out_specs=pl.BlockSpec(memory_space=pl.ANY),
        scratch_shapes=[pltpu.VMEM((ITEM_ROWS, d // 2), jnp.uint32), pltpu.VMEM((ITEM_ROWS, d), BF16),
                        pltpu.VMEM((ITEM_ROWS, d), F32), pltpu.VMEM((2, ITEM_ROWS, d // 2), jnp.uint32),
                        pltpu.SemaphoreType.DMA(()), pltpu.SemaphoreType.DMA((2,))],
    )
    return pl.pallas_call(
        functools.partial(_expert_kernel, m=m, nj=nj),
        grid_spec=grid_spec,
        out_shape=jax.ShapeDtypeStruct((_sorted_rows(m), d // 2), jnp.uint32),
        compiler_params=_params("arbitrary", "arbitrary"),
        name="moe_experts",
    )(item_e, item_start, item_n, n_items, x_sorted, w_gate, w_up, w_down)


def _combine_kernel(dest_ref, y_hbm, x_ref, wt_ref, g_ref, o_ref, yv_ref, sem, *, tm):
    i = pl.program_id(0)
    slot = lax.rem(i, 2)

    def issue_tile(tile, sl):
        def issue(r, c):
            base = TOPK_IN_GROUP * (tile * tm + r)
            for k in range(TOPK_IN_GROUP):
                _row_copy(y_hbm, dest_ref[base + k], yv_ref.at[sl, k], r, sem.at[sl]).start()
            return c

        lax.fori_loop(0, tm, issue, 0, unroll=4)

    @pl.when(i == 0)
    def _():
        issue_tile(0, 0)

    @pl.when(i + 1 < pl.num_programs(0))
    def _():
        issue_tile(i + 1, 1 - slot)

    for k in range(TOPK_IN_GROUP):
        pltpu.make_async_copy(y_hbm.at[pl.ds(0, tm), :], yv_ref.at[slot, k], sem.at[slot]).wait()
    w = wt_ref[...]
    half = x_ref.shape[1] // 2
    parts = [x_ref[:, :half], x_ref[:, half:]]
    for k in range(TOPK_IN_GROUP):
        lo, hi = _unpack_bf16_pairs(yv_ref[slot, k])
        parts = [parts[0] + lo.astype(F32) * w[:, k:k + 1], parts[1] + hi.astype(F32) * w[:, k:k + 1]]
    ms = (jnp.sum(parts[0] * parts[0], axis=-1, keepdims=True)
          + jnp.sum(parts[1] * parts[1], axis=-1, keepdims=True)) / x_ref.shape[1]
    r = lax.rsqrt(ms + EPS)
    o_ref[:, :half] = parts[0] * r * g_ref[:, :half]
    o_ref[:, half:] = parts[1] * r * g_ref[:, half:]


def _combine(dest, y_sorted, x2, wts, g_final, tm=128):
    n, d = x2.shape
    grid_spec = pltpu.PrefetchScalarGridSpec(
        num_scalar_prefetch=1,
        grid=(n // tm,),
        in_specs=[pl.BlockSpec(memory_space=pl.ANY),
                  pl.BlockSpec((tm, d), lambda i, dr: (i, 0)),
                  pl.BlockSpec((tm, LANES), lambda i, dr: (i, 0)),
                  pl.BlockSpec((1, d), lambda i, dr: (0, 0))],
        out_specs=pl.BlockSpec((tm, d), lambda i, dr: (i, 0)),
        scratch_shapes=[pltpu.VMEM((2, TOPK_IN_GROUP, tm, d // 2), y_sorted.dtype), pltpu.SemaphoreType.DMA((2,))],
    )
    return pl.pallas_call(
        functools.partial(_combine_kernel, tm=tm),
        grid_spec=grid_spec,
        out_shape=jax.ShapeDtypeStruct((n, d), F32),
        compiler_params=_params("arbitrary"),
        name="moe_combine_final_norm",
    )(dest, y_sorted, x2, wts, g_final.reshape(1, d))


def _pad_cols(w, width):
    return jnp.pad(w, ((0, 0), (0, width - w.shape[1])))


def _mixer(x2d, bsz, s, g_norm, w_in, b_f, g_q_lat, g_kv_lat, g_idx_k, b_idx_k, w_uq, w_idx_q, w_uk, w_uv,
           w_up_a, w_up_b, w_out):
    d = x2d.shape[1]
    topk = min(DSA_TOPK_MAX, s // 4)
    h1 = _rmsnorm(x2d, g_norm, BF16)
    c_small = 3 * FOX_WIDTH
    o_fa, o_cq = c_small, c_small + FOX_HEADS
    o_ckv = o_cq + Q_LORA
    o_kr = o_ckv + KV_LORA
    o_ik = o_kr + DSA_ROPE_DIM
    o_iw = o_ik + IDX_DIM
    o_ga = o_iw + IDX_HEADS
    o_gb = o_ga + d
    w_t = jnp.swapaxes(w_in, 0, 1)
    pad_rows = lambda w: jnp.pad(w, ((0, LANES - w.shape[0]), (0, 0)))
    w_small_t = jnp.concatenate([
        w_t[o_cq:o_ckv], w_t[o_ckv:o_kr], w_t[o_ik:o_iw],
        pad_rows(w_t[o_kr:o_ik]), pad_rows(w_t[o_fa:o_cq]), pad_rows(w_t[o_iw:o_ga])], axis=0)

    qkv = _mm(h1, w_t, 3 * FOX_WIDTH, w_rows=True, scaled_cols=(FOX_WIDTH, FOX_QSCALE), name="proj_qkv")
    small = _mm(h1, w_small_t, SMALL_W, out_dtype=F32, tm=512, tn=SMALL_W // 2, w_rows=True, name="proj_small")

    tabs64 = _rope_tables(s, DSA_ROPE_DIM)
    tabs128 = _rope_tables(s, IDX_DIM)
    cq_n, kv_cat, k_idx, w_idx, fa = _prep(small, g_q_lat, g_kv_lat, g_idx_k, b_idx_k, tabs64, tabs128, s)

    cum = _fox_cum(fa, _pad_cols(b_f.reshape(1, -1), LANES), bsz, s)
    att_a = _fox_attention(qkv, cum, bsz, s)

    w_uq3 = w_uq.reshape(Q_LORA, DSA_HEADS, DSA_NOPE_DIM + DSA_ROPE_DIM)
    w_uq_nope = w_uq3[:, :, :DSA_NOPE_DIM].reshape(Q_LORA, DSA_HEADS * DSA_NOPE_DIM).astype(BF16)
    w_uq_rope = jnp.pad(w_uq3[:, :, DSA_NOPE_DIM:], ((0, 0), (0, 0), (0, LANES - DSA_ROPE_DIM))).reshape(
        Q_LORA, DSA_HEADS * LANES).astype(BF16)
    q_cat = _q_latent(cq_n, w_uq_nope, w_uq_rope, w_uk.astype(BF16), tabs64, bsz, s)
    q_idx = _mm(cq_n, w_idx_q, IDX_HEADS * IDX_DIM, heads=(bsz, s), rope_tabs=tabs128, name="proj_q_idx")
    w_idx_t = jnp.transpose(w_idx.reshape(bsz, s, LANES)[:, :, :IDX_HEADS], (0, 2, 1))
    bias = _indexer_mask(q_idx, k_idx.reshape(bsz, s, IDX_DIM), w_idx_t, bsz, s, topk)
    kv3 = kv_cat.reshape(bsz, s, KV_PAD)
    kv_t = jnp.transpose(kv3.reshape(bsz, s // KEY_CHUNK, KEY_CHUNK, KV_PAD), (0, 1, 3, 2))
    o_lat = _dsa_attention(q_cat, kv_t, kv3, bias, bsz, s)
    att_b = _uv_project(o_lat, w_uv.astype(BF16), bsz, s)

    merged = _gated_merge(h1, att_a, att_b, w_t, o_ga, o_gb, w_up_a, w_up_b)
    return _mm(merged, w_out, d, out_dtype=F32, res=x2d, name="proj_out")


def _memory_layer(x1, mem, bsz, s, g_x, g_m, w_qm, w_km, w_vm, w_om):
    n_mem = mem.shape[1]
    d = x1.shape[1]
    hd = MEM_HEADS * MEM_HEAD_DIM
    mem_n = _rmsnorm(mem.reshape(bsz * n_mem, d), g_m, BF16)
    kv_m = _mm(mem_n, jnp.concatenate([w_km, w_vm], axis=1), 2 * hd, tm=bsz * n_mem, name="proj_kv_mem")
    km_t = jnp.transpose(kv_m[:, :hd].reshape(bsz, n_mem, hd), (0, 2, 1))
    return _mem_attention(x1, g_x, w_qm.astype(BF16), km_t, kv_m, w_om.astype(BF16), bsz, s, n_mem)


def _moe_layer(x2, g_ffn, w_rg, b_rg, w_re, b_re, w_gate, w_up, w_down, g_final):
    n, d = x2.shape
    m = n * TOPK_IN_GROUP
    w_r = _pad_cols(jnp.concatenate([w_re, w_rg], axis=1), LANES)
    b_r = _pad_cols(jnp.concatenate([b_re, b_rg]).reshape(1, -1), LANES)
    h3p, ridx, wts, cnt = _router(x2, g_ffn, w_r, b_r)
    counts = cnt[0, :N_EXPERTS].astype(jnp.int32)
    e_flat = ridx[:, :TOPK_IN_GROUP].reshape(m)
    rank_flat = ridx[:, TOPK_IN_GROUP:2 * TOPK_IN_GROUP].reshape(m)
    seg = (counts + SUBLANES - 1) // SUBLANES * SUBLANES
    starts = (jnp.cumsum(seg) - seg).astype(jnp.int32)
    n_it = (counts + ITEM_ROWS - 1) // ITEM_ROWS
    it_end = jnp.cumsum(n_it)
    it_first = it_end - n_it
    max_items = (m + N_EXPERTS * (ITEM_ROWS - 1)) // ITEM_ROWS
    t = jnp.arange(max_items, dtype=jnp.int32)
    item_e = jnp.minimum(jnp.sum(it_end[None, :] <= t[:, None], axis=1), N_EXPERTS - 1).astype(jnp.int32)
    k_in = t - it_first[item_e]
    item_start = (starts[item_e] + k_in * ITEM_ROWS).astype(jnp.int32)
    item_n = jnp.clip(counts[item_e] - k_in * ITEM_ROWS, 0, ITEM_ROWS).astype(jnp.int32)
    n_items = it_end[-1:].astype(jnp.int32)
    last = jnp.maximum(n_items[0] - 1, 0)
    item_e = jnp.where(t < n_items[0], item_e, item_e[last])
    item_start = jnp.where(t < n_items[0], item_start, 0)
    item_n = jnp.where(t < n_items[0], item_n, 0)
    e_ids = jnp.arange(N_EXPERTS, dtype=jnp.int32)
    dest = rank_flat + jnp.sum(jnp.where(e_flat[:, None] == e_ids[None, :], starts[None, :], 0), axis=1)
    x_sorted = _dispatch(dest, starts, counts, h3p)
    y_sorted = _experts(item_e, item_start, item_n, n_items, x_sorted, w_gate, w_up, w_down, max_items, m)
    return _combine(dest, y_sorted, x2, wts, g_final)


def kernel(x, mem, g_norm_mix, w_in, b_f, g_q_lat, g_kv_lat, g_idx_k, b_idx_k, w_uq, w_idx_q, w_uk, w_uv,
           w_up_a, w_up_b, w_out, g_norm_mem_x, g_mem, w_qm, w_km, w_vm, w_om, g_norm_ffn,
           w_rg, b_rg, w_re, b_re, w_gate, w_up, w_down, g_final):
    bsz, s, d = x.shape
    depth = w_in.shape[0]
    assert depth == 1, "the final norm is fused into the last MoE combine"
    l = 0
    x2d = x.reshape(bsz * s, d)
    x1 = _mixer(x2d, bsz, s, g_norm_mix[l], w_in[l], b_f[l], g_q_lat[l], g_kv_lat[l], g_idx_k[l], b_idx_k[l],
                w_uq[l], w_idx_q[l], w_uk[l], w_uv[l], w_up_a[l], w_up_b[l], w_out[l])
    x2 = _memory_layer(x1, mem, bsz, s, g_norm_mem_x[l], g_mem[l], w_qm[l], w_km[l], w_vm[l], w_om[l])
    out = _moe_layer(x2, g_norm_ffn[l], w_rg[l], b_rg[l], w_re[l], b_re[l], w_gate[l], w_up[l], w_down[l], g_final)
    return out.reshape(bsz, s, d)
```

```python
import functools

import jax
import jax.numpy as jnp
from jax import lax
from jax.experimental import pallas as pl
from jax.experimental.pallas import tpu as pltpu

F32 = jnp.float32
BF16 = jnp.bfloat16

EPS = 1e-6
ROPE_THETA = 10000.0
FOX_HEADS = 16
FOX_HEAD_DIM = 128
DSA_HEADS = 16
DSA_NOPE_DIM = 128
DSA_ROPE_DIM = 64
DSA_V_DIM = 128
Q_LORA = 1024
KV_LORA = 256
IDX_HEADS = 32
IDX_DIM = 128
DSA_TOPK_MAX = 256
MEM_HEADS = 4
MEM_HEAD_DIM = 128
N_GROUPS = 8
EXPERTS_PER_GROUP = 8
N_EXPERTS = N_GROUPS * EXPERTS_PER_GROUP
TOPK_IN_GROUP = 2
D_EXPERT = 512

FOX_WIDTH = FOX_HEADS * FOX_HEAD_DIM
LANES = 128
SUBLANES = 8
KV_PAD = KV_LORA + LANES
NEG_BIG = -1e30
LOG2E = 1.4426950408889634
FOX_QSCALE = FOX_HEAD_DIM ** -0.5 * LOG2E
FOX_HB = 4
DSA_QSCALE = (DSA_NOPE_DIM + DSA_ROPE_DIM) ** -0.5 * LOG2E
VMEM_LIMIT_BYTES = 56 * 1024 * 1024
ITEM_ROWS = 384
ITEM_SUB = 128
F_CHUNK = 256
KEY_CHUNK = 256


def _params(*sem):
    return pltpu.CompilerParams(dimension_semantics=sem, vmem_limit_bytes=VMEM_LIMIT_BYTES)


def _rmsnorm_kernel(x_ref, g_ref, o_ref):
    x = x_ref[...].astype(F32)
    ms = jnp.mean(x * x, axis=-1, keepdims=True)
    o_ref[...] = (x * lax.rsqrt(ms + EPS) * g_ref[...]).astype(o_ref.dtype)


def _rmsnorm(x2d, g, out_dtype, tm=256):
    n, d = x2d.shape
    return pl.pallas_call(
        _rmsnorm_kernel,
        grid=(n // tm,),
        in_specs=[pl.BlockSpec((tm, d), lambda i: (i, 0)), pl.BlockSpec((1, d), lambda i: (0, 0))],
        out_specs=pl.BlockSpec((tm, d), lambda i: (i, 0)),
        out_shape=jax.ShapeDtypeStruct((n, d), out_dtype),
        compiler_params=_params("arbitrary"),
        name="rmsnorm",
    )(x2d, g.reshape(1, d).astype(F32))


_NT_DIMS = (((1,), (1,)), ((), ()))


def _w_rows_spec(tn, k, row0, index_of_j):
    if row0 % tn == 0:
        return pl.BlockSpec((tn, k), lambda *g: (index_of_j(*g) + row0 // tn, 0))
    assert row0 % SUBLANES == 0
    return pl.BlockSpec((pl.Element(tn), pl.Element(k)),
                        lambda *g: (pl.multiple_of(index_of_j(*g) * tn + row0, SUBLANES), 0))


def _mm_kernel(a_ref, b_ref, *rest, has_res, head_out, rope, w_rows, col_scale):
    pos = 0
    res_ref = cos_ref = sin_ref = None
    if has_res:
        res_ref = rest[pos]
        pos += 1
    if rope:
        cos_ref, sin_ref = rest[pos], rest[pos + 1]
        pos += 2
    o_ref = rest[pos]
    if b_ref.dtype == BF16:
        wb_ref = b_ref
    else:
        wb_ref = rest[pos + 1]

        @pl.when(pl.program_id(1) == 0)
        def _():
            wb_ref[...] = b_ref[...].astype(BF16)

    if w_rows:
        acc = lax.dot_general(a_ref[...], wb_ref[...], _NT_DIMS, preferred_element_type=F32)
    else:
        acc = jnp.dot(a_ref[...], wb_ref[...], preferred_element_type=F32)
    if col_scale is not None:
        acc = acc * jnp.where(pl.program_id(0) < col_scale[0], col_scale[1], 1.0)
    if has_res:
        acc = acc + res_ref[...]
    if head_out:
        for hh in range(acc.shape[1] // LANES):
            part = acc[:, hh * LANES:(hh + 1) * LANES]
            if rope:
                part = part * cos_ref[...] + pltpu.roll(part, LANES // 2, 1) * sin_ref[...]
            o_ref[hh] = part.astype(o_ref.dtype)
    else:
        o_ref[...] = acc.astype(o_ref.dtype)


def _mm(a, w, n_cols, *, col0=0, out_dtype=BF16, tm=1024, tn=512, res=None, heads=None, rope_tabs=None,
        w_rows=False, scaled_cols=None, name="mm"):
    m, k = a.shape
    tm = min(tm, m if heads is None else heads[1])
    assert n_cols % tn == 0 and m % tm == 0
    if w_rows:
        w_spec = _w_rows_spec(tn, k, col0, lambda j, i: j)
    else:
        assert col0 % tn == 0
        w_spec = pl.BlockSpec((k, tn), lambda j, i: (0, j + col0 // tn))
    in_specs = [pl.BlockSpec((tm, k), lambda j, i: (i, 0)), w_spec]
    args = [a, w]
    if res is not None:
        in_specs.append(pl.BlockSpec((tm, tn), lambda j, i: (i, j)))
        args.append(res)
    if heads is not None:
        bsz, s = heads
        assert s % tm == 0
        spt = s // tm
        if rope_tabs is not None:
            in_specs += [pl.BlockSpec((tm, LANES), lambda j, i: (i % spt, 0))] * 2
            args += list(rope_tabs)
        hpt = tn // LANES
        out_spec = pl.BlockSpec((None, hpt, tm, LANES), lambda j, i: (i // spt, j, i % spt, 0))
        out_shape = jax.ShapeDtypeStruct((bsz, n_cols // LANES, s, LANES), out_dtype)
    else:
        out_spec = pl.BlockSpec((tm, tn), lambda j, i: (i, j))
        out_shape = jax.ShapeDtypeStruct((m, n_cols), out_dtype)
    col_scale = None
    if scaled_cols is not None:
        assert scaled_cols[0] % tn == 0
        col_scale = (scaled_cols[0] // tn, scaled_cols[1])
    kern = functools.partial(_mm_kernel, has_res=res is not None, head_out=heads is not None,
                             rope=rope_tabs is not None, w_rows=w_rows, col_scale=col_scale)
    return pl.pallas_call(
        kern,
        grid=(n_cols // tn, m // tm),
        in_specs=in_specs,
        out_specs=out_spec,
        out_shape=out_shape,
        scratch_shapes=[] if w.dtype == BF16 else [pltpu.VMEM((tn, k) if w_rows else (k, tn), BF16)],
        compiler_params=_params("arbitrary", "arbitrary"),
        name=name,
    )(*args)


def _rope_tables(s, d):
    inv = jnp.power(ROPE_THETA, -jnp.arange(0, d, 2, dtype=F32) / d)
    ang = jnp.arange(s, dtype=jnp.int32).astype(F32)[:, None] * inv[None, :]
    cos, sin = jnp.cos(ang), jnp.sin(ang)
    cos_t = jnp.tile(jnp.concatenate([cos, cos], axis=-1), (1, LANES // d))
    sin_t = jnp.tile(jnp.concatenate([-sin, sin], axis=-1), (1, LANES // d))
    return cos_t, sin_t


def _rot_half64(x):
    lane = lax.broadcasted_iota(jnp.int32, x.shape, 1)
    return jnp.where((lane % 64) < 32, pltpu.roll(x, 96, 1), pltpu.roll(x, 32, 1))


_C_CQ = 0
_C_CKV = Q_LORA
_C_IK = _C_CKV + KV_LORA
_C_KR = _C_IK + IDX_DIM
_C_FA = _C_KR + LANES
_C_IW = _C_FA + LANES
SMALL_W = _C_IW + LANES


def _prep_kernel(s_ref, gq_ref, gkv_ref, gik_ref, bik_ref, c64_ref, s64_ref, c128_ref, s128_ref,
                 cq_ref, kv_ref, kidx_ref, widx_ref, fa_ref):
    cq = s_ref[:, _C_CQ:_C_CQ + Q_LORA]
    cq_ref[...] = (cq * lax.rsqrt(jnp.mean(cq * cq, axis=-1, keepdims=True) + EPS) * gq_ref[...]).astype(BF16)
    ckv = s_ref[:, _C_CKV:_C_CKV + KV_LORA]
    kv_ref[:, :KV_LORA] = (ckv * lax.rsqrt(jnp.mean(ckv * ckv, axis=-1, keepdims=True) + EPS)
                           * gkv_ref[...]).astype(BF16)
    kr = s_ref[:, _C_KR:_C_KR + LANES]
    kv_ref[:, KV_LORA:] = (kr * c64_ref[...] + _rot_half64(kr) * s64_ref[...]).astype(BF16)
    ik = s_ref[:, _C_IK:_C_IK + IDX_DIM]
    mu = jnp.mean(ik, axis=-1, keepdims=True)
    xc = ik - mu
    ikn = xc * lax.rsqrt(jnp.mean(xc * xc, axis=-1, keepdims=True) + EPS) * gik_ref[...] + bik_ref[...]
    kidx_ref[...] = (ikn * c128_ref[...] + pltpu.roll(ikn, LANES // 2, 1) * s128_ref[...]).astype(BF16)
    widx_ref[...] = s_ref[:, _C_IW:_C_IW + LANES] * (IDX_HEADS * IDX_DIM) ** -0.5
    fa_ref[...] = s_ref[:, _C_FA:_C_FA + LANES]


def _prep(small, g_q, g_kv, g_ik, b_ik, tabs64, tabs128, s, tm=256):
    n = small.shape[0]
    spt = s // tm
    row = lambda w: pl.BlockSpec((tm, w), lambda i: (i, 0))
    vec = lambda w: pl.BlockSpec((1, w), lambda i: (0, 0))
    tab = pl.BlockSpec((tm, LANES), lambda i: (i % spt, 0))
    return pl.pallas_call(
        _prep_kernel,
        grid=(n // tm,),
        in_specs=[row(SMALL_W), vec(Q_LORA), vec(KV_LORA), vec(IDX_DIM), vec(IDX_DIM), tab, tab, tab, tab],
        out_specs=[row(Q_LORA), row(KV_PAD), row(IDX_DIM), row(LANES), row(LANES)],
        out_shape=[jax.ShapeDtypeStruct((n, Q_LORA), BF16), jax.ShapeDtypeStruct((n, KV_PAD), BF16),
                   jax.ShapeDtypeStruct((n, IDX_DIM), BF16), jax.ShapeDtypeStruct((n, LANES), F32),
                   jax.ShapeDtypeStruct((n, LANES), F32)],
        compiler_params=_params("arbitrary"),
        name="mixer_prep",
    )(small, g_q.reshape(1, -1), g_kv.reshape(1, -1), g_ik.reshape(1, -1), b_ik.reshape(1, -1),
      tabs64[0], tabs64[1], tabs128[0], tabs128[1])


def _cum_kernel(fa_ref, bf_ref, cum_ref, carry_ref):
    @pl.when(pl.program_id(1) == 0)
    def _():
        carry_ref[...] = jnp.zeros_like(carry_ref)

    z = fa_ref[...] + bf_ref[...]
    logf = jnp.minimum(z, 0.0) - jnp.log1p(jnp.exp(-jnp.abs(z)))
    t = z.shape[0]
    tri = (lax.broadcasted_iota(jnp.int32, (t, t), 0) >= lax.broadcasted_iota(jnp.int32, (t, t), 1)).astype(F32)
    c = jnp.dot(tri, logf, precision=lax.Precision.HIGHEST, preferred_element_type=F32) + carry_ref[...]
    cum_ref[...] = c
    carry_ref[...] = c[t - 1:t, :]


def _fox_cum(fa, b_f_pad, bsz, s, t=256):
    return pl.pallas_call(
        _cum_kernel,
        grid=(bsz, s // t),
        in_specs=[pl.BlockSpec((None, t, LANES), lambda b, i: (b, i, 0)),
                  pl.BlockSpec((1, LANES), lambda b, i: (0, 0))],
        out_specs=pl.BlockSpec((None, t, LANES), lambda b, i: (b, i, 0)),
        out_shape=jax.ShapeDtypeStruct((bsz, s, LANES), F32),
        scratch_shapes=[pltpu.VMEM((1, LANES), F32)],
        compiler_params=_params("arbitrary", "arbitrary"),
        name="fox_cumsum",
    )(fa.reshape(bsz, s, LANES), b_f_pad)


def _lane_fold(x, op):
    out = x[:, :LANES]
    for c in range(1, x.shape[1] // LANES):
        out = op(out, x[:, c * LANES:(c + 1) * LANES])
    return out


def _split3(x):
    t1 = x.astype(BF16)
    r = x - t1.astype(F32)
    t2 = r.astype(BF16)
    t3 = (r - t2.astype(F32)).astype(BF16)
    return t1, t2, t3


def _gate_lanes(c, own_first):
    t1, t2, t3 = (t.astype(F32) for t in _split3(c))
    lane = lax.broadcasted_iota(jnp.int32, c.shape, 1)
    a, b = (0, 3) if own_first else (3, 0)
    ones3 = jnp.where((lane >= b) & (lane < b + 3), 1.0, 0.0)
    out = jnp.where(lane == a, t1, jnp.where(lane == a + 1, t2, jnp.where(lane == a + 2, t3, ones3)))
    return out.astype(BF16)


def _fox_kernel(q_ref, k_ref, v_ref, cum_ref, o_ref, kf_ref, vf_ref, s_ref, *, t):
    hp = pl.program_id(1)
    i = pl.program_id(2)
    d = FOX_HEAD_DIM
    heads = range(FOX_HB)
    row = lax.broadcasted_iota(jnp.int32, (LANES, LANES), 0)

    def gate_column(c, hh):
        pick = (row == hp * FOX_HB + hh).astype(F32)
        return jnp.dot(c, pick, precision=lax.Precision.HIGHEST, preferred_element_type=F32) * LOG2E

    @pl.when(i == 0)
    def _():
        for hh in heads:
            kf_ref[hh, :, :d] = k_ref[:, hh * d:(hh + 1) * d]
            kf_ref[hh, :, d:] = _gate_lanes(-gate_column(cum_ref[...], hh), True)
            vf_ref[hh, :, :d] = v_ref[:, hh * d:(hh + 1) * d]
            vf_ref[hh, :, d:] = jnp.ones((vf_ref.shape[1], LANES), BF16)

    cum_q = cum_ref[pl.ds(pl.multiple_of(i * t, t), t), :]
    qs = [jnp.concatenate([q_ref[:, hh * d:(hh + 1) * d], _gate_lanes(gate_column(cum_q, hh), False)], axis=1)
          for hh in heads]

    def logits(hh, kc):
        kb = kf_ref[hh, pl.ds(pl.multiple_of(kc * t, t), t), :]
        return lax.dot_general(qs[hh], kb, _NT_DIMS, preferred_element_type=F32)

    def pass1(kc, m_lanes):
        out = []
        for hh in heads:
            s = logits(hh, kc)
            s_ref[hh, kc] = s
            out.append(jnp.maximum(m_lanes[hh], _lane_fold(s, jnp.maximum)))
        return tuple(out)

    m_lanes = lax.fori_loop(0, i, pass1, tuple(jnp.full((t, LANES), NEG_BIG, F32) for _ in heads))
    causal = lax.broadcasted_iota(jnp.int32, (t, t), 0) >= lax.broadcasted_iota(jnp.int32, (t, t), 1)
    ms = []
    for hh in heads:
        s = jnp.where(causal, logits(hh, i), NEG_BIG)
        s_ref[hh, i] = s
        m_lane = jnp.maximum(m_lanes[hh], _lane_fold(s, jnp.maximum))
        m = jnp.broadcast_to(jnp.max(m_lane, axis=1, keepdims=True), (t, LANES))
        ms.append(jnp.concatenate([m] * (t // LANES), axis=1))

    def pass2(kc, accs):
        out = []
        for hh in heads:
            p = jnp.exp2(s_ref[hh, kc] - ms[hh])
            vb = vf_ref[hh, pl.ds(pl.multiple_of(kc * t, t), t), :]
            out.append(accs[hh] + jnp.dot(p.astype(BF16), vb, preferred_element_type=F32))
        return tuple(out)

    accs = lax.fori_loop(0, i + 1, pass2, tuple(jnp.zeros((t, d + LANES), F32) for _ in heads))
    for hh in heads:
        o_ref[:, hh * d:(hh + 1) * d] = (accs[hh][:, :d] / accs[hh][:, d:]).astype(o_ref.dtype)


def _fox_attention(qkv, cum, bsz, s, t=512):
    n = bsz * s
    nq = s // t
    hp = FOX_HEADS // FOX_HB
    w = FOX_HB * LANES
    assert FOX_HEAD_DIM == LANES and FOX_HEADS % FOX_HB == 0
    kern = functools.partial(_fox_kernel, t=t)
    return pl.pallas_call(
        kern,
        grid=(bsz, hp, nq),
        in_specs=[pl.BlockSpec((t, w), lambda b, h, i: (b * nq + i, h)),
                  pl.BlockSpec((s, w), lambda b, h, i: (b, hp + h)),
                  pl.BlockSpec((s, w), lambda b, h, i: (b, 2 * hp + h)),
                  pl.BlockSpec((None, s, LANES), lambda b, h, i: (b, 0, 0))],
        out_specs=pl.BlockSpec((t, w), lambda b, h, i: (b * nq + i, h)),
        out_shape=jax.ShapeDtypeStruct((n, FOX_WIDTH), BF16),
        scratch_shapes=[pltpu.VMEM((FOX_HB, s, 2 * LANES), BF16), pltpu.VMEM((FOX_HB, s, 2 * LANES), BF16),
                        pltpu.VMEM((FOX_HB, nq, t, t), F32)],
        compiler_params=_params("arbitrary", "arbitrary", "arbitrary"),
        name="fox_attention",
    )(qkv, qkv, qkv, cum)


def _qlat_kernel(cq_ref, wn_ref, wr_ref, wuk_ref, c64_ref, s64_ref, o_ref):
    cq = cq_ref[...]
    qn = jnp.dot(cq, wn_ref[...], preferred_element_type=F32).astype(BF16)
    qr = jnp.dot(cq, wr_ref[...], preferred_element_type=F32)
    for h in range(DSA_HEADS):
        cols = slice(h * LANES, (h + 1) * LANES)
        lat = jnp.dot(qn[:, cols], wuk_ref[h], preferred_element_type=F32)
        o_ref[h, :, :KV_LORA] = (lat * DSA_QSCALE).astype(BF16)
        r = qr[:, cols]
        o_ref[h, :, KV_LORA:] = ((r * c64_ref[...] + _rot_half64(r) * s64_ref[...]) * DSA_QSCALE).astype(BF16)


def _q_latent(cq_n, w_nope_b, w_rope_b, w_uk_b, tabs64, bsz, s, tm=256):
    spt = s // tm
    kq = cq_n.shape[1]
    wide = DSA_HEADS * LANES
    return pl.pallas_call(
        _qlat_kernel,
        grid=(bsz * spt,),
        in_specs=[pl.BlockSpec((tm, kq), lambda i: (i, 0)),
                  pl.BlockSpec((kq, wide), lambda i: (0, 0)),
                  pl.BlockSpec((kq, wide), lambda i: (0, 0)),
                  pl.BlockSpec((DSA_HEADS, DSA_NOPE_DIM, KV_LORA), lambda i: (0, 0, 0)),
                  pl.BlockSpec((tm, LANES), lambda i: (i % spt, 0)),
                  pl.BlockSpec((tm, LANES), lambda i: (i % spt, 0))],
        out_specs=pl.BlockSpec((None, DSA_HEADS, tm, KV_PAD), lambda i: (i // spt, 0, i % spt, 0)),
        out_shape=jax.ShapeDtypeStruct((bsz, DSA_HEADS, s, KV_PAD), BF16),
        compiler_params=_params("arbitrary"),
        name="q_latent",
    )(cq_n, w_nope_b, w_rope_b, w_uk_b, tabs64[0], tabs64[1])


def _needed_chunks(i, tq):
    return ((i + 1) * tq + KEY_CHUNK - 1) // KEY_CHUNK


BISECT_MAX_STEPS = 320
BISECT_UNROLL = 4


def _indexer_kernel(q_ref, k_ref, wt_ref, o_ref, sc_ref, thr_ref, cut_ref, *, tq, topk, hg):
    i = pl.program_id(1)
    tk = KEY_CHUNK
    n_chunks = sc_ref.shape[0]
    nk = _needed_chunks(i, tq)
    qpos = i * tq + lax.broadcasted_iota(jnp.int32, (tk, tq), 1)
    krow = lax.broadcasted_iota(jnp.int32, (tk, tq), 0)
    inf = jnp.inf

    def score_chunk(kc, c):
        k0 = pl.multiple_of(kc * tk, tk)
        kb = k_ref[pl.ds(k0, tk), :]
        sc = jnp.zeros((tk, tq), F32)
        for g in range(IDX_HEADS // hg):
            q2 = q_ref[g * hg:(g + 1) * hg].reshape(hg * tq, IDX_DIM)
            d = lax.dot_general(kb, q2, _NT_DIMS, preferred_element_type=F32)
            for hh in range(hg):
                h = g * hg + hh
                sc = sc + wt_ref[h:h + 1, :] * jnp.maximum(d[:, hh * tq:(hh + 1) * tq], 0.0)
        sc_ref[kc] = jnp.where(k0 + krow <= qpos, sc, -inf)
        return c

    lax.fori_loop(0, nk, score_chunk, 0)

    def pad_chunk(kc, c):
        sc_ref[kc] = jnp.full((tk, tq), -inf, F32)
        return c

    lax.fori_loop(nk, n_chunks, pad_chunk, 0)
    kf = float(topk)
    n_causal = (i * tq + lax.broadcasted_iota(jnp.int32, (1, tq), 1) + 1).astype(F32)

    def fold(x, op):
        return op(x.reshape(tk // SUBLANES, SUBLANES, tq), axis=0)

    def key_pos(kc):
        return (kc * tk + krow).astype(F32)

    def threshold(n_counted):
        def count_ge(t):
            acc = jnp.zeros((SUBLANES, tq), F32)
            for kc in range(n_counted):
                acc = acc + fold(jnp.where(sc_ref[kc] >= t, 1.0, 0.0), jnp.sum)
            return jnp.sum(acc, axis=0, keepdims=True)

        hi = jnp.full((SUBLANES, tq), -inf, F32)
        lo = jnp.full((SUBLANES, tq), inf, F32)
        for kc in range(n_counted):
            s = sc_ref[kc]
            hi = jnp.maximum(hi, fold(s, jnp.max))
            lo = jnp.minimum(lo, fold(jnp.where(s == -inf, inf, s), jnp.min))
        hi = jnp.max(hi, axis=0, keepdims=True)
        lo = jnp.min(lo, axis=0, keepdims=True)
        few = n_causal <= kf
        n_max = count_ge(hi)
        at_max = n_max >= kf
        lo = jnp.where(few, -inf, jnp.where(at_max, hi, lo))
        n_lo = jnp.where(at_max, n_max, n_causal)
        done = jnp.where(few | at_max, 1.0, 0.0)

        def unsettled(state):
            it, _, _, _, done = state
            return (it < BISECT_MAX_STEPS) & (jnp.sum(1.0 - done) > 0.0)

        def step(state):
            it, lo, hi, n_lo, done = state
            for _ in range(BISECT_UNROLL):
                mid = 0.5 * lo + 0.5 * hi
                cnt = count_ge(mid)
                ge = cnt >= kf
                closed = (mid <= lo) | (mid >= hi)
                live = done == 0.0
                lo = jnp.where(live & ge, mid, lo)
                n_lo = jnp.where(live & ge, cnt, n_lo)
                hi = jnp.where(live & jnp.logical_not(ge), mid, hi)
                done = jnp.where(live & ((ge & (cnt == kf)) | closed), 1.0, done)
            return it + BISECT_UNROLL, lo, hi, n_lo, done

        _, lo, _, n_lo, _ = lax.while_loop(unsettled, step, (jnp.int32(0), lo, hi, n_lo, done))
        thr_ref[...] = lo
        cut_ref[...] = jnp.full((1, tq), float(n_chunks * tk), F32)

        over = jnp.where((n_lo > kf) & jnp.logical_not(few), 1.0, 0.0)

        @pl.when(jnp.sum(over) > 0.0)
        def _():
            def count_where(pred):
                def body(kc, acc):
                    hit = pred(sc_ref[kc], key_pos(kc))
                    return acc + fold(jnp.where(hit, 1.0, 0.0), jnp.sum)

                acc = lax.fori_loop(0, n_counted, body, jnp.zeros((SUBLANES, tq), F32))
                return jnp.sum(acc, axis=0, keepdims=True)

            need = kf - count_where(lambda s, kp: s > lo)

            def narrow(_, bracket):
                lo_p, hi_p = bracket
                mid = jnp.floor(0.5 * (lo_p + hi_p))
                ok = count_where(lambda s, kp: (s == lo) & (kp <= mid)) >= need
                return jnp.where(ok, lo_p, mid), jnp.where(ok, mid, hi_p)

            n_pos = n_counted * tk
            _, hi_p = lax.fori_loop(0, n_pos.bit_length(), narrow,
                                    (jnp.full((1, tq), -1.0, F32), jnp.full((1, tq), n_pos - 1.0, F32)))
            cut_ref[...] = jnp.where(over > 0.0, hi_p, cut_ref[...])

    half = max(n_chunks // 2, 1)
    if half < n_chunks:
        pl.when(nk <= half)(lambda: threshold(half))
        pl.when(nk > half)(lambda: threshold(n_chunks))
    else:
        threshold(n_chunks)
    thr = thr_ref[...]
    cut = cut_ref[...]

    def write_chunk(kc, c):
        s = sc_ref[kc]
        keep = ((s > thr) | ((s == thr) & (key_pos(kc) <= cut))) & (kc * tk + krow <= qpos)
        o_ref[kc] = jnp.where(keep, 0.0, NEG_BIG).T.astype(o_ref.dtype)
        return c

    lax.fori_loop(0, nk, write_chunk, 0)

    def fill_chunk(kc, c):
        o_ref[kc] = jnp.full((tq, tk), NEG_BIG, o_ref.dtype)
        return c

    lax.fori_loop(nk, n_chunks, fill_chunk, 0)


def _indexer_mask(q_idx, k_idx, w_idx_t, bsz, s, topk, tq=128, hg=4):
    kern = functools.partial(_indexer_kernel, tq=tq, topk=topk, hg=hg)
    nc = s // KEY_CHUNK
    return pl.pallas_call(
        kern,
        grid=(bsz, s // tq),
        in_specs=[pl.BlockSpec((None, IDX_HEADS, tq, IDX_DIM), lambda b, i: (b, 0, i, 0)),
                  pl.BlockSpec((None, s, IDX_DIM), lambda b, i: (b, 0, 0)),
                  pl.BlockSpec((None, IDX_HEADS, tq), lambda b, i: (b, 0, i))],
        out_specs=pl.BlockSpec((None, nc, tq, KEY_CHUNK), lambda b, i: (b, 0, i, 0)),
        out_shape=jax.ShapeDtypeStruct((bsz, nc, s, KEY_CHUNK), BF16),
        scratch_shapes=[pltpu.VMEM((nc, KEY_CHUNK, tq), F32), pltpu.VMEM((1, tq), F32),
                        pltpu.VMEM((1, tq), F32)],
        compiler_params=_params("arbitrary", "arbitrary"),
        name="indexer_topk_mask",
    )(q_idx, k_idx, w_idx_t)


def _dsa_kernel(q_ref, kvt_ref, kv_ref, bias_ref, o_ref, s_ref, m_ref, l_ref, acc_ref, *, tq):
    i = pl.program_id(1)
    tk = KEY_CHUNK
    rows = DSA_HEADS * tq
    nk = _needed_chunks(i, tq)
    q2 = q_ref[...].reshape(rows, KV_PAD)
    m_ref[...] = jnp.full(m_ref.shape, NEG_BIG, F32)

    n_pairs = (nk + 1) // 2

    def logits(kc):
        sc = jnp.dot(q2, kvt_ref[kc], preferred_element_type=F32)
        sc = (sc.reshape(DSA_HEADS, tq, tk) + bias_ref[kc].astype(F32)[None]).reshape(rows, tk)
        s_ref[kc] = sc
        return _lane_fold(sc, jnp.maximum)

    def pass1(pr, c):
        m_ref[...] = jnp.maximum(m_ref[...], jnp.maximum(logits(2 * pr), logits(2 * pr + 1)))
        return c

    lax.fori_loop(0, n_pairs, pass1, 0)
    m_ref[...] = jnp.broadcast_to(jnp.max(m_ref[...], axis=1, keepdims=True), m_ref.shape)
    l_ref[...] = jnp.zeros(l_ref.shape, F32)
    acc_ref[...] = jnp.zeros(acc_ref.shape, F32)

    def probs(kc):
        m = m_ref[...]
        p = jnp.concatenate([jnp.exp2(s_ref[kc, :, g * LANES:(g + 1) * LANES] - m)
                             for g in range(tk // LANES)], axis=1)
        kvb = kv_ref[pl.ds(pl.multiple_of(kc * tk, tk), tk), :KV_LORA]
        return _lane_fold(p, jnp.add), jnp.dot(p.astype(BF16), kvb, preferred_element_type=F32)

    def pass2(pr, c):
        l0, a0 = probs(2 * pr)
        l1, a1 = probs(2 * pr + 1)
        l_ref[...] += l0 + l1
        acc_ref[...] += a0 + a1
        return c

    lax.fori_loop(0, n_pairs, pass2, 0)
    l = jnp.dot(l_ref[...], jnp.ones((LANES, LANES), F32), precision=lax.Precision.HIGHEST,
                preferred_element_type=F32)
    o = acc_ref[...] / jnp.concatenate([l] * (KV_LORA // LANES), axis=1)
    o_ref[...] = o.reshape(DSA_HEADS, tq, KV_LORA).astype(o_ref.dtype)


def _dsa_attention(q_cat, kv_t, kv, bias, bsz, s, tq=128):
    kern = functools.partial(_dsa_kernel, tq=tq)
    nc = s // KEY_CHUNK
    assert nc % 2 == 0, "the kernel walks key chunks in pairs"
    rows = DSA_HEADS * tq
    return pl.pallas_call(
        kern,
        grid=(bsz, s // tq),
        in_specs=[pl.BlockSpec((None, DSA_HEADS, tq, KV_PAD), lambda b, i: (b, 0, i, 0)),
                  pl.BlockSpec((None, nc, KV_PAD, KEY_CHUNK), lambda b, i: (b, 0, 0, 0)),
                  pl.BlockSpec((None, s, KV_PAD), lambda b, i: (b, 0, 0)),
                  pl.BlockSpec((None, nc, tq, KEY_CHUNK), lambda b, i: (b, 0, i, 0))],
        out_specs=pl.BlockSpec((None, DSA_HEADS, tq, KV_LORA), lambda b, i: (b, 0, i, 0)),
        out_shape=jax.ShapeDtypeStruct((bsz, DSA_HEADS, s, KV_LORA), BF16),
        scratch_shapes=[pltpu.VMEM((nc, rows, KEY_CHUNK), F32), pltpu.VMEM((rows, LANES), F32),
                        pltpu.VMEM((rows, LANES), F32), pltpu.VMEM((rows, KV_LORA), F32)],
        compiler_params=_params("arbitrary", "arbitrary"),
        name="dsa_attention",
    )(q_cat, kv_t, kv, bias)


def _uv_kernel(o_ref, w_ref, y_ref):
    for h in range(DSA_HEADS):
        y_ref[:, h * DSA_V_DIM:(h + 1) * DSA_V_DIM] = jnp.dot(
            o_ref[h], w_ref[h], preferred_element_type=F32).astype(y_ref.dtype)


def _uv_project(o_lat, w_uv_b, bsz, s, tm=512):
    spt = s // tm
    return pl.pallas_call(
        _uv_kernel,
        grid=(bsz * spt,),
        in_specs=[pl.BlockSpec((None, DSA_HEADS, tm, KV_LORA), lambda i: (i // spt, 0, i % spt, 0)),
                  pl.BlockSpec((DSA_HEADS, KV_LORA, DSA_V_DIM), lambda i: (0, 0, 0))],
        out_specs=pl.BlockSpec((tm, DSA_HEADS * DSA_V_DIM), lambda i: (i, 0)),
        out_shape=jax.ShapeDtypeStruct((bsz * s, DSA_HEADS * DSA_V_DIM), BF16),
        compiler_params=_params("arbitrary"),
        name="uv_project",
    )(o_lat, w_uv_b)


def _merge_kernel(h_ref, a_ref, b_ref, wga_ref, wgb_ref, wua_ref, wub_ref, o_ref,
                  cga_ref, cgb_ref, cua_ref, cub_ref):
    @pl.when(pl.program_id(1) == 0)
    def _():
        cga_ref[...] = wga_ref[...].astype(BF16)
        cgb_ref[...] = wgb_ref[...].astype(BF16)
        cua_ref[...] = wua_ref[...].astype(BF16)
        cub_ref[...] = wub_ref[...].astype(BF16)

    h = h_ref[...]
    ga = lax.dot_general(h, cga_ref[...], _NT_DIMS, preferred_element_type=F32)
    gb = lax.dot_general(h, cgb_ref[...], _NT_DIMS, preferred_element_type=F32)
    ya = jnp.dot(a_ref[...], cua_ref[...], preferred_element_type=F32)
    yb = jnp.dot(b_ref[...], cub_ref[...], preferred_element_type=F32)
    o_ref[...] = (jax.nn.sigmoid(ga) * ya + jax.nn.sigmoid(gb) * yb).astype(o_ref.dtype)


def _gated_merge(h1, att_a, att_b, w_t, row_ga, row_gb, w_up_a, w_up_b, tm=512, tn=256):
    n, d = h1.shape
    ka, kb = att_a.shape[1], att_b.shape[1]
    return pl.pallas_call(
        _merge_kernel,
        grid=(d // tn, n // tm),
        in_specs=[pl.BlockSpec((tm, d), lambda j, i: (i, 0)),
                  pl.BlockSpec((tm, ka), lambda j, i: (i, 0)),
                  pl.BlockSpec((tm, kb), lambda j, i: (i, 0)),
                  _w_rows_spec(tn, d, row_ga, lambda j, i: j),
                  _w_rows_spec(tn, d, row_gb, lambda j, i: j),
                  pl.BlockSpec((ka, tn), lambda j, i: (0, j)),
                  pl.BlockSpec((kb, tn), lambda j, i: (0, j))],
        out_specs=pl.BlockSpec((tm, tn), lambda j, i: (i, j)),
        out_shape=jax.ShapeDtypeStruct((n, d), BF16),
        scratch_shapes=[pltpu.VMEM((tn, d), BF16), pltpu.VMEM((tn, d), BF16),
                        pltpu.VMEM((ka, tn), BF16), pltpu.VMEM((kb, tn), BF16)],
        compiler_params=_params("arbitrary", "arbitrary"),
        name="gated_merge",
    )(h1, att_a, att_b, w_t, w_t, w_up_a, w_up_b)


def _mem_kernel(x_ref, g_ref, wq_ref, kt_ref, v_ref, wom_ref, o_ref):
    x = x_ref[...]
    hn = (x * lax.rsqrt(jnp.mean(x * x, axis=-1, keepdims=True) + EPS) * g_ref[...]).astype(BF16)
    q = jnp.dot(hn, wq_ref[...], preferred_element_type=F32).astype(BF16)
    scale = MEM_HEAD_DIM ** -0.5
    outs = []
    for h in range(MEM_HEADS):
        sl = slice(h * MEM_HEAD_DIM, (h + 1) * MEM_HEAD_DIM)
        s = jnp.dot(q[:, sl], kt_ref[sl, :], preferred_element_type=F32) * scale
        m = jnp.max(s, axis=-1, keepdims=True)
        p = jnp.exp(s - m)
        p = p / jnp.sum(p, axis=-1, keepdims=True)
        outs.append(jnp.dot(p.astype(BF16), v_ref[:, sl], preferred_element_type=F32).astype(BF16))
    o = jnp.concatenate(outs, axis=-1)
    o_ref[...] = x + jnp.dot(o, wom_ref[...], preferred_element_type=F32)


def _mem_attention(x1, g_x, w_qm_b, km_t, kv_m, w_om_b, bsz, s, n_mem, tm=256):
    spt = s // tm
    hd = MEM_HEADS * MEM_HEAD_DIM
    d = x1.shape[1]
    return pl.pallas_call(
        _mem_kernel,
        grid=(bsz, spt),
        in_specs=[pl.BlockSpec((tm, d), lambda b, i: (b * spt + i, 0)),
                  pl.BlockSpec((1, d), lambda b, i: (0, 0)),
                  pl.BlockSpec((d, hd), lambda b, i: (0, 0)),
                  pl.BlockSpec((None, hd, n_mem), lambda b, i: (b, 0, 0)),
                  pl.BlockSpec((n_mem, hd), lambda b, i: (b, 1)),
                  pl.BlockSpec((hd, d), lambda b, i: (0, 0))],
        out_specs=pl.BlockSpec((tm, d), lambda b, i: (b * spt + i, 0)),
        out_shape=jax.ShapeDtypeStruct(x1.shape, F32),
        compiler_params=_params("arbitrary", "arbitrary"),
        name="memory_cross_attention",
    )(x1, g_x.reshape(1, d), w_qm_b, km_t, kv_m, w_om_b)


def _pack_bf16_pairs(x):
    c = x.shape[1] // 2
    return pltpu.pack_elementwise([x[:, :c], x[:, c:]], packed_dtype=BF16)


def _unpack_bf16_pairs(w):
    lo = pltpu.unpack_elementwise(w, index=0, packed_dtype=BF16, unpacked_dtype=F32)
    hi = pltpu.unpack_elementwise(w, index=1, packed_dtype=BF16, unpacked_dtype=F32)
    return lo.astype(BF16), hi.astype(BF16)


def _router_kernel(x_ref, g_ref, wr_ref, br_ref, h_ref, idx_ref, wt_ref, cnt_ref, carry_ref):
    @pl.when(pl.program_id(0) == 0)
    def _():
        carry_ref[...] = jnp.zeros_like(carry_ref)

    x = x_ref[...]
    hn = x * lax.rsqrt(jnp.mean(x * x, axis=-1, keepdims=True) + EPS) * g_ref[...]
    h_ref[...] = _pack_bf16_pairs(hn)
    hn_hi = hn.astype(BF16)
    hn_lo = (hn - hn_hi.astype(F32)).astype(BF16)
    wcat = wr_ref[...]
    a = jnp.dot(hn_hi, wcat, preferred_element_type=F32)
    b = jnp.dot(hn_lo, wcat[:, :LANES], preferred_element_type=F32)
    logits = a[:, :LANES] + (a[:, LANES:] + b) + br_ref[...]
    tm = x.shape[0]
    lane_i = lax.broadcasted_iota(jnp.int32, (tm, LANES), 1)
    lane = lane_i.astype(F32)
    lane_grp = lax.shift_right_logical(lane_i, EXPERTS_PER_GROUP.bit_length() - 1).astype(F32)
    big = 1e9
    is_g = (lane_i >= N_EXPERTS) & (lane_i < N_EXPERTS + N_GROUPS)
    lg = jnp.where(is_g, logits, -jnp.inf)
    mg = jnp.max(lg, axis=1, keepdims=True)
    gate_g = 1.0 / jnp.sum(jnp.exp(lg - mg), axis=1, keepdims=True)
    grp = jnp.min(jnp.where(lg == mg, lane, big), axis=1, keepdims=True) - N_EXPERTS
    in_grp = (lane_i < N_EXPERTS) & (lane_grp == grp)
    le = jnp.where(in_grp, logits, -jnp.inf)
    m1 = jnp.max(le, axis=1, keepdims=True)
    e1 = jnp.min(jnp.where(le == m1, lane, big), axis=1, keepdims=True)
    le2 = jnp.where(lane == e1, -jnp.inf, le)
    m2 = jnp.max(le2, axis=1, keepdims=True)
    e2 = jnp.min(jnp.where(le2 == m2, lane, big), axis=1, keepdims=True)
    r = jnp.exp(m2 - m1)
    w1 = gate_g * (1.0 / (1.0 + r))
    w2 = gate_g * (r / (1.0 + r))
    oh1 = (lane == e1).astype(F32)
    oh2 = (lane == e2).astype(F32)
    oh = oh1 + oh2
    tri = (lax.broadcasted_iota(jnp.int32, (tm, tm), 0) > lax.broadcasted_iota(jnp.int32, (tm, tm), 1)).astype(BF16)
    prefix = jnp.dot(tri, oh.astype(BF16), preferred_element_type=F32) + carry_ref[...]
    rank1 = jnp.sum(prefix * oh1, axis=1, keepdims=True)
    rank2 = jnp.sum(prefix * oh2, axis=1, keepdims=True)
    new_cnt = carry_ref[...] + jnp.sum(oh, axis=0, keepdims=True)
    carry_ref[...] = new_cnt
    cnt_ref[...] = new_cnt
    idx_ref[...] = jnp.where(lane_i == 0, e1, jnp.where(lane_i == 1, e2, jnp.where(
        lane_i == 2, rank1, jnp.where(lane_i == 3, rank2, 0.0)))).astype(jnp.int32)
    wt_ref[...] = jnp.where(lane_i == 0, w1, jnp.where(lane_i == 1, w2, 0.0))


def _router(x2, g, w_r, b_r, tm=256):
    n, d = x2.shape
    row = lambda w: pl.BlockSpec((tm, w), lambda i: (i, 0))
    return pl.pallas_call(
        _router_kernel,
        grid=(n // tm,),
        in_specs=[row(d), pl.BlockSpec((1, d), lambda i: (0, 0)), pl.BlockSpec((d, 2 * LANES), lambda i: (0, 0)),
                  pl.BlockSpec((1, LANES), lambda i: (0, 0))],
        out_specs=[row(d // 2), row(LANES), row(LANES), pl.BlockSpec((1, LANES), lambda i: (0, 0))],
        out_shape=[jax.ShapeDtypeStruct((n, d // 2), jnp.uint32), jax.ShapeDtypeStruct((n, LANES), jnp.int32),
                   jax.ShapeDtypeStruct((n, LANES), F32), jax.ShapeDtypeStruct((1, LANES), F32)],
        scratch_shapes=[pltpu.VMEM((1, LANES), F32)],
        compiler_params=_params("arbitrary"),
        name="moe_router",
    )(x2, g.reshape(1, d), w_r, b_r)


def _row_copy(src_hbm, row, dst, r, sem):
    return pltpu.make_async_copy(src_hbm.at[pl.ds(row, 1), :], dst.at[pl.ds(r, 1), :], sem)


def _sorted_rows(m):
    return m + N_EXPERTS * SUBLANES + ITEM_ROWS


def _dispatch_kernel(dest_ref, seg0_ref, cnt_ref, h_ref, x_hbm, zero_ref, sem, *, tm, m):
    i = pl.program_id(0)

    @pl.when(i == 0)
    def _():
        zero_ref[...] = jnp.zeros_like(zero_ref)
        tail = [pltpu.make_async_copy(zero_ref, x_hbm.at[pl.ds(r0, ITEM_SUB), :], sem)
                for r0 in range(m, _sorted_rows(m), ITEM_SUB)]
        for c in tail:
            c.start()
        for c in tail:
            c.wait()

        def fill_gap(e, total):
            gap = (-cnt_ref[e]) & (SUBLANES - 1)

            def one(r, c):
                _row_copy(zero_ref, 0, x_hbm, seg0_ref[e] + cnt_ref[e] + r, sem).start()
                return c

            lax.fori_loop(0, gap, one, 0)
            return total + gap

        n_gap = lax.fori_loop(0, N_EXPERTS, fill_gap, 0)

        def wait_gap(r, c):
            _row_copy(zero_ref, 0, x_hbm, 0, sem).wait()
            return c

        lax.fori_loop(0, n_gap, wait_gap, 0)

    def issue(r, c):
        t = i * tm + r
        for sl in range(TOPK_IN_GROUP):
            _row_copy(h_ref, r, x_hbm, dest_ref[TOPK_IN_GROUP * t + sl], sem).start()
        return c

    lax.fori_loop(0, tm, issue, 0, unroll=4)
    for _ in range(TOPK_IN_GROUP):
        pltpu.make_async_copy(h_ref, x_hbm.at[pl.ds(0, tm), :], sem).wait()


def _dispatch(dest, seg_start, counts, h3p, tm=256):
    n, c = h3p.shape
    m = dest.shape[0]
    assert (_sorted_rows(m) - m) % ITEM_SUB == 0
    grid_spec = pltpu.PrefetchScalarGridSpec(
        num_scalar_prefetch=3,
        grid=(n // tm,),
        in_specs=[pl.BlockSpec((tm, c), lambda i, d, s0, cn: (i, 0))],
        out_specs=pl.BlockSpec(memory_space=pl.ANY),
        scratch_shapes=[pltpu.VMEM((ITEM_SUB, c), h3p.dtype), pltpu.SemaphoreType.DMA(())],
    )
    return pl.pallas_call(
        functools.partial(_dispatch_kernel, tm=tm, m=m),
        grid_spec=grid_spec,
        out_shape=jax.ShapeDtypeStruct((_sorted_rows(m), c), h3p.dtype),
        compiler_params=_params("arbitrary"),
        name="moe_dispatch",
    )(dest, seg_start, counts, h3p)


def _expert_kernel(ie_ref, is_ref, in_ref, ni_ref, x_hbm, wg_ref, wu_ref, wd_ref, y_hbm,
                   xin_ref, xb_ref, acc_ref, yp_ref, in_sem, out_sem, *, m, nj):
    i = pl.program_id(0)
    j = pl.program_id(1)
    n_items = ni_ref[0]
    valid = i < n_items
    slot = lax.rem(i, 2)
    half = xb_ref.shape[1] // 2
    n_sub = ITEM_ROWS // ITEM_SUB

    def first_row(item):
        return pl.multiple_of(is_ref[item], SUBLANES)

    def blocks(item):
        return (in_ref[item] + ITEM_SUB - 1) // ITEM_SUB

    def each_block(item, fn):
        for sub in range(n_sub):
            pl.when(sub * ITEM_SUB < in_ref[item])(functools.partial(fn, sub))

    def in_copy(item, sub):
        rows = pl.ds(sub * ITEM_SUB, ITEM_SUB)
        return pltpu.make_async_copy(x_hbm.at[pl.ds(first_row(item) + sub * ITEM_SUB, ITEM_SUB), :],
                                     xin_ref.at[rows, :], in_sem)

    def out_copy(item, sl, sub):
        rows = pl.ds(sub * ITEM_SUB, ITEM_SUB)
        return pltpu.make_async_copy(yp_ref.at[sl, rows, :],
                                     y_hbm.at[pl.ds(first_row(item) + sub * ITEM_SUB, ITEM_SUB), :], out_sem.at[sl])

    def by_block_count(item, fn):
        for nb in range(1, n_sub + 1):
            pl.when(blocks(item) == nb)(functools.partial(fn, nb))

    @pl.when((i == 0) & (j == 0))
    def _():
        yp_ref[1, :ITEM_SUB, :] = jnp.zeros((ITEM_SUB, yp_ref.shape[2]), yp_ref.dtype)
        tail = [pltpu.make_async_copy(yp_ref.at[1, pl.ds(0, ITEM_SUB), :],
                                      y_hbm.at[pl.ds(r0, ITEM_SUB), :], out_sem.at[1])
                for r0 in range(m, _sorted_rows(m), ITEM_SUB)]
        for c in tail:
            c.start()
        for c in tail:
            c.wait()
        each_block(0, lambda sub: in_copy(0, sub).start())

    @pl.when(valid & (j == 0))
    def _():
        each_block(i, lambda sub: in_copy(i, sub).wait())

        def unpack(nb):
            rows = pl.ds(0, nb * ITEM_SUB)
            lo, hi = _unpack_bf16_pairs(xin_ref[rows, :])
            xb_ref[rows, :half] = lo
            xb_ref[rows, half:] = hi
            acc_ref[rows, :] = jnp.zeros((nb * ITEM_SUB, acc_ref.shape[1]), F32)

        by_block_count(i, unpack)

        @pl.when(i + 1 < n_items)
        def _():
            each_block(i + 1, lambda sub: in_copy(i + 1, sub).start())

    @pl.when(valid)
    def _():
        wg = wg_ref[...].astype(BF16)
        wu = wu_ref[...].astype(BF16)
        wd = wd_ref[...].astype(BF16)

        def mlp(nb):
            rows = pl.ds(0, nb * ITEM_SUB)
            xb = xb_ref[rows, :]
            g = jnp.dot(xb, wg, preferred_element_type=F32)
            u = jnp.dot(xb, wu, preferred_element_type=F32)
            hm = (g * jax.nn.sigmoid(g) * u).astype(BF16)
            acc_ref[rows, :] += jnp.dot(hm, wd, preferred_element_type=F32)

        by_block_count(i, mlp)

    @pl.when(valid & (j == nj - 1))
    def _():
        @pl.when(i > 0)
        def _():
            each_block(i - 1, lambda sub: out_copy(i - 1, 1 - slot, sub).wait())

        def pack(nb):
            rows = pl.ds(0, nb * ITEM_SUB)
            yp_ref[slot, rows, :] = _pack_bf16_pairs(acc_ref[rows, :])

        by_block_count(i, pack)
        each_block(i, lambda sub: out_copy(i, slot, sub).start())

        @pl.when(i == n_items - 1)
        def _():
            each_block(i, lambda sub: out_copy(i, slot, sub).wait())


def _experts(item_e, item_start, item_n, n_items, x_sorted, w_gate, w_up, w_down, max_items, m):
    d = w_gate.shape[1]
    nj = D_EXPERT // F_CHUNK

    def jj(i, j, ni):
        return jnp.where(i < ni[0], j, nj - 1)

    grid_spec = pltpu.PrefetchScalarGridSpec(
        num_scalar_prefetch=4,
        grid=(max_items, nj),
        in_specs=[pl.BlockSpec(memory_space=pl.ANY),
                  pl.BlockSpec((None, d, F_CHUNK), lambda i, j, ie, s, n, ni: (ie[i], 0, jj(i, j, ni))),
                  pl.BlockSpec((None, d, F_CHUNK), lambda i, j, ie, s, n, ni: (ie[i], 0, jj(i, j, ni))),
                  pl.BlockSpec((None, F_CHUNK, d), lambda i, j, ie, s, n, ni: (ie[i], jj(i, j, ni), 0))],
        out_specs=pl.BlockSpec(memory_space=pl.ANY),
        scratch_shapes=[pltpu.VMEM((ITEM_ROWS, d // 2), jnp.uint32), pltpu.VMEM((ITEM_ROWS, d), BF16),
                        pltpu.VMEM((ITEM_ROWS, d), F32), pltpu.VMEM((2, ITEM_ROWS, d // 2), jnp.uint32),
                        pltpu.SemaphoreType.DMA(()), pltpu.SemaphoreType.DMA((2,))],
    )
    return pl.pallas_call(
        functools.partial(_expert_kernel, m=m, nj=nj),
        grid_spec=grid_spec,
        out_shape=jax.ShapeDtypeStruct((_sorted_rows(m), d // 2), jnp.uint32),
        compiler_params=_params("arbitrary", "arbitrary"),
        name="moe_experts",
    )(item_e, item_start, item_n, n_items, x_sorted, w_gate, w_up, w_down)


def _combine_kernel(dest_ref, y_hbm, x_ref, wt_ref, g_ref, o_ref, yv_ref, sem, *, tm):
    i = pl.program_id(0)
    slot = lax.rem(i, 2)

    def issue_tile(tile, sl):
        def issue(r, c):
            base = TOPK_IN_GROUP * (tile * tm + r)
            for k in range(TOPK_IN_GROUP):
                _row_copy(y_hbm, dest_ref[base + k], yv_ref.at[sl, k], r, sem.at[sl]).start()
            return c

        lax.fori_loop(0, tm, issue, 0, unroll=4)

    @pl.when(i == 0)
    def _():
        issue_tile(0, 0)

    @pl.when(i + 1 < pl.num_programs(0))
    def _():
        issue_tile(i + 1, 1 - slot)

    for k in range(TOPK_IN_GROUP):
        pltpu.make_async_copy(y_hbm.at[pl.ds(0, tm), :], yv_ref.at[slot, k], sem.at[slot]).wait()
    w = wt_ref[...]
    half = x_ref.shape[1] // 2
    parts = [x_ref[:, :half], x_ref[:, half:]]
    for k in range(TOPK_IN_GROUP):
        lo, hi = _unpack_bf16_pairs(yv_ref[slot, k])
        parts = [parts[0] + lo.astype(F32) * w[:, k:k + 1], parts[1] + hi.astype(F32) * w[:, k:k + 1]]
    ms = (jnp.sum(parts[0] * parts[0], axis=-1, keepdims=True)
          + jnp.sum(parts[1] * parts[1], axis=-1, keepdims=True)) / x_ref.shape[1]
    r = lax.rsqrt(ms + EPS)
    o_ref[:, :half] = parts[0] * r * g_ref[:, :half]
    o_ref[:, half:] = parts[1] * r * g_ref[:, half:]


def _combine(dest, y_sorted, x2, wts, g_final, tm=128):
    n, d = x2.shape
    grid_spec = pltpu.PrefetchScalarGridSpec(
        num_scalar_prefetch=1,
        grid=(n // tm,),
        in_specs=[pl.BlockSpec(memory_space=pl.ANY),
                  pl.BlockSpec((tm, d), lambda i, dr: (i, 0)),
                  pl.BlockSpec((tm, LANES), lambda i, dr: (i, 0)),
                  pl.BlockSpec((1, d), lambda i, dr: (0, 0))],
        out_specs=pl.BlockSpec((tm, d), lambda i, dr: (i, 0)),
        scratch_shapes=[pltpu.VMEM((2, TOPK_IN_GROUP, tm, d // 2), y_sorted.dtype), pltpu.SemaphoreType.DMA((2,))],
    )
    return pl.pallas_call(
        functools.partial(_combine_kernel, tm=tm),
        grid_spec=grid_spec,
        out_shape=jax.ShapeDtypeStruct((n, d), F32),
        compiler_params=_params("arbitrary"),
        name="moe_combine_final_norm",
    )(dest, y_sorted, x2, wts, g_final.reshape(1, d))


def _pad_cols(w, width):
    return jnp.pad(w, ((0, 0), (0, width - w.shape[1])))


def _mixer(x2d, bsz, s, g_norm, w_in, b_f, g_q_lat, g_kv_lat, g_idx_k, b_idx_k, w_uq, w_idx_q, w_uk, w_uv,
           w_up_a, w_up_b, w_out):
    d = x2d.shape[1]
    topk = min(DSA_TOPK_MAX, s // 4)
    h1 = _rmsnorm(x2d, g_norm, BF16)
    c_small = 3 * FOX_WIDTH
    o_fa, o_cq = c_small, c_small + FOX_HEADS
    o_ckv = o_cq + Q_LORA
    o_kr = o_ckv + KV_LORA
    o_ik = o_kr + DSA_ROPE_DIM
    o_iw = o_ik + IDX_DIM
    o_ga = o_iw + IDX_HEADS
    o_gb = o_ga + d
    w_t = jnp.swapaxes(w_in, 0, 1)
    pad_rows = lambda w: jnp.pad(w, ((0, LANES - w.shape[0]), (0, 0)))
    w_small_t = jnp.concatenate([
        w_t[o_cq:o_ckv], w_t[o_ckv:o_kr], w_t[o_ik:o_iw],
        pad_rows(w_t[o_kr:o_ik]), pad_rows(w_t[o_fa:o_cq]), pad_rows(w_t[o_iw:o_ga])], axis=0)

    qkv = _mm(h1, w_t, 3 * FOX_WIDTH, w_rows=True, scaled_cols=(FOX_WIDTH, FOX_QSCALE), name="proj_qkv")
    small = _mm(h1, w_small_t, SMALL_W, out_dtype=F32, tm=512, tn=SMALL_W // 2, w_rows=True, name="proj_small")

    tabs64 = _rope_tables(s, DSA_ROPE_DIM)
    tabs128 = _rope_tables(s, IDX_DIM)
    cq_n, kv_cat, k_idx, w_idx, fa = _prep(small, g_q_lat, g_kv_lat, g_idx_k, b_idx_k, tabs64, tabs128, s)

    cum = _fox_cum(fa, _pad_cols(b_f.reshape(1, -1), LANES), bsz, s)
    att_a = _fox_attention(qkv, cum, bsz, s)

    w_uq3 = w_uq.reshape(Q_LORA, DSA_HEADS, DSA_NOPE_DIM + DSA_ROPE_DIM)
    w_uq_nope = w_uq3[:, :, :DSA_NOPE_DIM].reshape(Q_LORA, DSA_HEADS * DSA_NOPE_DIM).astype(BF16)
    w_uq_rope = jnp.pad(w_uq3[:, :, DSA_NOPE_DIM:], ((0, 0), (0, 0), (0, LANES - DSA_ROPE_DIM))).reshape(
        Q_LORA, DSA_HEADS * LANES).astype(BF16)
    q_cat = _q_latent(cq_n, w_uq_nope, w_uq_rope, w_uk.astype(BF16), tabs64, bsz, s)
    q_idx = _mm(cq_n, w_idx_q, IDX_HEADS * IDX_DIM, heads=(bsz, s), rope_tabs=tabs128, name="proj_q_idx")
    w_idx_t = jnp.transpose(w_idx.reshape(bsz, s, LANES)[:, :, :IDX_HEADS], (0, 2, 1))
    bias = _indexer_mask(q_idx, k_idx.reshape(bsz, s, IDX_DIM), w_idx_t, bsz, s, topk)
    kv3 = kv_cat.reshape(bsz, s, KV_PAD)
    kv_t = jnp.transpose(kv3.reshape(bsz, s // KEY_CHUNK, KEY_CHUNK, KV_PAD), (0, 1, 3, 2))
    o_lat = _dsa_attention(q_cat, kv_t, kv3, bias, bsz, s)
    att_b = _uv_project(o_lat, w_uv.astype(BF16), bsz, s)

    merged = _gated_merge(h1, att_a, att_b, w_t, o_ga, o_gb, w_up_a, w_up_b)
    return _mm(merged, w_out, d, out_dtype=F32, res=x2d, name="proj_out")


def _memory_layer(x1, mem, bsz, s, g_x, g_m, w_qm, w_km, w_vm, w_om):
    n_mem = mem.shape[1]
    d = x1.shape[1]
    hd = MEM_HEADS * MEM_HEAD_DIM
    mem_n = _rmsnorm(mem.reshape(bsz * n_mem, d), g_m, BF16)
    kv_m = _mm(mem_n, jnp.concatenate([w_km, w_vm], axis=1), 2 * hd, tm=bsz * n_mem, name="proj_kv_mem")
    km_t = jnp.transpose(kv_m[:, :hd].reshape(bsz, n_mem, hd), (0, 2, 1))
    return _mem_attention(x1, g_x, w_qm.astype(BF16), km_t, kv_m, w_om.astype(BF16), bsz, s, n_mem)


def _moe_layer(x2, g_ffn, w_rg, b_rg, w_re, b_re, w_gate, w_up, w_down, g_final):
    n, d = x2.shape
    m = n * TOPK_IN_GROUP
    w_r = _pad_cols(jnp.concatenate([w_re, w_rg], axis=1), LANES)
    w_r_hi = w_r.astype(BF16)
    w_r_split = jnp.concatenate([w_r_hi, (w_r - w_r_hi.astype(F32)).astype(BF16)], axis=1)
    b_r = _pad_cols(jnp.concatenate([b_re, b_rg]).reshape(1, -1), LANES)
    h3p, ridx, wts, cnt = _router(x2, g_ffn, w_r_split, b_r)
    counts = cnt[0, :N_EXPERTS].astype(jnp.int32)
    e_flat = ridx[:, :TOPK_IN_GROUP].reshape(m)
    rank_flat = ridx[:, TOPK_IN_GROUP:2 * TOPK_IN_GROUP].reshape(m)
    seg = (counts + SUBLANES - 1) // SUBLANES * SUBLANES
    starts = (jnp.cumsum(seg) - seg).astype(jnp.int32)
    n_it = (counts + ITEM_ROWS - 1) // ITEM_ROWS
    it_end = jnp.cumsum(n_it)
    it_first = it_end - n_it
    max_items = (m + N_EXPERTS * (ITEM_ROWS - 1)) // ITEM_ROWS
    t = jnp.arange(max_items, dtype=jnp.int32)
    item_e = jnp.minimum(jnp.sum(it_end[None, :] <= t[:, None], axis=1), N_EXPERTS - 1).astype(jnp.int32)
    k_in = t - it_first[item_e]
    item_start = (starts[item_e] + k_in * ITEM_ROWS).astype(jnp.int32)
    item_n = jnp.clip(counts[item_e] - k_in * ITEM_ROWS, 0, ITEM_ROWS).astype(jnp.int32)
    n_items = it_end[-1:].astype(jnp.int32)
    last = jnp.maximum(n_items[0] - 1, 0)
    item_e = jnp.where(t < n_items[0], item_e, item_e[last])
    item_start = jnp.where(t < n_items[0], item_start, 0)
    item_n = jnp.where(t < n_items[0], item_n, 0)
    e_ids = jnp.arange(N_EXPERTS, dtype=jnp.int32)
    dest = rank_flat + jnp.sum(jnp.where(e_flat[:, None] == e_ids[None, :], starts[None, :], 0), axis=1)
    x_sorted = _dispatch(dest, starts, counts, h3p)
    y_sorted = _experts(item_e, item_start, item_n, n_items, x_sorted, w_gate, w_up, w_down, max_items, m)
    return _combine(dest, y_sorted, x2, wts, g_final)


def kernel(x, mem, g_norm_mix, w_in, b_f, g_q_lat, g_kv_lat, g_idx_k, b_idx_k, w_uq, w_idx_q, w_uk, w_uv,
           w_up_a, w_up_b, w_out, g_norm_mem_x, g_mem, w_qm, w_km, w_vm, w_om, g_norm_ffn,
           w_rg, b_rg, w_re, b_re, w_gate, w_up, w_down, g_final):
    bsz, s, d = x.shape
    depth = w_in.shape[0]
    assert depth == 1, "the final norm is fused into the last MoE combine"
    l = 0
    x2d = x.reshape(bsz * s, d)
    x1 = _mixer(x2d, bsz, s, g_norm_mix[l], w_in[l], b_f[l], g_q_lat[l], g_kv_lat[l], g_idx_k[l], b_idx_k[l],
                w_uq[l], w_idx_q[l], w_uk[l], w_uv[l], w_up_a[l], w_up_b[l], w_out[l])
    x2 = _memory_layer(x1, mem, bsz, s, g_norm_mem_x[l], g_mem[l], w_qm[l], w_km[l], w_vm[l], w_om[l])
    out = _moe_layer(x2, g_norm_ffn[l], w_rg[l], b_rg[l], w_re[l], b_re[l], w_gate[l], w_up[l], w_down[l], g_final)
    return out.reshape(bsz, s, d)
```

```python
import functools

import jax
import jax.numpy as jnp
from jax import lax
from jax.experimental import pallas as pl
from jax.experimental.pallas import tpu as pltpu

F32 = jnp.float32
BF16 = jnp.bfloat16

EPS = 1e-6
ROPE_THETA = 10000.0
FOX_HEADS = 16
FOX_HEAD_DIM = 128
DSA_HEADS = 16
DSA_NOPE_DIM = 128
DSA_ROPE_DIM = 64
DSA_V_DIM = 128
Q_LORA = 1024
KV_LORA = 256
IDX_HEADS = 32
IDX_DIM = 128
DSA_TOPK_MAX = 256
MEM_HEADS = 4
MEM_HEAD_DIM = 128
N_GROUPS = 8
EXPERTS_PER_GROUP = 8
N_EXPERTS = N_GROUPS * EXPERTS_PER_GROUP
TOPK_IN_GROUP = 2
D_EXPERT = 512

FOX_WIDTH = FOX_HEADS * FOX_HEAD_DIM
LANES = 128
SUBLANES = 8
KV_PAD = KV_LORA + LANES
NEG_BIG = -1e30
LOG2E = 1.4426950408889634
FOX_QSCALE = FOX_HEAD_DIM ** -0.5 * LOG2E
FOX_HB = 4
DSA_QSCALE = (DSA_NOPE_DIM + DSA_ROPE_DIM) ** -0.5 * LOG2E
VMEM_LIMIT_BYTES = 56 * 1024 * 1024
ITEM_ROWS = 384
ITEM_SUB = 128
F_CHUNK = 256
KEY_CHUNK = 256


def _params(*sem):
    return pltpu.CompilerParams(dimension_semantics=sem, vmem_limit_bytes=VMEM_LIMIT_BYTES)


def _rmsnorm_kernel(x_ref, g_ref, o_ref):
    x = x_ref[...].astype(F32)
    ms = jnp.mean(x * x, axis=-1, keepdims=True)
    o_ref[...] = (x * lax.rsqrt(ms + EPS) * g_ref[...]).astype(o_ref.dtype)


def _rmsnorm(x2d, g, out_dtype, tm=256):
    n, d = x2d.shape
    return pl.pallas_call(
        _rmsnorm_kernel,
        grid=(n // tm,),
        in_specs=[pl.BlockSpec((tm, d), lambda i: (i, 0)), pl.BlockSpec((1, d), lambda i: (0, 0))],
        out_specs=pl.BlockSpec((tm, d), lambda i: (i, 0)),
        out_shape=jax.ShapeDtypeStruct((n, d), out_dtype),
        compiler_params=_params("arbitrary"),
        name="rmsnorm",
    )(x2d, g.reshape(1, d).astype(F32))


_NT_DIMS = (((1,), (1,)), ((), ()))


def _w_rows_spec(tn, k, row0, index_of_j):
    if row0 % tn == 0:
        return pl.BlockSpec((tn, k), lambda *g: (index_of_j(*g) + row0 // tn, 0))
    assert row0 % SUBLANES == 0
    return pl.BlockSpec((pl.Element(tn), pl.Element(k)),
                        lambda *g: (pl.multiple_of(index_of_j(*g) * tn + row0, SUBLANES), 0))


def _mm_kernel(a_ref, b_ref, *rest, has_res, head_out, rope, w_rows, col_scale):
    pos = 0
    res_ref = cos_ref = sin_ref = None
    if has_res:
        res_ref = rest[pos]
        pos += 1
    if rope:
        cos_ref, sin_ref = rest[pos], rest[pos + 1]
        pos += 2
    o_ref = rest[pos]
    if b_ref.dtype == BF16:
        wb_ref = b_ref
    else:
        wb_ref = rest[pos + 1]

        @pl.when(pl.program_id(1) == 0)
        def _():
            wb_ref[...] = b_ref[...].astype(BF16)

    if w_rows:
        acc = lax.dot_general(a_ref[...], wb_ref[...], _NT_DIMS, preferred_element_type=F32)
    else:
        acc = jnp.dot(a_ref[...], wb_ref[...], preferred_element_type=F32)
    if col_scale is not None:
        acc = acc * jnp.where(pl.program_id(0) < col_scale[0], col_scale[1], 1.0)
    if has_res:
        acc = acc + res_ref[...]
    if head_out:
        for hh in range(acc.shape[1] // LANES):
            part = acc[:, hh * LANES:(hh + 1) * LANES]
            if rope:
                part = part * cos_ref[...] + pltpu.roll(part, LANES // 2, 1) * sin_ref[...]
            o_ref[hh] = part.astype(o_ref.dtype)
    else:
        o_ref[...] = acc.astype(o_ref.dtype)


def _mm(a, w, n_cols, *, col0=0, out_dtype=BF16, tm=1024, tn=512, res=None, heads=None, rope_tabs=None,
        w_rows=False, scaled_cols=None, name="mm"):
    m, k = a.shape
    tm = min(tm, m if heads is None else heads[1])
    assert n_cols % tn == 0 and m % tm == 0
    if w_rows:
        w_spec = _w_rows_spec(tn, k, col0, lambda j, i: j)
    else:
        assert col0 % tn == 0
        w_spec = pl.BlockSpec((k, tn), lambda j, i: (0, j + col0 // tn))
    in_specs = [pl.BlockSpec((tm, k), lambda j, i: (i, 0)), w_spec]
    args = [a, w]
    if res is not None:
        in_specs.append(pl.BlockSpec((tm, tn), lambda j, i: (i, j)))
        args.append(res)
    if heads is not None:
        bsz, s = heads
        assert s % tm == 0
        spt = s // tm
        if rope_tabs is not None:
            in_specs += [pl.BlockSpec((tm, LANES), lambda j, i: (i % spt, 0))] * 2
            args += list(rope_tabs)
        hpt = tn // LANES
        out_spec = pl.BlockSpec((None, hpt, tm, LANES), lambda j, i: (i // spt, j, i % spt, 0))
        out_shape = jax.ShapeDtypeStruct((bsz, n_cols // LANES, s, LANES), out_dtype)
    else:
        out_spec = pl.BlockSpec((tm, tn), lambda j, i: (i, j))
        out_shape = jax.ShapeDtypeStruct((m, n_cols), out_dtype)
    col_scale = None
    if scaled_cols is not None:
        assert scaled_cols[0] % tn == 0
        col_scale = (scaled_cols[0] // tn, scaled_cols[1])
    kern = functools.partial(_mm_kernel, has_res=res is not None, head_out=heads is not None,
                             rope=rope_tabs is not None, w_rows=w_rows, col_scale=col_scale)
    return pl.pallas_call(
        kern,
        grid=(n_cols // tn, m // tm),
        in_specs=in_specs,
        out_specs=out_spec,
        out_shape=out_shape,
        scratch_shapes=[] if w.dtype == BF16 else [pltpu.VMEM((tn, k) if w_rows else (k, tn), BF16)],
        compiler_params=_params("arbitrary", "arbitrary"),
        name=name,
    )(*args)


def _rope_tables(s, d):
    inv = jnp.power(ROPE_THETA, -jnp.arange(0, d, 2, dtype=F32) / d)
    ang = jnp.arange(s, dtype=jnp.int32).astype(F32)[:, None] * inv[None, :]
    cos, sin = jnp.cos(ang), jnp.sin(ang)
    cos_t = jnp.tile(jnp.concatenate([cos, cos], axis=-1), (1, LANES // d))
    sin_t = jnp.tile(jnp.concatenate([-sin, sin], axis=-1), (1, LANES // d))
    return cos_t, sin_t


def _rot_half64(x):
    lane = lax.broadcasted_iota(jnp.int32, x.shape, 1)
    return jnp.where((lane % 64) < 32, pltpu.roll(x, 96, 1), pltpu.roll(x, 32, 1))


_C_CQ = 0
_C_CKV = Q_LORA
_C_IK = _C_CKV + KV_LORA
_C_KR = _C_IK + IDX_DIM
_C_FA = _C_KR + LANES
_C_IW = _C_FA + LANES
SMALL_W = _C_IW + LANES


def _prep_kernel(s_ref, gq_ref, gkv_ref, gik_ref, bik_ref, c64_ref, s64_ref, c128_ref, s128_ref,
                 cq_ref, kv_ref, kidx_ref, widx_ref, fa_ref):
    cq = s_ref[:, _C_CQ:_C_CQ + Q_LORA]
    cq_ref[...] = (cq * lax.rsqrt(jnp.mean(cq * cq, axis=-1, keepdims=True) + EPS) * gq_ref[...]).astype(BF16)
    ckv = s_ref[:, _C_CKV:_C_CKV + KV_LORA]
    kv_ref[:, :KV_LORA] = (ckv * lax.rsqrt(jnp.mean(ckv * ckv, axis=-1, keepdims=True) + EPS)
                           * gkv_ref[...]).astype(BF16)
    kr = s_ref[:, _C_KR:_C_KR + LANES]
    kv_ref[:, KV_LORA:] = (kr * c64_ref[...] + _rot_half64(kr) * s64_ref[...]).astype(BF16)
    ik = s_ref[:, _C_IK:_C_IK + IDX_DIM]
    mu = jnp.mean(ik, axis=-1, keepdims=True)
    xc = ik - mu
    ikn = xc * lax.rsqrt(jnp.mean(xc * xc, axis=-1, keepdims=True) + EPS) * gik_ref[...] + bik_ref[...]
    kidx_ref[...] = (ikn * c128_ref[...] + pltpu.roll(ikn, LANES // 2, 1) * s128_ref[...]).astype(BF16)
    widx_ref[...] = s_ref[:, _C_IW:_C_IW + LANES] * (IDX_HEADS * IDX_DIM) ** -0.5
    fa_ref[...] = s_ref[:, _C_FA:_C_FA + LANES]


def _prep(small, g_q, g_kv, g_ik, b_ik, tabs64, tabs128, s, tm=256):
    n = small.shape[0]
    spt = s // tm
    row = lambda w: pl.BlockSpec((tm, w), lambda i: (i, 0))
    vec = lambda w: pl.BlockSpec((1, w), lambda i: (0, 0))
    tab = pl.BlockSpec((tm, LANES), lambda i: (i % spt, 0))
    return pl.pallas_call(
        _prep_kernel,
        grid=(n // tm,),
        in_specs=[row(SMALL_W), vec(Q_LORA), vec(KV_LORA), vec(IDX_DIM), vec(IDX_DIM), tab, tab, tab, tab],
        out_specs=[row(Q_LORA), row(KV_PAD), row(IDX_DIM), row(LANES), row(LANES)],
        out_shape=[jax.ShapeDtypeStruct((n, Q_LORA), BF16), jax.ShapeDtypeStruct((n, KV_PAD), BF16),
                   jax.ShapeDtypeStruct((n, IDX_DIM), BF16), jax.ShapeDtypeStruct((n, LANES), F32),
                   jax.ShapeDtypeStruct((n, LANES), F32)],
        compiler_params=_params("arbitrary"),
        name="mixer_prep",
    )(small, g_q.reshape(1, -1), g_kv.reshape(1, -1), g_ik.reshape(1, -1), b_ik.reshape(1, -1),
      tabs64[0], tabs64[1], tabs128[0], tabs128[1])


def _cum_kernel(fa_ref, bf_ref, cum_ref, carry_ref):
    @pl.when(pl.program_id(1) == 0)
    def _():
        carry_ref[...] = jnp.zeros_like(carry_ref)

    z = fa_ref[...] + bf_ref[...]
    logf = jnp.minimum(z, 0.0) - jnp.log1p(jnp.exp(-jnp.abs(z)))
    t = z.shape[0]
    tri = (lax.broadcasted_iota(jnp.int32, (t, t), 0) >= lax.broadcasted_iota(jnp.int32, (t, t), 1)).astype(F32)
    c = jnp.dot(tri, logf, precision=lax.Precision.HIGHEST, preferred_element_type=F32) + carry_ref[...]
    cum_ref[...] = c
    carry_ref[...] = c[t - 1:t, :]


def _fox_cum(fa, b_f_pad, bsz, s, t=256):
    return pl.pallas_call(
        _cum_kernel,
        grid=(bsz, s // t),
        in_specs=[pl.BlockSpec((None, t, LANES), lambda b, i: (b, i, 0)),
                  pl.BlockSpec((1, LANES), lambda b, i: (0, 0))],
        out_specs=pl.BlockSpec((None, t, LANES), lambda b, i: (b, i, 0)),
        out_shape=jax.ShapeDtypeStruct((bsz, s, LANES), F32),
        scratch_shapes=[pltpu.VMEM((1, LANES), F32)],
        compiler_params=_params("arbitrary", "arbitrary"),
        name="fox_cumsum",
    )(fa.reshape(bsz, s, LANES), b_f_pad)


def _lane_fold(x, op):
    out = x[:, :LANES]
    for c in range(1, x.shape[1] // LANES):
        out = op(out, x[:, c * LANES:(c + 1) * LANES])
    return out


def _split3(x):
    t1 = x.astype(BF16)
    r = x - t1.astype(F32)
    t2 = r.astype(BF16)
    t3 = (r - t2.astype(F32)).astype(BF16)
    return t1, t2, t3


def _gate_lanes(parts, own_first):
    t1, t2, t3 = parts
    lane = lax.broadcasted_iota(jnp.int32, t1.shape, 1)
    a, b = (0, 3) if own_first else (3, 0)
    ones3 = jnp.where((lane >= b) & (lane < b + 3), 1.0, 0.0)
    out = jnp.where(lane == a, t1, jnp.where(lane == a + 1, t2, jnp.where(lane == a + 2, t3, ones3)))
    return out.astype(BF16)


def _fox_kernel(q_ref, k_ref, v_ref, cum_ref, o_ref, kf_ref, vf_ref, s_ref, *, t):
    hp = pl.program_id(1)
    i = pl.program_id(2)
    d = FOX_HEAD_DIM
    heads = range(FOX_HB)
    row = lax.broadcasted_iota(jnp.int32, (LANES, LANES), 0)

    def gate_columns(c):
        terms = _split3(c * LOG2E)
        out = []
        for hh in heads:
            pick = (row == hp * FOX_HB + hh).astype(BF16)
            out.append([jnp.dot(term, pick, preferred_element_type=F32) for term in terms])
        return out

    @pl.when(i == 0)
    def _():
        ck = gate_columns(-cum_ref[...])
        for hh in heads:
            kf_ref[hh, :, :d] = k_ref[:, hh * d:(hh + 1) * d]
            kf_ref[hh, :, d:] = _gate_lanes(ck[hh], True)
            vf_ref[hh, :, :d] = v_ref[:, hh * d:(hh + 1) * d]
            vf_ref[hh, :, d:] = jnp.ones((vf_ref.shape[1], LANES), BF16)

    cq = gate_columns(cum_ref[pl.ds(pl.multiple_of(i * t, t), t), :])
    qs = [jnp.concatenate([q_ref[:, hh * d:(hh + 1) * d], _gate_lanes(cq[hh], False)], axis=1) for hh in heads]

    def logits(hh, kc):
        kb = kf_ref[hh, pl.ds(pl.multiple_of(kc * t, t), t), :]
        return lax.dot_general(qs[hh], kb, _NT_DIMS, preferred_element_type=F32)

    def pass1(kc, m_lanes):
        out = []
        for hh in heads:
            s = logits(hh, kc)
            s_ref[hh, kc] = s
            out.append(jnp.maximum(m_lanes[hh], _lane_fold(s, jnp.maximum)))
        return tuple(out)

    m_lanes = lax.fori_loop(0, i, pass1, tuple(jnp.full((t, LANES), NEG_BIG, F32) for _ in heads))
    causal = lax.broadcasted_iota(jnp.int32, (t, t), 0) >= lax.broadcasted_iota(jnp.int32, (t, t), 1)
    ms = []
    for hh in heads:
        s = jnp.where(causal, logits(hh, i), NEG_BIG)
        s_ref[hh, i] = s
        m_lane = jnp.maximum(m_lanes[hh], _lane_fold(s, jnp.maximum))
        m = jnp.broadcast_to(jnp.max(m_lane, axis=1, keepdims=True), (t, LANES))
        ms.append(jnp.concatenate([m] * (t // LANES), axis=1))

    def pass2(kc, accs):
        out = []
        for hh in heads:
            p = jnp.exp2(s_ref[hh, kc] - ms[hh])
            vb = vf_ref[hh, pl.ds(pl.multiple_of(kc * t, t), t), :]
            out.append(accs[hh] + jnp.dot(p.astype(BF16), vb, preferred_element_type=F32))
        return tuple(out)

    accs = lax.fori_loop(0, i + 1, pass2, tuple(jnp.zeros((t, d + LANES), F32) for _ in heads))
    for hh in heads:
        o_ref[:, hh * d:(hh + 1) * d] = (accs[hh][:, :d] / accs[hh][:, d:]).astype(o_ref.dtype)


def _fox_attention(qkv, cum, bsz, s, t=512):
    n = bsz * s
    nq = s // t
    hp = FOX_HEADS // FOX_HB
    w = FOX_HB * LANES
    assert FOX_HEAD_DIM == LANES and FOX_HEADS % FOX_HB == 0
    kern = functools.partial(_fox_kernel, t=t)
    return pl.pallas_call(
        kern,
        grid=(bsz, hp, nq),
        in_specs=[pl.BlockSpec((t, w), lambda b, h, i: (b * nq + i, h)),
                  pl.BlockSpec((s, w), lambda b, h, i: (b, hp + h)),
                  pl.BlockSpec((s, w), lambda b, h, i: (b, 2 * hp + h)),
                  pl.BlockSpec((None, s, LANES), lambda b, h, i: (b, 0, 0))],
        out_specs=pl.BlockSpec((t, w), lambda b, h, i: (b * nq + i, h)),
        out_shape=jax.ShapeDtypeStruct((n, FOX_WIDTH), BF16),
        scratch_shapes=[pltpu.VMEM((FOX_HB, s, 2 * LANES), BF16), pltpu.VMEM((FOX_HB, s, 2 * LANES), BF16),
                        pltpu.VMEM((FOX_HB, nq, t, t), F32)],
        compiler_params=_params("arbitrary", "arbitrary", "arbitrary"),
        name="fox_attention",
    )(qkv, qkv, qkv, cum)


def _qlat_kernel(cq_ref, wn_ref, wr_ref, wuk_ref, c64_ref, s64_ref, o_ref):
    cq = cq_ref[...]
    qn = jnp.dot(cq, wn_ref[...], preferred_element_type=F32).astype(BF16)
    qr = jnp.dot(cq, wr_ref[...], preferred_element_type=F32)
    for h in range(DSA_HEADS):
        cols = slice(h * LANES, (h + 1) * LANES)
        lat = jnp.dot(qn[:, cols], wuk_ref[h], preferred_element_type=F32)
        o_ref[h, :, :KV_LORA] = (lat * DSA_QSCALE).astype(BF16)
        r = qr[:, cols]
        o_ref[h, :, KV_LORA:] = ((r * c64_ref[...] + _rot_half64(r) * s64_ref[...]) * DSA_QSCALE).astype(BF16)


def _q_latent(cq_n, w_nope_b, w_rope_b, w_uk_b, tabs64, bsz, s, tm=256):
    spt = s // tm
    kq = cq_n.shape[1]
    wide = DSA_HEADS * LANES
    return pl.pallas_call(
        _qlat_kernel,
        grid=(bsz * spt,),
        in_specs=[pl.BlockSpec((tm, kq), lambda i: (i, 0)),
                  pl.BlockSpec((kq, wide), lambda i: (0, 0)),
                  pl.BlockSpec((kq, wide), lambda i: (0, 0)),
                  pl.BlockSpec((DSA_HEADS, DSA_NOPE_DIM, KV_LORA), lambda i: (0, 0, 0)),
                  pl.BlockSpec((tm, LANES), lambda i: (i % spt, 0)),
                  pl.BlockSpec((tm, LANES), lambda i: (i % spt, 0))],
        out_specs=pl.BlockSpec((None, DSA_HEADS, tm, KV_PAD), lambda i: (i // spt, 0, i % spt, 0)),
        out_shape=jax.ShapeDtypeStruct((bsz, DSA_HEADS, s, KV_PAD), BF16),
        compiler_params=_params("arbitrary"),
        name="q_latent",
    )(cq_n, w_nope_b, w_rope_b, w_uk_b, tabs64[0], tabs64[1])


def _needed_chunks(i, tq):
    return ((i + 1) * tq + KEY_CHUNK - 1) // KEY_CHUNK


BISECT_MAX_STEPS = 320
BISECT_UNROLL = 4


def _indexer_kernel(q_ref, k_ref, wt_ref, o_ref, sc_ref, thr_ref, cut_ref, *, tq, topk, hg):
    i = pl.program_id(1)
    tk = KEY_CHUNK
    n_chunks = sc_ref.shape[0]
    nk = _needed_chunks(i, tq)
    qpos = i * tq + lax.broadcasted_iota(jnp.int32, (tk, tq), 1)
    krow = lax.broadcasted_iota(jnp.int32, (tk, tq), 0)
    inf = jnp.inf

    def score_chunk(kc, c):
        k0 = pl.multiple_of(kc * tk, tk)
        kb = k_ref[pl.ds(k0, tk), :]
        sc = jnp.zeros((tk, tq), F32)
        for g in range(IDX_HEADS // hg):
            q2 = q_ref[g * hg:(g + 1) * hg].reshape(hg * tq, IDX_DIM)
            d = lax.dot_general(kb, q2, _NT_DIMS, preferred_element_type=F32)
            for hh in range(hg):
                h = g * hg + hh
                sc = sc + wt_ref[h:h + 1, :] * jnp.maximum(d[:, hh * tq:(hh + 1) * tq], 0.0)
        sc_ref[kc] = jnp.where(k0 + krow <= qpos, sc, -inf)
        return c

    lax.fori_loop(0, nk, score_chunk, 0)

    def pad_chunk(kc, c):
        sc_ref[kc] = jnp.full((tk, tq), -inf, F32)
        return c

    lax.fori_loop(nk, n_chunks, pad_chunk, 0)
    kf = float(topk)
    n_causal = (i * tq + lax.broadcasted_iota(jnp.int32, (1, tq), 1) + 1).astype(F32)

    def fold(x, op):
        return op(x.reshape(tk // SUBLANES, SUBLANES, tq), axis=0)

    def key_pos(kc):
        return (kc * tk + krow).astype(F32)

    def threshold(n_counted):
        def count_ge(t):
            acc = jnp.zeros((SUBLANES, tq), F32)
            for kc in range(n_counted):
                acc = acc + fold(jnp.where(sc_ref[kc] >= t, 1.0, 0.0), jnp.sum)
            return jnp.sum(acc, axis=0, keepdims=True)

        hi = jnp.full((SUBLANES, tq), -inf, F32)
        lo = jnp.full((SUBLANES, tq), inf, F32)
        for kc in range(n_counted):
            s = sc_ref[kc]
            hi = jnp.maximum(hi, fold(s, jnp.max))
            lo = jnp.minimum(lo, fold(jnp.where(s == -inf, inf, s), jnp.min))
        hi = jnp.max(hi, axis=0, keepdims=True)
        lo = jnp.min(lo, axis=0, keepdims=True)
        few = n_causal <= kf
        n_max = count_ge(hi)
        at_max = n_max >= kf
        lo = jnp.where(few, -inf, jnp.where(at_max, hi, lo))
        n_lo = jnp.where(at_max, n_max, n_causal)
        done = jnp.where(few | at_max, 1.0, 0.0)

        def unsettled(state):
            it, _, _, _, done = state
            return (it < BISECT_MAX_STEPS) & (jnp.sum(1.0 - done) > 0.0)

        def step(state):
            it, lo, hi, n_lo, done = state
            for _ in range(BISECT_UNROLL):
                mid = 0.5 * lo + 0.5 * hi
                cnt = count_ge(mid)
                ge = cnt >= kf
                closed = (mid <= lo) | (mid >= hi)
                live = done == 0.0
                lo = jnp.where(live & ge, mid, lo)
                n_lo = jnp.where(live & ge, cnt, n_lo)
                hi = jnp.where(live & jnp.logical_not(ge), mid, hi)
                done = jnp.where(live & ((ge & (cnt == kf)) | closed), 1.0, done)
            return it + BISECT_UNROLL, lo, hi, n_lo, done

        _, lo, _, n_lo, _ = lax.while_loop(unsettled, step, (jnp.int32(0), lo, hi, n_lo, done))
        thr_ref[...] = lo
        cut_ref[...] = jnp.full((1, tq), float(n_chunks * tk), F32)

        over = jnp.where((n_lo > kf) & jnp.logical_not(few), 1.0, 0.0)

        @pl.when(jnp.sum(over) > 0.0)
        def _():
            def count_where(pred):
                def body(kc, acc):
                    hit = pred(sc_ref[kc], key_pos(kc))
                    return acc + fold(jnp.where(hit, 1.0, 0.0), jnp.sum)

                acc = lax.fori_loop(0, n_counted, body, jnp.zeros((SUBLANES, tq), F32))
                return jnp.sum(acc, axis=0, keepdims=True)

            need = kf - count_where(lambda s, kp: s > lo)

            def narrow(_, bracket):
                lo_p, hi_p = bracket
                mid = jnp.floor(0.5 * (lo_p + hi_p))
                ok = count_where(lambda s, kp: (s == lo) & (kp <= mid)) >= need
                return jnp.where(ok, lo_p, mid), jnp.where(ok, mid, hi_p)

            n_pos = n_counted * tk
            _, hi_p = lax.fori_loop(0, n_pos.bit_length(), narrow,
                                    (jnp.full((1, tq), -1.0, F32), jnp.full((1, tq), n_pos - 1.0, F32)))
            cut_ref[...] = jnp.where(over > 0.0, hi_p, cut_ref[...])

    half = max(n_chunks // 2, 1)
    if half < n_chunks:
        pl.when(nk <= half)(lambda: threshold(half))
        pl.when(nk > half)(lambda: threshold(n_chunks))
    else:
        threshold(n_chunks)
    thr = thr_ref[...]
    cut = cut_ref[...]

    def write_chunk(kc, c):
        s = sc_ref[kc]
        keep = ((s > thr) | ((s == thr) & (key_pos(kc) <= cut))) & (kc * tk + krow <= qpos)
        o_ref[kc] = jnp.where(keep, 0.0, NEG_BIG).T.astype(o_ref.dtype)
        return c

    lax.fori_loop(0, nk, write_chunk, 0)

    def fill_chunk(kc, c):
        o_ref[kc] = jnp.full((tq, tk), NEG_BIG, o_ref.dtype)
        return c

    lax.fori_loop(nk, n_chunks, fill_chunk, 0)


def _indexer_mask(q_idx, k_idx, w_idx_t, bsz, s, topk, tq=128, hg=4):
    kern = functools.partial(_indexer_kernel, tq=tq, topk=topk, hg=hg)
    nc = s // KEY_CHUNK
    return pl.pallas_call(
        kern,
        grid=(bsz, s // tq),
        in_specs=[pl.BlockSpec((None, IDX_HEADS, tq, IDX_DIM), lambda b, i: (b, 0, i, 0)),
                  pl.BlockSpec((None, s, IDX_DIM), lambda b, i: (b, 0, 0)),
                  pl.BlockSpec((None, IDX_HEADS, tq), lambda b, i: (b, 0, i))],
        out_specs=pl.BlockSpec((None, nc, tq, KEY_CHUNK), lambda b, i: (b, 0, i, 0)),
        out_shape=jax.ShapeDtypeStruct((bsz, nc, s, KEY_CHUNK), BF16),
        scratch_shapes=[pltpu.VMEM((nc, KEY_CHUNK, tq), F32), pltpu.VMEM((1, tq), F32),
                        pltpu.VMEM((1, tq), F32)],
        compiler_params=_params("arbitrary", "arbitrary"),
        name="indexer_topk_mask",
    )(q_idx, k_idx, w_idx_t)


def _dsa_kernel(q_ref, kvt_ref, kv_ref, bias_ref, o_ref, s_ref, m_ref, l_ref, acc_ref, *, tq):
    i = pl.program_id(1)
    tk = KEY_CHUNK
    rows = DSA_HEADS * tq
    nk = _needed_chunks(i, tq)
    q2 = q_ref[...].reshape(rows, KV_PAD)
    m_ref[...] = jnp.full(m_ref.shape, NEG_BIG, F32)

    n_pairs = (nk + 1) // 2

    def logits(kc):
        sc = jnp.dot(q2, kvt_ref[kc], preferred_element_type=F32)
        sc = (sc.reshape(DSA_HEADS, tq, tk) + bias_ref[kc].astype(F32)[None]).reshape(rows, tk)
        s_ref[kc] = sc
        return _lane_fold(sc, jnp.maximum)

    def pass1(pr, c):
        m_ref[...] = jnp.maximum(m_ref[...], jnp.maximum(logits(2 * pr), logits(2 * pr + 1)))
        return c

    lax.fori_loop(0, n_pairs, pass1, 0)
    m_ref[...] = jnp.broadcast_to(jnp.max(m_ref[...], axis=1, keepdims=True), m_ref.shape)
    l_ref[...] = jnp.zeros(l_ref.shape, F32)
    acc_ref[...] = jnp.zeros(acc_ref.shape, F32)

    def probs(kc):
        m = m_ref[...]
        p = jnp.concatenate([jnp.exp2(s_ref[kc, :, g * LANES:(g + 1) * LANES] - m)
                             for g in range(tk // LANES)], axis=1)
        kvb = kv_ref[pl.ds(pl.multiple_of(kc * tk, tk), tk), :KV_LORA]
        return _lane_fold(p, jnp.add), jnp.dot(p.astype(BF16), kvb, preferred_element_type=F32)

    def pass2(pr, c):
        l0, a0 = probs(2 * pr)
        l1, a1 = probs(2 * pr + 1)
        l_ref[...] += l0 + l1
        acc_ref[...] += a0 + a1
        return c

    lax.fori_loop(0, n_pairs, pass2, 0)
    l = jnp.dot(l_ref[...], jnp.ones((LANES, LANES), F32), precision=lax.Precision.HIGHEST,
                preferred_element_type=F32)
    o = acc_ref[...] / jnp.concatenate([l] * (KV_LORA // LANES), axis=1)
    o_ref[...] = o.reshape(DSA_HEADS, tq, KV_LORA).astype(o_ref.dtype)


def _dsa_attention(q_cat, kv_t, kv, bias, bsz, s, tq=128):
    kern = functools.partial(_dsa_kernel, tq=tq)
    nc = s // KEY_CHUNK
    assert nc % 2 == 0, "the kernel walks key chunks in pairs"
    rows = DSA_HEADS * tq
    return pl.pallas_call(
        kern,
        grid=(bsz, s // tq),
        in_specs=[pl.BlockSpec((None, DSA_HEADS, tq, KV_PAD), lambda b, i: (b, 0, i, 0)),
                  pl.BlockSpec((None, nc, KV_PAD, KEY_CHUNK), lambda b, i: (b, 0, 0, 0)),
                  pl.BlockSpec((None, s, KV_PAD), lambda b, i: (b, 0, 0)),
                  pl.BlockSpec((None, nc, tq, KEY_CHUNK), lambda b, i: (b, 0, i, 0))],
        out_specs=pl.BlockSpec((None, DSA_HEADS, tq, KV_LORA), lambda b, i: (b, 0, i, 0)),
        out_shape=jax.ShapeDtypeStruct((bsz, DSA_HEADS, s, KV_LORA), BF16),
        scratch_shapes=[pltpu.VMEM((nc, rows, KEY_CHUNK), F32), pltpu.VMEM((rows, LANES), F32),
                        pltpu.VMEM((rows, LANES), F32), pltpu.VMEM((rows, KV_LORA), F32)],
        compiler_params=_params("arbitrary", "arbitrary"),
        name="dsa_attention",
    )(q_cat, kv_t, kv, bias)


def _uv_kernel(o_ref, w_ref, y_ref):
    for h in range(DSA_HEADS):
        y_ref[:, h * DSA_V_DIM:(h + 1) * DSA_V_DIM] = jnp.dot(
            o_ref[h], w_ref[h], preferred_element_type=F32).astype(y_ref.dtype)


def _uv_project(o_lat, w_uv_b, bsz, s, tm=512):
    spt = s // tm
    return pl.pallas_call(
        _uv_kernel,
        grid=(bsz * spt,),
        in_specs=[pl.BlockSpec((None, DSA_HEADS, tm, KV_LORA), lambda i: (i // spt, 0, i % spt, 0)),
                  pl.BlockSpec((DSA_HEADS, KV_LORA, DSA_V_DIM), lambda i: (0, 0, 0))],
        out_specs=pl.BlockSpec((tm, DSA_HEADS * DSA_V_DIM), lambda i: (i, 0)),
        out_shape=jax.ShapeDtypeStruct((bsz * s, DSA_HEADS * DSA_V_DIM), BF16),
        compiler_params=_params("arbitrary"),
        name="uv_project",
    )(o_lat, w_uv_b)


def _merge_kernel(h_ref, a_ref, b_ref, wga_ref, wgb_ref, wua_ref, wub_ref, o_ref,
                  cga_ref, cgb_ref, cua_ref, cub_ref):
    @pl.when(pl.program_id(1) == 0)
    def _():
        cga_ref[...] = wga_ref[...].astype(BF16)
        cgb_ref[...] = wgb_ref[...].astype(BF16)
        cua_ref[...] = wua_ref[...].astype(BF16)
        cub_ref[...] = wub_ref[...].astype(BF16)

    h = h_ref[...]
    ga = lax.dot_general(h, cga_ref[...], _NT_DIMS, preferred_element_type=F32)
    gb = lax.dot_general(h, cgb_ref[...], _NT_DIMS, preferred_element_type=F32)
    ya = jnp.dot(a_ref[...], cua_ref[...], preferred_element_type=F32)
    yb = jnp.dot(b_ref[...], cub_ref[...], preferred_element_type=F32)
    o_ref[...] = (jax.nn.sigmoid(ga) * ya + jax.nn.sigmoid(gb) * yb).astype(o_ref.dtype)


def _gated_merge(h1, att_a, att_b, w_t, row_ga, row_gb, w_up_a, w_up_b, tm=512, tn=256):
    n, d = h1.shape
    ka, kb = att_a.shape[1], att_b.shape[1]
    return pl.pallas_call(
        _merge_kernel,
        grid=(d // tn, n // tm),
        in_specs=[pl.BlockSpec((tm, d), lambda j, i: (i, 0)),
                  pl.BlockSpec((tm, ka), lambda j, i: (i, 0)),
                  pl.BlockSpec((tm, kb), lambda j, i: (i, 0)),
                  _w_rows_spec(tn, d, row_ga, lambda j, i: j),
                  _w_rows_spec(tn, d, row_gb, lambda j, i: j),
                  pl.BlockSpec((ka, tn), lambda j, i: (0, j)),
                  pl.BlockSpec((kb, tn), lambda j, i: (0, j))],
        out_specs=pl.BlockSpec((tm, tn), lambda j, i: (i, j)),
        out_shape=jax.ShapeDtypeStruct((n, d), BF16),
        scratch_shapes=[pltpu.VMEM((tn, d), BF16), pltpu.VMEM((tn, d), BF16),
                        pltpu.VMEM((ka, tn), BF16), pltpu.VMEM((kb, tn), BF16)],
        compiler_params=_params("arbitrary", "arbitrary"),
        name="gated_merge",
    )(h1, att_a, att_b, w_t, w_t, w_up_a, w_up_b)


def _mem_kernel(x_ref, g_ref, wq_ref, kt_ref, v_ref, wom_ref, o_ref):
    x = x_ref[...]
    hn = (x * lax.rsqrt(jnp.mean(x * x, axis=-1, keepdims=True) + EPS) * g_ref[...]).astype(BF16)
    q = jnp.dot(hn, wq_ref[...], preferred_element_type=F32).astype(BF16)
    scale = MEM_HEAD_DIM ** -0.5
    outs = []
    for h in range(MEM_HEADS):
        sl = slice(h * MEM_HEAD_DIM, (h + 1) * MEM_HEAD_DIM)
        s = jnp.dot(q[:, sl], kt_ref[sl, :], preferred_element_type=F32) * scale
        m = jnp.max(s, axis=-1, keepdims=True)
        p = jnp.exp(s - m)
        p = p / jnp.sum(p, axis=-1, keepdims=True)
        outs.append(jnp.dot(p.astype(BF16), v_ref[:, sl], preferred_element_type=F32).astype(BF16))
    o = jnp.concatenate(outs, axis=-1)
    o_ref[...] = x + jnp.dot(o, wom_ref[...], preferred_element_type=F32)


def _mem_attention(x1, g_x, w_qm_b, km_t, kv_m, w_om_b, bsz, s, n_mem, tm=256):
    spt = s // tm
    hd = MEM_HEADS * MEM_HEAD_DIM
    d = x1.shape[1]
    return pl.pallas_call(
        _mem_kernel,
        grid=(bsz, spt),
        in_specs=[pl.BlockSpec((tm, d), lambda b, i: (b * spt + i, 0)),
                  pl.BlockSpec((1, d), lambda b, i: (0, 0)),
                  pl.BlockSpec((d, hd), lambda b, i: (0, 0)),
                  pl.BlockSpec((None, hd, n_mem), lambda b, i: (b, 0, 0)),
                  pl.BlockSpec((n_mem, hd), lambda b, i: (b, 1)),
                  pl.BlockSpec((hd, d), lambda b, i: (0, 0))],
        out_specs=pl.BlockSpec((tm, d), lambda b, i: (b * spt + i, 0)),
        out_shape=jax.ShapeDtypeStruct(x1.shape, F32),
        compiler_params=_params("arbitrary", "arbitrary"),
        name="memory_cross_attention",
    )(x1, g_x.reshape(1, d), w_qm_b, km_t, kv_m, w_om_b)


def _pack_bf16_pairs(x):
    c = x.shape[1] // 2
    return pltpu.pack_elementwise([x[:, :c], x[:, c:]], packed_dtype=BF16)


def _unpack_bf16_pairs(w):
    lo = pltpu.unpack_elementwise(w, index=0, packed_dtype=BF16, unpacked_dtype=F32)
    hi = pltpu.unpack_elementwise(w, index=1, packed_dtype=BF16, unpacked_dtype=F32)
    return lo.astype(BF16), hi.astype(BF16)


def _router_kernel(x_ref, g_ref, wr_ref, br_ref, h_ref, idx_ref, wt_ref, cnt_ref, carry_ref):
    @pl.when(pl.program_id(0) == 0)
    def _():
        carry_ref[...] = jnp.zeros_like(carry_ref)

    x = x_ref[...]
    hn = x * lax.rsqrt(jnp.mean(x * x, axis=-1, keepdims=True) + EPS) * g_ref[...]
    h_ref[...] = _pack_bf16_pairs(hn)
    hn_hi = hn.astype(BF16)
    hn_lo = (hn - hn_hi.astype(F32)).astype(BF16)
    wcat = wr_ref[...]
    a = jnp.dot(hn_hi, wcat, preferred_element_type=F32)
    b = jnp.dot(hn_lo, wcat[:, :LANES], preferred_element_type=F32)
    logits = a[:, :LANES] + (a[:, LANES:] + b) + br_ref[...]
    tm = x.shape[0]
    lane_i = lax.broadcasted_iota(jnp.int32, (tm, LANES), 1)
    lane = lane_i.astype(F32)
    lane_grp = lax.shift_right_logical(lane_i, EXPERTS_PER_GROUP.bit_length() - 1).astype(F32)
    big = 1e9
    is_g = (lane_i >= N_EXPERTS) & (lane_i < N_EXPERTS + N_GROUPS)
    lg = jnp.where(is_g, logits, -jnp.inf)
    mg = jnp.max(lg, axis=1, keepdims=True)
    gate_g = 1.0 / jnp.sum(jnp.exp(lg - mg), axis=1, keepdims=True)
    grp = jnp.min(jnp.where(lg == mg, lane, big), axis=1, keepdims=True) - N_EXPERTS
    in_grp = (lane_i < N_EXPERTS) & (lane_grp == grp)
    le = jnp.where(in_grp, logits, -jnp.inf)
    m1 = jnp.max(le, axis=1, keepdims=True)
    e1 = jnp.min(jnp.where(le == m1, lane, big), axis=1, keepdims=True)
    le2 = jnp.where(lane == e1, -jnp.inf, le)
    m2 = jnp.max(le2, axis=1, keepdims=True)
    e2 = jnp.min(jnp.where(le2 == m2, lane, big), axis=1, keepdims=True)
    r = jnp.exp(m2 - m1)
    w1 = gate_g * (1.0 / (1.0 + r))
    w2 = gate_g * (r / (1.0 + r))
    oh1 = (lane == e1).astype(F32)
    oh2 = (lane == e2).astype(F32)
    oh = oh1 + oh2
    tri = (lax.broadcasted_iota(jnp.int32, (tm, tm), 0) > lax.broadcasted_iota(jnp.int32, (tm, tm), 1)).astype(BF16)
    prefix = jnp.dot(tri, oh.astype(BF16), preferred_element_type=F32) + carry_ref[...]
    rank1 = jnp.sum(prefix * oh1, axis=1, keepdims=True)
    rank2 = jnp.sum(prefix * oh2, axis=1, keepdims=True)
    new_cnt = carry_ref[...] + jnp.sum(oh, axis=0, keepdims=True)
    carry_ref[...] = new_cnt
    cnt_ref[...] = new_cnt
    idx_ref[...] = jnp.where(lane_i == 0, e1, jnp.where(lane_i == 1, e2, jnp.where(
        lane_i == 2, rank1, jnp.where(lane_i == 3, rank2, 0.0)))).astype(jnp.int32)
    wt_ref[...] = jnp.where(lane_i == 0, w1, jnp.where(lane_i == 1, w2, 0.0))


def _router(x2, g, w_r, b_r, tm=256):
    n, d = x2.shape
    row = lambda w: pl.BlockSpec((tm, w), lambda i: (i, 0))
    return pl.pallas_call(
        _router_kernel,
        grid=(n // tm,),
        in_specs=[row(d), pl.BlockSpec((1, d), lambda i: (0, 0)), pl.BlockSpec((d, 2 * LANES), lambda i: (0, 0)),
                  pl.BlockSpec((1, LANES), lambda i: (0, 0))],
        out_specs=[row(d // 2), row(LANES), row(LANES), pl.BlockSpec((1, LANES), lambda i: (0, 0))],
        out_shape=[jax.ShapeDtypeStruct((n, d // 2), jnp.uint32), jax.ShapeDtypeStruct((n, LANES), jnp.int32),
                   jax.ShapeDtypeStruct((n, LANES), F32), jax.ShapeDtypeStruct((1, LANES), F32)],
        scratch_shapes=[pltpu.VMEM((1, LANES), F32)],
        compiler_params=_params("arbitrary"),
        name="moe_router",
    )(x2, g.reshape(1, d), w_r, b_r)


def _row_copy(src_hbm, row, dst, r, sem):
    return pltpu.make_async_copy(src_hbm.at[pl.ds(row, 1), :], dst.at[pl.ds(r, 1), :], sem)


def _sorted_rows(m):
    return m + N_EXPERTS * SUBLANES + ITEM_ROWS


def _dispatch_kernel(dest_ref, seg0_ref, cnt_ref, h_ref, x_hbm, zero_ref, sem, *, tm, m):
    i = pl.program_id(0)

    @pl.when(i == 0)
    def _():
        zero_ref[...] = jnp.zeros_like(zero_ref)
        tail = [pltpu.make_async_copy(zero_ref, x_hbm.at[pl.ds(r0, ITEM_SUB), :], sem)
                for r0 in range(m, _sorted_rows(m), ITEM_SUB)]
        for c in tail:
            c.start()
        for c in tail:
            c.wait()

        def fill_gap(e, total):
            gap = (-cnt_ref[e]) & (SUBLANES - 1)

            def one(r, c):
                _row_copy(zero_ref, 0, x_hbm, seg0_ref[e] + cnt_ref[e] + r, sem).start()
                return c

            lax.fori_loop(0, gap, one, 0)
            return total + gap

        n_gap = lax.fori_loop(0, N_EXPERTS, fill_gap, 0)

        def wait_gap(r, c):
            _row_copy(zero_ref, 0, x_hbm, 0, sem).wait()
            return c

        lax.fori_loop(0, n_gap, wait_gap, 0)

    def issue(r, c):
        t = i * tm + r
        for sl in range(TOPK_IN_GROUP):
            _row_copy(h_ref, r, x_hbm, dest_ref[TOPK_IN_GROUP * t + sl], sem).start()
        return c

    lax.fori_loop(0, tm, issue, 0, unroll=4)
    for _ in range(TOPK_IN_GROUP):
        pltpu.make_async_copy(h_ref, x_hbm.at[pl.ds(0, tm), :], sem).wait()


def _dispatch(dest, seg_start, counts, h3p, tm=256):
    n, c = h3p.shape
    m = dest.shape[0]
    assert (_sorted_rows(m) - m) % ITEM_SUB == 0
    grid_spec = pltpu.PrefetchScalarGridSpec(
        num_scalar_prefetch=3,
        grid=(n // tm,),
        in_specs=[pl.BlockSpec((tm, c), lambda i, d, s0, cn: (i, 0))],
        out_specs=pl.BlockSpec(memory_space=pl.ANY),
        scratch_shapes=[pltpu.VMEM((ITEM_SUB, c), h3p.dtype), pltpu.SemaphoreType.DMA(())],
    )
    return pl.pallas_call(
        functools.partial(_dispatch_kernel, tm=tm, m=m),
        grid_spec=grid_spec,
        out_shape=jax.ShapeDtypeStruct((_sorted_rows(m), c), h3p.dtype),
        compiler_params=_params("arbitrary"),
        name="moe_dispatch",
    )(dest, seg_start, counts, h3p)


def _expert_kernel(ie_ref, is_ref, in_ref, ni_ref, x_hbm, wg_ref, wu_ref, wd_ref, y_hbm,
                   xin_ref, xb_ref, acc_ref, yp_ref, in_sem, out_sem, *, m, nj):
    i = pl.program_id(0)
    j = pl.program_id(1)
    n_items = ni_ref[0]
    valid = i < n_items
    slot = lax.rem(i, 2)
    half = xb_ref.shape[1] // 2
    n_sub = ITEM_ROWS // ITEM_SUB

    def first_row(item):
        return pl.multiple_of(is_ref[item], SUBLANES)

    def blocks(item):
        return (in_ref[item] + ITEM_SUB - 1) // ITEM_SUB

    def each_block(item, fn):
        for sub in range(n_sub):
            pl.when(sub * ITEM_SUB < in_ref[item])(functools.partial(fn, sub))

    def in_copy(item, sub):
        rows = pl.ds(sub * ITEM_SUB, ITEM_SUB)
        return pltpu.make_async_copy(x_hbm.at[pl.ds(first_row(item) + sub * ITEM_SUB, ITEM_SUB), :],
                                     xin_ref.at[rows, :], in_sem)

    def out_copy(item, sl, sub):
        rows = pl.ds(sub * ITEM_SUB, ITEM_SUB)
        return pltpu.make_async_copy(yp_ref.at[sl, rows, :],
                                     y_hbm.at[pl.ds(first_row(item) + sub * ITEM_SUB, ITEM_SUB), :], out_sem.at[sl])

    def by_block_count(item, fn):
        for nb in range(1, n_sub + 1):
            pl.when(blocks(item) == nb)(functools.partial(fn, nb))

    @pl.when((i == 0) & (j == 0))
    def _():
        yp_ref[1, :ITEM_SUB, :] = jnp.zeros((ITEM_SUB, yp_ref.shape[2]), yp_ref.dtype)
        tail = [pltpu.make_async_copy(yp_ref.at[1, pl.ds(0, ITEM_SUB), :],
                                      y_hbm.at[pl.ds(r0, ITEM_SUB), :], out_sem.at[1])
                for r0 in range(m, _sorted_rows(m), ITEM_SUB)]
        for c in tail:
            c.start()
        for c in tail:
            c.wait()
        each_block(0, lambda sub: in_copy(0, sub).start())

    @pl.when(valid & (j == 0))
    def _():
        each_block(i, lambda sub: in_copy(i, sub).wait())

        def unpack(nb):
            rows = pl.ds(0, nb * ITEM_SUB)
            lo, hi = _unpack_bf16_pairs(xin_ref[rows, :])
            xb_ref[rows, :half] = lo
            xb_ref[rows, half:] = hi
            acc_ref[rows, :] = jnp.zeros((nb * ITEM_SUB, acc_ref.shape[1]), F32)

        by_block_count(i, unpack)

        @pl.when(i + 1 < n_items)
        def _():
            each_block(i + 1, lambda sub: in_copy(i + 1, sub).start())

    @pl.when(valid)
    def _():
        wg = wg_ref[...].astype(BF16)
        wu = wu_ref[...].astype(BF16)
        wd = wd_ref[...].astype(BF16)

        def mlp(nb):
            rows = pl.ds(0, nb * ITEM_SUB)
            xb = xb_ref[rows, :]
            g = jnp.dot(xb, wg, preferred_element_type=F32)
            u = jnp.dot(xb, wu, preferred_element_type=F32)
            hm = (g * jax.nn.sigmoid(g) * u).astype(BF16)
            acc_ref[rows, :] += jnp.dot(hm, wd, preferred_element_type=F32)

        by_block_count(i, mlp)

    @pl.when(valid & (j == nj - 1))
    def _():
        @pl.when(i > 0)
        def _():
            each_block(i - 1, lambda sub: out_copy(i - 1, 1 - slot, sub).wait())

        def pack(nb):
            rows = pl.ds(0, nb * ITEM_SUB)
            yp_ref[slot, rows, :] = _pack_bf16_pairs(acc_ref[rows, :])

        by_block_count(i, pack)
        each_block(i, lambda sub: out_copy(i, slot, sub).start())

        @pl.when(i == n_items - 1)
        def _():
            each_block(i, lambda sub: out_copy(i, slot, sub).wait())


def _experts(item_e, item_start, item_n, n_items, x_sorted, w_gate, w_up, w_down, max_items, m):
    d = w_gate.shape[1]
    nj = D_EXPERT // F_CHUNK

    def jj(i, j, ni):
        return jnp.where(i < ni[0], j, nj - 1)

    grid_spec = pltpu.PrefetchScalarGridSpec(
        num_scalar_prefetch=4,
        grid=(max_items, nj),
        in_specs=[pl.BlockSpec(memory_space=pl.ANY),
                  pl.BlockSpec((None, d, F_CHUNK), lambda i, j, ie, s, n, ni: (ie[i], 0, jj(i, j, ni))),
                  pl.BlockSpec((None, d, F_CHUNK), lambda i, j, ie, s, n, ni: (ie[i], 0, jj(i, j, ni))),
                  pl.BlockSpec((None, F_CHUNK, d), lambda i, j, ie, s, n, ni: (ie[i], jj(i, j, ni), 0))],
        out_specs=pl.BlockSpec(memory_space=pl.ANY),
        scratch_shapes=[pltpu.VMEM((ITEM_ROWS, d // 2), jnp.uint32), pltpu.VMEM((ITEM_ROWS, d), BF16),
                        pltpu.VMEM((ITEM_ROWS, d), F32), pltpu.VMEM((2, ITEM_ROWS, d // 2), jnp.uint32),
                        pltpu.SemaphoreType.DMA(()), pltpu.SemaphoreType.DMA((2,))],
    )
    return pl.pallas_call(
        functools.partial(_expert_kernel, m=m, nj=nj),
        grid_spec=grid_spec,
        out_shape=jax.ShapeDtypeStruct((_sorted_rows(m), d // 2), jnp.uint32),
        compiler_params=_params("arbitrary", "arbitrary"),
        name="moe_experts",
    )(item_e, item_start, item_n, n_items, x_sorted, w_gate, w_up, w_down)


def _combine_kernel(dest_ref, y_hbm, x_ref, wt_ref, g_ref, o_ref, yv_ref, sem, *, tm):
    i = pl.program_id(0)
    slot = lax.rem(i, 2)

    def issue_tile(tile, sl):
        def issue(r, c):
            base = TOPK_IN_GROUP * (tile * tm + r)
            for k in range(TOPK_IN_GROUP):
                _row_copy(y_hbm, dest_ref[base + k], yv_ref.at[sl, k], r, sem.at[sl]).start()
            return c

        lax.fori_loop(0, tm, issue, 0, unroll=4)

    @pl.when(i == 0)
    def _():
        issue_tile(0, 0)

    @pl.when(i + 1 < pl.num_programs(0))
    def _():
        issue_tile(i + 1, 1 - slot)

    for k in range(TOPK_IN_GROUP):
        pltpu.make_async_copy(y_hbm.at[pl.ds(0, tm), :], yv_ref.at[slot, k], sem.at[slot]).wait()
    w = wt_ref[...]
    half = x_ref.shape[1] // 2
    parts = [x_ref[:, :half], x_ref[:, half:]]
    for k in range(TOPK_IN_GROUP):
        lo, hi = _unpack_bf16_pairs(yv_ref[slot, k])
        parts = [parts[0] + lo.astype(F32) * w[:, k:k + 1], parts[1] + hi.astype(F32) * w[:, k:k + 1]]
    ms = (jnp.sum(parts[0] * parts[0], axis=-1, keepdims=True)
          + jnp.sum(parts[1] * parts[1], axis=-1, keepdims=True)) / x_ref.shape[1]
    r = lax.rsqrt(ms + EPS)
    o_ref[:, :half] = parts[0] * r * g_ref[:, :half]
    o_ref[:, half:] = parts[1] * r * g_ref[:, half:]


def _combine(dest, y_sorted, x2, wts, g_final, tm=128):
    n, d = x2.shape
    grid_spec = pltpu.PrefetchScalarGridSpec(
        num_scalar_prefetch=1,
        grid=(n // tm,),
        in_specs=[pl.BlockSpec(memory_space=pl.ANY),
                  pl.BlockSpec((tm, d), lambda i, dr: (i, 0)),
                  pl.BlockSpec((tm, LANES), lambda i, dr: (i, 0)),
                  pl.BlockSpec((1, d), lambda i, dr: (0, 0))],
        out_specs=pl.BlockSpec((tm, d), lambda i, dr: (i, 0)),
        scratch_shapes=[pltpu.VMEM((2, TOPK_IN_GROUP, tm, d // 2), y_sorted.dtype), pltpu.SemaphoreType.DMA((2,))],
    )
    return pl.pallas_call(
        functools.partial(_combine_kernel, tm=tm),
        grid_spec=grid_spec,
        out_shape=jax.ShapeDtypeStruct((n, d), F32),
        compiler_params=_params("arbitrary"),
        name="moe_combine_final_norm",
    )(dest, y_sorted, x2, wts, g_final.reshape(1, d))


def _pad_cols(w, width):
    return jnp.pad(w, ((0, 0), (0, width - w.shape[1])))


def _mixer(x2d, bsz, s, g_norm, w_in, b_f, g_q_lat, g_kv_lat, g_idx_k, b_idx_k, w_uq, w_idx_q, w_uk, w_uv,
           w_up_a, w_up_b, w_out):
    d = x2d.shape[1]
    topk = min(DSA_TOPK_MAX, s // 4)
    h1 = _rmsnorm(x2d, g_norm, BF16)
    c_small = 3 * FOX_WIDTH
    o_fa, o_cq = c_small, c_small + FOX_HEADS
    o_ckv = o_cq + Q_LORA
    o_kr = o_ckv + KV_LORA
    o_ik = o_kr + DSA_ROPE_DIM
    o_iw = o_ik + IDX_DIM
    o_ga = o_iw + IDX_HEADS
    o_gb = o_ga + d
    w_t = jnp.swapaxes(w_in, 0, 1)
    pad_rows = lambda w: jnp.pad(w, ((0, LANES - w.shape[0]), (0, 0)))
    w_small_t = jnp.concatenate([
        w_t[o_cq:o_ckv], w_t[o_ckv:o_kr], w_t[o_ik:o_iw],
        pad_rows(w_t[o_kr:o_ik]), pad_rows(w_t[o_fa:o_cq]), pad_rows(w_t[o_iw:o_ga])], axis=0)

    qkv = _mm(h1, w_t, 3 * FOX_WIDTH, w_rows=True, scaled_cols=(FOX_WIDTH, FOX_QSCALE), name="proj_qkv")
    small = _mm(h1, w_small_t, SMALL_W, out_dtype=F32, tm=512, tn=SMALL_W // 2, w_rows=True, name="proj_small")

    tabs64 = _rope_tables(s, DSA_ROPE_DIM)
    tabs128 = _rope_tables(s, IDX_DIM)
    cq_n, kv_cat, k_idx, w_idx, fa = _prep(small, g_q_lat, g_kv_lat, g_idx_k, b_idx_k, tabs64, tabs128, s)

    cum = _fox_cum(fa, _pad_cols(b_f.reshape(1, -1), LANES), bsz, s)
    att_a = _fox_attention(qkv, cum, bsz, s)

    w_uq3 = w_uq.reshape(Q_LORA, DSA_HEADS, DSA_NOPE_DIM + DSA_ROPE_DIM)
    w_uq_nope = w_uq3[:, :, :DSA_NOPE_DIM].reshape(Q_LORA, DSA_HEADS * DSA_NOPE_DIM).astype(BF16)
    w_uq_rope = jnp.pad(w_uq3[:, :, DSA_NOPE_DIM:], ((0, 0), (0, 0), (0, LANES - DSA_ROPE_DIM))).reshape(
        Q_LORA, DSA_HEADS * LANES).astype(BF16)
    q_cat = _q_latent(cq_n, w_uq_nope, w_uq_rope, w_uk.astype(BF16), tabs64, bsz, s)
    q_idx = _mm(cq_n, w_idx_q, IDX_HEADS * IDX_DIM, heads=(bsz, s), rope_tabs=tabs128, tn=1024,
                name="proj_q_idx")
    w_idx_t = jnp.transpose(w_idx.reshape(bsz, s, LANES)[:, :, :IDX_HEADS], (0, 2, 1))
    bias = _indexer_mask(q_idx, k_idx.reshape(bsz, s, IDX_DIM), w_idx_t, bsz, s, topk)
    kv3 = kv_cat.reshape(bsz, s, KV_PAD)
    kv_t = jnp.transpose(kv3.reshape(bsz, s // KEY_CHUNK, KEY_CHUNK, KV_PAD), (0, 1, 3, 2))
    o_lat = _dsa_attention(q_cat, kv_t, kv3, bias, bsz, s)
    att_b = _uv_project(o_lat, w_uv.astype(BF16), bsz, s)

    merged = _gated_merge(h1, att_a, att_b, w_t, o_ga, o_gb, w_up_a, w_up_b)
    return _mm(merged, w_out, d, out_dtype=F32, res=x2d, name="proj_out")


def _memory_layer(x1, mem, bsz, s, g_x, g_m, w_qm, w_km, w_vm, w_om):
    n_mem = mem.shape[1]
    d = x1.shape[1]
    hd = MEM_HEADS * MEM_HEAD_DIM
    mem_n = _rmsnorm(mem.reshape(bsz * n_mem, d), g_m, BF16)
    kv_m = _mm(mem_n, jnp.concatenate([w_km, w_vm], axis=1), 2 * hd, tm=bsz * n_mem, name="proj_kv_mem")
    km_t = jnp.transpose(kv_m[:, :hd].reshape(bsz, n_mem, hd), (0, 2, 1))
    return _mem_attention(x1, g_x, w_qm.astype(BF16), km_t, kv_m, w_om.astype(BF16), bsz, s, n_mem)


def _moe_layer(x2, g_ffn, w_rg, b_rg, w_re, b_re, w_gate, w_up, w_down, g_final):
    n, d = x2.shape
    m = n * TOPK_IN_GROUP
    w_r = _pad_cols(jnp.concatenate([w_re, w_rg], axis=1), LANES)
    w_r_hi = w_r.astype(BF16)
    w_r_split = jnp.concatenate([w_r_hi, (w_r - w_r_hi.astype(F32)).astype(BF16)], axis=1)
    b_r = _pad_cols(jnp.concatenate([b_re, b_rg]).reshape(1, -1), LANES)
    h3p, ridx, wts, cnt = _router(x2, g_ffn, w_r_split, b_r)
    counts = cnt[0, :N_EXPERTS].astype(jnp.int32)
    e_flat = ridx[:, :TOPK_IN_GROUP].reshape(m)
    rank_flat = ridx[:, TOPK_IN_GROUP:2 * TOPK_IN_GROUP].reshape(m)
    seg = (counts + SUBLANES - 1) // SUBLANES * SUBLANES
    starts = (jnp.cumsum(seg) - seg).astype(jnp.int32)
    n_it = (counts + ITEM_ROWS - 1) // ITEM_ROWS
    it_end = jnp.cumsum(n_it)
    it_first = it_end - n_it
    max_items = (m + N_EXPERTS * (ITEM_ROWS - 1)) // ITEM_ROWS
    t = jnp.arange(max_items, dtype=jnp.int32)
    item_e = jnp.minimum(jnp.sum(it_end[None, :] <= t[:, None], axis=1), N_EXPERTS - 1).astype(jnp.int32)
    k_in = t - it_first[item_e]
    item_start = (starts[item_e] + k_in * ITEM_ROWS).astype(jnp.int32)
    item_n = jnp.clip(counts[item_e] - k_in * ITEM_ROWS, 0, ITEM_ROWS).astype(jnp.int32)
    n_items = it_end[-1:].astype(jnp.int32)
    last = jnp.maximum(n_items[0] - 1, 0)
    item_e = jnp.where(t < n_items[0], item_e, item_e[last])
    item_start = jnp.where(t < n_items[0], item_start, 0)
    item_n = jnp.where(t < n_items[0], item_n, 0)
    e_ids = jnp.arange(N_EXPERTS, dtype=jnp.int32)
    dest = rank_flat + jnp.sum(jnp.where(e_flat[:, None] == e_ids[None, :], starts[None, :], 0), axis=1)
    x_sorted = _dispatch(dest, starts, counts, h3p)
    y_sorted = _experts(item_e, item_start, item_n, n_items, x_sorted, w_gate, w_up, w_down, max_items, m)
    return _combine(dest, y_sorted, x2, wts, g_final)


def kernel(x, mem, g_norm_mix, w_in, b_f, g_q_lat, g_kv_lat, g_idx_k, b_idx_k, w_uq, w_idx_q, w_uk, w_uv,
           w_up_a, w_up_b, w_out, g_norm_mem_x, g_mem, w_qm, w_km, w_vm, w_om, g_norm_ffn,
           w_rg, b_rg, w_re, b_re, w_gate, w_up, w_down, g_final):
    bsz, s, d = x.shape
    depth = w_in.shape[0]
    assert depth == 1, "the final norm is fused into the last MoE combine"
    l = 0
    x2d = x.reshape(bsz * s, d)
    x1 = _mixer(x2d, bsz, s, g_norm_mix[l], w_in[l], b_f[l], g_q_lat[l], g_kv_lat[l], g_idx_k[l], b_idx_k[l],
                w_uq[l], w_idx_q[l], w_uk[l], w_uv[l], w_up_a[l], w_up_b[l], w_out[l])
    x2 = _memory_layer(x1, mem, bsz, s, g_norm_mem_x[l], g_mem[l], w_qm[l], w_km[l], w_vm[l], w_om[l])
    out = _moe_layer(x2, g_norm_ffn[l], w_rg[l], b_rg[l], w_re[l], b_re[l], w_gate[l], w_up[l], w_down[l], g_final)
    return out.reshape(bsz, s, d)
```

```python
import functools

import jax
import jax.numpy as jnp
from jax import lax
from jax.experimental import pallas as pl
from jax.experimental.pallas import tpu as pltpu

F32 = jnp.float32
BF16 = jnp.bfloat16

EPS = 1e-6
ROPE_THETA = 10000.0
FOX_HEADS = 16
FOX_HEAD_DIM = 128
DSA_HEADS = 16
DSA_NOPE_DIM = 128
DSA_ROPE_DIM = 64
DSA_V_DIM = 128
Q_LORA = 1024
KV_LORA = 256
IDX_HEADS = 32
IDX_DIM = 128
DSA_TOPK_MAX = 256
MEM_HEADS = 4
MEM_HEAD_DIM = 128
N_GROUPS = 8
EXPERTS_PER_GROUP = 8
N_EXPERTS = N_GROUPS * EXPERTS_PER_GROUP
TOPK_IN_GROUP = 2
D_EXPERT = 512

FOX_WIDTH = FOX_HEADS * FOX_HEAD_DIM
LANES = 128
SUBLANES = 8
KV_PAD = KV_LORA + LANES
NEG_BIG = -1e30
LOG2E = 1.4426950408889634
FOX_QSCALE = FOX_HEAD_DIM ** -0.5 * LOG2E
FOX_HB = 4
DSA_QSCALE = (DSA_NOPE_DIM + DSA_ROPE_DIM) ** -0.5 * LOG2E
VMEM_LIMIT_BYTES = 56 * 1024 * 1024
ITEM_ROWS = 384
ITEM_SUB = 128
F_CHUNK = 256
KEY_CHUNK = 256


def _params(*sem):
    return pltpu.CompilerParams(dimension_semantics=sem, vmem_limit_bytes=VMEM_LIMIT_BYTES)


def _rmsnorm_kernel(x_ref, g_ref, o_ref):
    x = x_ref[...].astype(F32)
    ms = jnp.mean(x * x, axis=-1, keepdims=True)
    o_ref[...] = (x * lax.rsqrt(ms + EPS) * g_ref[...]).astype(o_ref.dtype)


def _rmsnorm(x2d, g, out_dtype, tm=512):
    n, d = x2d.shape
    tm = min(tm, n)
    assert n % tm == 0
    return pl.pallas_call(
        _rmsnorm_kernel,
        grid=(n // tm,),
        in_specs=[pl.BlockSpec((tm, d), lambda i: (i, 0)), pl.BlockSpec((1, d), lambda i: (0, 0))],
        out_specs=pl.BlockSpec((tm, d), lambda i: (i, 0)),
        out_shape=jax.ShapeDtypeStruct((n, d), out_dtype),
        compiler_params=_params("arbitrary"),
        name="rmsnorm",
    )(x2d, g.reshape(1, d).astype(F32))


_NT_DIMS = (((1,), (1,)), ((), ()))


def _w_rows_spec(tn, k, row0, index_of_j):
    if row0 % tn == 0:
        return pl.BlockSpec((tn, k), lambda *g: (index_of_j(*g) + row0 // tn, 0))
    assert row0 % SUBLANES == 0
    return pl.BlockSpec((pl.Element(tn), pl.Element(k)),
                        lambda *g: (pl.multiple_of(index_of_j(*g) * tn + row0, SUBLANES), 0))


def _mm_kernel(a_ref, b_ref, *rest, has_res, head_out, rope, w_rows, col_scale):
    pos = 0
    res_ref = cos_ref = sin_ref = None
    if has_res:
        res_ref = rest[pos]
        pos += 1
    if rope:
        cos_ref, sin_ref = rest[pos], rest[pos + 1]
        pos += 2
    o_ref = rest[pos]
    if b_ref.dtype == BF16:
        wb_ref = b_ref
    else:
        wb_ref = rest[pos + 1]

        @pl.when(pl.program_id(1) == 0)
        def _():
            wb_ref[...] = b_ref[...].astype(BF16)

    if w_rows:
        acc = lax.dot_general(a_ref[...], wb_ref[...], _NT_DIMS, preferred_element_type=F32)
    else:
        acc = jnp.dot(a_ref[...], wb_ref[...], preferred_element_type=F32)
    if col_scale is not None:
        acc = acc * jnp.where(pl.program_id(0) < col_scale[0], col_scale[1], 1.0)
    if has_res:
        acc = acc + res_ref[...]
    if head_out:
        for hh in range(acc.shape[1] // LANES):
            part = acc[:, hh * LANES:(hh + 1) * LANES]
            if rope:
                part = part * cos_ref[...] + pltpu.roll(part, LANES // 2, 1) * sin_ref[...]
            o_ref[hh] = part.astype(o_ref.dtype)
    else:
        o_ref[...] = acc.astype(o_ref.dtype)


def _mm(a, w, n_cols, *, col0=0, out_dtype=BF16, tm=1024, tn=512, res=None, heads=None, rope_tabs=None,
        w_rows=False, scaled_cols=None, name="mm"):
    m, k = a.shape
    tm = min(tm, m if heads is None else heads[1])
    assert n_cols % tn == 0 and m % tm == 0
    if w_rows:
        w_spec = _w_rows_spec(tn, k, col0, lambda j, i: j)
    else:
        assert col0 % tn == 0
        w_spec = pl.BlockSpec((k, tn), lambda j, i: (0, j + col0 // tn))
    in_specs = [pl.BlockSpec((tm, k), lambda j, i: (i, 0)), w_spec]
    args = [a, w]
    if res is not None:
        in_specs.append(pl.BlockSpec((tm, tn), lambda j, i: (i, j)))
        args.append(res)
    if heads is not None:
        bsz, s = heads
        assert s % tm == 0
        spt = s // tm
        if rope_tabs is not None:
            in_specs += [pl.BlockSpec((tm, LANES), lambda j, i: (i % spt, 0))] * 2
            args += list(rope_tabs)
        hpt = tn // LANES
        out_spec = pl.BlockSpec((None, hpt, tm, LANES), lambda j, i: (i // spt, j, i % spt, 0))
        out_shape = jax.ShapeDtypeStruct((bsz, n_cols // LANES, s, LANES), out_dtype)
    else:
        out_spec = pl.BlockSpec((tm, tn), lambda j, i: (i, j))
        out_shape = jax.ShapeDtypeStruct((m, n_cols), out_dtype)
    col_scale = None
    if scaled_cols is not None:
        assert scaled_cols[0] % tn == 0
        col_scale = (scaled_cols[0] // tn, scaled_cols[1])
    kern = functools.partial(_mm_kernel, has_res=res is not None, head_out=heads is not None,
                             rope=rope_tabs is not None, w_rows=w_rows, col_scale=col_scale)
    return pl.pallas_call(
        kern,
        grid=(n_cols // tn, m // tm),
        in_specs=in_specs,
        out_specs=out_spec,
        out_shape=out_shape,
        scratch_shapes=[] if w.dtype == BF16 else [pltpu.VMEM((tn, k) if w_rows else (k, tn), BF16)],
        compiler_params=_params("arbitrary", "arbitrary"),
        name=name,
    )(*args)


def _rope_tables(s, d):
    inv = jnp.power(ROPE_THETA, -jnp.arange(0, d, 2, dtype=F32) / d)
    ang = jnp.arange(s, dtype=jnp.int32).astype(F32)[:, None] * inv[None, :]
    cos, sin = jnp.cos(ang), jnp.sin(ang)
    cos_t = jnp.tile(jnp.concatenate([cos, cos], axis=-1), (1, LANES // d))
    sin_t = jnp.tile(jnp.concatenate([-sin, sin], axis=-1), (1, LANES // d))
    return cos_t, sin_t


def _rot_half64(x):
    lane = lax.broadcasted_iota(jnp.int32, x.shape, 1)
    return jnp.where((lane % 64) < 32, pltpu.roll(x, 96, 1), pltpu.roll(x, 32, 1))


_C_CQ = 0
_C_CKV = Q_LORA
_C_IK = _C_CKV + KV_LORA
_C_KR = _C_IK + IDX_DIM
_C_FA = _C_KR + LANES
_C_IW = _C_FA + LANES
SMALL_W = _C_IW + LANES


def _prep_kernel(s_ref, gq_ref, gkv_ref, gik_ref, bik_ref, c64_ref, s64_ref, c128_ref, s128_ref,
                 cq_ref, kv_ref, kidx_ref, widx_ref, fa_ref):
    cq = s_ref[:, _C_CQ:_C_CQ + Q_LORA]
    cq_ref[...] = (cq * lax.rsqrt(jnp.mean(cq * cq, axis=-1, keepdims=True) + EPS) * gq_ref[...]).astype(BF16)
    ckv = s_ref[:, _C_CKV:_C_CKV + KV_LORA]
    kv_ref[:, :KV_LORA] = (ckv * lax.rsqrt(jnp.mean(ckv * ckv, axis=-1, keepdims=True) + EPS)
                           * gkv_ref[...]).astype(BF16)
    kr = s_ref[:, _C_KR:_C_KR + LANES]
    kv_ref[:, KV_LORA:] = (kr * c64_ref[...] + _rot_half64(kr) * s64_ref[...]).astype(BF16)
    ik = s_ref[:, _C_IK:_C_IK + IDX_DIM]
    mu = jnp.mean(ik, axis=-1, keepdims=True)
    xc = ik - mu
    ikn = xc * lax.rsqrt(jnp.mean(xc * xc, axis=-1, keepdims=True) + EPS) * gik_ref[...] + bik_ref[...]
    kidx_ref[...] = (ikn * c128_ref[...] + pltpu.roll(ikn, LANES // 2, 1) * s128_ref[...]).astype(BF16)
    widx_ref[...] = s_ref[:, _C_IW:_C_IW + LANES] * (IDX_HEADS * IDX_DIM) ** -0.5
    fa_ref[...] = s_ref[:, _C_FA:_C_FA + LANES]


def _prep(small, g_q, g_kv, g_ik, b_ik, tabs64, tabs128, s, tm=256):
    n = small.shape[0]
    spt = s // tm
    row = lambda w: pl.BlockSpec((tm, w), lambda i: (i, 0))
    vec = lambda w: pl.BlockSpec((1, w), lambda i: (0, 0))
    tab = pl.BlockSpec((tm, LANES), lambda i: (i % spt, 0))
    return pl.pallas_call(
        _prep_kernel,
        grid=(n // tm,),
        in_specs=[row(SMALL_W), vec(Q_LORA), vec(KV_LORA), vec(IDX_DIM), vec(IDX_DIM), tab, tab, tab, tab],
        out_specs=[row(Q_LORA), row(KV_PAD), row(IDX_DIM), row(LANES), row(LANES)],
        out_shape=[jax.ShapeDtypeStruct((n, Q_LORA), BF16), jax.ShapeDtypeStruct((n, KV_PAD), BF16),
                   jax.ShapeDtypeStruct((n, IDX_DIM), BF16), jax.ShapeDtypeStruct((n, LANES), F32),
                   jax.ShapeDtypeStruct((n, LANES), F32)],
        compiler_params=_params("arbitrary"),
        name="mixer_prep",
    )(small, g_q.reshape(1, -1), g_kv.reshape(1, -1), g_ik.reshape(1, -1), b_ik.reshape(1, -1),
      tabs64[0], tabs64[1], tabs128[0], tabs128[1])


def _cum_kernel(fa_ref, bf_ref, cum_ref, carry_ref):
    @pl.when(pl.program_id(1) == 0)
    def _():
        carry_ref[...] = jnp.zeros_like(carry_ref)

    z = fa_ref[...] + bf_ref[...]
    logf = jnp.minimum(z, 0.0) - jnp.log1p(jnp.exp(-jnp.abs(z)))
    t = z.shape[0]
    tri = (lax.broadcasted_iota(jnp.int32, (t, t), 0) >= lax.broadcasted_iota(jnp.int32, (t, t), 1)).astype(F32)
    c = jnp.dot(tri, logf, precision=lax.Precision.HIGHEST, preferred_element_type=F32) + carry_ref[...]
    cum_ref[...] = c
    carry_ref[...] = c[t - 1:t, :]


def _fox_cum(fa, b_f_pad, bsz, s, t=256):
    return pl.pallas_call(
        _cum_kernel,
        grid=(bsz, s // t),
        in_specs=[pl.BlockSpec((None, t, LANES), lambda b, i: (b, i, 0)),
                  pl.BlockSpec((1, LANES), lambda b, i: (0, 0))],
        out_specs=pl.BlockSpec((None, t, LANES), lambda b, i: (b, i, 0)),
        out_shape=jax.ShapeDtypeStruct((bsz, s, LANES), F32),
        scratch_shapes=[pltpu.VMEM((1, LANES), F32)],
        compiler_params=_params("arbitrary", "arbitrary"),
        name="fox_cumsum",
    )(fa.reshape(bsz, s, LANES), b_f_pad)


def _lane_fold(x, op):
    out = x[:, :LANES]
    for c in range(1, x.shape[1] // LANES):
        out = op(out, x[:, c * LANES:(c + 1) * LANES])
    return out


def _split3(x):
    t1 = x.astype(BF16)
    r = x - t1.astype(F32)
    t2 = r.astype(BF16)
    t3 = (r - t2.astype(F32)).astype(BF16)
    return t1, t2, t3


def _gate_lanes(parts, own_first):
    t1, t2, t3 = parts
    lane = lax.broadcasted_iota(jnp.int32, t1.shape, 1)
    a, b = (0, 3) if own_first else (3, 0)
    ones3 = jnp.where((lane >= b) & (lane < b + 3), 1.0, 0.0)
    out = jnp.where(lane == a, t1, jnp.where(lane == a + 1, t2, jnp.where(lane == a + 2, t3, ones3)))
    return out.astype(BF16)


def _fox_kernel(q_ref, k_ref, v_ref, cum_ref, o_ref, kf_ref, vf_ref, s_ref, *, t):
    hp = pl.program_id(1)
    i = pl.program_id(2)
    d = FOX_HEAD_DIM
    heads = range(FOX_HB)
    row = lax.broadcasted_iota(jnp.int32, (LANES, LANES), 0)

    def gate_columns(c):
        terms = _split3(c * LOG2E)
        out = []
        for hh in heads:
            pick = (row == hp * FOX_HB + hh).astype(BF16)
            out.append([jnp.dot(term, pick, preferred_element_type=F32) for term in terms])
        return out

    @pl.when(i == 0)
    def _():
        ck = gate_columns(-cum_ref[...])
        for hh in heads:
            kf_ref[hh, :, :d] = k_ref[:, hh * d:(hh + 1) * d]
            kf_ref[hh, :, d:] = _gate_lanes(ck[hh], True)
            vf_ref[hh, :, :d] = v_ref[:, hh * d:(hh + 1) * d]
            vf_ref[hh, :, d:] = jnp.ones((vf_ref.shape[1], LANES), BF16)

    cq = gate_columns(cum_ref[pl.ds(pl.multiple_of(i * t, t), t), :])
    qs = [jnp.concatenate([q_ref[:, hh * d:(hh + 1) * d], _gate_lanes(cq[hh], False)], axis=1) for hh in heads]

    def logits(hh, kc):
        kb = kf_ref[hh, pl.ds(pl.multiple_of(kc * t, t), t), :]
        return lax.dot_general(qs[hh], kb, _NT_DIMS, preferred_element_type=F32)

    def pass1(kc, m_lanes):
        out = []
        for hh in heads:
            s = logits(hh, kc)
            s_ref[hh, kc] = s
            out.append(jnp.maximum(m_lanes[hh], _lane_fold(s, jnp.maximum)))
        return tuple(out)

    m_lanes = lax.fori_loop(0, i, pass1, tuple(jnp.full((t, LANES), NEG_BIG, F32) for _ in heads))
    causal = lax.broadcasted_iota(jnp.int32, (t, t), 0) >= lax.broadcasted_iota(jnp.int32, (t, t), 1)
    ms = []
    for hh in heads:
        s = jnp.where(causal, logits(hh, i), NEG_BIG)
        s_ref[hh, i] = s
        m_lane = jnp.maximum(m_lanes[hh], _lane_fold(s, jnp.maximum))
        m = jnp.broadcast_to(jnp.max(m_lane, axis=1, keepdims=True), (t, LANES))
        ms.append(jnp.concatenate([m] * (t // LANES), axis=1))

    def pass2(kc, accs):
        out = []
        for hh in heads:
            p = jnp.exp2(s_ref[hh, kc] - ms[hh])
            vb = vf_ref[hh, pl.ds(pl.multiple_of(kc * t, t), t), :]
            out.append(accs[hh] + jnp.dot(p.astype(BF16), vb, preferred_element_type=F32))
        return tuple(out)

    accs = lax.fori_loop(0, i + 1, pass2, tuple(jnp.zeros((t, d + LANES), F32) for _ in heads))
    for hh in heads:
        o_ref[:, hh * d:(hh + 1) * d] = (accs[hh][:, :d] / accs[hh][:, d:]).astype(o_ref.dtype)


def _fox_attention(qkv, cum, bsz, s, t=512):
    n = bsz * s
    nq = s // t
    hp = FOX_HEADS // FOX_HB
    w = FOX_HB * LANES
    assert FOX_HEAD_DIM == LANES and FOX_HEADS % FOX_HB == 0
    kern = functools.partial(_fox_kernel, t=t)
    return pl.pallas_call(
        kern,
        grid=(bsz, hp, nq),
        in_specs=[pl.BlockSpec((t, w), lambda b, h, i: (b * nq + i, h)),
                  pl.BlockSpec((s, w), lambda b, h, i: (b, hp + h)),
                  pl.BlockSpec((s, w), lambda b, h, i: (b, 2 * hp + h)),
                  pl.BlockSpec((None, s, LANES), lambda b, h, i: (b, 0, 0))],
        out_specs=pl.BlockSpec((t, w), lambda b, h, i: (b * nq + i, h)),
        out_shape=jax.ShapeDtypeStruct((n, FOX_WIDTH), BF16),
        scratch_shapes=[pltpu.VMEM((FOX_HB, s, 2 * LANES), BF16), pltpu.VMEM((FOX_HB, s, 2 * LANES), BF16),
                        pltpu.VMEM((FOX_HB, nq, t, t), F32)],
        compiler_params=_params("arbitrary", "arbitrary", "arbitrary"),
        name="fox_attention",
    )(qkv, qkv, qkv, cum)


def _qlat_kernel(cq_ref, wn_ref, wr_ref, wuk_ref, c64_ref, s64_ref, o_ref):
    cq = cq_ref[...]
    qn = jnp.dot(cq, wn_ref[...], preferred_element_type=F32).astype(BF16)
    qr = jnp.dot(cq, wr_ref[...], preferred_element_type=F32)
    for h in range(DSA_HEADS):
        cols = slice(h * LANES, (h + 1) * LANES)
        lat = jnp.dot(qn[:, cols], wuk_ref[h], preferred_element_type=F32)
        o_ref[h, :, :KV_LORA] = (lat * DSA_QSCALE).astype(BF16)
        r = qr[:, cols]
        o_ref[h, :, KV_LORA:] = ((r * c64_ref[...] + _rot_half64(r) * s64_ref[...]) * DSA_QSCALE).astype(BF16)


def _q_latent(cq_n, w_nope_b, w_rope_b, w_uk_b, tabs64, bsz, s, tm=512):
    spt = s // tm
    kq = cq_n.shape[1]
    wide = DSA_HEADS * LANES
    return pl.pallas_call(
        _qlat_kernel,
        grid=(bsz * spt,),
        in_specs=[pl.BlockSpec((tm, kq), lambda i: (i, 0)),
                  pl.BlockSpec((kq, wide), lambda i: (0, 0)),
                  pl.BlockSpec((kq, wide), lambda i: (0, 0)),
                  pl.BlockSpec((DSA_HEADS, DSA_NOPE_DIM, KV_LORA), lambda i: (0, 0, 0)),
                  pl.BlockSpec((tm, LANES), lambda i: (i % spt, 0)),
                  pl.BlockSpec((tm, LANES), lambda i: (i % spt, 0))],
        out_specs=pl.BlockSpec((None, DSA_HEADS, tm, KV_PAD), lambda i: (i // spt, 0, i % spt, 0)),
        out_shape=jax.ShapeDtypeStruct((bsz, DSA_HEADS, s, KV_PAD), BF16),
        compiler_params=_params("arbitrary"),
        name="q_latent",
    )(cq_n, w_nope_b, w_rope_b, w_uk_b, tabs64[0], tabs64[1])


def _needed_chunks(i, tq):
    return ((i + 1) * tq + KEY_CHUNK - 1) // KEY_CHUNK


BISECT_MAX_STEPS = 320
BISECT_UNROLL = 4


def _indexer_kernel(q_ref, k_ref, wt_ref, o_ref, sc_ref, thr_ref, cut_ref, *, tq, topk, hg):
    i = pl.program_id(1)
    tk = KEY_CHUNK
    n_chunks = sc_ref.shape[0]
    nk = _needed_chunks(i, tq)
    qpos = i * tq + lax.broadcasted_iota(jnp.int32, (tk, tq), 1)
    krow = lax.broadcasted_iota(jnp.int32, (tk, tq), 0)
    inf = jnp.inf

    def score_chunk(kc, c):
        k0 = pl.multiple_of(kc * tk, tk)
        kb = k_ref[pl.ds(k0, tk), :]
        sc = jnp.zeros((tk, tq), F32)
        for g in range(IDX_HEADS // hg):
            q2 = q_ref[g * hg:(g + 1) * hg].reshape(hg * tq, IDX_DIM)
            d = lax.dot_general(kb, q2, _NT_DIMS, preferred_element_type=F32)
            for hh in range(hg):
                h = g * hg + hh
                sc = sc + wt_ref[h:h + 1, :] * jnp.maximum(d[:, hh * tq:(hh + 1) * tq], 0.0)
        sc_ref[kc] = jnp.where(k0 + krow <= qpos, sc, -inf)
        return c

    lax.fori_loop(0, nk, score_chunk, 0)

    def pad_chunk(kc, c):
        sc_ref[kc] = jnp.full((tk, tq), -inf, F32)
        return c

    lax.fori_loop(nk, n_chunks, pad_chunk, 0)
    kf = float(topk)
    n_causal = (i * tq + lax.broadcasted_iota(jnp.int32, (1, tq), 1) + 1).astype(F32)

    def fold(x, op):
        return op(x.reshape(tk // SUBLANES, SUBLANES, tq), axis=0)

    def key_pos(kc):
        return (kc * tk + krow).astype(F32)

    def threshold(n_counted):
        def count_ge(t):
            acc = jnp.zeros((SUBLANES, tq), F32)
            for kc in range(n_counted):
                acc = acc + fold(jnp.where(sc_ref[kc] >= t, 1.0, 0.0), jnp.sum)
            return jnp.sum(acc, axis=0, keepdims=True)

        hi = jnp.full((SUBLANES, tq), -inf, F32)
        lo = jnp.full((SUBLANES, tq), inf, F32)
        for kc in range(n_counted):
            s = sc_ref[kc]
            hi = jnp.maximum(hi, fold(s, jnp.max))
            lo = jnp.minimum(lo, fold(jnp.where(s == -inf, inf, s), jnp.min))
        hi = jnp.max(hi, axis=0, keepdims=True)
        lo = jnp.min(lo, axis=0, keepdims=True)
        few = n_causal <= kf
        n_max = count_ge(hi)
        at_max = n_max >= kf
        lo = jnp.where(few, -inf, jnp.where(at_max, hi, lo))
        n_lo = jnp.where(at_max, n_max, n_causal)
        done = jnp.where(few | at_max, 1.0, 0.0)

        def unsettled(state):
            it, _, _, _, done = state
            return (it < BISECT_MAX_STEPS) & (jnp.sum(1.0 - done) > 0.0)

        def step(state):
            it, lo, hi, n_lo, done = state
            for _ in range(BISECT_UNROLL):
                mid = 0.5 * lo + 0.5 * hi
                cnt = count_ge(mid)
                ge = cnt >= kf
                closed = (mid <= lo) | (mid >= hi)
                live = done == 0.0
                lo = jnp.where(live & ge, mid, lo)
                n_lo = jnp.where(live & ge, cnt, n_lo)
                hi = jnp.where(live & jnp.logical_not(ge), mid, hi)
                done = jnp.where(live & ((ge & (cnt == kf)) | closed), 1.0, done)
            return it + BISECT_UNROLL, lo, hi, n_lo, done

        _, lo, _, n_lo, _ = lax.while_loop(unsettled, step, (jnp.int32(0), lo, hi, n_lo, done))
        thr_ref[...] = lo
        cut_ref[...] = jnp.full((1, tq), float(n_chunks * tk), F32)

        over = jnp.where((n_lo > kf) & jnp.logical_not(few), 1.0, 0.0)

        @pl.when(jnp.sum(over) > 0.0)
        def _():
            def count_where(pred):
                def body(kc, acc):
                    hit = pred(sc_ref[kc], key_pos(kc))
                    return acc + fold(jnp.where(hit, 1.0, 0.0), jnp.sum)

                acc = lax.fori_loop(0, n_counted, body, jnp.zeros((SUBLANES, tq), F32))
                return jnp.sum(acc, axis=0, keepdims=True)

            need = kf - count_where(lambda s, kp: s > lo)

            def narrow(_, bracket):
                lo_p, hi_p = bracket
                mid = jnp.floor(0.5 * (lo_p + hi_p))
                ok = count_where(lambda s, kp: (s == lo) & (kp <= mid)) >= need
                return jnp.where(ok, lo_p, mid), jnp.where(ok, mid, hi_p)

            n_pos = n_counted * tk
            _, hi_p = lax.fori_loop(0, n_pos.bit_length(), narrow,
                                    (jnp.full((1, tq), -1.0, F32), jnp.full((1, tq), n_pos - 1.0, F32)))
            cut_ref[...] = jnp.where(over > 0.0, hi_p, cut_ref[...])

    prev = 0
    for level in sorted({max(n_chunks // 4, 1), max(n_chunks // 2, 1), n_chunks}):
        pl.when((nk > prev) & (nk <= level))(functools.partial(threshold, level))
        prev = level
    thr = thr_ref[...]
    cut = cut_ref[...]

    def write_chunk(kc, c):
        s = sc_ref[kc]
        keep = ((s > thr) | ((s == thr) & (key_pos(kc) <= cut))) & (kc * tk + krow <= qpos)
        o_ref[kc] = jnp.where(keep, 0.0, NEG_BIG).T.astype(o_ref.dtype)
        return c

    lax.fori_loop(0, nk, write_chunk, 0)

    def fill_chunk(kc, c):
        o_ref[kc] = jnp.full((tq, tk), NEG_BIG, o_ref.dtype)
        return c

    lax.fori_loop(nk, n_chunks, fill_chunk, 0)


def _indexer_mask(q_idx, k_idx, w_idx_t, bsz, s, topk, tq=128, hg=4):
    kern = functools.partial(_indexer_kernel, tq=tq, topk=topk, hg=hg)
    nc = s // KEY_CHUNK
    return pl.pallas_call(
        kern,
        grid=(bsz, s // tq),
        in_specs=[pl.BlockSpec((None, IDX_HEADS, tq, IDX_DIM), lambda b, i: (b, 0, i, 0)),
                  pl.BlockSpec((None, s, IDX_DIM), lambda b, i: (b, 0, 0)),
                  pl.BlockSpec((None, IDX_HEADS, tq), lambda b, i: (b, 0, i))],
        out_specs=pl.BlockSpec((None, nc, tq, KEY_CHUNK), lambda b, i: (b, 0, i, 0)),
        out_shape=jax.ShapeDtypeStruct((bsz, nc, s, KEY_CHUNK), BF16),
        scratch_shapes=[pltpu.VMEM((nc, KEY_CHUNK, tq), F32), pltpu.VMEM((1, tq), F32),
                        pltpu.VMEM((1, tq), F32)],
        compiler_params=_params("arbitrary", "arbitrary"),
        name="indexer_topk_mask",
    )(q_idx, k_idx, w_idx_t)


def _dsa_kernel(q_ref, kvt_ref, kv_ref, bias_ref, o_ref, s_ref, m_ref, l_ref, acc_ref, *, tq):
    i = pl.program_id(1)
    tk = KEY_CHUNK
    rows = DSA_HEADS * tq
    nk = _needed_chunks(i, tq)
    q2 = q_ref[...].reshape(rows, KV_PAD)
    m_ref[...] = jnp.full(m_ref.shape, NEG_BIG, F32)

    n_pairs = (nk + 1) // 2

    def logits(kc):
        sc = jnp.dot(q2, kvt_ref[kc], preferred_element_type=F32)
        sc = (sc.reshape(DSA_HEADS, tq, tk) + bias_ref[kc].astype(F32)[None]).reshape(rows, tk)
        s_ref[kc] = sc
        return _lane_fold(sc, jnp.maximum)

    def pass1(pr, c):
        m_ref[...] = jnp.maximum(m_ref[...], jnp.maximum(logits(2 * pr), logits(2 * pr + 1)))
        return c

    lax.fori_loop(0, n_pairs, pass1, 0)
    m_ref[...] = jnp.broadcast_to(jnp.max(m_ref[...], axis=1, keepdims=True), m_ref.shape)
    l_ref[...] = jnp.zeros(l_ref.shape, F32)
    acc_ref[...] = jnp.zeros(acc_ref.shape, F32)

    def probs(kc):
        m = m_ref[...]
        p = jnp.concatenate([jnp.exp2(s_ref[kc, :, g * LANES:(g + 1) * LANES] - m)
                             for g in range(tk // LANES)], axis=1)
        kvb = kv_ref[pl.ds(pl.multiple_of(kc * tk, tk), tk), :KV_LORA]
        return _lane_fold(p, jnp.add), jnp.dot(p.astype(BF16), kvb, preferred_element_type=F32)

    def pass2(pr, c):
        l0, a0 = probs(2 * pr)
        l1, a1 = probs(2 * pr + 1)
        l_ref[...] += l0 + l1
        acc_ref[...] += a0 + a1
        return c

    lax.fori_loop(0, n_pairs, pass2, 0)
    l = jnp.dot(l_ref[...], jnp.ones((LANES, LANES), F32), precision=lax.Precision.HIGHEST,
                preferred_element_type=F32)
    o = acc_ref[...] / jnp.concatenate([l] * (KV_LORA // LANES), axis=1)
    o_ref[...] = o.reshape(DSA_HEADS, tq, KV_LORA).astype(o_ref.dtype)


def _dsa_attention(q_cat, kv_t, kv, bias, bsz, s, tq=128):
    kern = functools.partial(_dsa_kernel, tq=tq)
    nc = s // KEY_CHUNK
    assert nc % 2 == 0, "the kernel walks key chunks in pairs"
    rows = DSA_HEADS * tq
    return pl.pallas_call(
        kern,
        grid=(bsz, s // tq),
        in_specs=[pl.BlockSpec((None, DSA_HEADS, tq, KV_PAD), lambda b, i: (b, 0, i, 0)),
                  pl.BlockSpec((None, nc, KV_PAD, KEY_CHUNK), lambda b, i: (b, 0, 0, 0)),
                  pl.BlockSpec((None, s, KV_PAD), lambda b, i: (b, 0, 0)),
                  pl.BlockSpec((None, nc, tq, KEY_CHUNK), lambda b, i: (b, 0, i, 0))],
        out_specs=pl.BlockSpec((None, DSA_HEADS, tq, KV_LORA), lambda b, i: (b, 0, i, 0)),
        out_shape=jax.ShapeDtypeStruct((bsz, DSA_HEADS, s, KV_LORA), BF16),
        scratch_shapes=[pltpu.VMEM((nc, rows, KEY_CHUNK), F32), pltpu.VMEM((rows, LANES), F32),
                        pltpu.VMEM((rows, LANES), F32), pltpu.VMEM((rows, KV_LORA), F32)],
        compiler_params=_params("arbitrary", "arbitrary"),
        name="dsa_attention",
    )(q_cat, kv_t, kv, bias)


def _uv_kernel(o_ref, w_ref, y_ref):
    for h in range(DSA_HEADS):
        y_ref[:, h * DSA_V_DIM:(h + 1) * DSA_V_DIM] = jnp.dot(
            o_ref[h], w_ref[h], preferred_element_type=F32).astype(y_ref.dtype)


def _uv_project(o_lat, w_uv_b, bsz, s, tm=512):
    spt = s // tm
    return pl.pallas_call(
        _uv_kernel,
        grid=(bsz * spt,),
        in_specs=[pl.BlockSpec((None, DSA_HEADS, tm, KV_LORA), lambda i: (i // spt, 0, i % spt, 0)),
                  pl.BlockSpec((DSA_HEADS, KV_LORA, DSA_V_DIM), lambda i: (0, 0, 0))],
        out_specs=pl.BlockSpec((tm, DSA_HEADS * DSA_V_DIM), lambda i: (i, 0)),
        out_shape=jax.ShapeDtypeStruct((bsz * s, DSA_HEADS * DSA_V_DIM), BF16),
        compiler_params=_params("arbitrary"),
        name="uv_project",
    )(o_lat, w_uv_b)


def _merge_kernel(h_ref, a_ref, b_ref, wga_ref, wgb_ref, wua_ref, wub_ref, o_ref,
                  cga_ref, cgb_ref, cua_ref, cub_ref):
    @pl.when(pl.program_id(1) == 0)
    def _():
        cga_ref[...] = wga_ref[...].astype(BF16)
        cgb_ref[...] = wgb_ref[...].astype(BF16)
        cua_ref[...] = wua_ref[...].astype(BF16)
        cub_ref[...] = wub_ref[...].astype(BF16)

    h = h_ref[...]
    ga = lax.dot_general(h, cga_ref[...], _NT_DIMS, preferred_element_type=F32)
    gb = lax.dot_general(h, cgb_ref[...], _NT_DIMS, preferred_element_type=F32)
    ya = jnp.dot(a_ref[...], cua_ref[...], preferred_element_type=F32)
    yb = jnp.dot(b_ref[...], cub_ref[...], preferred_element_type=F32)
    o_ref[...] = (jax.nn.sigmoid(ga) * ya + jax.nn.sigmoid(gb) * yb).astype(o_ref.dtype)


def _gated_merge(h1, att_a, att_b, w_t, row_ga, row_gb, w_up_a, w_up_b, tm=512, tn=256):
    n, d = h1.shape
    ka, kb = att_a.shape[1], att_b.shape[1]
    return pl.pallas_call(
        _merge_kernel,
        grid=(d // tn, n // tm),
        in_specs=[pl.BlockSpec((tm, d), lambda j, i: (i, 0)),
                  pl.BlockSpec((tm, ka), lambda j, i: (i, 0)),
                  pl.BlockSpec((tm, kb), lambda j, i: (i, 0)),
                  _w_rows_spec(tn, d, row_ga, lambda j, i: j),
                  _w_rows_spec(tn, d, row_gb, lambda j, i: j),
                  pl.BlockSpec((ka, tn), lambda j, i: (0, j)),
                  pl.BlockSpec((kb, tn), lambda j, i: (0, j))],
        out_specs=pl.BlockSpec((tm, tn), lambda j, i: (i, j)),
        out_shape=jax.ShapeDtypeStruct((n, d), BF16),
        scratch_shapes=[pltpu.VMEM((tn, d), BF16), pltpu.VMEM((tn, d), BF16),
                        pltpu.VMEM((ka, tn), BF16), pltpu.VMEM((kb, tn), BF16)],
        compiler_params=_params("arbitrary", "arbitrary"),
        name="gated_merge",
    )(h1, att_a, att_b, w_t, w_t, w_up_a, w_up_b)


def _mem_kernel(x_ref, g_ref, wq_ref, kt_ref, v_ref, wom_ref, o_ref):
    x = x_ref[...]
    hn = (x * lax.rsqrt(jnp.mean(x * x, axis=-1, keepdims=True) + EPS) * g_ref[...]).astype(BF16)
    q = jnp.dot(hn, wq_ref[...], preferred_element_type=F32).astype(BF16)
    scale = MEM_HEAD_DIM ** -0.5
    outs = []
    for h in range(MEM_HEADS):
        sl = slice(h * MEM_HEAD_DIM, (h + 1) * MEM_HEAD_DIM)
        s = jnp.dot(q[:, sl], kt_ref[sl, :], preferred_element_type=F32) * scale
        m = jnp.max(s, axis=-1, keepdims=True)
        p = jnp.exp(s - m)
        p = p / jnp.sum(p, axis=-1, keepdims=True)
        outs.append(jnp.dot(p.astype(BF16), v_ref[:, sl], preferred_element_type=F32).astype(BF16))
    o = jnp.concatenate(outs, axis=-1)
    o_ref[...] = x + jnp.dot(o, wom_ref[...], preferred_element_type=F32)


def _mem_attention(x1, g_x, w_qm_b, km_t, kv_m, w_om_b, bsz, s, n_mem, tm=256):
    spt = s // tm
    hd = MEM_HEADS * MEM_HEAD_DIM
    d = x1.shape[1]
    return pl.pallas_call(
        _mem_kernel,
        grid=(bsz, spt),
        in_specs=[pl.BlockSpec((tm, d), lambda b, i: (b * spt + i, 0)),
                  pl.BlockSpec((1, d), lambda b, i: (0, 0)),
                  pl.BlockSpec((d, hd), lambda b, i: (0, 0)),
                  pl.BlockSpec((None, hd, n_mem), lambda b, i: (b, 0, 0)),
                  pl.BlockSpec((n_mem, hd), lambda b, i: (b, 1)),
                  pl.BlockSpec((hd, d), lambda b, i: (0, 0))],
        out_specs=pl.BlockSpec((tm, d), lambda b, i: (b * spt + i, 0)),
        out_shape=jax.ShapeDtypeStruct(x1.shape, F32),
        compiler_params=_params("arbitrary", "arbitrary"),
        name="memory_cross_attention",
    )(x1, g_x.reshape(1, d), w_qm_b, km_t, kv_m, w_om_b)


def _pack_bf16_pairs(x):
    c = x.shape[1] // 2
    return pltpu.pack_elementwise([x[:, :c], x[:, c:]], packed_dtype=BF16)


def _unpack_bf16_pairs(w):
    lo = pltpu.unpack_elementwise(w, index=0, packed_dtype=BF16, unpacked_dtype=F32)
    hi = pltpu.unpack_elementwise(w, index=1, packed_dtype=BF16, unpacked_dtype=F32)
    return lo.astype(BF16), hi.astype(BF16)


def _router_kernel(x_ref, g_ref, wr_ref, br_ref, h_ref, idx_ref, wt_ref, cnt_ref, carry_ref):
    @pl.when(pl.program_id(0) == 0)
    def _():
        carry_ref[...] = jnp.zeros_like(carry_ref)

    x = x_ref[...]
    hn = x * lax.rsqrt(jnp.mean(x * x, axis=-1, keepdims=True) + EPS) * g_ref[...]
    h_ref[...] = _pack_bf16_pairs(hn)
    hn_hi = hn.astype(BF16)
    hn_lo = (hn - hn_hi.astype(F32)).astype(BF16)
    wcat = wr_ref[...]
    a = jnp.dot(hn_hi, wcat, preferred_element_type=F32)
    b = jnp.dot(hn_lo, wcat[:, :LANES], preferred_element_type=F32)
    logits = a[:, :LANES] + (a[:, LANES:] + b) + br_ref[...]
    tm = x.shape[0]
    lane_i = lax.broadcasted_iota(jnp.int32, (tm, LANES), 1)
    lane = lane_i.astype(F32)
    lane_grp = lax.shift_right_logical(lane_i, EXPERTS_PER_GROUP.bit_length() - 1).astype(F32)
    big = 1e9
    is_g = (lane_i >= N_EXPERTS) & (lane_i < N_EXPERTS + N_GROUPS)
    lg = jnp.where(is_g, logits, -jnp.inf)
    mg = jnp.max(lg, axis=1, keepdims=True)
    gate_g = 1.0 / jnp.sum(jnp.exp(lg - mg), axis=1, keepdims=True)
    grp = jnp.min(jnp.where(lg == mg, lane, big), axis=1, keepdims=True) - N_EXPERTS
    in_grp = (lane_i < N_EXPERTS) & (lane_grp == grp)
    le = jnp.where(in_grp, logits, -jnp.inf)
    m1 = jnp.max(le, axis=1, keepdims=True)
    e1 = jnp.min(jnp.where(le == m1, lane, big), axis=1, keepdims=True)
    le2 = jnp.where(lane == e1, -jnp.inf, le)
    m2 = jnp.max(le2, axis=1, keepdims=True)
    e2 = jnp.min(jnp.where(le2 == m2, lane, big), axis=1, keepdims=True)
    r = jnp.exp(m2 - m1)
    w1 = gate_g * (1.0 / (1.0 + r))
    w2 = gate_g * (r / (1.0 + r))
    oh1 = (lane == e1).astype(F32)
    oh2 = (lane == e2).astype(F32)
    oh = oh1 + oh2
    tri = (lax.broadcasted_iota(jnp.int32, (tm, tm), 0) > lax.broadcasted_iota(jnp.int32, (tm, tm), 1)).astype(BF16)
    prefix = jnp.dot(tri, oh.astype(BF16), preferred_element_type=F32) + carry_ref[...]
    rank1 = jnp.sum(prefix * oh1, axis=1, keepdims=True)
    rank2 = jnp.sum(prefix * oh2, axis=1, keepdims=True)
    new_cnt = carry_ref[...] + jnp.sum(oh, axis=0, keepdims=True)
    carry_ref[...] = new_cnt
    cnt_ref[...] = new_cnt
    idx_ref[...] = jnp.where(lane_i == 0, e1, jnp.where(lane_i == 1, e2, jnp.where(
        lane_i == 2, rank1, jnp.where(lane_i == 3, rank2, 0.0)))).astype(jnp.int32)
    wt_ref[...] = jnp.where(lane_i == 0, w1, jnp.where(lane_i == 1, w2, 0.0))


def _router(x2, g, w_r, b_r, tm=256):
    n, d = x2.shape
    row = lambda w: pl.BlockSpec((tm, w), lambda i: (i, 0))
    return pl.pallas_call(
        _router_kernel,
        grid=(n // tm,),
        in_specs=[row(d), pl.BlockSpec((1, d), lambda i: (0, 0)), pl.BlockSpec((d, 2 * LANES), lambda i: (0, 0)),
                  pl.BlockSpec((1, LANES), lambda i: (0, 0))],
        out_specs=[row(d // 2), row(LANES), row(LANES), pl.BlockSpec((1, LANES), lambda i: (0, 0))],
        out_shape=[jax.ShapeDtypeStruct((n, d // 2), jnp.uint32), jax.ShapeDtypeStruct((n, LANES), jnp.int32),
                   jax.ShapeDtypeStruct((n, LANES), F32), jax.ShapeDtypeStruct((1, LANES), F32)],
        scratch_shapes=[pltpu.VMEM((1, LANES), F32)],
        compiler_params=_params("arbitrary"),
        name="moe_router",
    )(x2, g.reshape(1, d), w_r, b_r)


def _row_copy(src_hbm, row, dst, r, sem):
    return pltpu.make_async_copy(src_hbm.at[pl.ds(row, 1), :], dst.at[pl.ds(r, 1), :], sem)


def _sorted_rows(m):
    return m + N_EXPERTS * SUBLANES + ITEM_ROWS


def _dispatch_kernel(dest_ref, seg0_ref, cnt_ref, h_ref, x_hbm, zero_ref, sem, *, tm, m):
    i = pl.program_id(0)

    @pl.when(i == 0)
    def _():
        zero_ref[...] = jnp.zeros_like(zero_ref)
        tail = [pltpu.make_async_copy(zero_ref, x_hbm.at[pl.ds(r0, ITEM_SUB), :], sem)
                for r0 in range(m, _sorted_rows(m), ITEM_SUB)]
        for c in tail:
            c.start()
        for c in tail:
            c.wait()

        def fill_gap(e, total):
            gap = (-cnt_ref[e]) & (SUBLANES - 1)

            def one(r, c):
                _row_copy(zero_ref, 0, x_hbm, seg0_ref[e] + cnt_ref[e] + r, sem).start()
                return c

            lax.fori_loop(0, gap, one, 0)
            return total + gap

        n_gap = lax.fori_loop(0, N_EXPERTS, fill_gap, 0)

        def wait_gap(r, c):
            _row_copy(zero_ref, 0, x_hbm, 0, sem).wait()
            return c

        lax.fori_loop(0, n_gap, wait_gap, 0)

    def issue(r, c):
        t = i * tm + r
        for sl in range(TOPK_IN_GROUP):
            _row_copy(h_ref, r, x_hbm, dest_ref[TOPK_IN_GROUP * t + sl], sem).start()
        return c

    lax.fori_loop(0, tm, issue, 0, unroll=4)
    for _ in range(TOPK_IN_GROUP):
        pltpu.make_async_copy(h_ref, x_hbm.at[pl.ds(0, tm), :], sem).wait()


def _dispatch(dest, seg_start, counts, h3p, tm=256):
    n, c = h3p.shape
    m = dest.shape[0]
    assert (_sorted_rows(m) - m) % ITEM_SUB == 0
    grid_spec = pltpu.PrefetchScalarGridSpec(
        num_scalar_prefetch=3,
        grid=(n // tm,),
        in_specs=[pl.BlockSpec((tm, c), lambda i, d, s0, cn: (i, 0))],
        out_specs=pl.BlockSpec(memory_space=pl.ANY),
        scratch_shapes=[pltpu.VMEM((ITEM_SUB, c), h3p.dtype), pltpu.SemaphoreType.DMA(())],
    )
    return pl.pallas_call(
        functools.partial(_dispatch_kernel, tm=tm, m=m),
        grid_spec=grid_spec,
        out_shape=jax.ShapeDtypeStruct((_sorted_rows(m), c), h3p.dtype),
        compiler_params=_params("arbitrary"),
        name="moe_dispatch",
    )(dest, seg_start, counts, h3p)


def _expert_kernel(ie_ref, is_ref, in_ref, ni_ref, x_hbm, wg_ref, wu_ref, wd_ref, y_hbm,
                   xin_ref, xb_ref, acc_ref, yp_ref, in_sem, out_sem, *, m, nj):
    i = pl.program_id(0)
    j = pl.program_id(1)
    n_items = ni_ref[0]
    valid = i < n_items
    slot = lax.rem(i, 2)
    half = xb_ref.shape[1] // 2
    n_sub = ITEM_ROWS // ITEM_SUB

    def first_row(item):
        return pl.multiple_of(is_ref[item], SUBLANES)

    def blocks(item):
        return (in_ref[item] + ITEM_SUB - 1) // ITEM_SUB

    def each_block(item, fn):
        for sub in range(n_sub):
            pl.when(sub * ITEM_SUB < in_ref[item])(functools.partial(fn, sub))

    def in_copy(item, sub):
        rows = pl.ds(sub * ITEM_SUB, ITEM_SUB)
        return pltpu.make_async_copy(x_hbm.at[pl.ds(first_row(item) + sub * ITEM_SUB, ITEM_SUB), :],
                                     xin_ref.at[rows, :], in_sem)

    def out_copy(item, sl, sub):
        rows = pl.ds(sub * ITEM_SUB, ITEM_SUB)
        return pltpu.make_async_copy(yp_ref.at[sl, rows, :],
                                     y_hbm.at[pl.ds(first_row(item) + sub * ITEM_SUB, ITEM_SUB), :], out_sem.at[sl])

    def by_block_count(item, fn):
        for nb in range(1, n_sub + 1):
            pl.when(blocks(item) == nb)(functools.partial(fn, nb))

    @pl.when((i == 0) & (j == 0))
    def _():
        yp_ref[1, :ITEM_SUB, :] = jnp.zeros((ITEM_SUB, yp_ref.shape[2]), yp_ref.dtype)
        tail = [pltpu.make_async_copy(yp_ref.at[1, pl.ds(0, ITEM_SUB), :],
                                      y_hbm.at[pl.ds(r0, ITEM_SUB), :], out_sem.at[1])
                for r0 in range(m, _sorted_rows(m), ITEM_SUB)]
        for c in tail:
            c.start()
        for c in tail:
            c.wait()
        each_block(0, lambda sub: in_copy(0, sub).start())

    @pl.when(valid & (j == 0))
    def _():
        each_block(i, lambda sub: in_copy(i, sub).wait())

        def unpack(nb):
            rows = pl.ds(0, nb * ITEM_SUB)
            lo, hi = _unpack_bf16_pairs(xin_ref[rows, :])
            xb_ref[rows, :half] = lo
            xb_ref[rows, half:] = hi
            acc_ref[rows, :] = jnp.zeros((nb * ITEM_SUB, acc_ref.shape[1]), F32)

        by_block_count(i, unpack)

        @pl.when(i + 1 < n_items)
        def _():
            each_block(i + 1, lambda sub: in_copy(i + 1, sub).start())

    @pl.when(valid)
    def _():
        wg = wg_ref[...].astype(BF16)
        wu = wu_ref[...].astype(BF16)
        wd = wd_ref[...].astype(BF16)

        def mlp(nb):
            rows = pl.ds(0, nb * ITEM_SUB)
            xb = xb_ref[rows, :]
            g = jnp.dot(xb, wg, preferred_element_type=F32)
            u = jnp.dot(xb, wu, preferred_element_type=F32)
            hm = (g * jax.nn.sigmoid(g) * u).astype(BF16)
            acc_ref[rows, :] += jnp.dot(hm, wd, preferred_element_type=F32)

        by_block_count(i, mlp)

    @pl.when(valid & (j == nj - 1))
    def _():
        @pl.when(i > 0)
        def _():
            each_block(i - 1, lambda sub: out_copy(i - 1, 1 - slot, sub).wait())

        def pack(nb):
            rows = pl.ds(0, nb * ITEM_SUB)
            yp_ref[slot, rows, :] = _pack_bf16_pairs(acc_ref[rows, :])

        by_block_count(i, pack)
        each_block(i, lambda sub: out_copy(i, slot, sub).start())

        @pl.when(i == n_items - 1)
        def _():
            each_block(i, lambda sub: out_copy(i, slot, sub).wait())


def _experts(item_e, item_start, item_n, n_items, x_sorted, w_gate, w_up, w_down, max_items, m):
    d = w_gate.shape[1]
    nj = D_EXPERT // F_CHUNK

    def jj(i, j, ni):
        return jnp.where(i < ni[0], j, nj - 1)

    grid_spec = pltpu.PrefetchScalarGridSpec(
        num_scalar_prefetch=4,
        grid=(max_items, nj),
        in_specs=[pl.BlockSpec(memory_space=pl.ANY),
                  pl.BlockSpec((None, d, F_CHUNK), lambda i, j, ie, s, n, ni: (ie[i], 0, jj(i, j, ni))),
                  pl.BlockSpec((None, d, F_CHUNK), lambda i, j, ie, s, n, ni: (ie[i], 0, jj(i, j, ni))),
                  pl.BlockSpec((None, F_CHUNK, d), lambda i, j, ie, s, n, ni: (ie[i], jj(i, j, ni), 0))],
        out_specs=pl.BlockSpec(memory_space=pl.ANY),
        scratch_shapes=[pltpu.VMEM((ITEM_ROWS, d // 2), jnp.uint32), pltpu.VMEM((ITEM_ROWS, d), BF16),
                        pltpu.VMEM((ITEM_ROWS, d), F32), pltpu.VMEM((2, ITEM_ROWS, d // 2), jnp.uint32),
                        pltpu.SemaphoreType.DMA(()), pltpu.SemaphoreType.DMA((2,))],
    )
    return pl.pallas_call(
        functools.partial(_expert_kernel, m=m, nj=nj),
        grid_spec=grid_spec,
        out_shape=jax.ShapeDtypeStruct((_sorted_rows(m), d // 2), jnp.uint32),
        compiler_params=_params("arbitrary", "arbitrary"),
        name="moe_experts",
    )(item_e, item_start, item_n, n_items, x_sorted, w_gate, w_up, w_down)


def _combine_kernel(dest_ref, y_hbm, x_ref, wt_ref, g_ref, o_ref, yv_ref, sem, *, tm):
    i = pl.program_id(0)
    slot = lax.rem(i, 2)

    def issue_tile(tile, sl):
        def issue(r, c):
            base = TOPK_IN_GROUP * (tile * tm + r)
            for k in range(TOPK_IN_GROUP):
                _row_copy(y_hbm, dest_ref[base + k], yv_ref.at[sl, k], r, sem.at[sl]).start()
            return c

        lax.fori_loop(0, tm, issue, 0, unroll=4)

    @pl.when(i == 0)
    def _():
        issue_tile(0, 0)

    @pl.when(i + 1 < pl.num_programs(0))
    def _():
        issue_tile(i + 1, 1 - slot)

    for k in range(TOPK_IN_GROUP):
        pltpu.make_async_copy(y_hbm.at[pl.ds(0, tm), :], yv_ref.at[slot, k], sem.at[slot]).wait()
    w = wt_ref[...]
    half = x_ref.shape[1] // 2
    parts = [x_ref[:, :half], x_ref[:, half:]]
    for k in range(TOPK_IN_GROUP):
        lo, hi = _unpack_bf16_pairs(yv_ref[slot, k])
        parts = [parts[0] + lo.astype(F32) * w[:, k:k + 1], parts[1] + hi.astype(F32) * w[:, k:k + 1]]
    ms = (jnp.sum(parts[0] * parts[0], axis=-1, keepdims=True)
          + jnp.sum(parts[1] * parts[1], axis=-1, keepdims=True)) / x_ref.shape[1]
    r = lax.rsqrt(ms + EPS)
    o_ref[:, :half] = parts[0] * r * g_ref[:, :half]
    o_ref[:, half:] = parts[1] * r * g_ref[:, half:]


def _combine(dest, y_sorted, x2, wts, g_final, tm=128):
    n, d = x2.shape
    grid_spec = pltpu.PrefetchScalarGridSpec(
        num_scalar_prefetch=1,
        grid=(n // tm,),
        in_specs=[pl.BlockSpec(memory_space=pl.ANY),
                  pl.BlockSpec((tm, d), lambda i, dr: (i, 0)),
                  pl.BlockSpec((tm, LANES), lambda i, dr: (i, 0)),
                  pl.BlockSpec((1, d), lambda i, dr: (0, 0))],
        out_specs=pl.BlockSpec((tm, d), lambda i, dr: (i, 0)),
        scratch_shapes=[pltpu.VMEM((2, TOPK_IN_GROUP, tm, d // 2), y_sorted.dtype), pltpu.SemaphoreType.DMA((2,))],
    )
    return pl.pallas_call(
        functools.partial(_combine_kernel, tm=tm),
        grid_spec=grid_spec,
        out_shape=jax.ShapeDtypeStruct((n, d), F32),
        compiler_params=_params("arbitrary"),
        name="moe_combine_final_norm",
    )(dest, y_sorted, x2, wts, g_final.reshape(1, d))


def _pad_cols(w, width):
    return jnp.pad(w, ((0, 0), (0, width - w.shape[1])))


def _mixer(x2d, bsz, s, g_norm, w_in, b_f, g_q_lat, g_kv_lat, g_idx_k, b_idx_k, w_uq, w_idx_q, w_uk, w_uv,
           w_up_a, w_up_b, w_out):
    d = x2d.shape[1]
    topk = min(DSA_TOPK_MAX, s // 4)
    h1 = _rmsnorm(x2d, g_norm, BF16)
    c_small = 3 * FOX_WIDTH
    o_fa, o_cq = c_small, c_small + FOX_HEADS
    o_ckv = o_cq + Q_LORA
    o_kr = o_ckv + KV_LORA
    o_ik = o_kr + DSA_ROPE_DIM
    o_iw = o_ik + IDX_DIM
    o_ga = o_iw + IDX_HEADS
    o_gb = o_ga + d
    w_t = jnp.swapaxes(w_in, 0, 1)
    pad_rows = lambda w: jnp.pad(w, ((0, LANES - w.shape[0]), (0, 0)))
    w_small_t = jnp.concatenate([
        w_t[o_cq:o_ckv], w_t[o_ckv:o_kr], w_t[o_ik:o_iw],
        pad_rows(w_t[o_kr:o_ik]), pad_rows(w_t[o_fa:o_cq]), pad_rows(w_t[o_iw:o_ga])], axis=0)

    qkv = _mm(h1, w_t, 3 * FOX_WIDTH, w_rows=True, scaled_cols=(FOX_WIDTH, FOX_QSCALE), name="proj_qkv")
    small = _mm(h1, w_small_t, SMALL_W, out_dtype=F32, tm=512, tn=SMALL_W // 2, w_rows=True, name="proj_small")

    tabs64 = _rope_tables(s, DSA_ROPE_DIM)
    tabs128 = _rope_tables(s, IDX_DIM)
    cq_n, kv_cat, k_idx, w_idx, fa = _prep(small, g_q_lat, g_kv_lat, g_idx_k, b_idx_k, tabs64, tabs128, s)

    cum = _fox_cum(fa, _pad_cols(b_f.reshape(1, -1), LANES), bsz, s)
    att_a = _fox_attention(qkv, cum, bsz, s)

    w_uq3 = w_uq.reshape(Q_LORA, DSA_HEADS, DSA_NOPE_DIM + DSA_ROPE_DIM)
    w_uq_nope = w_uq3[:, :, :DSA_NOPE_DIM].reshape(Q_LORA, DSA_HEADS * DSA_NOPE_DIM).astype(BF16)
    w_uq_rope = jnp.pad(w_uq3[:, :, DSA_NOPE_DIM:], ((0, 0), (0, 0), (0, LANES - DSA_ROPE_DIM))).reshape(
        Q_LORA, DSA_HEADS * LANES).astype(BF16)
    q_cat = _q_latent(cq_n, w_uq_nope, w_uq_rope, w_uk.astype(BF16), tabs64, bsz, s)
    q_idx = _mm(cq_n, w_idx_q, IDX_HEADS * IDX_DIM, heads=(bsz, s), rope_tabs=tabs128, tn=1024,
                name="proj_q_idx")
    w_idx_t = jnp.transpose(w_idx.reshape(bsz, s, LANES)[:, :, :IDX_HEADS], (0, 2, 1))
    bias = _indexer_mask(q_idx, k_idx.reshape(bsz, s, IDX_DIM), w_idx_t, bsz, s, topk)
    kv3 = kv_cat.reshape(bsz, s, KV_PAD)
    kv_t = jnp.transpose(kv3.reshape(bsz, s // KEY_CHUNK, KEY_CHUNK, KV_PAD), (0, 1, 3, 2))
    o_lat = _dsa_attention(q_cat, kv_t, kv3, bias, bsz, s)
    att_b = _uv_project(o_lat, w_uv.astype(BF16), bsz, s)

    merged = _gated_merge(h1, att_a, att_b, w_t, o_ga, o_gb, w_up_a, w_up_b)
    return _mm(merged, w_out, d, out_dtype=F32, res=x2d, name="proj_out")


def _memory_layer(x1, mem, bsz, s, g_x, g_m, w_qm, w_km, w_vm, w_om):
    n_mem = mem.shape[1]
    d = x1.shape[1]
    hd = MEM_HEADS * MEM_HEAD_DIM
    mem_n = _rmsnorm(mem.reshape(bsz * n_mem, d), g_m, BF16)
    kv_m = _mm(mem_n, jnp.concatenate([w_km, w_vm], axis=1), 2 * hd, tm=bsz * n_mem, name="proj_kv_mem")
    km_t = jnp.transpose(kv_m[:, :hd].reshape(bsz, n_mem, hd), (0, 2, 1))
    return _mem_attention(x1, g_x, w_qm.astype(BF16), km_t, kv_m, w_om.astype(BF16), bsz, s, n_mem)


def _moe_layer(x2, g_ffn, w_rg, b_rg, w_re, b_re, w_gate, w_up, w_down, g_final):
    n, d = x2.shape
    m = n * TOPK_IN_GROUP
    w_r = _pad_cols(jnp.concatenate([w_re, w_rg], axis=1), LANES)
    w_r_hi = w_r.astype(BF16)
    w_r_split = jnp.concatenate([w_r_hi, (w_r - w_r_hi.astype(F32)).astype(BF16)], axis=1)
    b_r = _pad_cols(jnp.concatenate([b_re, b_rg]).reshape(1, -1), LANES)
    h3p, ridx, wts, cnt = _router(x2, g_ffn, w_r_split, b_r)
    counts = cnt[0, :N_EXPERTS].astype(jnp.int32)
    e_flat = ridx[:, :TOPK_IN_GROUP].reshape(m)
    rank_flat = ridx[:, TOPK_IN_GROUP:2 * TOPK_IN_GROUP].reshape(m)
    seg = (counts + SUBLANES - 1) // SUBLANES * SUBLANES
    starts = (jnp.cumsum(seg) - seg).astype(jnp.int32)
    n_it = (counts + ITEM_ROWS - 1) // ITEM_ROWS
    it_end = jnp.cumsum(n_it)
    it_first = it_end - n_it
    max_items = (m + N_EXPERTS * (ITEM_ROWS - 1)) // ITEM_ROWS
    t = jnp.arange(max_items, dtype=jnp.int32)
    item_e = jnp.minimum(jnp.sum(it_end[None, :] <= t[:, None], axis=1), N_EXPERTS - 1).astype(jnp.int32)
    k_in = t - it_first[item_e]
    item_start = (starts[item_e] + k_in * ITEM_ROWS).astype(jnp.int32)
    item_n = jnp.clip(counts[item_e] - k_in * ITEM_ROWS, 0, ITEM_ROWS).astype(jnp.int32)
    n_items = it_end[-1:].astype(jnp.int32)
    last = jnp.maximum(n_items[0] - 1, 0)
    item_e = jnp.where(t < n_items[0], item_e, item_e[last])
    item_start = jnp.where(t < n_items[0], item_start, 0)
    item_n = jnp.where(t < n_items[0], item_n, 0)
    e_ids = jnp.arange(N_EXPERTS, dtype=jnp.int32)
    dest = rank_flat + jnp.sum(jnp.where(e_flat[:, None] == e_ids[None, :], starts[None, :], 0), axis=1)
    x_sorted = _dispatch(dest, starts, counts, h3p)
    y_sorted = _experts(item_e, item_start, item_n, n_items, x_sorted, w_gate, w_up, w_down, max_items, m)
    return _combine(dest, y_sorted, x2, wts, g_final)


def kernel(x, mem, g_norm_mix, w_in, b_f, g_q_lat, g_kv_lat, g_idx_k, b_idx_k, w_uq, w_idx_q, w_uk, w_uv,
           w_up_a, w_up_b, w_out, g_norm_mem_x, g_mem, w_qm, w_km, w_vm, w_om, g_norm_ffn,
           w_rg, b_rg, w_re, b_re, w_gate, w_up, w_down, g_final):
    bsz, s, d = x.shape
    depth = w_in.shape[0]
    assert depth == 1, "the final norm is fused into the last MoE combine"
    l = 0
    x2d = x.reshape(bsz * s, d)
    x1 = _mixer(x2d, bsz, s, g_norm_mix[l], w_in[l], b_f[l], g_q_lat[l], g_kv_lat[l], g_idx_k[l], b_idx_k[l],
                w_uq[l], w_idx_q[l], w_uk[l], w_uv[l], w_up_a[l], w_up_b[l], w_out[l])
    x2 = _memory_layer(x1, mem, bsz, s, g_norm_mem_x[l], g_mem[l], w_qm[l], w_km[l], w_vm[l], w_om[l])
    out = _moe_layer(x2, g_norm_ffn[l], w_rg[l], b_rg[l], w_re[l], b_re[l], w_gate[l], w_up[l], w_down[l], g_final)
    return out.reshape(bsz, s, d)
```

```python
import functools

import jax
import jax.numpy as jnp
from jax import lax
from jax.experimental import pallas as pl
from jax.experimental.pallas import tpu as pltpu

F32 = jnp.float32
BF16 = jnp.bfloat16

EPS = 1e-6
ROPE_THETA = 10000.0
FOX_HEADS = 16
FOX_HEAD_DIM = 128
DSA_HEADS = 16
DSA_NOPE_DIM = 128
DSA_ROPE_DIM = 64
DSA_V_DIM = 128
Q_LORA = 1024
KV_LORA = 256
IDX_HEADS = 32
IDX_DIM = 128
DSA_TOPK_MAX = 256
MEM_HEADS = 4
MEM_HEAD_DIM = 128
N_GROUPS = 8
EXPERTS_PER_GROUP = 8
N_EXPERTS = N_GROUPS * EXPERTS_PER_GROUP
TOPK_IN_GROUP = 2
D_EXPERT = 512

FOX_WIDTH = FOX_HEADS * FOX_HEAD_DIM
LANES = 128
SUBLANES = 8
KV_PAD = KV_LORA + LANES
NEG_BIG = -1e30
LOG2E = 1.4426950408889634
FOX_QSCALE = FOX_HEAD_DIM ** -0.5 * LOG2E
FOX_HB = 4
DSA_QSCALE = (DSA_NOPE_DIM + DSA_ROPE_DIM) ** -0.5 * LOG2E
VMEM_LIMIT_BYTES = 56 * 1024 * 1024
ITEM_ROWS = 384
ITEM_SUB = 128
F_CHUNK = 256
KEY_CHUNK = 256


def _params(*sem):
    return pltpu.CompilerParams(dimension_semantics=sem, vmem_limit_bytes=VMEM_LIMIT_BYTES)


def _rmsnorm_kernel(x_ref, g_ref, o_ref):
    x = x_ref[...].astype(F32)
    ms = jnp.mean(x * x, axis=-1, keepdims=True)
    o_ref[...] = (x * lax.rsqrt(ms + EPS) * g_ref[...]).astype(o_ref.dtype)


def _rmsnorm(x2d, g, out_dtype, tm=512):
    n, d = x2d.shape
    tm = min(tm, n)
    assert n % tm == 0
    return pl.pallas_call(
        _rmsnorm_kernel,
        grid=(n // tm,),
        in_specs=[pl.BlockSpec((tm, d), lambda i: (i, 0)), pl.BlockSpec((1, d), lambda i: (0, 0))],
        out_specs=pl.BlockSpec((tm, d), lambda i: (i, 0)),
        out_shape=jax.ShapeDtypeStruct((n, d), out_dtype),
        compiler_params=_params("arbitrary"),
        name="rmsnorm",
    )(x2d, g.reshape(1, d).astype(F32))


_NT_DIMS = (((1,), (1,)), ((), ()))


def _w_rows_spec(tn, k, row0, index_of_j):
    if row0 % tn == 0:
        return pl.BlockSpec((tn, k), lambda *g: (index_of_j(*g) + row0 // tn, 0))
    assert row0 % SUBLANES == 0
    return pl.BlockSpec((pl.Element(tn), pl.Element(k)),
                        lambda *g: (pl.multiple_of(index_of_j(*g) * tn + row0, SUBLANES), 0))


def _mm_kernel(a_ref, b_ref, *rest, has_res, head_out, rope, w_rows, col_scale):
    pos = 0
    res_ref = cos_ref = sin_ref = None
    if has_res:
        res_ref = rest[pos]
        pos += 1
    if rope:
        cos_ref, sin_ref = rest[pos], rest[pos + 1]
        pos += 2
    o_ref = rest[pos]
    if b_ref.dtype == BF16:
        wb_ref = b_ref
    else:
        wb_ref = rest[pos + 1]

        @pl.when(pl.program_id(1) == 0)
        def _():
            wb_ref[...] = b_ref[...].astype(BF16)

    if w_rows:
        acc = lax.dot_general(a_ref[...], wb_ref[...], _NT_DIMS, preferred_element_type=F32)
    else:
        acc = jnp.dot(a_ref[...], wb_ref[...], preferred_element_type=F32)
    if col_scale is not None:
        acc = acc * jnp.where(pl.program_id(0) < col_scale[0], col_scale[1], 1.0)
    if has_res:
        acc = acc + res_ref[...]
    if head_out:
        for hh in range(acc.shape[1] // LANES):
            part = acc[:, hh * LANES:(hh + 1) * LANES]
            if rope:
                part = part * cos_ref[...] + pltpu.roll(part, LANES // 2, 1) * sin_ref[...]
            o_ref[hh] = part.astype(o_ref.dtype)
    else:
        o_ref[...] = acc.astype(o_ref.dtype)


def _mm(a, w, n_cols, *, col0=0, out_dtype=BF16, tm=1024, tn=512, res=None, heads=None, rope_tabs=None,
        w_rows=False, scaled_cols=None, name="mm"):
    m, k = a.shape
    tm = min(tm, m if heads is None else heads[1])
    assert n_cols % tn == 0 and m % tm == 0
    if w_rows:
        w_spec = _w_rows_spec(tn, k, col0, lambda j, i: j)
    else:
        assert col0 % tn == 0
        w_spec = pl.BlockSpec((k, tn), lambda j, i: (0, j + col0 // tn))
    in_specs = [pl.BlockSpec((tm, k), lambda j, i: (i, 0)), w_spec]
    args = [a, w]
    if res is not None:
        in_specs.append(pl.BlockSpec((tm, tn), lambda j, i: (i, j)))
        args.append(res)
    if heads is not None:
        bsz, s = heads
        assert s % tm == 0
        spt = s // tm
        if rope_tabs is not None:
            in_specs += [pl.BlockSpec((tm, LANES), lambda j, i: (i % spt, 0))] * 2
            args += list(rope_tabs)
        hpt = tn // LANES
        out_spec = pl.BlockSpec((None, hpt, tm, LANES), lambda j, i: (i // spt, j, i % spt, 0))
        out_shape = jax.ShapeDtypeStruct((bsz, n_cols // LANES, s, LANES), out_dtype)
    else:
        out_spec = pl.BlockSpec((tm, tn), lambda j, i: (i, j))
        out_shape = jax.ShapeDtypeStruct((m, n_cols), out_dtype)
    col_scale = None
    if scaled_cols is not None:
        assert scaled_cols[0] % tn == 0
        col_scale = (scaled_cols[0] // tn, scaled_cols[1])
    kern = functools.partial(_mm_kernel, has_res=res is not None, head_out=heads is not None,
                             rope=rope_tabs is not None, w_rows=w_rows, col_scale=col_scale)
    return pl.pallas_call(
        kern,
        grid=(n_cols // tn, m // tm),
        in_specs=in_specs,
        out_specs=out_spec,
        out_shape=out_shape,
        scratch_shapes=[] if w.dtype == BF16 else [pltpu.VMEM((tn, k) if w_rows else (k, tn), BF16)],
        compiler_params=_params("arbitrary", "arbitrary"),
        name=name,
    )(*args)


def _rope_tables(s, d):
    inv = jnp.power(ROPE_THETA, -jnp.arange(0, d, 2, dtype=F32) / d)
    ang = jnp.arange(s, dtype=jnp.int32).astype(F32)[:, None] * inv[None, :]
    cos, sin = jnp.cos(ang), jnp.sin(ang)
    cos_t = jnp.tile(jnp.concatenate([cos, cos], axis=-1), (1, LANES // d))
    sin_t = jnp.tile(jnp.concatenate([-sin, sin], axis=-1), (1, LANES // d))
    return cos_t, sin_t


def _rot_half64(x):
    lane = lax.broadcasted_iota(jnp.int32, x.shape, 1)
    return jnp.where((lane % 64) < 32, pltpu.roll(x, 96, 1), pltpu.roll(x, 32, 1))


_C_CQ = 0
_C_CKV = Q_LORA
_C_IK = _C_CKV + KV_LORA
_C_KR = _C_IK + IDX_DIM
_C_FA = _C_KR + LANES
_C_IW = _C_FA + LANES
SMALL_W = _C_IW + LANES


def _prep_kernel(s_ref, gq_ref, gkv_ref, gik_ref, bik_ref, c64_ref, s64_ref, c128_ref, s128_ref,
                 cq_ref, kv_ref, kidx_ref, widx_ref, fa_ref):
    cq = s_ref[:, _C_CQ:_C_CQ + Q_LORA]
    cq_ref[...] = (cq * lax.rsqrt(jnp.mean(cq * cq, axis=-1, keepdims=True) + EPS) * gq_ref[...]).astype(BF16)
    ckv = s_ref[:, _C_CKV:_C_CKV + KV_LORA]
    kv_ref[:, :KV_LORA] = (ckv * lax.rsqrt(jnp.mean(ckv * ckv, axis=-1, keepdims=True) + EPS)
                           * gkv_ref[...]).astype(BF16)
    kr = s_ref[:, _C_KR:_C_KR + LANES]
    kv_ref[:, KV_LORA:] = (kr * c64_ref[...] + _rot_half64(kr) * s64_ref[...]).astype(BF16)
    ik = s_ref[:, _C_IK:_C_IK + IDX_DIM]
    mu = jnp.mean(ik, axis=-1, keepdims=True)
    xc = ik - mu
    ikn = xc * lax.rsqrt(jnp.mean(xc * xc, axis=-1, keepdims=True) + EPS) * gik_ref[...] + bik_ref[...]
    kidx_ref[...] = (ikn * c128_ref[...] + pltpu.roll(ikn, LANES // 2, 1) * s128_ref[...]).astype(BF16)
    widx_ref[...] = s_ref[:, _C_IW:_C_IW + LANES] * (IDX_HEADS * IDX_DIM) ** -0.5
    fa_ref[...] = s_ref[:, _C_FA:_C_FA + LANES]


def _prep(small, g_q, g_kv, g_ik, b_ik, tabs64, tabs128, s, tm=256):
    n = small.shape[0]
    spt = s // tm
    row = lambda w: pl.BlockSpec((tm, w), lambda i: (i, 0))
    vec = lambda w: pl.BlockSpec((1, w), lambda i: (0, 0))
    tab = pl.BlockSpec((tm, LANES), lambda i: (i % spt, 0))
    return pl.pallas_call(
        _prep_kernel,
        grid=(n // tm,),
        in_specs=[row(SMALL_W), vec(Q_LORA), vec(KV_LORA), vec(IDX_DIM), vec(IDX_DIM), tab, tab, tab, tab],
        out_specs=[row(Q_LORA), row(KV_PAD), row(IDX_DIM), row(LANES), row(LANES)],
        out_shape=[jax.ShapeDtypeStruct((n, Q_LORA), BF16), jax.ShapeDtypeStruct((n, KV_PAD), BF16),
                   jax.ShapeDtypeStruct((n, IDX_DIM), BF16), jax.ShapeDtypeStruct((n, LANES), F32),
                   jax.ShapeDtypeStruct((n, LANES), F32)],
        compiler_params=_params("arbitrary"),
        name="mixer_prep",
    )(small, g_q.reshape(1, -1), g_kv.reshape(1, -1), g_ik.reshape(1, -1), b_ik.reshape(1, -1),
      tabs64[0], tabs64[1], tabs128[0], tabs128[1])


def _cum_kernel(fa_ref, bf_ref, cum_ref, carry_ref):
    @pl.when(pl.program_id(1) == 0)
    def _():
        carry_ref[...] = jnp.zeros_like(carry_ref)

    z = fa_ref[...] + bf_ref[...]
    logf = jnp.minimum(z, 0.0) - jnp.log1p(jnp.exp(-jnp.abs(z)))
    t = z.shape[0]
    tri = (lax.broadcasted_iota(jnp.int32, (t, t), 0) >= lax.broadcasted_iota(jnp.int32, (t, t), 1)).astype(F32)
    c = jnp.dot(tri, logf, precision=lax.Precision.HIGHEST, preferred_element_type=F32) + carry_ref[...]
    cum_ref[...] = c
    carry_ref[...] = c[t - 1:t, :]


def _fox_cum(fa, b_f_pad, bsz, s, t=256):
    return pl.pallas_call(
        _cum_kernel,
        grid=(bsz, s // t),
        in_specs=[pl.BlockSpec((None, t, LANES), lambda b, i: (b, i, 0)),
                  pl.BlockSpec((1, LANES), lambda b, i: (0, 0))],
        out_specs=pl.BlockSpec((None, t, LANES), lambda b, i: (b, i, 0)),
        out_shape=jax.ShapeDtypeStruct((bsz, s, LANES), F32),
        scratch_shapes=[pltpu.VMEM((1, LANES), F32)],
        compiler_params=_params("arbitrary", "arbitrary"),
        name="fox_cumsum",
    )(fa.reshape(bsz, s, LANES), b_f_pad)


def _lane_fold(x, op):
    out = x[:, :LANES]
    for c in range(1, x.shape[1] // LANES):
        out = op(out, x[:, c * LANES:(c + 1) * LANES])
    return out


def _split3(x):
    t1 = x.astype(BF16)
    r = x - t1.astype(F32)
    t2 = r.astype(BF16)
    t3 = (r - t2.astype(F32)).astype(BF16)
    return t1, t2, t3


def _gate_lanes(parts, own_first):
    t1, t2, t3 = parts
    lane = lax.broadcasted_iota(jnp.int32, t1.shape, 1)
    a, b = (0, 3) if own_first else (3, 0)
    ones3 = jnp.where((lane >= b) & (lane < b + 3), 1.0, 0.0)
    out = jnp.where(lane == a, t1, jnp.where(lane == a + 1, t2, jnp.where(lane == a + 2, t3, ones3)))
    return out.astype(BF16)


def _fox_kernel(q_ref, k_ref, v_ref, cum_ref, o_ref, kf_ref, vf_ref, s_ref, *, t):
    hp = pl.program_id(1)
    i = pl.program_id(2)
    d = FOX_HEAD_DIM
    heads = range(FOX_HB)
    row = lax.broadcasted_iota(jnp.int32, (LANES, LANES), 0)

    def gate_columns(c):
        terms = _split3(c * LOG2E)
        out = []
        for hh in heads:
            pick = (row == hp * FOX_HB + hh).astype(BF16)
            out.append([jnp.dot(term, pick, preferred_element_type=F32) for term in terms])
        return out

    @pl.when(i == 0)
    def _():
        ck = gate_columns(-cum_ref[...])
        for hh in heads:
            kf_ref[hh, :, :d] = k_ref[:, hh * d:(hh + 1) * d]
            kf_ref[hh, :, d:] = _gate_lanes(ck[hh], True)
            vf_ref[hh, :, :d] = v_ref[:, hh * d:(hh + 1) * d]
            vf_ref[hh, :, d:] = jnp.ones((vf_ref.shape[1], LANES), BF16)

    cq = gate_columns(cum_ref[pl.ds(pl.multiple_of(i * t, t), t), :])
    qs = [jnp.concatenate([q_ref[:, hh * d:(hh + 1) * d], _gate_lanes(cq[hh], False)], axis=1) for hh in heads]

    def logits(hh, kc):
        kb = kf_ref[hh, pl.ds(pl.multiple_of(kc * t, t), t), :]
        return lax.dot_general(qs[hh], kb, _NT_DIMS, preferred_element_type=F32)

    def pass1(kc, m_lanes):
        out = []
        for hh in heads:
            s = logits(hh, kc)
            s_ref[hh, kc] = s
            out.append(jnp.maximum(m_lanes[hh], _lane_fold(s, jnp.maximum)))
        return tuple(out)

    m_lanes = lax.fori_loop(0, i, pass1, tuple(jnp.full((t, LANES), NEG_BIG, F32) for _ in heads))
    causal = lax.broadcasted_iota(jnp.int32, (t, t), 0) >= lax.broadcasted_iota(jnp.int32, (t, t), 1)
    ms = []
    for hh in heads:
        s = jnp.where(causal, logits(hh, i), NEG_BIG)
        s_ref[hh, i] = s
        m_lane = jnp.maximum(m_lanes[hh], _lane_fold(s, jnp.maximum))
        m = jnp.broadcast_to(jnp.max(m_lane, axis=1, keepdims=True), (t, LANES))
        ms.append(jnp.concatenate([m] * (t // LANES), axis=1))

    def pass2(kc, accs):
        out = []
        for hh in heads:
            p = jnp.exp2(s_ref[hh, kc] - ms[hh])
            vb = vf_ref[hh, pl.ds(pl.multiple_of(kc * t, t), t), :]
            out.append(accs[hh] + jnp.dot(p.astype(BF16), vb, preferred_element_type=F32))
        return tuple(out)

    accs = lax.fori_loop(0, i + 1, pass2, tuple(jnp.zeros((t, d + LANES), F32) for _ in heads))
    for hh in heads:
        o_ref[:, hh * d:(hh + 1) * d] = (accs[hh][:, :d] / accs[hh][:, d:]).astype(o_ref.dtype)


def _fox_attention(qkv, cum, bsz, s, t=512):
    n = bsz * s
    nq = s // t
    hp = FOX_HEADS // FOX_HB
    w = FOX_HB * LANES
    assert FOX_HEAD_DIM == LANES and FOX_HEADS % FOX_HB == 0
    kern = functools.partial(_fox_kernel, t=t)
    return pl.pallas_call(
        kern,
        grid=(bsz, hp, nq),
        in_specs=[pl.BlockSpec((t, w), lambda b, h, i: (b * nq + i, h)),
                  pl.BlockSpec((s, w), lambda b, h, i: (b, hp + h)),
                  pl.BlockSpec((s, w), lambda b, h, i: (b, 2 * hp + h)),
                  pl.BlockSpec((None, s, LANES), lambda b, h, i: (b, 0, 0))],
        out_specs=pl.BlockSpec((t, w), lambda b, h, i: (b * nq + i, h)),
        out_shape=jax.ShapeDtypeStruct((n, FOX_WIDTH), BF16),
        scratch_shapes=[pltpu.VMEM((FOX_HB, s, 2 * LANES), BF16), pltpu.VMEM((FOX_HB, s, 2 * LANES), BF16),
                        pltpu.VMEM((FOX_HB, nq, t, t), F32)],
        compiler_params=_params("arbitrary", "arbitrary", "arbitrary"),
        name="fox_attention",
    )(qkv, qkv, qkv, cum)


def _qlat_kernel(cq_ref, wn_ref, wr_ref, wuk_ref, c64_ref, s64_ref, o_ref):
    cq = cq_ref[...]
    qn = jnp.dot(cq, wn_ref[...], preferred_element_type=F32).astype(BF16)
    qr = jnp.dot(cq, wr_ref[...], preferred_element_type=F32)
    for h in range(DSA_HEADS):
        cols = slice(h * LANES, (h + 1) * LANES)
        lat = jnp.dot(qn[:, cols], wuk_ref[h], preferred_element_type=F32)
        o_ref[h, :, :KV_LORA] = (lat * DSA_QSCALE).astype(BF16)
        r = qr[:, cols]
        o_ref[h, :, KV_LORA:] = ((r * c64_ref[...] + _rot_half64(r) * s64_ref[...]) * DSA_QSCALE).astype(BF16)


def _q_latent(cq_n, w_nope_b, w_rope_b, w_uk_b, tabs64, bsz, s, tm=512):
    spt = s // tm
    kq = cq_n.shape[1]
    wide = DSA_HEADS * LANES
    return pl.pallas_call(
        _qlat_kernel,
        grid=(bsz * spt,),
        in_specs=[pl.BlockSpec((tm, kq), lambda i: (i, 0)),
                  pl.BlockSpec((kq, wide), lambda i: (0, 0)),
                  pl.BlockSpec((kq, wide), lambda i: (0, 0)),
                  pl.BlockSpec((DSA_HEADS, DSA_NOPE_DIM, KV_LORA), lambda i: (0, 0, 0)),
                  pl.BlockSpec((tm, LANES), lambda i: (i % spt, 0)),
                  pl.BlockSpec((tm, LANES), lambda i: (i % spt, 0))],
        out_specs=pl.BlockSpec((None, DSA_HEADS, tm, KV_PAD), lambda i: (i // spt, 0, i % spt, 0)),
        out_shape=jax.ShapeDtypeStruct((bsz, DSA_HEADS, s, KV_PAD), BF16),
        compiler_params=_params("arbitrary"),
        name="q_latent",
    )(cq_n, w_nope_b, w_rope_b, w_uk_b, tabs64[0], tabs64[1])


def _needed_chunks(i, tq):
    return ((i + 1) * tq + KEY_CHUNK - 1) // KEY_CHUNK


BISECT_MAX_STEPS = 320
BISECT_UNROLL = 4


def _indexer_kernel(q_ref, k_ref, wt_ref, o_ref, sc_ref, thr_ref, cut_ref, *, tq, topk, hg):
    i = pl.program_id(1)
    tk = KEY_CHUNK
    n_chunks = sc_ref.shape[0]
    nk = _needed_chunks(i, tq)
    qpos = i * tq + lax.broadcasted_iota(jnp.int32, (tk, tq), 1)
    krow = lax.broadcasted_iota(jnp.int32, (tk, tq), 0)
    inf = jnp.inf

    def score_chunk(kc, c):
        k0 = pl.multiple_of(kc * tk, tk)
        kb = k_ref[pl.ds(k0, tk), :]
        sc = jnp.zeros((tk, tq), F32)
        for g in range(IDX_HEADS // hg):
            q2 = q_ref[g * hg:(g + 1) * hg].reshape(hg * tq, IDX_DIM)
            d = lax.dot_general(kb, q2, _NT_DIMS, preferred_element_type=F32)
            for hh in range(hg):
                h = g * hg + hh
                sc = sc + wt_ref[h:h + 1, :] * jnp.maximum(d[:, hh * tq:(hh + 1) * tq], 0.0)
        sc_ref[kc] = jnp.where(k0 + krow <= qpos, sc, -inf)
        return c

    lax.fori_loop(0, nk, score_chunk, 0)

    def pad_chunk(kc, c):
        sc_ref[kc] = jnp.full((tk, tq), -inf, F32)
        return c

    lax.fori_loop(nk, n_chunks, pad_chunk, 0)
    kf = float(topk)
    n_causal = (i * tq + lax.broadcasted_iota(jnp.int32, (1, tq), 1) + 1).astype(F32)

    def fold(x, op):
        return op(x.reshape(tk // SUBLANES, SUBLANES, tq), axis=0)

    def key_pos(kc):
        return (kc * tk + krow).astype(F32)

    def threshold(n_counted):
        def count_ge(t):
            acc = jnp.zeros((SUBLANES, tq), F32)
            for kc in range(n_counted):
                acc = acc + fold(jnp.where(sc_ref[kc] >= t, 1.0, 0.0), jnp.sum)
            return jnp.sum(acc, axis=0, keepdims=True)

        hi = jnp.full((SUBLANES, tq), -inf, F32)
        lo = jnp.full((SUBLANES, tq), inf, F32)
        for kc in range(n_counted):
            s = sc_ref[kc]
            hi = jnp.maximum(hi, fold(s, jnp.max))
            lo = jnp.minimum(lo, fold(jnp.where(s == -inf, inf, s), jnp.min))
        hi = jnp.max(hi, axis=0, keepdims=True)
        lo = jnp.min(lo, axis=0, keepdims=True)
        few = n_causal <= kf
        n_max = count_ge(hi)
        at_max = n_max >= kf
        lo = jnp.where(few, -inf, jnp.where(at_max, hi, lo))
        n_lo = jnp.where(at_max, n_max, n_causal)
        done = jnp.where(few | at_max, 1.0, 0.0)

        def unsettled(state):
            it, _, _, _, done = state
            return (it < BISECT_MAX_STEPS) & (jnp.sum(1.0 - done) > 0.0)

        def step(state):
            it, lo, hi, n_lo, done = state
            for _ in range(BISECT_UNROLL):
                mid = 0.5 * lo + 0.5 * hi
                cnt = count_ge(mid)
                ge = cnt >= kf
                closed = (mid <= lo) | (mid >= hi)
                live = done == 0.0
                lo = jnp.where(live & ge, mid, lo)
                n_lo = jnp.where(live & ge, cnt, n_lo)
                hi = jnp.where(live & jnp.logical_not(ge), mid, hi)
                done = jnp.where(live & ((ge & (cnt == kf)) | closed), 1.0, done)
            return it + BISECT_UNROLL, lo, hi, n_lo, done

        _, lo, _, n_lo, _ = lax.while_loop(unsettled, step, (jnp.int32(0), lo, hi, n_lo, done))
        thr_ref[...] = lo
        cut_ref[...] = jnp.full((1, tq), float(n_chunks * tk), F32)

        over = jnp.where((n_lo > kf) & jnp.logical_not(few), 1.0, 0.0)

        @pl.when(jnp.sum(over) > 0.0)
        def _():
            def count_where(pred):
                def body(kc, acc):
                    hit = pred(sc_ref[kc], key_pos(kc))
                    return acc + fold(jnp.where(hit, 1.0, 0.0), jnp.sum)

                acc = lax.fori_loop(0, n_counted, body, jnp.zeros((SUBLANES, tq), F32))
                return jnp.sum(acc, axis=0, keepdims=True)

            need = kf - count_where(lambda s, kp: s > lo)

            def narrow(_, bracket):
                lo_p, hi_p = bracket
                mid = jnp.floor(0.5 * (lo_p + hi_p))
                ok = count_where(lambda s, kp: (s == lo) & (kp <= mid)) >= need
                return jnp.where(ok, lo_p, mid), jnp.where(ok, mid, hi_p)

            n_pos = n_counted * tk
            _, hi_p = lax.fori_loop(0, n_pos.bit_length(), narrow,
                                    (jnp.full((1, tq), -1.0, F32), jnp.full((1, tq), n_pos - 1.0, F32)))
            cut_ref[...] = jnp.where(over > 0.0, hi_p, cut_ref[...])

    prev = 0
    for level in sorted({max(n_chunks // 4, 1), max(n_chunks // 2, 1), n_chunks}):
        pl.when((nk > prev) & (nk <= level))(functools.partial(threshold, level))
        prev = level
    thr = thr_ref[...]
    cut = cut_ref[...]

    def write_chunk(kc, c):
        s = sc_ref[kc]
        keep = ((s > thr) | ((s == thr) & (key_pos(kc) <= cut))) & (kc * tk + krow <= qpos)
        o_ref[kc] = jnp.where(keep, 0.0, NEG_BIG).T.astype(o_ref.dtype)
        return c

    lax.fori_loop(0, nk, write_chunk, 0)

    def fill_chunk(kc, c):
        o_ref[kc] = jnp.full((tq, tk), NEG_BIG, o_ref.dtype)
        return c

    lax.fori_loop(nk, n_chunks, fill_chunk, 0)


def _indexer_mask(q_idx, k_idx, w_idx_t, bsz, s, topk, tq=128, hg=4):
    kern = functools.partial(_indexer_kernel, tq=tq, topk=topk, hg=hg)
    nc = s // KEY_CHUNK
    return pl.pallas_call(
        kern,
        grid=(bsz, s // tq),
        in_specs=[pl.BlockSpec((None, IDX_HEADS, tq, IDX_DIM), lambda b, i: (b, 0, i, 0)),
                  pl.BlockSpec((None, s, IDX_DIM), lambda b, i: (b, 0, 0)),
                  pl.BlockSpec((None, IDX_HEADS, tq), lambda b, i: (b, 0, i))],
        out_specs=pl.BlockSpec((None, nc, tq, KEY_CHUNK), lambda b, i: (b, 0, i, 0)),
        out_shape=jax.ShapeDtypeStruct((bsz, nc, s, KEY_CHUNK), BF16),
        scratch_shapes=[pltpu.VMEM((nc, KEY_CHUNK, tq), F32), pltpu.VMEM((1, tq), F32),
                        pltpu.VMEM((1, tq), F32)],
        compiler_params=_params("arbitrary", "arbitrary"),
        name="indexer_topk_mask",
    )(q_idx, k_idx, w_idx_t)


def _dsa_kernel(q_ref, kvt_ref, kv_ref, bias_ref, o_ref, s_ref, m_ref, l_ref, acc_ref, *, tq):
    i = pl.program_id(1)
    tk = KEY_CHUNK
    rows = DSA_HEADS * tq
    nk = _needed_chunks(i, tq)
    q2 = q_ref[...].reshape(rows, KV_PAD)
    m_ref[...] = jnp.full(m_ref.shape, NEG_BIG, F32)

    n_pairs = (nk + 1) // 2

    def logits(kc):
        sc = jnp.dot(q2, kvt_ref[kc], preferred_element_type=F32)
        sc = (sc.reshape(DSA_HEADS, tq, tk) + bias_ref[kc].astype(F32)[None]).reshape(rows, tk)
        s_ref[kc] = sc
        return _lane_fold(sc, jnp.maximum)

    def pass1(pr, c):
        m_ref[...] = jnp.maximum(m_ref[...], jnp.maximum(logits(2 * pr), logits(2 * pr + 1)))
        return c

    lax.fori_loop(0, n_pairs, pass1, 0)
    m_ref[...] = jnp.broadcast_to(jnp.max(m_ref[...], axis=1, keepdims=True), m_ref.shape)
    l_ref[...] = jnp.zeros(l_ref.shape, F32)
    acc_ref[...] = jnp.zeros(acc_ref.shape, F32)

    def probs(kc):
        m = m_ref[...]
        p = jnp.concatenate([jnp.exp2(s_ref[kc, :, g * LANES:(g + 1) * LANES] - m)
                             for g in range(tk // LANES)], axis=1)
        kvb = kv_ref[pl.ds(pl.multiple_of(kc * tk, tk), tk), :KV_LORA]
        return _lane_fold(p, jnp.add), jnp.dot(p.astype(BF16), kvb, preferred_element_type=F32)

    def pass2(pr, c):
        l0, a0 = probs(2 * pr)
        l1, a1 = probs(2 * pr + 1)
        l_ref[...] += l0 + l1
        acc_ref[...] += a0 + a1
        return c

    lax.fori_loop(0, n_pairs, pass2, 0)
    l = jnp.dot(l_ref[...], jnp.ones((LANES, LANES), F32), precision=lax.Precision.HIGHEST,
                preferred_element_type=F32)
    o = acc_ref[...] / jnp.concatenate([l] * (KV_LORA // LANES), axis=1)
    o_ref[...] = o.reshape(DSA_HEADS, tq, KV_LORA).astype(o_ref.dtype)


def _dsa_attention(q_cat, kv_t, kv, bias, bsz, s, tq=128):
    kern = functools.partial(_dsa_kernel, tq=tq)
    nc = s // KEY_CHUNK
    assert nc % 2 == 0, "the kernel walks key chunks in pairs"
    rows = DSA_HEADS * tq
    return pl.pallas_call(
        kern,
        grid=(bsz, s // tq),
        in_specs=[pl.BlockSpec((None, DSA_HEADS, tq, KV_PAD), lambda b, i: (b, 0, i, 0)),
                  pl.BlockSpec((None, nc, KV_PAD, KEY_CHUNK), lambda b, i: (b, 0, 0, 0)),
                  pl.BlockSpec((None, s, KV_PAD), lambda b, i: (b, 0, 0)),
                  pl.BlockSpec((None, nc, tq, KEY_CHUNK), lambda b, i: (b, 0, i, 0))],
        out_specs=pl.BlockSpec((None, DSA_HEADS, tq, KV_LORA), lambda b, i: (b, 0, i, 0)),
        out_shape=jax.ShapeDtypeStruct((bsz, DSA_HEADS, s, KV_LORA), BF16),
        scratch_shapes=[pltpu.VMEM((nc, rows, KEY_CHUNK), F32), pltpu.VMEM((rows, LANES), F32),
                        pltpu.VMEM((rows, LANES), F32), pltpu.VMEM((rows, KV_LORA), F32)],
        compiler_params=_params("arbitrary", "arbitrary"),
        name="dsa_attention",
    )(q_cat, kv_t, kv, bias)


def _uv_kernel(o_ref, w_ref, y_ref):
    for h in range(DSA_HEADS):
        y_ref[:, h * DSA_V_DIM:(h + 1) * DSA_V_DIM] = jnp.dot(
            o_ref[h], w_ref[h], preferred_element_type=F32).astype(y_ref.dtype)


def _uv_project(o_lat, w_uv_b, bsz, s, tm=512):
    spt = s // tm
    return pl.pallas_call(
        _uv_kernel,
        grid=(bsz * spt,),
        in_specs=[pl.BlockSpec((None, DSA_HEADS, tm, KV_LORA), lambda i: (i // spt, 0, i % spt, 0)),
                  pl.BlockSpec((DSA_HEADS, KV_LORA, DSA_V_DIM), lambda i: (0, 0, 0))],
        out_specs=pl.BlockSpec((tm, DSA_HEADS * DSA_V_DIM), lambda i: (i, 0)),
        out_shape=jax.ShapeDtypeStruct((bsz * s, DSA_HEADS * DSA_V_DIM), BF16),
        compiler_params=_params("arbitrary"),
        name="uv_project",
    )(o_lat, w_uv_b)


def _merge_kernel(h_ref, a_ref, b_ref, wga_ref, wgb_ref, wua_ref, wub_ref, o_ref,
                  cga_ref, cgb_ref, cua_ref, cub_ref):
    @pl.when(pl.program_id(1) == 0)
    def _():
        cga_ref[...] = wga_ref[...].astype(BF16)
        cgb_ref[...] = wgb_ref[...].astype(BF16)
        cua_ref[...] = wua_ref[...].astype(BF16)
        cub_ref[...] = wub_ref[...].astype(BF16)

    h = h_ref[...]
    ga = lax.dot_general(h, cga_ref[...], _NT_DIMS, preferred_element_type=F32)
    gb = lax.dot_general(h, cgb_ref[...], _NT_DIMS, preferred_element_type=F32)
    ya = jnp.dot(a_ref[...], cua_ref[...], preferred_element_type=F32)
    yb = jnp.dot(b_ref[...], cub_ref[...], preferred_element_type=F32)
    o_ref[...] = (jax.nn.sigmoid(ga) * ya + jax.nn.sigmoid(gb) * yb).astype(o_ref.dtype)


def _gated_merge(h1, att_a, att_b, w_t, row_ga, row_gb, w_up_a, w_up_b, tm=512, tn=256):
    n, d = h1.shape
    ka, kb = att_a.shape[1], att_b.shape[1]
    return pl.pallas_call(
        _merge_kernel,
        grid=(d // tn, n // tm),
        in_specs=[pl.BlockSpec((tm, d), lambda j, i: (i, 0)),
                  pl.BlockSpec((tm, ka), lambda j, i: (i, 0)),
                  pl.BlockSpec((tm, kb), lambda j, i: (i, 0)),
                  _w_rows_spec(tn, d, row_ga, lambda j, i: j),
                  _w_rows_spec(tn, d, row_gb, lambda j, i: j),
                  pl.BlockSpec((ka, tn), lambda j, i: (0, j)),
                  pl.BlockSpec((kb, tn), lambda j, i: (0, j))],
        out_specs=pl.BlockSpec((tm, tn), lambda j, i: (i, j)),
        out_shape=jax.ShapeDtypeStruct((n, d), BF16),
        scratch_shapes=[pltpu.VMEM((tn, d), BF16), pltpu.VMEM((tn, d), BF16),
                        pltpu.VMEM((ka, tn), BF16), pltpu.VMEM((kb, tn), BF16)],
        compiler_params=_params("arbitrary", "arbitrary"),
        name="gated_merge",
    )(h1, att_a, att_b, w_t, w_t, w_up_a, w_up_b)


def _mem_kernel(x_ref, g_ref, wq_ref, kt_ref, v_ref, wom_ref, o_ref):
    x = x_ref[...]
    hn = (x * lax.rsqrt(jnp.mean(x * x, axis=-1, keepdims=True) + EPS) * g_ref[...]).astype(BF16)
    q = jnp.dot(hn, wq_ref[...], preferred_element_type=F32).astype(BF16)
    scale = MEM_HEAD_DIM ** -0.5
    outs = []
    for h in range(MEM_HEADS):
        sl = slice(h * MEM_HEAD_DIM, (h + 1) * MEM_HEAD_DIM)
        s = jnp.dot(q[:, sl], kt_ref[sl, :], preferred_element_type=F32) * scale
        m = jnp.max(s, axis=-1, keepdims=True)
        p = jnp.exp(s - m)
        p = p / jnp.sum(p, axis=-1, keepdims=True)
        outs.append(jnp.dot(p.astype(BF16), v_ref[:, sl], preferred_element_type=F32).astype(BF16))
    o = jnp.concatenate(outs, axis=-1)
    o_ref[...] = x + jnp.dot(o, wom_ref[...], preferred_element_type=F32)


def _mem_attention(x1, g_x, w_qm_b, km_t, kv_m, w_om_b, bsz, s, n_mem, tm=256):
    spt = s // tm
    hd = MEM_HEADS * MEM_HEAD_DIM
    d = x1.shape[1]
    return pl.pallas_call(
        _mem_kernel,
        grid=(bsz, spt),
        in_specs=[pl.BlockSpec((tm, d), lambda b, i: (b * spt + i, 0)),
                  pl.BlockSpec((1, d), lambda b, i: (0, 0)),
                  pl.BlockSpec((d, hd), lambda b, i: (0, 0)),
                  pl.BlockSpec((None, hd, n_mem), lambda b, i: (b, 0, 0)),
                  pl.BlockSpec((n_mem, hd), lambda b, i: (b, 1)),
                  pl.BlockSpec((hd, d), lambda b, i: (0, 0))],
        out_specs=pl.BlockSpec((tm, d), lambda b, i: (b * spt + i, 0)),
        out_shape=jax.ShapeDtypeStruct(x1.shape, F32),
        compiler_params=_params("arbitrary", "arbitrary"),
        name="memory_cross_attention",
    )(x1, g_x.reshape(1, d), w_qm_b, km_t, kv_m, w_om_b)


def _pack_bf16_pairs(x):
    c = x.shape[1] // 2
    return pltpu.pack_elementwise([x[:, :c], x[:, c:]], packed_dtype=BF16)


def _unpack_bf16_pairs(w):
    lo = pltpu.unpack_elementwise(w, index=0, packed_dtype=BF16, unpacked_dtype=F32)
    hi = pltpu.unpack_elementwise(w, index=1, packed_dtype=BF16, unpacked_dtype=F32)
    return lo.astype(BF16), hi.astype(BF16)


def _router_kernel(x_ref, g_ref, wr_ref, br_ref, h_ref, idx_ref, wt_ref, cnt_ref, carry_ref):
    @pl.when(pl.program_id(0) == 0)
    def _():
        carry_ref[...] = jnp.zeros_like(carry_ref)

    x = x_ref[...]
    hn = x * lax.rsqrt(jnp.mean(x * x, axis=-1, keepdims=True) + EPS) * g_ref[...]
    h_ref[...] = _pack_bf16_pairs(hn)
    hn_hi = hn.astype(BF16)
    hn_lo = (hn - hn_hi.astype(F32)).astype(BF16)
    wcat = wr_ref[...]
    a = jnp.dot(hn_hi, wcat, preferred_element_type=F32)
    b = jnp.dot(hn_lo, wcat[:, :LANES], preferred_element_type=F32)
    logits = a[:, :LANES] + (a[:, LANES:] + b) + br_ref[...]
    tm = x.shape[0]
    lane_i = lax.broadcasted_iota(jnp.int32, (tm, LANES), 1)
    lane = lane_i.astype(F32)
    lane_grp = lax.shift_right_logical(lane_i, EXPERTS_PER_GROUP.bit_length() - 1).astype(F32)
    big = 1e9
    is_g = (lane_i >= N_EXPERTS) & (lane_i < N_EXPERTS + N_GROUPS)
    lg = jnp.where(is_g, logits, -jnp.inf)
    mg = jnp.max(lg, axis=1, keepdims=True)
    gate_g = 1.0 / jnp.sum(jnp.exp(lg - mg), axis=1, keepdims=True)
    grp = jnp.min(jnp.where(lg == mg, lane, big), axis=1, keepdims=True) - N_EXPERTS
    in_grp = (lane_i < N_EXPERTS) & (lane_grp == grp)
    le = jnp.where(in_grp, logits, -jnp.inf)
    m1 = jnp.max(le, axis=1, keepdims=True)
    e1 = jnp.min(jnp.where(le == m1, lane, big), axis=1, keepdims=True)
    le2 = jnp.where(lane == e1, -jnp.inf, le)
    m2 = jnp.max(le2, axis=1, keepdims=True)
    e2 = jnp.min(jnp.where(le2 == m2, lane, big), axis=1, keepdims=True)
    r = jnp.exp(m2 - m1)
    w1 = gate_g * (1.0 / (1.0 + r))
    w2 = gate_g * (r / (1.0 + r))
    oh1 = (lane == e1).astype(F32)
    oh2 = (lane == e2).astype(F32)
    oh = oh1 + oh2
    tri = (lax.broadcasted_iota(jnp.int32, (tm, tm), 0) > lax.broadcasted_iota(jnp.int32, (tm, tm), 1)).astype(BF16)
    prefix = jnp.dot(tri, oh.astype(BF16), preferred_element_type=F32) + carry_ref[...]
    rank1 = jnp.sum(prefix * oh1, axis=1, keepdims=True)
    rank2 = jnp.sum(prefix * oh2, axis=1, keepdims=True)
    new_cnt = carry_ref[...] + jnp.sum(oh, axis=0, keepdims=True)
    carry_ref[...] = new_cnt
    cnt_ref[...] = new_cnt
    idx_ref[...] = jnp.where(lane_i == 0, e1, jnp.where(lane_i == 1, e2, jnp.where(
        lane_i == 2, rank1, jnp.where(lane_i == 3, rank2, 0.0)))).astype(jnp.int32)
    wt_ref[...] = jnp.where(lane_i == 0, w1, jnp.where(lane_i == 1, w2, 0.0))


def _router(x2, g, w_r, b_r, tm=256):
    n, d = x2.shape
    row = lambda w: pl.BlockSpec((tm, w), lambda i: (i, 0))
    return pl.pallas_call(
        _router_kernel,
        grid=(n // tm,),
        in_specs=[row(d), pl.BlockSpec((1, d), lambda i: (0, 0)), pl.BlockSpec((d, 2 * LANES), lambda i: (0, 0)),
                  pl.BlockSpec((1, LANES), lambda i: (0, 0))],
        out_specs=[row(d // 2), row(LANES), row(LANES), pl.BlockSpec((1, LANES), lambda i: (0, 0))],
        out_shape=[jax.ShapeDtypeStruct((n, d // 2), jnp.uint32), jax.ShapeDtypeStruct((n, LANES), jnp.int32),
                   jax.ShapeDtypeStruct((n, LANES), F32), jax.ShapeDtypeStruct((1, LANES), F32)],
        scratch_shapes=[pltpu.VMEM((1, LANES), F32)],
        compiler_params=_params("arbitrary"),
        name="moe_router",
    )(x2, g.reshape(1, d), w_r, b_r)


def _row_copy(src_hbm, row, dst, r, sem):
    return pltpu.make_async_copy(src_hbm.at[pl.ds(row, 1), :], dst.at[pl.ds(r, 1), :], sem)


def _sorted_rows(m):
    return m + N_EXPERTS * SUBLANES + ITEM_ROWS


def _dispatch_kernel(dest_ref, seg0_ref, cnt_ref, h_ref, x_hbm, zero_ref, sem, *, tm, m):
    i = pl.program_id(0)

    @pl.when(i == 0)
    def _():
        zero_ref[...] = jnp.zeros_like(zero_ref)
        tail = [pltpu.make_async_copy(zero_ref, x_hbm.at[pl.ds(r0, ITEM_SUB), :], sem)
                for r0 in range(m, _sorted_rows(m), ITEM_SUB)]
        for c in tail:
            c.start()
        for c in tail:
            c.wait()

        def fill_gap(e, total):
            gap = (-cnt_ref[e]) & (SUBLANES - 1)

            def one(r, c):
                _row_copy(zero_ref, 0, x_hbm, seg0_ref[e] + cnt_ref[e] + r, sem).start()
                return c

            lax.fori_loop(0, gap, one, 0)
            return total + gap

        n_gap = lax.fori_loop(0, N_EXPERTS, fill_gap, 0)

        def wait_gap(r, c):
            _row_copy(zero_ref, 0, x_hbm, 0, sem).wait()
            return c

        lax.fori_loop(0, n_gap, wait_gap, 0)

    def issue(r, c):
        t = i * tm + r
        for sl in range(TOPK_IN_GROUP):
            _row_copy(h_ref, r, x_hbm, dest_ref[TOPK_IN_GROUP * t + sl], sem).start()
        return c

    lax.fori_loop(0, tm, issue, 0, unroll=4)
    for _ in range(TOPK_IN_GROUP):
        pltpu.make_async_copy(h_ref, x_hbm.at[pl.ds(0, tm), :], sem).wait()


def _dispatch(dest, seg_start, counts, h3p, tm=256):
    n, c = h3p.shape
    m = dest.shape[0]
    assert (_sorted_rows(m) - m) % ITEM_SUB == 0
    grid_spec = pltpu.PrefetchScalarGridSpec(
        num_scalar_prefetch=3,
        grid=(n // tm,),
        in_specs=[pl.BlockSpec((tm, c), lambda i, d, s0, cn: (i, 0))],
        out_specs=pl.BlockSpec(memory_space=pl.ANY),
        scratch_shapes=[pltpu.VMEM((ITEM_SUB, c), h3p.dtype), pltpu.SemaphoreType.DMA(())],
    )
    return pl.pallas_call(
        functools.partial(_dispatch_kernel, tm=tm, m=m),
        grid_spec=grid_spec,
        out_shape=jax.ShapeDtypeStruct((_sorted_rows(m), c), h3p.dtype),
        compiler_params=_params("arbitrary"),
        name="moe_dispatch",
    )(dest, seg_start, counts, h3p)


def _expert_kernel(ie_ref, is_ref, in_ref, ni_ref, x_hbm, wg_ref, wu_ref, wd_ref, y_hbm,
                   xin_ref, xb_ref, acc_ref, yp_ref, in_sem, out_sem, *, m, nj):
    i = pl.program_id(0)
    j = pl.program_id(1)
    n_items = ni_ref[0]
    valid = i < n_items
    slot = lax.rem(i, 2)
    half = xb_ref.shape[1] // 2
    n_sub = ITEM_ROWS // ITEM_SUB

    def first_row(item):
        return pl.multiple_of(is_ref[item], SUBLANES)

    def blocks(item):
        return (in_ref[item] + ITEM_SUB - 1) // ITEM_SUB

    def each_block(item, fn):
        for sub in range(n_sub):
            pl.when(sub * ITEM_SUB < in_ref[item])(functools.partial(fn, sub))

    def in_copy(item, sub):
        rows = pl.ds(sub * ITEM_SUB, ITEM_SUB)
        return pltpu.make_async_copy(x_hbm.at[pl.ds(first_row(item) + sub * ITEM_SUB, ITEM_SUB), :],
                                     xin_ref.at[rows, :], in_sem)

    def out_copy(item, sl, sub):
        rows = pl.ds(sub * ITEM_SUB, ITEM_SUB)
        return pltpu.make_async_copy(yp_ref.at[sl, rows, :],
                                     y_hbm.at[pl.ds(first_row(item) + sub * ITEM_SUB, ITEM_SUB), :], out_sem.at[sl])

    def by_block_count(item, fn):
        for nb in range(1, n_sub + 1):
            pl.when(blocks(item) == nb)(functools.partial(fn, nb))

    @pl.when((i == 0) & (j == 0))
    def _():
        yp_ref[1, :ITEM_SUB, :] = jnp.zeros((ITEM_SUB, yp_ref.shape[2]), yp_ref.dtype)
        tail = [pltpu.make_async_copy(yp_ref.at[1, pl.ds(0, ITEM_SUB), :],
                                      y_hbm.at[pl.ds(r0, ITEM_SUB), :], out_sem.at[1])
                for r0 in range(m, _sorted_rows(m), ITEM_SUB)]
        for c in tail:
            c.start()
        for c in tail:
            c.wait()
        each_block(0, lambda sub: in_copy(0, sub).start())

    @pl.when(valid & (j == 0))
    def _():
        each_block(i, lambda sub: in_copy(i, sub).wait())

        def unpack(nb):
            rows = pl.ds(0, nb * ITEM_SUB)
            lo, hi = _unpack_bf16_pairs(xin_ref[rows, :])
            xb_ref[rows, :half] = lo
            xb_ref[rows, half:] = hi
            acc_ref[rows, :] = jnp.zeros((nb * ITEM_SUB, acc_ref.shape[1]), F32)

        by_block_count(i, unpack)

        @pl.when(i + 1 < n_items)
        def _():
            each_block(i + 1, lambda sub: in_copy(i + 1, sub).start())

    @pl.when(valid)
    def _():
        wg = wg_ref[...].astype(BF16)
        wu = wu_ref[...].astype(BF16)
        wd = wd_ref[...].astype(BF16)

        def mlp(nb):
            rows = pl.ds(0, nb * ITEM_SUB)
            xb = xb_ref[rows, :]
            g = jnp.dot(xb, wg, preferred_element_type=F32)
            u = jnp.dot(xb, wu, preferred_element_type=F32)
            hm = (g * jax.nn.sigmoid(g) * u).astype(BF16)
            acc_ref[rows, :] += jnp.dot(hm, wd, preferred_element_type=F32)

        by_block_count(i, mlp)

    @pl.when(valid & (j == nj - 1))
    def _():
        @pl.when(i > 0)
        def _():
            each_block(i - 1, lambda sub: out_copy(i - 1, 1 - slot, sub).wait())

        def pack(nb):
            rows = pl.ds(0, nb * ITEM_SUB)
            yp_ref[slot, rows, :] = _pack_bf16_pairs(acc_ref[rows, :])

        by_block_count(i, pack)
        each_block(i, lambda sub: out_copy(i, slot, sub).start())

        @pl.when(i == n_items - 1)
        def _():
            each_block(i, lambda sub: out_copy(i, slot, sub).wait())


def _experts(item_e, item_start, item_n, n_items, x_sorted, w_gate, w_up, w_down, max_items, m):
    d = w_gate.shape[1]
    nj = D_EXPERT // F_CHUNK

    def jj(i, j, ni):
        return jnp.where(i < ni[0], j, nj - 1)

    grid_spec = pltpu.PrefetchScalarGridSpec(
        num_scalar_prefetch=4,
        grid=(max_items, nj),
        in_specs=[pl.BlockSpec(memory_space=pl.ANY),
                  pl.BlockSpec((None, d, F_CHUNK), lambda i, j, ie, s, n, ni: (ie[i], 0, jj(i, j, ni))),
                  pl.BlockSpec((None, d, F_CHUNK), lambda i, j, ie, s, n, ni: (ie[i], 0, jj(i, j, ni))),
                  pl.BlockSpec((None, F_CHUNK, d), lambda i, j, ie, s, n, ni: (ie[i], jj(i, j, ni), 0))],
        out_specs=pl.BlockSpec(memory_space=pl.ANY),
        scratch_shapes=[pltpu.VMEM((ITEM_ROWS, d // 2), jnp.uint32), pltpu.VMEM((ITEM_ROWS, d), BF16),
                        pltpu.VMEM((ITEM_ROWS, d), F32), pltpu.VMEM((2, ITEM_ROWS, d // 2), jnp.uint32),
                        pltpu.SemaphoreType.DMA(()), pltpu.SemaphoreType.DMA((2,))],
    )
    return pl.pallas_call(
        functools.partial(_expert_kernel, m=m, nj=nj),
        grid_spec=grid_spec,
        out_shape=jax.ShapeDtypeStruct((_sorted_rows(m), d // 2), jnp.uint32),
        compiler_params=_params("arbitrary", "arbitrary"),
        name="moe_experts",
    )(item_e, item_start, item_n, n_items, x_sorted, w_gate, w_up, w_down)


def _combine_kernel(dest_ref, y_hbm, x_ref, wt_ref, g_ref, o_ref, yv_ref, sem, *, tm):
    i = pl.program_id(0)
    slot = lax.rem(i, 2)

    def issue_tile(tile, sl):
        def issue(r, c):
            base = TOPK_IN_GROUP * (tile * tm + r)
            for k in range(TOPK_IN_GROUP):
                _row_copy(y_hbm, dest_ref[base + k], yv_ref.at[sl, k], r, sem.at[sl]).start()
            return c

        lax.fori_loop(0, tm, issue, 0, unroll=4)

    @pl.when(i == 0)
    def _():
        issue_tile(0, 0)

    @pl.when(i + 1 < pl.num_programs(0))
    def _():
        issue_tile(i + 1, 1 - slot)

    for k in range(TOPK_IN_GROUP):
        pltpu.make_async_copy(y_hbm.at[pl.ds(0, tm), :], yv_ref.at[slot, k], sem.at[slot]).wait()
    w = wt_ref[...]
    half = x_ref.shape[1] // 2
    parts = [x_ref[:, :half], x_ref[:, half:]]
    for k in range(TOPK_IN_GROUP):
        lo, hi = _unpack_bf16_pairs(yv_ref[slot, k])
        parts = [parts[0] + lo.astype(F32) * w[:, k:k + 1], parts[1] + hi.astype(F32) * w[:, k:k + 1]]
    ms = (jnp.sum(parts[0] * parts[0], axis=-1, keepdims=True)
          + jnp.sum(parts[1] * parts[1], axis=-1, keepdims=True)) / x_ref.shape[1]
    r = lax.rsqrt(ms + EPS)
    o_ref[:, :half] = parts[0] * r * g_ref[:, :half]
    o_ref[:, half:] = parts[1] * r * g_ref[:, half:]


def _combine(dest, y_sorted, x2, wts, g_final, tm=256):
    n, d = x2.shape
    grid_spec = pltpu.PrefetchScalarGridSpec(
        num_scalar_prefetch=1,
        grid=(n // tm,),
        in_specs=[pl.BlockSpec(memory_space=pl.ANY),
                  pl.BlockSpec((tm, d), lambda i, dr: (i, 0)),
                  pl.BlockSpec((tm, LANES), lambda i, dr: (i, 0)),
                  pl.BlockSpec((1, d), lambda i, dr: (0, 0))],
        out_specs=pl.BlockSpec((tm, d), lambda i, dr: (i, 0)),
        scratch_shapes=[pltpu.VMEM((2, TOPK_IN_GROUP, tm, d // 2), y_sorted.dtype), pltpu.SemaphoreType.DMA((2,))],
    )
    return pl.pallas_call(
        functools.partial(_combine_kernel, tm=tm),
        grid_spec=grid_spec,
        out_shape=jax.ShapeDtypeStruct((n, d), F32),
        compiler_params=_params("arbitrary"),
        name="moe_combine_final_norm",
    )(dest, y_sorted, x2, wts, g_final.reshape(1, d))


def _pad_cols(w, width):
    return jnp.pad(w, ((0, 0), (0, width - w.shape[1])))


def _mixer(x2d, bsz, s, g_norm, w_in, b_f, g_q_lat, g_kv_lat, g_idx_k, b_idx_k, w_uq, w_idx_q, w_uk, w_uv,
           w_up_a, w_up_b, w_out):
    d = x2d.shape[1]
    topk = min(DSA_TOPK_MAX, s // 4)
    h1 = _rmsnorm(x2d, g_norm, BF16)
    c_small = 3 * FOX_WIDTH
    o_fa, o_cq = c_small, c_small + FOX_HEADS
    o_ckv = o_cq + Q_LORA
    o_kr = o_ckv + KV_LORA
    o_ik = o_kr + DSA_ROPE_DIM
    o_iw = o_ik + IDX_DIM
    o_ga = o_iw + IDX_HEADS
    o_gb = o_ga + d
    w_t = jnp.swapaxes(w_in, 0, 1)
    pad_rows = lambda w: jnp.pad(w, ((0, LANES - w.shape[0]), (0, 0)))
    w_small_t = jnp.concatenate([
        w_t[o_cq:o_ckv], w_t[o_ckv:o_kr], w_t[o_ik:o_iw],
        pad_rows(w_t[o_kr:o_ik]), pad_rows(w_t[o_fa:o_cq]), pad_rows(w_t[o_iw:o_ga])], axis=0)

    qkv = _mm(h1, w_t, 3 * FOX_WIDTH, w_rows=True, scaled_cols=(FOX_WIDTH, FOX_QSCALE), name="proj_qkv")
    small = _mm(h1, w_small_t, SMALL_W, out_dtype=F32, tm=512, tn=SMALL_W // 2, w_rows=True, name="proj_small")

    tabs64 = _rope_tables(s, DSA_ROPE_DIM)
    tabs128 = _rope_tables(s, IDX_DIM)
    cq_n, kv_cat, k_idx, w_idx, fa = _prep(small, g_q_lat, g_kv_lat, g_idx_k, b_idx_k, tabs64, tabs128, s)

    cum = _fox_cum(fa, _pad_cols(b_f.reshape(1, -1), LANES), bsz, s)
    att_a = _fox_attention(qkv, cum, bsz, s)

    w_uq3 = w_uq.reshape(Q_LORA, DSA_HEADS, DSA_NOPE_DIM + DSA_ROPE_DIM)
    w_uq_nope = w_uq3[:, :, :DSA_NOPE_DIM].reshape(Q_LORA, DSA_HEADS * DSA_NOPE_DIM).astype(BF16)
    w_uq_rope = jnp.pad(w_uq3[:, :, DSA_NOPE_DIM:], ((0, 0), (0, 0), (0, LANES - DSA_ROPE_DIM))).reshape(
        Q_LORA, DSA_HEADS * LANES).astype(BF16)
    q_cat = _q_latent(cq_n, w_uq_nope, w_uq_rope, w_uk.astype(BF16), tabs64, bsz, s)
    q_idx = _mm(cq_n, w_idx_q, IDX_HEADS * IDX_DIM, heads=(bsz, s), rope_tabs=tabs128, tn=1024,
                name="proj_q_idx")
    w_idx_t = jnp.transpose(w_idx.reshape(bsz, s, LANES)[:, :, :IDX_HEADS], (0, 2, 1))
    bias = _indexer_mask(q_idx, k_idx.reshape(bsz, s, IDX_DIM), w_idx_t, bsz, s, topk)
    kv3 = kv_cat.reshape(bsz, s, KV_PAD)
    kv_t = jnp.transpose(kv3.reshape(bsz, s // KEY_CHUNK, KEY_CHUNK, KV_PAD), (0, 1, 3, 2))
    o_lat = _dsa_attention(q_cat, kv_t, kv3, bias, bsz, s)
    att_b = _uv_project(o_lat, w_uv.astype(BF16), bsz, s)

    merged = _gated_merge(h1, att_a, att_b, w_t, o_ga, o_gb, w_up_a, w_up_b)
    return _mm(merged, w_out, d, out_dtype=F32, res=x2d, name="proj_out")


def _memory_layer(x1, mem, bsz, s, g_x, g_m, w_qm, w_km, w_vm, w_om):
    n_mem = mem.shape[1]
    d = x1.shape[1]
    hd = MEM_HEADS * MEM_HEAD_DIM
    mem_n = _rmsnorm(mem.reshape(bsz * n_mem, d), g_m, BF16)
    kv_m = _mm(mem_n, jnp.concatenate([w_km, w_vm], axis=1), 2 * hd, tm=bsz * n_mem, name="proj_kv_mem")
    km_t = jnp.transpose(kv_m[:, :hd].reshape(bsz, n_mem, hd), (0, 2, 1))
    return _mem_attention(x1, g_x, w_qm.astype(BF16), km_t, kv_m, w_om.astype(BF16), bsz, s, n_mem)


def _moe_layer(x2, g_ffn, w_rg, b_rg, w_re, b_re, w_gate, w_up, w_down, g_final):
    n, d = x2.shape
    m = n * TOPK_IN_GROUP
    w_r = _pad_cols(jnp.concatenate([w_re, w_rg], axis=1), LANES)
    w_r_hi = w_r.astype(BF16)
    w_r_split = jnp.concatenate([w_r_hi, (w_r - w_r_hi.astype(F32)).astype(BF16)], axis=1)
    b_r = _pad_cols(jnp.concatenate([b_re, b_rg]).reshape(1, -1), LANES)
    h3p, ridx, wts, cnt = _router(x2, g_ffn, w_r_split, b_r)
    counts = cnt[0, :N_EXPERTS].astype(jnp.int32)
    e_flat = ridx[:, :TOPK_IN_GROUP].reshape(m)
    rank_flat = ridx[:, TOPK_IN_GROUP:2 * TOPK_IN_GROUP].reshape(m)
    seg = (counts + SUBLANES - 1) // SUBLANES * SUBLANES
    starts = (jnp.cumsum(seg) - seg).astype(jnp.int32)
    n_it = (counts + ITEM_ROWS - 1) // ITEM_ROWS
    it_end = jnp.cumsum(n_it)
    it_first = it_end - n_it
    max_items = (m + N_EXPERTS * (ITEM_ROWS - 1)) // ITEM_ROWS
    t = jnp.arange(max_items, dtype=jnp.int32)
    item_e = jnp.minimum(jnp.sum(it_end[None, :] <= t[:, None], axis=1), N_EXPERTS - 1).astype(jnp.int32)
    k_in = t - it_first[item_e]
    item_start = (starts[item_e] + k_in * ITEM_ROWS).astype(jnp.int32)
    item_n = jnp.clip(counts[item_e] - k_in * ITEM_ROWS, 0, ITEM_ROWS).astype(jnp.int32)
    n_items = it_end[-1:].astype(jnp.int32)
    last = jnp.maximum(n_items[0] - 1, 0)
    item_e = jnp.where(t < n_items[0], item_e, item_e[last])
    item_start = jnp.where(t < n_items[0], item_start, 0)
    item_n = jnp.where(t < n_items[0], item_n, 0)
    e_ids = jnp.arange(N_EXPERTS, dtype=jnp.int32)
    dest = rank_flat + jnp.sum(jnp.where(e_flat[:, None] == e_ids[None, :], starts[None, :], 0), axis=1)
    x_sorted = _dispatch(dest, starts, counts, h3p)
    y_sorted = _experts(item_e, item_start, item_n, n_items, x_sorted, w_gate, w_up, w_down, max_items, m)
    return _combine(dest, y_sorted, x2, wts, g_final)


def kernel(x, mem, g_norm_mix, w_in, b_f, g_q_lat, g_kv_lat, g_idx_k, b_idx_k, w_uq, w_idx_q, w_uk, w_uv,
           w_up_a, w_up_b, w_out, g_norm_mem_x, g_mem, w_qm, w_km, w_vm, w_om, g_norm_ffn,
           w_rg, b_rg, w_re, b_re, w_gate, w_up, w_down, g_final):
    bsz, s, d = x.shape
    depth = w_in.shape[0]
    assert depth == 1, "the final norm is fused into the last MoE combine"
    l = 0
    x2d = x.reshape(bsz * s, d)
    x1 = _mixer(x2d, bsz, s, g_norm_mix[l], w_in[l], b_f[l], g_q_lat[l], g_kv_lat[l], g_idx_k[l], b_idx_k[l],
                w_uq[l], w_idx_q[l], w_uk[l], w_uv[l], w_up_a[l], w_up_b[l], w_out[l])
    x2 = _memory_layer(x1, mem, bsz, s, g_norm_mem_x[l], g_mem[l], w_qm[l], w_km[l], w_vm[l], w_om[l])
    out = _moe_layer(x2, g_norm_ffn[l], w_rg[l], b_rg[l], w_re[l], b_re[l], w_gate[l], w_up[l], w_down[l], g_final)
    return out.reshape(bsz, s, d)
```

```python
import functools

import jax
import jax.numpy as jnp
from jax import lax
from jax.experimental import pallas as pl
from jax.experimental.pallas import tpu as pltpu

F32 = jnp.float32
BF16 = jnp.bfloat16

EPS = 1e-6
ROPE_THETA = 10000.0
FOX_HEADS = 16
FOX_HEAD_DIM = 128
DSA_HEADS = 16
DSA_NOPE_DIM = 128
DSA_ROPE_DIM = 64
DSA_V_DIM = 128
Q_LORA = 1024
KV_LORA = 256
IDX_HEADS = 32
IDX_DIM = 128
DSA_TOPK_MAX = 256
MEM_HEADS = 4
MEM_HEAD_DIM = 128
N_GROUPS = 8
EXPERTS_PER_GROUP = 8
N_EXPERTS = N_GROUPS * EXPERTS_PER_GROUP
TOPK_IN_GROUP = 2
D_EXPERT = 512

FOX_WIDTH = FOX_HEADS * FOX_HEAD_DIM
LANES = 128
SUBLANES = 8
KV_PAD = KV_LORA + LANES
NEG_BIG = -1e30
LOG2E = 1.4426950408889634
FOX_QSCALE = FOX_HEAD_DIM ** -0.5 * LOG2E
FOX_HB = 4
DSA_QSCALE = (DSA_NOPE_DIM + DSA_ROPE_DIM) ** -0.5 * LOG2E
VMEM_LIMIT_BYTES = 56 * 1024 * 1024
ITEM_ROWS = 384
ITEM_SUB = 128
F_CHUNK = 256
KEY_CHUNK = 256


def _params(*sem):
    return pltpu.CompilerParams(dimension_semantics=sem, vmem_limit_bytes=VMEM_LIMIT_BYTES)


def _rmsnorm_kernel(x_ref, g_ref, o_ref):
    x = x_ref[...].astype(F32)
    ms = jnp.mean(x * x, axis=-1, keepdims=True)
    o_ref[...] = (x * lax.rsqrt(ms + EPS) * g_ref[...]).astype(o_ref.dtype)


def _rmsnorm(x2d, g, out_dtype, tm=512):
    n, d = x2d.shape
    tm = min(tm, n)
    assert n % tm == 0
    return pl.pallas_call(
        _rmsnorm_kernel,
        grid=(n // tm,),
        in_specs=[pl.BlockSpec((tm, d), lambda i: (i, 0)), pl.BlockSpec((1, d), lambda i: (0, 0))],
        out_specs=pl.BlockSpec((tm, d), lambda i: (i, 0)),
        out_shape=jax.ShapeDtypeStruct((n, d), out_dtype),
        compiler_params=_params("arbitrary"),
        name="rmsnorm",
    )(x2d, g.reshape(1, d).astype(F32))


_NT_DIMS = (((1,), (1,)), ((), ()))


def _w_rows_spec(tn, k, row0, index_of_j):
    if row0 % tn == 0:
        return pl.BlockSpec((tn, k), lambda *g: (index_of_j(*g) + row0 // tn, 0))
    assert row0 % SUBLANES == 0
    return pl.BlockSpec((pl.Element(tn), pl.Element(k)),
                        lambda *g: (pl.multiple_of(index_of_j(*g) * tn + row0, SUBLANES), 0))


def _mm_kernel(a_ref, b_ref, *rest, has_res, head_out, rope, w_rows, col_scale):
    pos = 0
    res_ref = cos_ref = sin_ref = None
    if has_res:
        res_ref = rest[pos]
        pos += 1
    if rope:
        cos_ref, sin_ref = rest[pos], rest[pos + 1]
        pos += 2
    o_ref = rest[pos]
    if b_ref.dtype == BF16:
        wb_ref = b_ref
    else:
        wb_ref = rest[pos + 1]

        @pl.when(pl.program_id(1) == 0)
        def _():
            wb_ref[...] = b_ref[...].astype(BF16)

    if w_rows:
        acc = lax.dot_general(a_ref[...], wb_ref[...], _NT_DIMS, preferred_element_type=F32)
    else:
        acc = jnp.dot(a_ref[...], wb_ref[...], preferred_element_type=F32)
    if col_scale is not None:
        acc = acc * jnp.where(pl.program_id(0) < col_scale[0], col_scale[1], 1.0)
    if has_res:
        acc = acc + res_ref[...]
    if head_out:
        for hh in range(acc.shape[1] // LANES):
            part = acc[:, hh * LANES:(hh + 1) * LANES]
            if rope:
                part = part * cos_ref[...] + pltpu.roll(part, LANES // 2, 1) * sin_ref[...]
            o_ref[hh] = part.astype(o_ref.dtype)
    else:
        o_ref[...] = acc.astype(o_ref.dtype)


def _mm(a, w, n_cols, *, col0=0, out_dtype=BF16, tm=1024, tn=512, res=None, heads=None, rope_tabs=None,
        w_rows=False, scaled_cols=None, name="mm"):
    m, k = a.shape
    tm = min(tm, m if heads is None else heads[1])
    assert n_cols % tn == 0 and m % tm == 0
    if w_rows:
        w_spec = _w_rows_spec(tn, k, col0, lambda j, i: j)
    else:
        assert col0 % tn == 0
        w_spec = pl.BlockSpec((k, tn), lambda j, i: (0, j + col0 // tn))
    in_specs = [pl.BlockSpec((tm, k), lambda j, i: (i, 0)), w_spec]
    args = [a, w]
    if res is not None:
        in_specs.append(pl.BlockSpec((tm, tn), lambda j, i: (i, j)))
        args.append(res)
    if heads is not None:
        bsz, s = heads
        assert s % tm == 0
        spt = s // tm
        if rope_tabs is not None:
            in_specs += [pl.BlockSpec((tm, LANES), lambda j, i: (i % spt, 0))] * 2
            args += list(rope_tabs)
        hpt = tn // LANES
        out_spec = pl.BlockSpec((None, hpt, tm, LANES), lambda j, i: (i // spt, j, i % spt, 0))
        out_shape = jax.ShapeDtypeStruct((bsz, n_cols // LANES, s, LANES), out_dtype)
    else:
        out_spec = pl.BlockSpec((tm, tn), lambda j, i: (i, j))
        out_shape = jax.ShapeDtypeStruct((m, n_cols), out_dtype)
    col_scale = None
    if scaled_cols is not None:
        assert scaled_cols[0] % tn == 0
        col_scale = (scaled_cols[0] // tn, scaled_cols[1])
    kern = functools.partial(_mm_kernel, has_res=res is not None, head_out=heads is not None,
                             rope=rope_tabs is not None, w_rows=w_rows, col_scale=col_scale)
    return pl.pallas_call(
        kern,
        grid=(n_cols // tn, m // tm),
        in_specs=in_specs,
        out_specs=out_spec,
        out_shape=out_shape,
        scratch_shapes=[] if w.dtype == BF16 else [pltpu.VMEM((tn, k) if w_rows else (k, tn), BF16)],
        compiler_params=_params("arbitrary", "arbitrary"),
        name=name,
    )(*args)


def _rope_tables(s, d):
    inv = jnp.power(ROPE_THETA, -jnp.arange(0, d, 2, dtype=F32) / d)
    ang = jnp.arange(s, dtype=jnp.int32).astype(F32)[:, None] * inv[None, :]
    cos, sin = jnp.cos(ang), jnp.sin(ang)
    cos_t = jnp.tile(jnp.concatenate([cos, cos], axis=-1), (1, LANES // d))
    sin_t = jnp.tile(jnp.concatenate([-sin, sin], axis=-1), (1, LANES // d))
    return cos_t, sin_t


def _rot_half64(x):
    lane = lax.broadcasted_iota(jnp.int32, x.shape, 1)
    return jnp.where((lane % 64) < 32, pltpu.roll(x, 96, 1), pltpu.roll(x, 32, 1))


_C_CQ = 0
_C_CKV = Q_LORA
_C_IK = _C_CKV + KV_LORA
_C_KR = _C_IK + IDX_DIM
_C_FA = _C_KR + LANES
_C_IW = _C_FA + LANES
SMALL_W = _C_IW + LANES


def _prep_kernel(s_ref, gq_ref, gkv_ref, gik_ref, bik_ref, c64_ref, s64_ref, c128_ref, s128_ref,
                 cq_ref, kv_ref, kidx_ref, widx_ref, fa_ref):
    cq = s_ref[:, _C_CQ:_C_CQ + Q_LORA]
    cq_ref[...] = (cq * lax.rsqrt(jnp.mean(cq * cq, axis=-1, keepdims=True) + EPS) * gq_ref[...]).astype(BF16)
    ckv = s_ref[:, _C_CKV:_C_CKV + KV_LORA]
    kv_ref[:, :KV_LORA] = (ckv * lax.rsqrt(jnp.mean(ckv * ckv, axis=-1, keepdims=True) + EPS)
                           * gkv_ref[...]).astype(BF16)
    kr = s_ref[:, _C_KR:_C_KR + LANES]
    kv_ref[:, KV_LORA:] = (kr * c64_ref[...] + _rot_half64(kr) * s64_ref[...]).astype(BF16)
    ik = s_ref[:, _C_IK:_C_IK + IDX_DIM]
    mu = jnp.mean(ik, axis=-1, keepdims=True)
    xc = ik - mu
    ikn = xc * lax.rsqrt(jnp.mean(xc * xc, axis=-1, keepdims=True) + EPS) * gik_ref[...] + bik_ref[...]
    kidx_ref[...] = (ikn * c128_ref[...] + pltpu.roll(ikn, LANES // 2, 1) * s128_ref[...]).astype(BF16)
    widx_ref[...] = s_ref[:, _C_IW:_C_IW + LANES] * (IDX_HEADS * IDX_DIM) ** -0.5
    fa_ref[...] = s_ref[:, _C_FA:_C_FA + LANES]


def _prep(small, g_q, g_kv, g_ik, b_ik, tabs64, tabs128, s, tm=256):
    n = small.shape[0]
    spt = s // tm
    row = lambda w: pl.BlockSpec((tm, w), lambda i: (i, 0))
    vec = lambda w: pl.BlockSpec((1, w), lambda i: (0, 0))
    tab = pl.BlockSpec((tm, LANES), lambda i: (i % spt, 0))
    return pl.pallas_call(
        _prep_kernel,
        grid=(n // tm,),
        in_specs=[row(SMALL_W), vec(Q_LORA), vec(KV_LORA), vec(IDX_DIM), vec(IDX_DIM), tab, tab, tab, tab],
        out_specs=[row(Q_LORA), row(KV_PAD), row(IDX_DIM), row(LANES), row(LANES)],
        out_shape=[jax.ShapeDtypeStruct((n, Q_LORA), BF16), jax.ShapeDtypeStruct((n, KV_PAD), BF16),
                   jax.ShapeDtypeStruct((n, IDX_DIM), BF16), jax.ShapeDtypeStruct((n, LANES), F32),
                   jax.ShapeDtypeStruct((n, LANES), F32)],
        compiler_params=_params("arbitrary"),
        name="mixer_prep",
    )(small, g_q.reshape(1, -1), g_kv.reshape(1, -1), g_ik.reshape(1, -1), b_ik.reshape(1, -1),
      tabs64[0], tabs64[1], tabs128[0], tabs128[1])


def _cum_kernel(fa_ref, bf_ref, cum_ref, carry_ref):
    @pl.when(pl.program_id(1) == 0)
    def _():
        carry_ref[...] = jnp.zeros_like(carry_ref)

    z = fa_ref[...] + bf_ref[...]
    logf = jnp.minimum(z, 0.0) - jnp.log1p(jnp.exp(-jnp.abs(z)))
    t = z.shape[0]
    tri = (lax.broadcasted_iota(jnp.int32, (t, t), 0) >= lax.broadcasted_iota(jnp.int32, (t, t), 1)).astype(F32)
    c = jnp.dot(tri, logf, precision=lax.Precision.HIGHEST, preferred_element_type=F32) + carry_ref[...]
    cum_ref[...] = c
    carry_ref[...] = c[t - 1:t, :]


def _fox_cum(fa, b_f_pad, bsz, s, t=256):
    return pl.pallas_call(
        _cum_kernel,
        grid=(bsz, s // t),
        in_specs=[pl.BlockSpec((None, t, LANES), lambda b, i: (b, i, 0)),
                  pl.BlockSpec((1, LANES), lambda b, i: (0, 0))],
        out_specs=pl.BlockSpec((None, t, LANES), lambda b, i: (b, i, 0)),
        out_shape=jax.ShapeDtypeStruct((bsz, s, LANES), F32),
        scratch_shapes=[pltpu.VMEM((1, LANES), F32)],
        compiler_params=_params("arbitrary", "arbitrary"),
        name="fox_cumsum",
    )(fa.reshape(bsz, s, LANES), b_f_pad)


def _lane_fold(x, op):
    out = x[:, :LANES]
    for c in range(1, x.shape[1] // LANES):
        out = op(out, x[:, c * LANES:(c + 1) * LANES])
    return out


def _split3(x):
    t1 = x.astype(BF16)
    r = x - t1.astype(F32)
    t2 = r.astype(BF16)
    t3 = (r - t2.astype(F32)).astype(BF16)
    return t1, t2, t3


def _gate_lanes(parts, own_first):
    t1, t2, t3 = parts
    lane = lax.broadcasted_iota(jnp.int32, t1.shape, 1)
    a, b = (0, 3) if own_first else (3, 0)
    ones3 = jnp.where((lane >= b) & (lane < b + 3), 1.0, 0.0)
    out = jnp.where(lane == a, t1, jnp.where(lane == a + 1, t2, jnp.where(lane == a + 2, t3, ones3)))
    return out.astype(BF16)


def _fox_kernel(q_ref, k_ref, v_ref, cum_ref, o_ref, kf_ref, vf_ref, s_ref, *, t):
    hp = pl.program_id(1)
    i = pl.program_id(2)
    d = FOX_HEAD_DIM
    heads = range(FOX_HB)
    row = lax.broadcasted_iota(jnp.int32, (LANES, LANES), 0)

    def gate_columns(c):
        terms = _split3(c * LOG2E)
        out = []
        for hh in heads:
            pick = (row == hp * FOX_HB + hh).astype(BF16)
            out.append([jnp.dot(term, pick, preferred_element_type=F32) for term in terms])
        return out

    @pl.when(i == 0)
    def _():
        ck = gate_columns(-cum_ref[...])
        for hh in heads:
            kf_ref[hh, :, :d] = k_ref[:, hh * d:(hh + 1) * d]
            kf_ref[hh, :, d:] = _gate_lanes(ck[hh], True)
            vf_ref[hh, :, :d] = v_ref[:, hh * d:(hh + 1) * d]
            vf_ref[hh, :, d:] = jnp.ones((vf_ref.shape[1], LANES), BF16)

    cq = gate_columns(cum_ref[pl.ds(pl.multiple_of(i * t, t), t), :])
    qs = [jnp.concatenate([q_ref[:, hh * d:(hh + 1) * d], _gate_lanes(cq[hh], False)], axis=1) for hh in heads]

    def logits(hh, kc):
        kb = kf_ref[hh, pl.ds(pl.multiple_of(kc * t, t), t), :]
        return lax.dot_general(qs[hh], kb, _NT_DIMS, preferred_element_type=F32)

    def pass1(kc, m_lanes):
        out = []
        for hh in heads:
            s = logits(hh, kc)
            s_ref[hh, kc] = s
            out.append(jnp.maximum(m_lanes[hh], _lane_fold(s, jnp.maximum)))
        return tuple(out)

    m_lanes = lax.fori_loop(0, i, pass1, tuple(jnp.full((t, LANES), NEG_BIG, F32) for _ in heads))
    causal = lax.broadcasted_iota(jnp.int32, (t, t), 0) >= lax.broadcasted_iota(jnp.int32, (t, t), 1)
    ms = []
    for hh in heads:
        s = jnp.where(causal, logits(hh, i), NEG_BIG)
        s_ref[hh, i] = s
        m_lane = jnp.maximum(m_lanes[hh], _lane_fold(s, jnp.maximum))
        m = jnp.broadcast_to(jnp.max(m_lane, axis=1, keepdims=True), (t, LANES))
        ms.append(jnp.concatenate([m] * (t // LANES), axis=1))

    def pass2(kc, accs):
        out = []
        for hh in heads:
            p = jnp.exp2(s_ref[hh, kc] - ms[hh])
            vb = vf_ref[hh, pl.ds(pl.multiple_of(kc * t, t), t), :]
            out.append(accs[hh] + jnp.dot(p.astype(BF16), vb, preferred_element_type=F32))
        return tuple(out)

    accs = lax.fori_loop(0, i + 1, pass2, tuple(jnp.zeros((t, d + LANES), F32) for _ in heads))
    for hh in heads:
        o_ref[:, hh * d:(hh + 1) * d] = (accs[hh][:, :d] / accs[hh][:, d:]).astype(o_ref.dtype)


def _fox_attention(qkv, cum, bsz, s, t=512):
    n = bsz * s
    nq = s // t
    hp = FOX_HEADS // FOX_HB
    w = FOX_HB * LANES
    assert FOX_HEAD_DIM == LANES and FOX_HEADS % FOX_HB == 0
    kern = functools.partial(_fox_kernel, t=t)
    return pl.pallas_call(
        kern,
        grid=(bsz, hp, nq),
        in_specs=[pl.BlockSpec((t, w), lambda b, h, i: (b * nq + i, h)),
                  pl.BlockSpec((s, w), lambda b, h, i: (b, hp + h)),
                  pl.BlockSpec((s, w), lambda b, h, i: (b, 2 * hp + h)),
                  pl.BlockSpec((None, s, LANES), lambda b, h, i: (b, 0, 0))],
        out_specs=pl.BlockSpec((t, w), lambda b, h, i: (b * nq + i, h)),
        out_shape=jax.ShapeDtypeStruct((n, FOX_WIDTH), BF16),
        scratch_shapes=[pltpu.VMEM((FOX_HB, s, 2 * LANES), BF16), pltpu.VMEM((FOX_HB, s, 2 * LANES), BF16),
                        pltpu.VMEM((FOX_HB, nq, t, t), F32)],
        compiler_params=_params("arbitrary", "arbitrary", "arbitrary"),
        name="fox_attention",
    )(qkv, qkv, qkv, cum)


def _qlat_kernel(cq_ref, wn_ref, wr_ref, wuk_ref, c64_ref, s64_ref, o_ref):
    cq = cq_ref[...]
    qn = jnp.dot(cq, wn_ref[...], preferred_element_type=F32).astype(BF16)
    qr = jnp.dot(cq, wr_ref[...], preferred_element_type=F32)
    for h in range(DSA_HEADS):
        cols = slice(h * LANES, (h + 1) * LANES)
        lat = jnp.dot(qn[:, cols], wuk_ref[h], preferred_element_type=F32)
        o_ref[h, :, :KV_LORA] = (lat * DSA_QSCALE).astype(BF16)
        r = qr[:, cols]
        o_ref[h, :, KV_LORA:] = ((r * c64_ref[...] + _rot_half64(r) * s64_ref[...]) * DSA_QSCALE).astype(BF16)


def _q_latent(cq_n, w_nope_b, w_rope_b, w_uk_b, tabs64, bsz, s, tm=512):
    spt = s // tm
    kq = cq_n.shape[1]
    wide = DSA_HEADS * LANES
    return pl.pallas_call(
        _qlat_kernel,
        grid=(bsz * spt,),
        in_specs=[pl.BlockSpec((tm, kq), lambda i: (i, 0)),
                  pl.BlockSpec((kq, wide), lambda i: (0, 0)),
                  pl.BlockSpec((kq, wide), lambda i: (0, 0)),
                  pl.BlockSpec((DSA_HEADS, DSA_NOPE_DIM, KV_LORA), lambda i: (0, 0, 0)),
                  pl.BlockSpec((tm, LANES), lambda i: (i % spt, 0)),
                  pl.BlockSpec((tm, LANES), lambda i: (i % spt, 0))],
        out_specs=pl.BlockSpec((None, DSA_HEADS, tm, KV_PAD), lambda i: (i // spt, 0, i % spt, 0)),
        out_shape=jax.ShapeDtypeStruct((bsz, DSA_HEADS, s, KV_PAD), BF16),
        compiler_params=_params("arbitrary"),
        name="q_latent",
    )(cq_n, w_nope_b, w_rope_b, w_uk_b, tabs64[0], tabs64[1])


def _needed_chunks(i, tq):
    return ((i + 1) * tq + KEY_CHUNK - 1) // KEY_CHUNK


BISECT_MAX_STEPS = 320
BISECT_UNROLL = 4


def _indexer_kernel(q_ref, k_ref, wt_ref, o_ref, sc_ref, thr_ref, cut_ref, *, tq, topk, hg):
    i = pl.program_id(1)
    tk = KEY_CHUNK
    n_chunks = sc_ref.shape[0]
    nk = _needed_chunks(i, tq)
    qpos = i * tq + lax.broadcasted_iota(jnp.int32, (tk, tq), 1)
    krow = lax.broadcasted_iota(jnp.int32, (tk, tq), 0)
    inf = jnp.inf

    def score_chunk(kc, c):
        k0 = pl.multiple_of(kc * tk, tk)
        kb = k_ref[pl.ds(k0, tk), :]
        sc = jnp.zeros((tk, tq), F32)
        for g in range(IDX_HEADS // hg):
            q2 = q_ref[g * hg:(g + 1) * hg].reshape(hg * tq, IDX_DIM)
            d = lax.dot_general(kb, q2, _NT_DIMS, preferred_element_type=F32)
            for hh in range(hg):
                h = g * hg + hh
                sc = sc + wt_ref[h:h + 1, :] * jnp.maximum(d[:, hh * tq:(hh + 1) * tq], 0.0)
        sc_ref[kc] = jnp.where(k0 + krow <= qpos, sc, -inf)
        return c

    lax.fori_loop(0, nk, score_chunk, 0)

    def pad_chunk(kc, c):
        sc_ref[kc] = jnp.full((tk, tq), -inf, F32)
        return c

    lax.fori_loop(nk, n_chunks, pad_chunk, 0)
    kf = float(topk)
    n_causal = (i * tq + lax.broadcasted_iota(jnp.int32, (1, tq), 1) + 1).astype(F32)

    def fold(x, op):
        return op(x.reshape(tk // SUBLANES, SUBLANES, tq), axis=0)

    def key_pos(kc):
        return (kc * tk + krow).astype(F32)

    def threshold(n_counted):
        def count_ge(t):
            acc = jnp.zeros((SUBLANES, tq), F32)
            for kc in range(n_counted):
                acc = acc + fold(jnp.where(sc_ref[kc] >= t, 1.0, 0.0), jnp.sum)
            return jnp.sum(acc, axis=0, keepdims=True)

        hi = jnp.full((SUBLANES, tq), -inf, F32)
        lo = jnp.full((SUBLANES, tq), inf, F32)
        for kc in range(n_counted):
            s = sc_ref[kc]
            hi = jnp.maximum(hi, fold(s, jnp.max))
            lo = jnp.minimum(lo, fold(jnp.where(s == -inf, inf, s), jnp.min))
        hi = jnp.max(hi, axis=0, keepdims=True)
        lo = jnp.min(lo, axis=0, keepdims=True)
        few = n_causal <= kf
        n_max = count_ge(hi)
        at_max = n_max >= kf
        lo = jnp.where(few, -inf, jnp.where(at_max, hi, lo))
        n_lo = jnp.where(at_max, n_max, n_causal)
        done = jnp.where(few | at_max, 1.0, 0.0)

        def unsettled(state):
            it, _, _, _, done = state
            return (it < BISECT_MAX_STEPS) & (jnp.sum(1.0 - done) > 0.0)

        def step(state):
            it, lo, hi, n_lo, done = state
            for _ in range(BISECT_UNROLL):
                mid = 0.5 * lo + 0.5 * hi
                cnt = count_ge(mid)
                ge = cnt >= kf
                closed = (mid <= lo) | (mid >= hi)
                live = done == 0.0
                lo = jnp.where(live & ge, mid, lo)
                n_lo = jnp.where(live & ge, cnt, n_lo)
                hi = jnp.where(live & jnp.logical_not(ge), mid, hi)
                done = jnp.where(live & ((ge & (cnt == kf)) | closed), 1.0, done)
            return it + BISECT_UNROLL, lo, hi, n_lo, done

        _, lo, _, n_lo, _ = lax.while_loop(unsettled, step, (jnp.int32(0), lo, hi, n_lo, done))
        thr_ref[...] = lo
        cut_ref[...] = jnp.full((1, tq), float(n_chunks * tk), F32)

        over = jnp.where((n_lo > kf) & jnp.logical_not(few), 1.0, 0.0)

        @pl.when(jnp.sum(over) > 0.0)
        def _():
            def count_where(pred):
                def body(kc, acc):
                    hit = pred(sc_ref[kc], key_pos(kc))
                    return acc + fold(jnp.where(hit, 1.0, 0.0), jnp.sum)

                acc = lax.fori_loop(0, n_counted, body, jnp.zeros((SUBLANES, tq), F32))
                return jnp.sum(acc, axis=0, keepdims=True)

            need = kf - count_where(lambda s, kp: s > lo)

            def narrow(_, bracket):
                lo_p, hi_p = bracket
                mid = jnp.floor(0.5 * (lo_p + hi_p))
                ok = count_where(lambda s, kp: (s == lo) & (kp <= mid)) >= need
                return jnp.where(ok, lo_p, mid), jnp.where(ok, mid, hi_p)

            n_pos = n_counted * tk
            _, hi_p = lax.fori_loop(0, n_pos.bit_length(), narrow,
                                    (jnp.full((1, tq), -1.0, F32), jnp.full((1, tq), n_pos - 1.0, F32)))
            cut_ref[...] = jnp.where(over > 0.0, hi_p, cut_ref[...])

    prev = 0
    for level in sorted({max(n_chunks // 4, 1), max(n_chunks // 2, 1), n_chunks}):
        pl.when((nk > prev) & (nk <= level))(functools.partial(threshold, level))
        prev = level
    thr = thr_ref[...]
    cut = cut_ref[...]

    def write_chunk(kc, c):
        s = sc_ref[kc]
        keep = ((s > thr) | ((s == thr) & (key_pos(kc) <= cut))) & (kc * tk + krow <= qpos)
        o_ref[kc] = jnp.where(keep, 0.0, NEG_BIG).T.astype(o_ref.dtype)
        return c

    lax.fori_loop(0, nk, write_chunk, 0)

    def fill_chunk(kc, c):
        o_ref[kc] = jnp.full((tq, tk), NEG_BIG, o_ref.dtype)
        return c

    lax.fori_loop(nk, n_chunks, fill_chunk, 0)


def _indexer_mask(q_idx, k_idx, w_idx_t, bsz, s, topk, tq=128, hg=4):
    kern = functools.partial(_indexer_kernel, tq=tq, topk=topk, hg=hg)
    nc = s // KEY_CHUNK
    return pl.pallas_call(
        kern,
        grid=(bsz, s // tq),
        in_specs=[pl.BlockSpec((None, IDX_HEADS, tq, IDX_DIM), lambda b, i: (b, 0, i, 0)),
                  pl.BlockSpec((None, s, IDX_DIM), lambda b, i: (b, 0, 0)),
                  pl.BlockSpec((None, IDX_HEADS, tq), lambda b, i: (b, 0, i))],
        out_specs=pl.BlockSpec((None, nc, tq, KEY_CHUNK), lambda b, i: (b, 0, i, 0)),
        out_shape=jax.ShapeDtypeStruct((bsz, nc, s, KEY_CHUNK), BF16),
        scratch_shapes=[pltpu.VMEM((nc, KEY_CHUNK, tq), F32), pltpu.VMEM((1, tq), F32),
                        pltpu.VMEM((1, tq), F32)],
        compiler_params=_params("arbitrary", "arbitrary"),
        name="indexer_topk_mask",
    )(q_idx, k_idx, w_idx_t)


def _dsa_kernel(q_ref, kvt_ref, kv_ref, bias_ref, o_ref, s_ref, m_ref, l_ref, acc_ref, *, tq):
    i = pl.program_id(1)
    tk = KEY_CHUNK
    rows = DSA_HEADS * tq
    nk = _needed_chunks(i, tq)
    q2 = q_ref[...].reshape(rows, KV_PAD)
    m_ref[...] = jnp.full(m_ref.shape, NEG_BIG, F32)

    n_pairs = (nk + 1) // 2

    def logits(kc):
        sc = jnp.dot(q2, kvt_ref[kc], preferred_element_type=F32)
        sc = (sc.reshape(DSA_HEADS, tq, tk) + bias_ref[kc].astype(F32)[None]).reshape(rows, tk)
        s_ref[kc] = sc
        return _lane_fold(sc, jnp.maximum)

    def pass1(pr, c):
        m_ref[...] = jnp.maximum(m_ref[...], jnp.maximum(logits(2 * pr), logits(2 * pr + 1)))
        return c

    lax.fori_loop(0, n_pairs, pass1, 0)
    m_ref[...] = jnp.broadcast_to(jnp.max(m_ref[...], axis=1, keepdims=True), m_ref.shape)
    l_ref[...] = jnp.zeros(l_ref.shape, F32)
    acc_ref[...] = jnp.zeros(acc_ref.shape, F32)

    def probs(kc):
        m = m_ref[...]
        p = jnp.concatenate([jnp.exp2(s_ref[kc, :, g * LANES:(g + 1) * LANES] - m)
                             for g in range(tk // LANES)], axis=1)
        kvb = kv_ref[pl.ds(pl.multiple_of(kc * tk, tk), tk), :KV_LORA]
        return _lane_fold(p, jnp.add), jnp.dot(p.astype(BF16), kvb, preferred_element_type=F32)

    def pass2(pr, c):
        l0, a0 = probs(2 * pr)
        l1, a1 = probs(2 * pr + 1)
        l_ref[...] += l0 + l1
        acc_ref[...] += a0 + a1
        return c

    lax.fori_loop(0, n_pairs, pass2, 0)
    l = jnp.dot(l_ref[...], jnp.ones((LANES, LANES), F32), precision=lax.Precision.HIGHEST,
                preferred_element_type=F32)
    o = acc_ref[...] / jnp.concatenate([l] * (KV_LORA // LANES), axis=1)
    o_ref[...] = o.reshape(DSA_HEADS, tq, KV_LORA).astype(o_ref.dtype)


def _dsa_attention(q_cat, kv_t, kv, bias, bsz, s, tq=128):
    kern = functools.partial(_dsa_kernel, tq=tq)
    nc = s // KEY_CHUNK
    assert nc % 2 == 0, "the kernel walks key chunks in pairs"
    rows = DSA_HEADS * tq
    return pl.pallas_call(
        kern,
        grid=(bsz, s // tq),
        in_specs=[pl.BlockSpec((None, DSA_HEADS, tq, KV_PAD), lambda b, i: (b, 0, i, 0)),
                  pl.BlockSpec((None, nc, KV_PAD, KEY_CHUNK), lambda b, i: (b, 0, 0, 0)),
                  pl.BlockSpec((None, s, KV_PAD), lambda b, i: (b, 0, 0)),
                  pl.BlockSpec((None, nc, tq, KEY_CHUNK), lambda b, i: (b, 0, i, 0))],
        out_specs=pl.BlockSpec((None, DSA_HEADS, tq, KV_LORA), lambda b, i: (b, 0, i, 0)),
        out_shape=jax.ShapeDtypeStruct((bsz, DSA_HEADS, s, KV_LORA), BF16),
        scratch_shapes=[pltpu.VMEM((nc, rows, KEY_CHUNK), F32), pltpu.VMEM((rows, LANES), F32),
                        pltpu.VMEM((rows, LANES), F32), pltpu.VMEM((rows, KV_LORA), F32)],
        compiler_params=_params("arbitrary", "arbitrary"),
        name="dsa_attention",
    )(q_cat, kv_t, kv, bias)


def _uv_kernel(o_ref, w_ref, y_ref):
    for h in range(DSA_HEADS):
        y_ref[:, h * DSA_V_DIM:(h + 1) * DSA_V_DIM] = jnp.dot(
            o_ref[h], w_ref[h], preferred_element_type=F32).astype(y_ref.dtype)


def _uv_project(o_lat, w_uv_b, bsz, s, tm=512):
    spt = s // tm
    return pl.pallas_call(
        _uv_kernel,
        grid=(bsz * spt,),
        in_specs=[pl.BlockSpec((None, DSA_HEADS, tm, KV_LORA), lambda i: (i // spt, 0, i % spt, 0)),
                  pl.BlockSpec((DSA_HEADS, KV_LORA, DSA_V_DIM), lambda i: (0, 0, 0))],
        out_specs=pl.BlockSpec((tm, DSA_HEADS * DSA_V_DIM), lambda i: (i, 0)),
        out_shape=jax.ShapeDtypeStruct((bsz * s, DSA_HEADS * DSA_V_DIM), BF16),
        compiler_params=_params("arbitrary"),
        name="uv_project",
    )(o_lat, w_uv_b)


def _merge_kernel(h_ref, a_ref, b_ref, wga_ref, wgb_ref, wua_ref, wub_ref, o_ref,
                  cga_ref, cgb_ref, cua_ref, cub_ref):
    @pl.when(pl.program_id(1) == 0)
    def _():
        cga_ref[...] = wga_ref[...].astype(BF16)
        cgb_ref[...] = wgb_ref[...].astype(BF16)
        cua_ref[...] = wua_ref[...].astype(BF16)
        cub_ref[...] = wub_ref[...].astype(BF16)

    h = h_ref[...]
    ga = lax.dot_general(h, cga_ref[...], _NT_DIMS, preferred_element_type=F32)
    gb = lax.dot_general(h, cgb_ref[...], _NT_DIMS, preferred_element_type=F32)
    ya = jnp.dot(a_ref[...], cua_ref[...], preferred_element_type=F32)
    yb = jnp.dot(b_ref[...], cub_ref[...], preferred_element_type=F32)
    o_ref[...] = (jax.nn.sigmoid(ga) * ya + jax.nn.sigmoid(gb) * yb).astype(o_ref.dtype)


def _gated_merge(h1, att_a, att_b, w_t, row_ga, row_gb, w_up_a, w_up_b, tm=512, tn=256):
    n, d = h1.shape
    ka, kb = att_a.shape[1], att_b.shape[1]
    return pl.pallas_call(
        _merge_kernel,
        grid=(d // tn, n // tm),
        in_specs=[pl.BlockSpec((tm, d), lambda j, i: (i, 0)),
                  pl.BlockSpec((tm, ka), lambda j, i: (i, 0)),
                  pl.BlockSpec((tm, kb), lambda j, i: (i, 0)),
                  _w_rows_spec(tn, d, row_ga, lambda j, i: j),
                  _w_rows_spec(tn, d, row_gb, lambda j, i: j),
                  pl.BlockSpec((ka, tn), lambda j, i: (0, j)),
                  pl.BlockSpec((kb, tn), lambda j, i: (0, j))],
        out_specs=pl.BlockSpec((tm, tn), lambda j, i: (i, j)),
        out_shape=jax.ShapeDtypeStruct((n, d), BF16),
        scratch_shapes=[pltpu.VMEM((tn, d), BF16), pltpu.VMEM((tn, d), BF16),
                        pltpu.VMEM((ka, tn), BF16), pltpu.VMEM((kb, tn), BF16)],
        compiler_params=_params("arbitrary", "arbitrary"),
        name="gated_merge",
    )(h1, att_a, att_b, w_t, w_t, w_up_a, w_up_b)


def _mem_kernel(x_ref, g_ref, wq_ref, kt_ref, v_ref, wom_ref, o_ref):
    x = x_ref[...]
    hn = (x * lax.rsqrt(jnp.mean(x * x, axis=-1, keepdims=True) + EPS) * g_ref[...]).astype(BF16)
    q = jnp.dot(hn, wq_ref[...], preferred_element_type=F32).astype(BF16)
    scale = MEM_HEAD_DIM ** -0.5
    outs = []
    for h in range(MEM_HEADS):
        sl = slice(h * MEM_HEAD_DIM, (h + 1) * MEM_HEAD_DIM)
        s = jnp.dot(q[:, sl], kt_ref[sl, :], preferred_element_type=F32) * scale
        m = jnp.max(s, axis=-1, keepdims=True)
        p = jnp.exp(s - m)
        p = p / jnp.sum(p, axis=-1, keepdims=True)
        outs.append(jnp.dot(p.astype(BF16), v_ref[:, sl], preferred_element_type=F32).astype(BF16))
    o = jnp.concatenate(outs, axis=-1)
    o_ref[...] = x + jnp.dot(o, wom_ref[...], preferred_element_type=F32)


def _mem_attention(x1, g_x, w_qm_b, km_t, kv_m, w_om_b, bsz, s, n_mem, tm=256):
    spt = s // tm
    hd = MEM_HEADS * MEM_HEAD_DIM
    d = x1.shape[1]
    return pl.pallas_call(
        _mem_kernel,
        grid=(bsz, spt),
        in_specs=[pl.BlockSpec((tm, d), lambda b, i: (b * spt + i, 0)),
                  pl.BlockSpec((1, d), lambda b, i: (0, 0)),
                  pl.BlockSpec((d, hd), lambda b, i: (0, 0)),
                  pl.BlockSpec((None, hd, n_mem), lambda b, i: (b, 0, 0)),
                  pl.BlockSpec((n_mem, hd), lambda b, i: (b, 1)),
                  pl.BlockSpec((hd, d), lambda b, i: (0, 0))],
        out_specs=pl.BlockSpec((tm, d), lambda b, i: (b * spt + i, 0)),
        out_shape=jax.ShapeDtypeStruct(x1.shape, F32),
        compiler_params=_params("arbitrary", "arbitrary"),
        name="memory_cross_attention",
    )(x1, g_x.reshape(1, d), w_qm_b, km_t, kv_m, w_om_b)


def _pack_bf16_pairs(x):
    c = x.shape[1] // 2
    return pltpu.pack_elementwise([x[:, :c], x[:, c:]], packed_dtype=BF16)


def _unpack_bf16_pairs(w):
    lo = pltpu.unpack_elementwise(w, index=0, packed_dtype=BF16, unpacked_dtype=F32)
    hi = pltpu.unpack_elementwise(w, index=1, packed_dtype=BF16, unpacked_dtype=F32)
    return lo.astype(BF16), hi.astype(BF16)


def _router_kernel(x_ref, g_ref, wr_ref, br_ref, h_ref, idx_ref, wt_ref, cnt_ref, carry_ref):
    @pl.when(pl.program_id(0) == 0)
    def _():
        carry_ref[...] = jnp.zeros_like(carry_ref)

    x = x_ref[...]
    hn = x * lax.rsqrt(jnp.mean(x * x, axis=-1, keepdims=True) + EPS) * g_ref[...]
    h_ref[...] = _pack_bf16_pairs(hn)
    hn_hi = hn.astype(BF16)
    hn_lo = (hn - hn_hi.astype(F32)).astype(BF16)
    wcat = wr_ref[...]
    a = jnp.dot(hn_hi, wcat, preferred_element_type=F32)
    b = jnp.dot(hn_lo, wcat[:, :LANES], preferred_element_type=F32)
    logits = a[:, :LANES] + (a[:, LANES:] + b) + br_ref[...]
    tm = x.shape[0]
    lane_i = lax.broadcasted_iota(jnp.int32, (tm, LANES), 1)
    lane = lane_i.astype(F32)
    lane_grp = lax.shift_right_logical(lane_i, EXPERTS_PER_GROUP.bit_length() - 1).astype(F32)
    big = 1e9
    is_g = (lane_i >= N_EXPERTS) & (lane_i < N_EXPERTS + N_GROUPS)
    lg = jnp.where(is_g, logits, -jnp.inf)
    mg = jnp.max(lg, axis=1, keepdims=True)
    gate_g = 1.0 / jnp.sum(jnp.exp(lg - mg), axis=1, keepdims=True)
    grp = jnp.min(jnp.where(lg == mg, lane, big), axis=1, keepdims=True) - N_EXPERTS
    in_grp = (lane_i < N_EXPERTS) & (lane_grp == grp)
    le = jnp.where(in_grp, logits, -jnp.inf)
    m1 = jnp.max(le, axis=1, keepdims=True)
    e1 = jnp.min(jnp.where(le == m1, lane, big), axis=1, keepdims=True)
    le2 = jnp.where(lane == e1, -jnp.inf, le)
    m2 = jnp.max(le2, axis=1, keepdims=True)
    e2 = jnp.min(jnp.where(le2 == m2, lane, big), axis=1, keepdims=True)
    r = jnp.exp(m2 - m1)
    w1 = gate_g * (1.0 / (1.0 + r))
    w2 = gate_g * (r / (1.0 + r))
    oh1 = (lane == e1).astype(F32)
    oh2 = (lane == e2).astype(F32)
    oh = oh1 + oh2
    tri = (lax.broadcasted_iota(jnp.int32, (tm, tm), 0) > lax.broadcasted_iota(jnp.int32, (tm, tm), 1)).astype(BF16)
    prefix = jnp.dot(tri, oh.astype(BF16), preferred_element_type=F32) + carry_ref[...]
    rank1 = jnp.sum(prefix * oh1, axis=1, keepdims=True)
    rank2 = jnp.sum(prefix * oh2, axis=1, keepdims=True)
    new_cnt = carry_ref[...] + jnp.sum(oh, axis=0, keepdims=True)
    carry_ref[...] = new_cnt
    cnt_ref[...] = new_cnt
    idx_ref[...] = jnp.where(lane_i == 0, e1, jnp.where(lane_i == 1, e2, jnp.where(
        lane_i == 2, rank1, jnp.where(lane_i == 3, rank2, 0.0)))).astype(jnp.int32)
    wt_ref[...] = jnp.where(lane_i == 0, w1, jnp.where(lane_i == 1, w2, 0.0))


def _router(x2, g, w_r, b_r, tm=256):
    n, d = x2.shape
    row = lambda w: pl.BlockSpec((tm, w), lambda i: (i, 0))
    return pl.pallas_call(
        _router_kernel,
        grid=(n // tm,),
        in_specs=[row(d), pl.BlockSpec((1, d), lambda i: (0, 0)), pl.BlockSpec((d, 2 * LANES), lambda i: (0, 0)),
                  pl.BlockSpec((1, LANES), lambda i: (0, 0))],
        out_specs=[row(d // 2), row(LANES), row(LANES), pl.BlockSpec((1, LANES), lambda i: (0, 0))],
        out_shape=[jax.ShapeDtypeStruct((n, d // 2), jnp.uint32), jax.ShapeDtypeStruct((n, LANES), jnp.int32),
                   jax.ShapeDtypeStruct((n, LANES), F32), jax.ShapeDtypeStruct((1, LANES), F32)],
        scratch_shapes=[pltpu.VMEM((1, LANES), F32)],
        compiler_params=_params("arbitrary"),
        name="moe_router",
    )(x2, g.reshape(1, d), w_r, b_r)


def _row_copy(src_hbm, row, dst, r, sem):
    return pltpu.make_async_copy(src_hbm.at[pl.ds(row, 1), :], dst.at[pl.ds(r, 1), :], sem)


def _sorted_rows(m):
    return m + N_EXPERTS * SUBLANES + ITEM_ROWS


def _dispatch_kernel(dest_ref, seg0_ref, cnt_ref, h_ref, x_hbm, zero_ref, sem, *, tm, m):
    i = pl.program_id(0)

    @pl.when(i == 0)
    def _():
        zero_ref[...] = jnp.zeros_like(zero_ref)
        tail = [pltpu.make_async_copy(zero_ref, x_hbm.at[pl.ds(r0, ITEM_SUB), :], sem)
                for r0 in range(m, _sorted_rows(m), ITEM_SUB)]
        for c in tail:
            c.start()
        for c in tail:
            c.wait()

        def fill_gap(e, total):
            gap = (-cnt_ref[e]) & (SUBLANES - 1)

            def one(r, c):
                _row_copy(zero_ref, 0, x_hbm, seg0_ref[e] + cnt_ref[e] + r, sem).start()
                return c

            lax.fori_loop(0, gap, one, 0)
            return total + gap

        n_gap = lax.fori_loop(0, N_EXPERTS, fill_gap, 0)

        def wait_gap(r, c):
            _row_copy(zero_ref, 0, x_hbm, 0, sem).wait()
            return c

        lax.fori_loop(0, n_gap, wait_gap, 0)

    def issue(r, c):
        t = i * tm + r
        for sl in range(TOPK_IN_GROUP):
            _row_copy(h_ref, r, x_hbm, dest_ref[TOPK_IN_GROUP * t + sl], sem).start(priority=sl % 2)
        return c

    lax.fori_loop(0, tm, issue, 0, unroll=4)
    for _ in range(TOPK_IN_GROUP):
        pltpu.make_async_copy(h_ref, x_hbm.at[pl.ds(0, tm), :], sem).wait()


def _dispatch(dest, seg_start, counts, h3p, tm=256):
    n, c = h3p.shape
    m = dest.shape[0]
    assert (_sorted_rows(m) - m) % ITEM_SUB == 0
    grid_spec = pltpu.PrefetchScalarGridSpec(
        num_scalar_prefetch=3,
        grid=(n // tm,),
        in_specs=[pl.BlockSpec((tm, c), lambda i, d, s0, cn: (i, 0))],
        out_specs=pl.BlockSpec(memory_space=pl.ANY),
        scratch_shapes=[pltpu.VMEM((ITEM_SUB, c), h3p.dtype), pltpu.SemaphoreType.DMA(())],
    )
    return pl.pallas_call(
        functools.partial(_dispatch_kernel, tm=tm, m=m),
        grid_spec=grid_spec,
        out_shape=jax.ShapeDtypeStruct((_sorted_rows(m), c), h3p.dtype),
        compiler_params=_params("arbitrary"),
        name="moe_dispatch",
    )(dest, seg_start, counts, h3p)


def _expert_kernel(ie_ref, is_ref, in_ref, ni_ref, x_hbm, wg_ref, wu_ref, wd_ref, y_hbm,
                   xin_ref, xb_ref, acc_ref, yp_ref, in_sem, out_sem, *, m, nj):
    i = pl.program_id(0)
    j = pl.program_id(1)
    n_items = ni_ref[0]
    valid = i < n_items
    slot = lax.rem(i, 2)
    half = xb_ref.shape[1] // 2
    n_sub = ITEM_ROWS // ITEM_SUB

    def first_row(item):
        return pl.multiple_of(is_ref[item], SUBLANES)

    def blocks(item):
        return (in_ref[item] + ITEM_SUB - 1) // ITEM_SUB

    def each_block(item, fn):
        for sub in range(n_sub):
            pl.when(sub * ITEM_SUB < in_ref[item])(functools.partial(fn, sub))

    def in_copy(item, sub):
        rows = pl.ds(sub * ITEM_SUB, ITEM_SUB)
        return pltpu.make_async_copy(x_hbm.at[pl.ds(first_row(item) + sub * ITEM_SUB, ITEM_SUB), :],
                                     xin_ref.at[rows, :], in_sem)

    def out_copy(item, sl, sub):
        rows = pl.ds(sub * ITEM_SUB, ITEM_SUB)
        return pltpu.make_async_copy(yp_ref.at[sl, rows, :],
                                     y_hbm.at[pl.ds(first_row(item) + sub * ITEM_SUB, ITEM_SUB), :], out_sem.at[sl])

    def by_block_count(item, fn):
        for nb in range(1, n_sub + 1):
            pl.when(blocks(item) == nb)(functools.partial(fn, nb))

    @pl.when((i == 0) & (j == 0))
    def _():
        yp_ref[1, :ITEM_SUB, :] = jnp.zeros((ITEM_SUB, yp_ref.shape[2]), yp_ref.dtype)
        tail = [pltpu.make_async_copy(yp_ref.at[1, pl.ds(0, ITEM_SUB), :],
                                      y_hbm.at[pl.ds(r0, ITEM_SUB), :], out_sem.at[1])
                for r0 in range(m, _sorted_rows(m), ITEM_SUB)]
        for c in tail:
            c.start()
        for c in tail:
            c.wait()
        each_block(0, lambda sub: in_copy(0, sub).start())

    @pl.when(valid & (j == 0))
    def _():
        each_block(i, lambda sub: in_copy(i, sub).wait())

        def unpack(nb):
            rows = pl.ds(0, nb * ITEM_SUB)
            lo, hi = _unpack_bf16_pairs(xin_ref[rows, :])
            xb_ref[rows, :half] = lo
            xb_ref[rows, half:] = hi
            acc_ref[rows, :] = jnp.zeros((nb * ITEM_SUB, acc_ref.shape[1]), F32)

        by_block_count(i, unpack)

        @pl.when(i + 1 < n_items)
        def _():
            each_block(i + 1, lambda sub: in_copy(i + 1, sub).start())

    @pl.when(valid)
    def _():
        wg = wg_ref[...].astype(BF16)
        wu = wu_ref[...].astype(BF16)
        wd = wd_ref[...].astype(BF16)

        def mlp(nb):
            rows = pl.ds(0, nb * ITEM_SUB)
            xb = xb_ref[rows, :]
            g = jnp.dot(xb, wg, preferred_element_type=F32)
            u = jnp.dot(xb, wu, preferred_element_type=F32)
            hm = (g * jax.nn.sigmoid(g) * u).astype(BF16)
            acc_ref[rows, :] += jnp.dot(hm, wd, preferred_element_type=F32)

        by_block_count(i, mlp)

    @pl.when(valid & (j == nj - 1))
    def _():
        @pl.when(i > 0)
        def _():
            each_block(i - 1, lambda sub: out_copy(i - 1, 1 - slot, sub).wait())

        def pack(nb):
            rows = pl.ds(0, nb * ITEM_SUB)
            yp_ref[slot, rows, :] = _pack_bf16_pairs(acc_ref[rows, :])

        by_block_count(i, pack)
        each_block(i, lambda sub: out_copy(i, slot, sub).start())

        @pl.when(i == n_items - 1)
        def _():
            each_block(i, lambda sub: out_copy(i, slot, sub).wait())


def _experts(item_e, item_start, item_n, n_items, x_sorted, w_gate, w_up, w_down, max_items, m):
    d = w_gate.shape[1]
    nj = D_EXPERT // F_CHUNK

    def jj(i, j, ni):
        return jnp.where(i < ni[0], j, nj - 1)

    grid_spec = pltpu.PrefetchScalarGridSpec(
        num_scalar_prefetch=4,
        grid=(max_items, nj),
        in_specs=[pl.BlockSpec(memory_space=pl.ANY),
                  pl.BlockSpec((None, d, F_CHUNK), lambda i, j, ie, s, n, ni: (ie[i], 0, jj(i, j, ni))),
                  pl.BlockSpec((None, d, F_CHUNK), lambda i, j, ie, s, n, ni: (ie[i], 0, jj(i, j, ni))),
                  pl.BlockSpec((None, F_CHUNK, d), lambda i, j, ie, s, n, ni: (ie[i], jj(i, j, ni), 0))],
        out_specs=pl.BlockSpec(memory_space=pl.ANY),
        scratch_shapes=[pltpu.VMEM((ITEM_ROWS, d // 2), jnp.uint32), pltpu.VMEM((ITEM_ROWS, d), BF16),
                        pltpu.VMEM((ITEM_ROWS, d), F32), pltpu.VMEM((2, ITEM_ROWS, d // 2), jnp.uint32),
                        pltpu.SemaphoreType.DMA(()), pltpu.SemaphoreType.DMA((2,))],
    )
    return pl.pallas_call(
        functools.partial(_expert_kernel, m=m, nj=nj),
        grid_spec=grid_spec,
        out_shape=jax.ShapeDtypeStruct((_sorted_rows(m), d // 2), jnp.uint32),
        compiler_params=_params("arbitrary", "arbitrary"),
        name="moe_experts",
    )(item_e, item_start, item_n, n_items, x_sorted, w_gate, w_up, w_down)


def _combine_kernel(dest_ref, y_hbm, x_ref, wt_ref, g_ref, o_ref, yv_ref, sem, *, tm):
    i = pl.program_id(0)
    slot = lax.rem(i, 2)

    def issue_tile(tile, sl):
        def issue(r, c):
            base = TOPK_IN_GROUP * (tile * tm + r)
            for k in range(TOPK_IN_GROUP):
                _row_copy(y_hbm, dest_ref[base + k], yv_ref.at[sl, k], r, sem.at[sl]).start(priority=k % 2)
            return c

        lax.fori_loop(0, tm, issue, 0, unroll=4)

    @pl.when(i == 0)
    def _():
        issue_tile(0, 0)

    @pl.when(i + 1 < pl.num_programs(0))
    def _():
        issue_tile(i + 1, 1 - slot)

    for k in range(TOPK_IN_GROUP):
        pltpu.make_async_copy(y_hbm.at[pl.ds(0, tm), :], yv_ref.at[slot, k], sem.at[slot]).wait()
    w = wt_ref[...]
    half = x_ref.shape[1] // 2
    parts = [x_ref[:, :half], x_ref[:, half:]]
    for k in range(TOPK_IN_GROUP):
        lo, hi = _unpack_bf16_pairs(yv_ref[slot, k])
        parts = [parts[0] + lo.astype(F32) * w[:, k:k + 1], parts[1] + hi.astype(F32) * w[:, k:k + 1]]
    ms = (jnp.sum(parts[0] * parts[0], axis=-1, keepdims=True)
          + jnp.sum(parts[1] * parts[1], axis=-1, keepdims=True)) / x_ref.shape[1]
    r = lax.rsqrt(ms + EPS)
    o_ref[:, :half] = parts[0] * r * g_ref[:, :half]
    o_ref[:, half:] = parts[1] * r * g_ref[:, half:]


def _combine(dest, y_sorted, x2, wts, g_final, tm=256):
    n, d = x2.shape
    grid_spec = pltpu.PrefetchScalarGridSpec(
        num_scalar_prefetch=1,
        grid=(n // tm,),
        in_specs=[pl.BlockSpec(memory_space=pl.ANY),
                  pl.BlockSpec((tm, d), lambda i, dr: (i, 0)),
                  pl.BlockSpec((tm, LANES), lambda i, dr: (i, 0)),
                  pl.BlockSpec((1, d), lambda i, dr: (0, 0))],
        out_specs=pl.BlockSpec((tm, d), lambda i, dr: (i, 0)),
        scratch_shapes=[pltpu.VMEM((2, TOPK_IN_GROUP, tm, d // 2), y_sorted.dtype), pltpu.SemaphoreType.DMA((2,))],
    )
    return pl.pallas_call(
        functools.partial(_combine_kernel, tm=tm),
        grid_spec=grid_spec,
        out_shape=jax.ShapeDtypeStruct((n, d), F32),
        compiler_params=_params("arbitrary"),
        name="moe_combine_final_norm",
    )(dest, y_sorted, x2, wts, g_final.reshape(1, d))


def _pad_cols(w, width):
    return jnp.pad(w, ((0, 0), (0, width - w.shape[1])))


def _mixer(x2d, bsz, s, g_norm, w_in, b_f, g_q_lat, g_kv_lat, g_idx_k, b_idx_k, w_uq, w_idx_q, w_uk, w_uv,
           w_up_a, w_up_b, w_out):
    d = x2d.shape[1]
    topk = min(DSA_TOPK_MAX, s // 4)
    h1 = _rmsnorm(x2d, g_norm, BF16)
    c_small = 3 * FOX_WIDTH
    o_fa, o_cq = c_small, c_small + FOX_HEADS
    o_ckv = o_cq + Q_LORA
    o_kr = o_ckv + KV_LORA
    o_ik = o_kr + DSA_ROPE_DIM
    o_iw = o_ik + IDX_DIM
    o_ga = o_iw + IDX_HEADS
    o_gb = o_ga + d
    w_t = jnp.swapaxes(w_in, 0, 1)
    pad_rows = lambda w: jnp.pad(w, ((0, LANES - w.shape[0]), (0, 0)))
    w_small_t = jnp.concatenate([
        w_t[o_cq:o_ckv], w_t[o_ckv:o_kr], w_t[o_ik:o_iw],
        pad_rows(w_t[o_kr:o_ik]), pad_rows(w_t[o_fa:o_cq]), pad_rows(w_t[o_iw:o_ga])], axis=0)

    qkv = _mm(h1, w_t, 3 * FOX_WIDTH, w_rows=True, scaled_cols=(FOX_WIDTH, FOX_QSCALE), name="proj_qkv")
    small = _mm(h1, w_small_t, SMALL_W, out_dtype=F32, tm=512, tn=SMALL_W // 2, w_rows=True, name="proj_small")

    tabs64 = _rope_tables(s, DSA_ROPE_DIM)
    tabs128 = _rope_tables(s, IDX_DIM)
    cq_n, kv_cat, k_idx, w_idx, fa = _prep(small, g_q_lat, g_kv_lat, g_idx_k, b_idx_k, tabs64, tabs128, s)

    cum = _fox_cum(fa, _pad_cols(b_f.reshape(1, -1), LANES), bsz, s)
    att_a = _fox_attention(qkv, cum, bsz, s)

    w_uq3 = w_uq.reshape(Q_LORA, DSA_HEADS, DSA_NOPE_DIM + DSA_ROPE_DIM)
    w_uq_nope = w_uq3[:, :, :DSA_NOPE_DIM].reshape(Q_LORA, DSA_HEADS * DSA_NOPE_DIM).astype(BF16)
    w_uq_rope = jnp.pad(w_uq3[:, :, DSA_NOPE_DIM:], ((0, 0), (0, 0), (0, LANES - DSA_ROPE_DIM))).reshape(
        Q_LORA, DSA_HEADS * LANES).astype(BF16)
    q_cat = _q_latent(cq_n, w_uq_nope, w_uq_rope, w_uk.astype(BF16), tabs64, bsz, s)
    q_idx = _mm(cq_n, w_idx_q, IDX_HEADS * IDX_DIM, heads=(bsz, s), rope_tabs=tabs128, tn=1024,
                name="proj_q_idx")
    w_idx_t = jnp.transpose(w_idx.reshape(bsz, s, LANES)[:, :, :IDX_HEADS], (0, 2, 1))
    bias = _indexer_mask(q_idx, k_idx.reshape(bsz, s, IDX_DIM), w_idx_t, bsz, s, topk)
    kv3 = kv_cat.reshape(bsz, s, KV_PAD)
    kv_t = jnp.transpose(kv3.reshape(bsz, s // KEY_CHUNK, KEY_CHUNK, KV_PAD), (0, 1, 3, 2))
    o_lat = _dsa_attention(q_cat, kv_t, kv3, bias, bsz, s)
    att_b = _uv_project(o_lat, w_uv.astype(BF16), bsz, s)

    merged = _gated_merge(h1, att_a, att_b, w_t, o_ga, o_gb, w_up_a, w_up_b)
    return _mm(merged, w_out, d, out_dtype=F32, res=x2d, name="proj_out")


def _memory_layer(x1, mem, bsz, s, g_x, g_m, w_qm, w_km, w_vm, w_om):
    n_mem = mem.shape[1]
    d = x1.shape[1]
    hd = MEM_HEADS * MEM_HEAD_DIM
    mem_n = _rmsnorm(mem.reshape(bsz * n_mem, d), g_m, BF16)
    kv_m = _mm(mem_n, jnp.concatenate([w_km, w_vm], axis=1), 2 * hd, tm=bsz * n_mem, name="proj_kv_mem")
    km_t = jnp.transpose(kv_m[:, :hd].reshape(bsz, n_mem, hd), (0, 2, 1))
    return _mem_attention(x1, g_x, w_qm.astype(BF16), km_t, kv_m, w_om.astype(BF16), bsz, s, n_mem)


def _moe_layer(x2, g_ffn, w_rg, b_rg, w_re, b_re, w_gate, w_up, w_down, g_final):
    n, d = x2.shape
    m = n * TOPK_IN_GROUP
    w_r = _pad_cols(jnp.concatenate([w_re, w_rg], axis=1), LANES)
    w_r_hi = w_r.astype(BF16)
    w_r_split = jnp.concatenate([w_r_hi, (w_r - w_r_hi.astype(F32)).astype(BF16)], axis=1)
    b_r = _pad_cols(jnp.concatenate([b_re, b_rg]).reshape(1, -1), LANES)
    h3p, ridx, wts, cnt = _router(x2, g_ffn, w_r_split, b_r)
    counts = cnt[0, :N_EXPERTS].astype(jnp.int32)
    e_flat = ridx[:, :TOPK_IN_GROUP].reshape(m)
    rank_flat = ridx[:, TOPK_IN_GROUP:2 * TOPK_IN_GROUP].reshape(m)
    seg = (counts + SUBLANES - 1) // SUBLANES * SUBLANES
    starts = (jnp.cumsum(seg) - seg).astype(jnp.int32)
    n_it = (counts + ITEM_ROWS - 1) // ITEM_ROWS
    it_end = jnp.cumsum(n_it)
    it_first = it_end - n_it
    max_items = (m + N_EXPERTS * (ITEM_ROWS - 1)) // ITEM_ROWS
    t = jnp.arange(max_items, dtype=jnp.int32)
    item_e = jnp.minimum(jnp.sum(it_end[None, :] <= t[:, None], axis=1), N_EXPERTS - 1).astype(jnp.int32)
    k_in = t - it_first[item_e]
    item_start = (starts[item_e] + k_in * ITEM_ROWS).astype(jnp.int32)
    item_n = jnp.clip(counts[item_e] - k_in * ITEM_ROWS, 0, ITEM_ROWS).astype(jnp.int32)
    n_items = it_end[-1:].astype(jnp.int32)
    last = jnp.maximum(n_items[0] - 1, 0)
    item_e = jnp.where(t < n_items[0], item_e, item_e[last])
    item_start = jnp.where(t < n_items[0], item_start, 0)
    item_n = jnp.where(t < n_items[0], item_n, 0)
    e_ids = jnp.arange(N_EXPERTS, dtype=jnp.int32)
    dest = rank_flat + jnp.sum(jnp.where(e_flat[:, None] == e_ids[None, :], starts[None, :], 0), axis=1)
    x_sorted = _dispatch(dest, starts, counts, h3p)
    y_sorted = _experts(item_e, item_start, item_n, n_items, x_sorted, w_gate, w_up, w_down, max_items, m)
    return _combine(dest, y_sorted, x2, wts, g_final)


def kernel(x, mem, g_norm_mix, w_in, b_f, g_q_lat, g_kv_lat, g_idx_k, b_idx_k, w_uq, w_idx_q, w_uk, w_uv,
           w_up_a, w_up_b, w_out, g_norm_mem_x, g_mem, w_qm, w_km, w_vm, w_om, g_norm_ffn,
           w_rg, b_rg, w_re, b_re, w_gate, w_up, w_down, g_final):
    bsz, s, d = x.shape
    depth = w_in.shape[0]
    assert depth == 1, "the final norm is fused into the last MoE combine"
    l = 0
    x2d = x.reshape(bsz * s, d)
    x1 = _mixer(x2d, bsz, s, g_norm_mix[l], w_in[l], b_f[l], g_q_lat[l], g_kv_lat[l], g_idx_k[l], b_idx_k[l],
                w_uq[l], w_idx_q[l], w_uk[l], w_uv[l], w_up_a[l], w_up_b[l], w_out[l])
    x2 = _memory_layer(x1, mem, bsz, s, g_norm_mem_x[l], g_mem[l], w_qm[l], w_km[l], w_vm[l], w_om[l])
    out = _moe_layer(x2, g_norm_ffn[l], w_rg[l], b_rg[l], w_re[l], b_re[l], w_gate[l], w_up[l], w_down[l], g_final)
    return out.reshape(bsz, s, d)
```
